```python
import jax, jax.numpy as jnp
from jax import lax
import numpy as np

D_MODEL = 1024
BATCH = 4
SEQ = 8192
DEPTH = 4

GRID_W = 64
CTX_LEN = 256
N_MIXERS = 3
Q_BLOCK = 128
ROPE_THETA = 10000.0
NORM_EPS = 1e-6

GQA_HEAD_DIM = 128
GQA_HEADS = D_MODEL // GQA_HEAD_DIM
GQA_KV_HEADS = GQA_HEADS // 4
GQA_GROUP = GQA_HEADS // GQA_KV_HEADS
GQA_WIDTH = GQA_HEADS * GQA_HEAD_DIM
GQA_KV_WIDTH = GQA_KV_HEADS * GQA_HEAD_DIM
GQA_PROJ = 2 * GQA_WIDTH + 2 * GQA_KV_WIDTH

NA_HEAD_DIM = 64
NA_HEADS = D_MODEL // NA_HEAD_DIM
NA_WIDTH = NA_HEADS * NA_HEAD_DIM
NA_KH = 8
NA_KW = 16
NA_PROJ = 4 * NA_WIDTH

MLA_HEADS = 8
MLA_Q_LORA = 512
MLA_KV_LORA = 256
MLA_NOPE = 128
MLA_ROPE = 64
MLA_V = 128
MLA_WIDTH = MLA_HEADS * MLA_V
MLA_PROJ = MLA_Q_LORA + MLA_KV_LORA + MLA_ROPE + MLA_WIDTH

kernel_name = 'hybrid_gqa_natten_mla_prefix_dit'


def rms_norm(x, g):
    xf = x.astype(jnp.float32)
    y = xf * lax.rsqrt(jnp.mean(xf * xf, axis=-1, keepdims=True) + NORM_EPS)
    return (y * g.astype(jnp.float32)).astype(x.dtype)


def axial_rope_tables(n_tok, rot_dim):
    n_freq = rot_dim // 4
    inv = ROPE_THETA ** (-jnp.arange(n_freq, dtype=jnp.float32) / n_freq)
    t = jnp.arange(n_tok)
    row = (t // GRID_W).astype(jnp.float32)
    col = (t % GRID_W).astype(jnp.float32)
    ang_r = row[:, None] * inv
    ang_c = col[:, None] * inv
    ang = jnp.concatenate([ang_r, ang_r, ang_c, ang_c], axis=-1)
    return jnp.cos(ang), jnp.sin(ang)


def apply_axial_rope(x, cos, sin):
    n_freq = x.shape[-1] // 4
    bshape = (cos.shape[0],) + (1,) * (x.ndim - 3) + (cos.shape[1],)
    xs = x.reshape(x.shape[:-1] + (2, 2, n_freq))
    rot = jnp.concatenate([-xs[..., 1:2, :], xs[..., 0:1, :]], axis=-2).reshape(x.shape)
    return x * cos.reshape(bshape).astype(x.dtype) + rot * sin.reshape(bshape).astype(x.dtype)


def blocked_attention(q, k, v, scale):
    b, n_q = q.shape[:2]
    nb = n_q // Q_BLOCK
    qb = jnp.moveaxis(q.reshape((b, nb, Q_BLOCK) + q.shape[2:]), 1, 0)

    def one_block(q_blk):
        s = jnp.einsum('bqkgd,bnkd->bkgqn', q_blk, k, preferred_element_type=jnp.float32) * scale
        p = jax.nn.softmax(s, axis=-1).astype(v.dtype)
        return jnp.einsum('bkgqn,bnkd->bqkgd', p, v)

    o = lax.map(one_block, qb)
    return jnp.moveaxis(o, 0, 1).reshape((b, n_q) + o.shape[3:])


def gqa_axial_mixer(h_lat, h_ctx, w_in, q_g, k_g, w_out, cos, sin, need_ctx):
    def project(h):
        b, n = h.shape[:2]
        q, k, v, g = jnp.split(h @ w_in, [GQA_WIDTH, GQA_WIDTH + GQA_KV_WIDTH, GQA_WIDTH + 2 * GQA_KV_WIDTH], axis=-1)
        q = rms_norm(q.reshape(b, n, GQA_KV_HEADS, GQA_GROUP, GQA_HEAD_DIM), q_g)
        k = rms_norm(k.reshape(b, n, GQA_KV_HEADS, GQA_HEAD_DIM), k_g)
        v = v.reshape(b, n, GQA_KV_HEADS, GQA_HEAD_DIM)
        return q, k, v, g

    q_l, k_l, v_l, g_l = project(h_lat)
    q_c, k_c, v_c, g_c = project(h_ctx)
    q_l = apply_axial_rope(q_l, cos, sin)
    k_l = apply_axial_rope(k_l, cos, sin)
    scale = GQA_HEAD_DIM ** -0.5
    o_l = blocked_attention(q_l, jnp.concatenate([k_c, k_l], axis=1), jnp.concatenate([v_c, v_l], axis=1), scale)
    y_l = (o_l.reshape(h_lat.shape[:2] + (GQA_WIDTH,)) * jax.nn.silu(g_l)) @ w_out
    if not need_ctx:
        return y_l, None
    o_c = blocked_attention(q_c, k_c, v_c, scale)
    y_c = (o_c.reshape(h_ctx.shape[:2] + (GQA_WIDTH,)) * jax.nn.silu(g_c)) @ w_out
    return y_l, y_c


def neighbourhood_mixer(h_lat, h_ctx, w_in, rpb, w_out, need_ctx):
    b, n = h_lat.shape[:2]
    rows = n // GRID_W
    kh = min(NA_KH, rows)
    kw = NA_KW

    def project(h):
        q, k, v, g = jnp.split(h @ w_in, [NA_WIDTH, 2 * NA_WIDTH, 3 * NA_WIDTH], axis=-1)
        sh = h.shape[:2] + (NA_HEADS, NA_HEAD_DIM)
        return q.reshape(sh), k.reshape(sh), v.reshape(sh), g

    q_l, k_l, v_l, g_l = project(h_lat)
    q_c, k_c, v_c, g_c = project(h_ctx)
    grid = (b, rows, GRID_W, NA_HEADS, NA_HEAD_DIM)
    qg, kg, vg = q_l.reshape(grid), k_l.reshape(grid), v_l.reshape(grid)
    scale = NA_HEAD_DIM ** -0.5
    row_start = jnp.clip(jnp.arange(rows) - kh // 2, 0, rows - kh)
    col_idx = jnp.clip(jnp.arange(GRID_W) - kw // 2, 0, GRID_W - kw)[:, None] + jnp.arange(kw)
    col_off = col_idx - jnp.arange(GRID_W)[:, None] + (NA_KW - 1)

    def row_block(r):
        rs = row_start[r]
        q_r = lax.dynamic_index_in_dim(qg, r, axis=1, keepdims=False)
        k_win = lax.dynamic_slice_in_dim(kg, rs, kh, axis=1)[:, :, col_idx]
        v_win = lax.dynamic_slice_in_dim(vg, rs, kh, axis=1)[:, :, col_idx]
        row_off = rs + jnp.arange(kh) - r + (NA_KH - 1)
        bias = rpb[:, row_off[None, :, None], col_off[:, None, :]]
        s_loc = jnp.einsum('bqhd,biqjhd->bhqij', q_r, k_win, preferred_element_type=jnp.float32) * scale + bias
        s_ctx = jnp.einsum('bqhd,bchd->bhqc', q_r, k_c, preferred_element_type=jnp.float32) * scale
        s = jnp.concatenate([s_loc.reshape(b, NA_HEADS, GRID_W, kh * kw), s_ctx], axis=-1)
        p = jax.nn.softmax(s, axis=-1).astype(v_l.dtype)
        p_loc = p[..., :kh * kw].reshape(b, NA_HEADS, GRID_W, kh, kw)
        p_ctx = p[..., kh * kw:]
        return jnp.einsum('bhqij,biqjhd->bqhd', p_loc, v_win) + jnp.einsum('bhqc,bchd->bqhd', p_ctx, v_c)

    o = lax.map(row_block, jnp.arange(rows))
    o = jnp.moveaxis(o, 0, 1).reshape(b, n, NA_WIDTH)
    y_l = (o * jax.nn.silu(g_l)) @ w_out
    if not need_ctx:
        return y_l, None
    o_c = blocked_attention(q_c[:, :, :, None, :], k_c, v_c, scale)
    y_c = (o_c.reshape(h_ctx.shape[:2] + (NA_WIDTH,)) * jax.nn.silu(g_c)) @ w_out
    return y_l, y_c


def mla_mixer(h_lat, h_ctx, w_in, q_g, kv_g, w_uq, w_ukv, w_out, cos, sin, need_ctx):
    def project(h, use_rope):
        b, n = h.shape[:2]
        c_q, c_kv, k_r, g = jnp.split(h @ w_in, [MLA_Q_LORA, MLA_Q_LORA + MLA_KV_LORA, MLA_Q_LORA + MLA_KV_LORA + MLA_ROPE], axis=-1)
        q = (rms_norm(c_q, q_g) @ w_uq).reshape(b, n, MLA_HEADS, MLA_NOPE + MLA_ROPE)
        kv = (rms_norm(c_kv, kv_g) @ w_ukv).reshape(b, n, MLA_HEADS, MLA_NOPE + MLA_V)
        q_nope, q_rope = q[..., :MLA_NOPE], q[..., MLA_NOPE:]
        k_nope, v = kv[..., :MLA_NOPE], kv[..., MLA_NOPE:]
        k_r = k_r[:, :, None, :]
        if use_rope:
            q_rope = apply_axial_rope(q_rope, cos, sin)
            k_r = apply_axial_rope(k_r, cos, sin)
        q = jnp.concatenate([q_nope, q_rope], axis=-1)
        k = jnp.concatenate([k_nope, jnp.broadcast_to(k_r, (b, n, MLA_HEADS, MLA_ROPE))], axis=-1)
        return q, k, v, g

    q_l, k_l, v_l, g_l = project(h_lat, True)
    q_c, k_c, v_c, g_c = project(h_ctx, False)
    scale = (MLA_NOPE + MLA_ROPE) ** -0.5
    o_l = blocked_attention(q_l[:, :, :, None, :], jnp.concatenate([k_c, k_l], axis=1), jnp.concatenate([v_c, v_l], axis=1), scale)
    y_l = (o_l.reshape(h_lat.shape[:2] + (MLA_WIDTH,)) * jax.nn.silu(g_l)) @ w_out
    if not need_ctx:
        return y_l, None
    o_c = blocked_attention(q_c[:, :, :, None, :], k_c, v_c, scale)
    y_c = (o_c.reshape(h_ctx.shape[:2] + (MLA_WIDTH,)) * jax.nn.silu(g_c)) @ w_out
    return y_l, y_c


def setup_inputs(seed: int = 0) -> dict:
    key = jax.random.key(seed)
    ks = jax.random.split(key, 24)
    f32 = jnp.float32
    D = D_MODEL
    n_a = len(range(0, DEPTH, N_MIXERS))
    n_b = len(range(1, DEPTH, N_MIXERS))
    n_c = len(range(2, DEPTH, N_MIXERS))

    def dense(k, shape):
        return jax.random.normal(k, shape, f32) * shape[-2] ** -0.5

    def gain(k, shape):
        return 1.0 + 0.05 * jax.random.normal(k, shape, f32)

    return {
        'x': jax.random.normal(ks[0], (BATCH, SEQ, D), f32),
        'c': jax.random.normal(ks[1], (BATCH, D), f32),
        'ctx': jax.random.normal(ks[2], (BATCH, CTX_LEN, D), f32),
        'c_ctx': jax.random.normal(ks[3], (D,), f32),
        'mod_w': 0.5 * dense(ks[4], (DEPTH, D, 3 * D)),
        'mod_b': 0.01 * jax.random.normal(ks[5], (DEPTH, 3 * D), f32),
        'norm_g': gain(ks[6], (DEPTH, D)),
        'final_g': gain(ks[7], (D,)),
        'ga_w_in': dense(ks[8], (n_a, D, GQA_PROJ)),
        'ga_q_g': gain(ks[9], (n_a, GQA_HEAD_DIM)),
        'ga_k_g': gain(ks[10], (n_a, GQA_HEAD_DIM)),
        'ga_w_out': dense(ks[11], (n_a, GQA_WIDTH, D)),
        'na_w_in': dense(ks[12], (n_b, D, NA_PROJ)),
        'na_rpb': 0.1 * jax.random.normal(ks[13], (n_b, NA_HEADS, 2 * NA_KH - 1, 2 * NA_KW - 1), f32),
        'na_w_out': dense(ks[14], (n_b, NA_WIDTH, D)),
        'mla_w_in': dense(ks[15], (n_c, D, MLA_PROJ)),
        'mla_q_g': gain(ks[16], (n_c, MLA_Q_LORA)),
        'mla_kv_g': gain(ks[17], (n_c, MLA_KV_LORA)),
        'mla_w_uq': dense(ks[18], (n_c, MLA_Q_LORA, MLA_HEADS * (MLA_NOPE + MLA_ROPE))),
        'mla_w_ukv': dense(ks[19], (n_c, MLA_KV_LORA, MLA_HEADS * (MLA_NOPE + MLA_V))),
        'mla_w_out': dense(ks[20], (n_c, MLA_WIDTH, D)),
    }


def reference(x, c, ctx, c_ctx, mod_w, mod_b, norm_g, final_g,
              ga_w_in, ga_q_g, ga_k_g, ga_w_out,
              na_w_in, na_rpb, na_w_out,
              mla_w_in, mla_q_g, mla_kv_g, mla_w_uq, mla_w_ukv, mla_w_out):
    n_lat = x.shape[1]
    cos_a, sin_a = axial_rope_tables(n_lat, GQA_HEAD_DIM)
    cos_m, sin_m = axial_rope_tables(n_lat, MLA_ROPE)
    xc = ctx
    for i in range(DEPTH):
        kind, j = i % N_MIXERS, i // N_MIXERS
        need_ctx = i < DEPTH - 1
        shift, scale, gate = jnp.split(jax.nn.silu(c) @ mod_w[i] + mod_b[i], 3, axis=-1)
        shift_c, scale_c, gate_c = jnp.split(jax.nn.silu(c_ctx) @ mod_w[i] + mod_b[i], 3, axis=-1)
        h = rms_norm(x, norm_g[i]) * (1.0 + scale[:, None]) + shift[:, None]
        hc = rms_norm(xc, norm_g[i]) * (1.0 + scale_c) + shift_c
        if kind == 0:
            y, yc = gqa_axial_mixer(h, hc, ga_w_in[j], ga_q_g[j], ga_k_g[j], ga_w_out[j], cos_a, sin_a, need_ctx)
        elif kind == 1:
            y, yc = neighbourhood_mixer(h, hc, na_w_in[j], na_rpb[j], na_w_out[j], need_ctx)
        else:
            y, yc = mla_mixer(h, hc, mla_w_in[j], mla_q_g[j], mla_kv_g[j], mla_w_uq[j], mla_w_ukv[j], mla_w_out[j], cos_m, sin_m, need_ctx)
        x = x + gate[:, None] * y
        if need_ctx:
            xc = xc + gate_c * yc
    return rms_norm(x, final_g)
```

```python
import functools

import jax
import jax.numpy as jnp
from jax import lax
from jax.experimental import pallas as pl
from jax.experimental.pallas import tpu as pltpu

F32 = jnp.float32
BF16 = jnp.bfloat16

NORM_EPS = 1e-6
ROPE_THETA = 10000.0
GRID_W = 64
N_MIXERS = 3
LANES = 128
MOD_ROWS = 8
MASK_VALUE = -1e30

GQA_HEAD_DIM = 128
GQA_GROUP = 4
NA_HEAD_DIM = 64
NA_KH = 8
NA_KW = 16
NA_Q_ROWS = 8
NA_K_ROWS = 16
MLA_HEADS = 8
MLA_Q_LORA = 512
MLA_KV_LORA = 256
MLA_NOPE = 128
MLA_ROPE = 64
MLA_V = 128
MLA_QK_PAD = 256

ROW_TILE = 256
Q_TILE = 512
KV_CHUNK = 512
VMEM_LIMIT = 48 * 1024 * 1024


def _params(*sem):
    return pltpu.CompilerParams(dimension_semantics=sem, vmem_limit_bytes=VMEM_LIMIT)


def _silu(v):
    return v * (1.0 / (1.0 + jnp.exp(-v)))


def _rms(v):
    return lax.rsqrt(jnp.mean(v * v, axis=-1, keepdims=True) + NORM_EPS)


def _rope(v, cos, sin_up, sin_dn, half):
    w = v.shape[-1]
    return v * cos + pltpu.roll(v, w - half, 1) * sin_up + pltpu.roll(v, half, 1) * sin_dn


def _dot(a, b):
    return jnp.dot(a, b, preferred_element_type=F32)


def _dot_nt(a, b):
    return lax.dot_general(a, b, (((1,), (1,)), ((), ())), preferred_element_type=F32)


def _mod_kernel(c_ref, w_ref, b_ref, o_ref):
    a = _silu(c_ref[...])
    w = w_ref[0]
    a_hi = a.astype(BF16)
    a_lo = (a - a_hi.astype(F32)).astype(BF16)
    w_hi = w.astype(BF16)
    w_lo = (w - w_hi.astype(F32)).astype(BF16)
    o_ref[0] = _dot(a_hi, w_hi) + _dot(a_lo, w_hi) + _dot(a_hi, w_lo) + b_ref[0]


def _modulation(cc, mod_w, mod_b):
    depth, d, n3 = mod_w.shape
    tn = 1024
    return pl.pallas_call(
        _mod_kernel,
        grid=(depth, n3 // tn),
        in_specs=[
            pl.BlockSpec((MOD_ROWS, d), lambda l, j: (0, 0)),
            pl.BlockSpec((1, d, tn), lambda l, j: (l, 0, j)),
            pl.BlockSpec((1, 1, tn), lambda l, j: (l, 0, j)),
        ],
        out_specs=pl.BlockSpec((1, MOD_ROWS, tn), lambda l, j: (l, 0, j)),
        out_shape=jax.ShapeDtypeStruct((depth, MOD_ROWS, n3), F32),
        compiler_params=_params("parallel", "parallel"),
        name="adaln_modulation",
    )(cc, mod_w, mod_b.reshape(depth, 1, n3))


def _modulated_norm(x_ref, ng_ref, mod_ref):
    x = x_ref[...]
    shift = mod_ref[0, 0:1, :]
    scale = mod_ref[0, 1:2, :]
    return ((x * _rms(x)) * ng_ref[...] * (1.0 + scale) + shift).astype(BF16)


def _gqa_proj_kernel(x_ref, mod_ref, ng_ref, w_ref, qg_ref, kg_ref, cos_ref, su_ref, sd_ref,
                     q_ref, k_ref, v_ref, g_ref, *, n_q, n_kv, scale):
    h = _modulated_norm(x_ref, ng_ref, mod_ref)
    cos, su, sd = cos_ref[...], su_ref[...], sd_ref[...]
    hd = GQA_HEAD_DIM
    qw, kw = n_q * hd, n_kv * hd
    q = _dot(h, w_ref[:, :qw])
    for i in range(n_q):
        qh = q[:, i * hd:(i + 1) * hd]
        qh = qh * _rms(qh) * qg_ref[...]
        q_ref[:, i * hd:(i + 1) * hd] = (_rope(qh, cos, su, sd, hd // 4) * scale).astype(BF16)
    k = _dot(h, w_ref[:, qw:qw + kw])
    for i in range(n_kv):
        kh = k[:, i * hd:(i + 1) * hd]
        kh = kh * _rms(kh) * kg_ref[...]
        k_ref[:, i * hd:(i + 1) * hd] = _rope(kh, cos, su, sd, hd // 4).astype(BF16)
    v_ref[...] = _dot(h, w_ref[:, qw + kw:qw + 2 * kw]).astype(BF16)
    g_ref[...] = _silu(_dot(h, w_ref[:, qw + 2 * kw:])).astype(BF16)


def _na_proj_kernel(x_ref, mod_ref, ng_ref, w_ref, q_ref, k_ref, v_ref, g_ref, *, width, scale):
    h = _modulated_norm(x_ref, ng_ref, mod_ref)
    q_ref[...] = (_dot(h, w_ref[:, :width]) * scale).astype(BF16)
    k_ref[...] = _dot(h, w_ref[:, width:2 * width]).astype(BF16)
    v_ref[...] = _dot(h, w_ref[:, 2 * width:3 * width]).astype(BF16)
    g_ref[...] = _silu(_dot(h, w_ref[:, 3 * width:])).astype(BF16)


def _mla_proj_kernel(x_ref, mod_ref, ng_ref, w_ref, qg_ref, kvg_ref, wuq_ref, wkv_ref,
                     cos_ref, su_ref, sd_ref, q_ref, k_ref, v_ref, g_ref, *, width, scale):
    h = _modulated_norm(x_ref, ng_ref, mod_ref)
    cos, su, sd = cos_ref[...], su_ref[...], sd_ref[...]
    o_kv = MLA_Q_LORA
    o_g = o_kv + MLA_KV_LORA
    o_kr = o_g + width
    c_q = _dot(h, w_ref[:, :o_kv])
    c_q = (c_q * _rms(c_q) * qg_ref[...]).astype(BF16)
    q = _dot(c_q, wuq_ref[...])
    for i in range(MLA_HEADS):
        a = i * MLA_QK_PAD
        q_ref[:, a:a + MLA_NOPE] = (q[:, a:a + MLA_NOPE] * scale).astype(BF16)
        q_ref[:, a + MLA_NOPE:a + MLA_QK_PAD] = (
            _rope(q[:, a + MLA_NOPE:a + MLA_QK_PAD], cos, su, sd, MLA_ROPE // 4) * scale).astype(BF16)
    c_kv = _dot(h, w_ref[:, o_kv:o_g])
    c_kv = (c_kv * _rms(c_kv) * kvg_ref[...]).astype(BF16)
    k_r = _rope(_dot(h, w_ref[:, o_kr:]), cos, su, sd, MLA_ROPE // 4).astype(BF16)
    kv_in = jnp.concatenate([c_kv, k_r], axis=1)
    kw = MLA_HEADS * MLA_QK_PAD
    k_ref[...] = _dot(kv_in, wkv_ref[:, :kw]).astype(BF16)
    v_ref[...] = _dot(kv_in, wkv_ref[:, kw:]).astype(BF16)
    g_ref[...] = _silu(_dot(h, w_ref[:, o_g:o_kr])).astype(BF16)


def _project(kern, xs, mod3, layer, norm_g, consts, tables, out_widths, n_lat):
    b, n, d = xs.shape
    tm = ROW_TILE
    n_lat_tiles = n_lat // tm
    ctx_row = b

    def mod_idx(bi, t):
        return (layer * MOD_ROWS + jnp.where(t < n_lat_tiles, bi, ctx_row), 0, 0)

    in_specs = [
        pl.BlockSpec((None, tm, d), lambda bi, t: (bi, t, 0)),
        pl.BlockSpec((1, 3, d), mod_idx),
        pl.BlockSpec((1, d), lambda bi, t: (0, 0)),
    ]
    in_specs += [pl.BlockSpec(a.shape, lambda bi, t: (0, 0)) for a in consts]
    in_specs += [pl.BlockSpec((tm, LANES), lambda bi, t: (t, 0)) for _ in tables]
    out_specs = [pl.BlockSpec((None, tm, w), lambda bi, t: (bi, t, 0)) for w in out_widths]
    out_shape = [jax.ShapeDtypeStruct((b, n, w), BF16) for w in out_widths]
    return pl.pallas_call(
        kern,
        grid=(b, n // tm),
        in_specs=in_specs,
        out_specs=out_specs,
        out_shape=out_shape,
        compiler_params=_params("parallel", "parallel"),
        name="proj_" + str(layer),
    )(xs, mod3, norm_g.reshape(1, d), *consts, *tables)


def _softmax_pv(s, v):
    m = jnp.max(s, axis=-1, keepdims=True)
    p = jnp.exp(s - m)
    return m, jnp.sum(p, axis=-1, keepdims=True), _dot(p.astype(BF16), v)


def _flash_kernel(q_ref, k_ref, v_ref, o_ref, *, group, dk, n_lat, n_ctx, bk):
    for g in range(group):
        q = q_ref[:, g * dk:(g + 1) * dk]
        m, l, acc = _softmax_pv(_dot_nt(q, k_ref[n_lat:n_lat + n_ctx, :]), v_ref[n_lat:n_lat + n_ctx, :])

        def body(c, carry, q=q):
            m, l, acc = carry
            off = pl.multiple_of(c * bk, bk)
            s = _dot_nt(q, k_ref[pl.ds(off, bk), :])
            m_new = jnp.maximum(m, jnp.max(s, axis=-1, keepdims=True))
            alpha = jnp.exp(m - m_new)
            p = jnp.exp(s - m_new)
            l = alpha * l + jnp.sum(p, axis=-1, keepdims=True)
            acc = alpha * acc + _dot(p.astype(BF16), v_ref[pl.ds(off, bk), :])
            return m_new, l, acc

        m, l, acc = lax.fori_loop(0, n_lat // bk, body, (m, l, acc))
        o_ref[:, g * LANES:(g + 1) * LANES] = (acc / l).astype(BF16)


def _flash_attention(q, k, v, *, group, dk, n_lat, layer):
    b, n, _ = q.shape
    n_kv = k.shape[2] // dk
    n_ctx = n - n_lat
    tq = min(Q_TILE, n_lat)
    bk = min(KV_CHUNK, n_lat)
    kern = functools.partial(_flash_kernel, group=group, dk=dk, n_lat=n_lat, n_ctx=n_ctx, bk=bk)
    return pl.pallas_call(
        kern,
        grid=(b, n_kv, n_lat // tq),
        in_specs=[
            pl.BlockSpec((None, tq, group * dk), lambda bi, h, t: (bi, t, h)),
            pl.BlockSpec((None, n, dk), lambda bi, h, t: (bi, 0, h)),
            pl.BlockSpec((None, n, LANES), lambda bi, h, t: (bi, 0, h)),
        ],
        out_specs=pl.BlockSpec((None, tq, group * LANES), lambda bi, h, t: (bi, t, h)),
        out_shape=jax.ShapeDtypeStruct((b, n_lat, n_kv * group * LANES), BF16),
        compiler_params=_params("parallel", "parallel", "parallel"),
        name="flash_" + str(layer),
    )(q, k, v)


def _ctx_attn_kernel(q_ref, k_ref, v_ref, o_ref, *, group, dk, sub):
    k = k_ref[...]
    v = v_ref[...]
    lane = lax.broadcasted_iota(jnp.int32, (1, LANES), 1)
    for g in range(group):
        q = q_ref[:, g * dk:(g + 1) * dk]
        if sub == 1:
            _, l, acc = _softmax_pv(_dot_nt(q, k), v)
            out = acc / l
        else:
            out = jnp.zeros((q.shape[0], LANES), F32)
            for j in range(sub):
                mine = (lane >= j * (LANES // sub)) & (lane < (j + 1) * (LANES // sub))
                _, l, acc = _softmax_pv(_dot_nt(jnp.where(mine, q, jnp.zeros_like(q)), k), v)
                out = jnp.where(mine, acc / l, out)
        o_ref[:, g * LANES:(g + 1) * LANES] = out.astype(BF16)


def _ctx_attention(q, k, v, *, group, dk, sub, n_lat, layer):
    b, n, _ = q.shape
    n_ctx = n - n_lat
    n_kv = k.shape[2] // dk
    t = n_lat // n_ctx
    kern = functools.partial(_ctx_attn_kernel, group=group, dk=dk, sub=sub)
    return pl.pallas_call(
        kern,
        grid=(b, n_kv),
        in_specs=[
            pl.BlockSpec((None, n_ctx, group * dk), lambda bi, h: (bi, t, h)),
            pl.BlockSpec((None, n_ctx, dk), lambda bi, h: (bi, t, h)),
            pl.BlockSpec((None, n_ctx, LANES), lambda bi, h: (bi, t, h)),
        ],
        out_specs=pl.BlockSpec((None, n_ctx, group * LANES), lambda bi, h: (bi, 0, h)),
        out_shape=jax.ShapeDtypeStruct((b, n_ctx, n_kv * group * LANES), BF16),
        compiler_params=_params("parallel", "parallel"),
        name="ctx_attn_" + str(layer),
    )(q, k, v)


def _na_kernel(q_ref, k_ref, v_ref, bias_ref, o_ref, *, n_lat, n_ctx, rows):
    rb = pl.program_id(2)
    ks = jnp.clip(rb * NA_Q_ROWS - NA_KH // 2, 0, rows - NA_K_ROWS)
    off = pl.multiple_of(ks * GRID_W, GRID_W)
    nk = NA_K_ROWS * GRID_W
    q = q_ref[...]
    kw = k_ref[pl.ds(off, nk), :]
    vw = v_ref[pl.ds(off, nk), :]
    kc = k_ref[n_lat:n_lat + n_ctx, :]
    vc = v_ref[n_lat:n_lat + n_ctx, :]
    lane = lax.broadcasted_iota(jnp.int32, (1, LANES), 1)
    sub = LANES // NA_HEAD_DIM
    out = jnp.zeros((q.shape[0], LANES), F32)
    for j in range(sub):
        mine = (lane >= j * NA_HEAD_DIM) & (lane < (j + 1) * NA_HEAD_DIM)
        qj = jnp.where(mine, q, jnp.zeros_like(q))
        s_loc = _dot_nt(qj, kw) + bias_ref[0, j]
        s_ctx = _dot_nt(qj, kc)
        m = jnp.maximum(jnp.max(s_loc, axis=-1, keepdims=True), jnp.max(s_ctx, axis=-1, keepdims=True))
        p_loc = jnp.exp(s_loc - m)
        p_ctx = jnp.exp(s_ctx - m)
        l = jnp.sum(p_loc, axis=-1, keepdims=True) + jnp.sum(p_ctx, axis=-1, keepdims=True)
        acc = _dot(p_loc.astype(BF16), vw) + _dot(p_ctx.astype(BF16), vc)
        out = jnp.where(mine, acc / l, out)
    o_ref[...] = out.astype(BF16)


def _na_bias(rpb, rows):
    n_rb = rows // NA_Q_ROWS
    qc = jnp.arange(GRID_W)
    cs = jnp.clip(qc - NA_KW // 2, 0, GRID_W - NA_KW)
    kc = jnp.arange(GRID_W)
    col_ok = (kc[None, :] >= cs[:, None]) & (kc[None, :] < cs[:, None] + NA_KW)
    col_off = jnp.clip(kc[None, :] - qc[:, None] + NA_KW - 1, 0, 2 * NA_KW - 2)
    out = []
    for rb in (0, min(1, n_rb - 1), n_rb - 1):
        r0 = rb * NA_Q_ROWS
        ks = min(max(r0 - NA_KH // 2, 0), rows - NA_K_ROWS)
        r = r0 + jnp.arange(NA_Q_ROWS)
        rs = jnp.clip(r - NA_KH // 2, 0, rows - NA_KH)
        kr = ks + jnp.arange(NA_K_ROWS)
        row_ok = (kr[None, :] >= rs[:, None]) & (kr[None, :] < rs[:, None] + NA_KH)
        row_off = jnp.clip(kr[None, :] - r[:, None] + NA_KH - 1, 0, 2 * NA_KH - 2)
        vals = rpb[:, row_off[:, None, :, None], col_off[None, :, None, :]]
        ok = row_ok[:, None, :, None] & col_ok[None, :, None, :]
        vals = jnp.where(ok[None], vals, MASK_VALUE)
        out.append(vals.reshape(rpb.shape[0], NA_Q_ROWS * GRID_W, NA_K_ROWS * GRID_W))
    return jnp.stack(out)


def _na_attention(q, k, v, bias, *, n_lat, layer):
    b, n, width = q.shape
    n_ctx = n - n_lat
    rows = n_lat // GRID_W
    n_rb = rows // NA_Q_ROWS
    tq = NA_Q_ROWS * GRID_W
    sub = LANES // NA_HEAD_DIM
    kern = functools.partial(_na_kernel, n_lat=n_lat, n_ctx=n_ctx, rows=rows)

    def bias_idx(hp, bi, rb):
        return (jnp.where(rb == 0, 0, jnp.where(rb == n_rb - 1, 2, 1)), hp, 0, 0)

    return pl.pallas_call(
        kern,
        grid=(width // LANES, b, n_rb),
        in_specs=[
            pl.BlockSpec((None, tq, LANES), lambda hp, bi, rb: (bi, rb, hp)),
            pl.BlockSpec((None, n, LANES), lambda hp, bi, rb: (bi, 0, hp)),
            pl.BlockSpec((None, n, LANES), lambda hp, bi, rb: (bi, 0, hp)),
            pl.BlockSpec((1, sub, tq, NA_K_ROWS * GRID_W), bias_idx),
        ],
        out_specs=pl.BlockSpec((None, tq, LANES), lambda hp, bi, rb: (bi, rb, hp)),
        out_shape=jax.ShapeDtypeStruct((b, n_lat, width), BF16),
        compiler_params=_params("parallel", "parallel", "arbitrary"),
        name="na_attn_" + str(layer),
    )(q, k, v, bias)


def _out_proj_kernel(o_ref, oc_ref, g_ref, w_ref, x_ref, mod_ref, xo_ref, *, n_lat_tiles):
    o = jnp.where(pl.program_id(1) >= n_lat_tiles, oc_ref[...], o_ref[...])
    a = (o.astype(F32) * g_ref[...].astype(F32)).astype(BF16)
    xo_ref[...] = x_ref[...] + mod_ref[0, 2:3, :] * _dot(a, w_ref[...])


def _out_proj_final_kernel(o_ref, g_ref, w_ref, x_ref, mod_ref, fg_ref, y_ref):
    a = (o_ref[...].astype(F32) * g_ref[...].astype(F32)).astype(BF16)
    xn = x_ref[...] + mod_ref[0, 2:3, :] * _dot(a, w_ref[...])
    y_ref[...] = xn * _rms(xn) * fg_ref[...]


def _out_proj(o, o_ctx, gs, w_out, xs, mod3, layer, n_lat, final_g=None):
    b, n, d = xs.shape
    tm = ROW_TILE
    n_lat_tiles = n_lat // tm
    ctx_row = b

    def mod_idx(bi, t):
        return (layer * MOD_ROWS + jnp.where(t < n_lat_tiles, bi, ctx_row), 0, 0)

    row = lambda bi, t: (bi, t, 0)
    w = o.shape[2]
    o_spec = pl.BlockSpec((None, tm, w), lambda bi, t: (bi, jnp.minimum(t, n_lat_tiles - 1), 0))
    tail = [
        pl.BlockSpec((None, tm, w), row),
        pl.BlockSpec(w_out.shape, lambda bi, t: (0, 0)),
        pl.BlockSpec((None, tm, d), row),
        pl.BlockSpec((1, 3, d), mod_idx),
    ]
    if final_g is None:
        return pl.pallas_call(
            functools.partial(_out_proj_kernel, n_lat_tiles=n_lat_tiles),
            grid=(b, n // tm),
            in_specs=[o_spec, pl.BlockSpec((None, tm, w), lambda bi, t: (bi, 0, 0))] + tail,
            out_specs=pl.BlockSpec((None, tm, d), row),
            out_shape=jax.ShapeDtypeStruct(xs.shape, F32),
            input_output_aliases={4: 0},
            compiler_params=_params("parallel", "parallel"),
            name="out_proj_" + str(layer),
        )(o, o_ctx, gs, w_out, xs, mod3)
    return pl.pallas_call(
        _out_proj_final_kernel,
        grid=(b, n_lat_tiles),
        in_specs=[o_spec] + tail + [pl.BlockSpec((1, d), lambda bi, t: (0, 0))],
        out_specs=pl.BlockSpec((None, tm, d), row),
        out_shape=jax.ShapeDtypeStruct((b, n_lat, d), F32),
        compiler_params=_params("parallel", "parallel"),
        name="out_proj_final",
    )(o, gs, w_out, xs, mod3, final_g.reshape(1, d))


def _rope_tables(n_lat, n_ctx, rot_dim):
    n_freq = rot_dim // 4
    inv = ROPE_THETA ** (-jnp.arange(n_freq, dtype=F32) / n_freq)
    t = jnp.arange(n_lat)
    ang_r = (t // GRID_W).astype(F32)[:, None] * inv
    ang_c = (t % GRID_W).astype(F32)[:, None] * inv
    ang = jnp.concatenate([ang_r, ang_r, ang_c, ang_c], axis=-1)
    cos, sin = jnp.cos(ang), jnp.sin(ang)
    first = (jnp.arange(rot_dim) % (2 * n_freq)) < n_freq
    sin_up = jnp.where(first, -sin, 0.0)
    sin_dn = jnp.where(first, 0.0, sin)
    pad = LANES - rot_dim

    def finish(tab, fill):
        tab = jnp.pad(tab, ((0, 0), (0, pad)), constant_values=fill)
        return jnp.pad(tab, ((0, n_ctx), (0, 0)), constant_values=fill)

    return finish(cos, 1.0), finish(sin_up, 0.0), finish(sin_dn, 0.0)


def _mla_weights(w_in, w_uq, w_ukv):
    d = w_in.shape[0]
    o_kv = MLA_Q_LORA
    o_kr = o_kv + MLA_KV_LORA
    o_g = o_kr + MLA_ROPE
    k_r = jnp.pad(w_in[:, o_kr:o_g], ((0, 0), (0, LANES - MLA_ROPE)))
    w_perm = jnp.concatenate([w_in[:, :o_kr], w_in[:, o_g:], k_r], axis=1).astype(BF16)
    uq = w_uq.reshape(MLA_Q_LORA, MLA_HEADS, MLA_NOPE + MLA_ROPE)
    uq = jnp.pad(uq, ((0, 0), (0, 0), (0, MLA_QK_PAD - MLA_NOPE - MLA_ROPE)))
    uq = uq.reshape(MLA_Q_LORA, MLA_HEADS * MLA_QK_PAD).astype(BF16)
    ukv = w_ukv.reshape(MLA_KV_LORA, MLA_HEADS, MLA_NOPE + MLA_V)
    k_top = jnp.pad(ukv[:, :, :MLA_NOPE], ((0, 0), (0, 0), (0, MLA_QK_PAD - MLA_NOPE)))
    eye = jnp.pad(jnp.eye(MLA_ROPE, dtype=F32), ((0, LANES - MLA_ROPE), (MLA_NOPE, MLA_QK_PAD - MLA_NOPE - MLA_ROPE)))
    k_bot = jnp.broadcast_to(eye[:, None, :], (LANES, MLA_HEADS, MLA_QK_PAD))
    k_aug = jnp.concatenate([k_top, k_bot], axis=0).reshape(MLA_KV_LORA + LANES, MLA_HEADS * MLA_QK_PAD)
    v_aug = jnp.pad(ukv[:, :, MLA_NOPE:].reshape(MLA_KV_LORA, MLA_HEADS * MLA_V), ((0, LANES), (0, 0)))
    del d
    return w_perm, uq, jnp.concatenate([k_aug, v_aug], axis=1).astype(BF16)


def kernel(x, c, ctx, c_ctx, mod_w, mod_b, norm_g, final_g, ga_w_in, ga_q_g, ga_k_g, ga_w_out, na_w_in, na_rpb, na_w_out, mla_w_in, mla_q_g, mla_kv_g, mla_w_uq, mla_w_ukv, mla_w_out):
    b, n_lat, d = x.shape
    n_ctx = ctx.shape[1]
    depth = mod_w.shape[0]
    assert n_lat % ROW_TILE == 0 and n_ctx == ROW_TILE and n_lat % n_ctx == 0
    assert b < MOD_ROWS and (n_lat // GRID_W) % NA_Q_ROWS == 0 and n_lat // GRID_W >= NA_K_ROWS

    xs = jnp.concatenate([x, ctx], axis=1)
    cc = jnp.concatenate([c, c_ctx[None, :], jnp.zeros((MOD_ROWS - b - 1, d), F32)], axis=0)
    mod3 = _modulation(cc, mod_w, mod_b).reshape(depth * MOD_ROWS, 3, d)

    tab_a = _rope_tables(n_lat, n_ctx, GQA_HEAD_DIM)
    tab_m = _rope_tables(n_lat, n_ctx, MLA_ROPE)

    out = None
    for i in range(depth):
        kind, j = i % N_MIXERS, i // N_MIXERS
        need_ctx = i < depth - 1
        if kind == 0:
            n_q = d // GQA_HEAD_DIM
            n_kv = n_q // GQA_GROUP
            kern = functools.partial(_gqa_proj_kernel, n_q=n_q, n_kv=n_kv, scale=GQA_HEAD_DIM ** -0.5)
            consts = [ga_w_in[j].astype(BF16), ga_q_g[j].reshape(1, -1), ga_k_g[j].reshape(1, -1)]
            kvw = n_kv * GQA_HEAD_DIM
            q, k, v, gs = _project(kern, xs, mod3, i, norm_g[i], consts, tab_a, [d, kvw, kvw, d], n_lat)
            o = _flash_attention(q, k, v, group=GQA_GROUP, dk=GQA_HEAD_DIM, n_lat=n_lat, layer=i)
            if need_ctx:
                oc = _ctx_attention(q, k, v, group=GQA_GROUP, dk=GQA_HEAD_DIM, sub=1, n_lat=n_lat, layer=i)
            w_out = ga_w_out[j]
        elif kind == 1:
            kern = functools.partial(_na_proj_kernel, width=d, scale=NA_HEAD_DIM ** -0.5)
            q, k, v, gs = _project(kern, xs, mod3, i, norm_g[i], [na_w_in[j].astype(BF16)], (), [d, d, d, d], n_lat)
            o = _na_attention(q, k, v, _na_bias(na_rpb[j], n_lat // GRID_W), n_lat=n_lat, layer=i)
            if need_ctx:
                oc = _ctx_attention(q, k, v, group=1, dk=LANES, sub=LANES // NA_HEAD_DIM, n_lat=n_lat, layer=i)
            w_out = na_w_out[j]
        else:
            kern = functools.partial(_mla_proj_kernel, width=d, scale=(MLA_NOPE + MLA_ROPE) ** -0.5)
            w_perm, uq, wkv = _mla_weights(mla_w_in[j], mla_w_uq[j], mla_w_ukv[j])
            consts = [w_perm, mla_q_g[j].reshape(1, -1), mla_kv_g[j].reshape(1, -1), uq, wkv]
            widths = [MLA_HEADS * MLA_QK_PAD, MLA_HEADS * MLA_QK_PAD, MLA_HEADS * MLA_V, d]
            q, k, v, gs = _project(kern, xs, mod3, i, norm_g[i], consts, tab_m, widths, n_lat)
            o = _flash_attention(q, k, v, group=1, dk=MLA_QK_PAD, n_lat=n_lat, layer=i)
            if need_ctx:
                oc = _ctx_attention(q, k, v, group=1, dk=MLA_QK_PAD, sub=1, n_lat=n_lat, layer=i)
            w_out = mla_w_out[j]
        if need_ctx:
            xs = _out_proj(o, oc, gs, w_out.astype(BF16), xs, mod3, i, n_lat)
        else:
            out = _out_proj(o, None, gs, w_out.astype(BF16), xs, mod3, i, n_lat, final_g=final_g)
    return out
```

```python
import functools

import jax
import jax.numpy as jnp
from jax import lax
from jax.experimental import pallas as pl
from jax.experimental.pallas import tpu as pltpu

F32 = jnp.float32
BF16 = jnp.bfloat16

NORM_EPS = 1e-6
ROPE_THETA = 10000.0
GRID_W = 64
N_MIXERS = 3
LANES = 128
MOD_ROWS = 8
MASK_VALUE = -1e30
LOG2E = 1.4426950408889634

GQA_HEAD_DIM = 128
GQA_GROUP = 4
NA_HEAD_DIM = 64
NA_KH = 8
NA_KW = 16
NA_Q_ROWS = 8
NA_K_ROWS = 16
MLA_HEADS = 8
MLA_Q_LORA = 512
MLA_KV_LORA = 256
MLA_NOPE = 128
MLA_ROPE = 64
MLA_V = 128
MLA_QK_PAD = 256

ROW_TILE = 256
Q_COLS = 1024
KV_CHUNK = 512
VMEM_LIMIT = 48 * 1024 * 1024


def _params(*sem):
    return pltpu.CompilerParams(dimension_semantics=sem, vmem_limit_bytes=VMEM_LIMIT)


def _silu(v):
    return v * (1.0 / (1.0 + jnp.exp(-v)))


def _rms(v):
    return lax.rsqrt(jnp.mean(v * v, axis=-1, keepdims=True) + NORM_EPS)


def _rope(v, cos, sin_up, sin_dn, half):
    w = v.shape[-1]
    return v * cos + pltpu.roll(v, w - half, 1) * sin_up + pltpu.roll(v, half, 1) * sin_dn


def _dot(a, b):
    return jnp.dot(a, b, preferred_element_type=F32)


def _dot_nt(a, b):
    return lax.dot_general(a, b, (((1,), (1,)), ((), ())), preferred_element_type=F32)


def _mod_kernel(c_ref, w_ref, b_ref, o_ref):
    a = _silu(c_ref[...])
    w = w_ref[0]
    a_hi = a.astype(BF16)
    a_lo = (a - a_hi.astype(F32)).astype(BF16)
    w_hi = w.astype(BF16)
    w_lo = (w - w_hi.astype(F32)).astype(BF16)
    o_ref[0] = _dot(a_hi, w_hi) + _dot(a_lo, w_hi) + _dot(a_hi, w_lo) + b_ref[0]


def _modulation(cc, mod_w, mod_b):
    depth, d, n3 = mod_w.shape
    tn = 1024
    return pl.pallas_call(
        _mod_kernel,
        grid=(depth, n3 // tn),
        in_specs=[
            pl.BlockSpec((MOD_ROWS, d), lambda l, j: (0, 0)),
            pl.BlockSpec((1, d, tn), lambda l, j: (l, 0, j)),
            pl.BlockSpec((1, 1, tn), lambda l, j: (l, 0, j)),
        ],
        out_specs=pl.BlockSpec((1, MOD_ROWS, tn), lambda l, j: (l, 0, j)),
        out_shape=jax.ShapeDtypeStruct((depth, MOD_ROWS, n3), F32),
        compiler_params=_params("parallel", "parallel"),
        name="adaln_modulation",
    )(cc, mod_w, mod_b.reshape(depth, 1, n3))


def _modulated_norm(x_ref, ng_ref, mod_ref):
    x = x_ref[...]
    shift = mod_ref[0, 0:1, :]
    scale = mod_ref[0, 1:2, :]
    return ((x * _rms(x)) * ng_ref[...] * (1.0 + scale) + shift).astype(BF16)


def _gqa_proj_kernel(x_ref, mod_ref, ng_ref, w_ref, qg_ref, kg_ref, cos_ref, su_ref, sd_ref,
                     q_ref, k_ref, vt_ref, g_ref, *, n_q, n_kv, scale):
    h = _modulated_norm(x_ref, ng_ref, mod_ref)
    cos, su, sd = cos_ref[...], su_ref[...], sd_ref[...]
    hd = GQA_HEAD_DIM
    qw, kw = n_q * hd, n_kv * hd
    q = _dot(h, w_ref[:, :qw])
    for i in range(n_q):
        qh = q[:, i * hd:(i + 1) * hd]
        qh = qh * _rms(qh) * qg_ref[...]
        q_ref[:, i * hd:(i + 1) * hd] = (_rope(qh, cos, su, sd, hd // 4) * scale).astype(BF16)
    k = _dot(h, w_ref[:, qw:qw + kw])
    for i in range(n_kv):
        kh = k[:, i * hd:(i + 1) * hd]
        kh = kh * _rms(kh) * kg_ref[...]
        k_ref[:, i * hd:(i + 1) * hd] = _rope(kh, cos, su, sd, hd // 4).astype(BF16)
    v = _dot(h, w_ref[:, qw + kw:qw + 2 * kw])
    for i in range(n_kv):
        vt_ref[i * hd:(i + 1) * hd, :] = v[:, i * hd:(i + 1) * hd].T.astype(BF16)
    g_ref[...] = _silu(_dot(h, w_ref[:, qw + 2 * kw:])).astype(BF16)


def _na_proj_kernel(x_ref, mod_ref, ng_ref, w_ref, q_ref, k_ref, v_ref, g_ref, *, width, scale):
    h = _modulated_norm(x_ref, ng_ref, mod_ref)
    q_ref[...] = (_dot(h, w_ref[:, :width]) * scale).astype(BF16)
    k_ref[...] = _dot(h, w_ref[:, width:2 * width]).astype(BF16)
    v_ref[...] = _dot(h, w_ref[:, 2 * width:3 * width]).astype(BF16)
    g_ref[...] = _silu(_dot(h, w_ref[:, 3 * width:])).astype(BF16)


def _mla_proj_kernel(x_ref, mod_ref, ng_ref, w_ref, qg_ref, kvg_ref, wuq_ref, wkv_ref,
                     cos_ref, su_ref, sd_ref, q_ref, k_ref, vt_ref, g_ref, *, width, scale):
    h = _modulated_norm(x_ref, ng_ref, mod_ref)
    cos, su, sd = cos_ref[...], su_ref[...], sd_ref[...]
    o_kv = MLA_Q_LORA
    o_g = o_kv + MLA_KV_LORA
    o_kr = o_g + width
    c_q = _dot(h, w_ref[:, :o_kv])
    c_q = (c_q * _rms(c_q) * qg_ref[...]).astype(BF16)
    q = _dot(c_q, wuq_ref[...])
    for i in range(MLA_HEADS):
        a = i * MLA_QK_PAD
        q_ref[:, a:a + MLA_NOPE] = (q[:, a:a + MLA_NOPE] * scale).astype(BF16)
        q_ref[:, a + MLA_NOPE:a + MLA_QK_PAD] = (
            _rope(q[:, a + MLA_NOPE:a + MLA_QK_PAD], cos, su, sd, MLA_ROPE // 4) * scale).astype(BF16)
    c_kv = _dot(h, w_ref[:, o_kv:o_g])
    c_kv = (c_kv * _rms(c_kv) * kvg_ref[...]).astype(BF16)
    k_r = _rope(_dot(h, w_ref[:, o_kr:]), cos, su, sd, MLA_ROPE // 4).astype(BF16)
    kv_in = jnp.concatenate([c_kv, k_r], axis=1)
    kw = MLA_HEADS * MLA_QK_PAD
    k_ref[...] = _dot(kv_in, wkv_ref[:, :kw]).astype(BF16)
    v = _dot(kv_in, wkv_ref[:, kw:])
    for i in range(MLA_HEADS):
        vt_ref[i * MLA_V:(i + 1) * MLA_V, :] = v[:, i * MLA_V:(i + 1) * MLA_V].T.astype(BF16)
    g_ref[...] = _silu(_dot(h, w_ref[:, o_g:o_kr])).astype(BF16)


def _project(kern, xs, mod3, layer, norm_g, consts, tables, out_widths, n_lat, v_transposed):
    b, n, d = xs.shape
    tm = ROW_TILE
    n_lat_tiles = n_lat // tm
    ctx_row = b

    def mod_idx(bi, t):
        return (layer * MOD_ROWS + jnp.where(t < n_lat_tiles, bi, ctx_row), 0, 0)

    in_specs = [
        pl.BlockSpec((None, tm, d), lambda bi, t: (bi, t, 0)),
        pl.BlockSpec((1, 3, d), mod_idx),
        pl.BlockSpec((1, d), lambda bi, t: (0, 0)),
    ]
    in_specs += [pl.BlockSpec(a.shape, lambda bi, t: (0, 0)) for a in consts]
    in_specs += [pl.BlockSpec((tm, LANES), lambda bi, t: (t, 0)) for _ in tables]
    out_specs = [pl.BlockSpec((None, tm, w), lambda bi, t: (bi, t, 0)) for w in out_widths]
    out_shape = [jax.ShapeDtypeStruct((b, n, w), BF16) for w in out_widths]
    if v_transposed:
        out_specs[2] = pl.BlockSpec((None, None, out_widths[2], tm), lambda bi, t: (bi, t, 0, 0))
        out_shape[2] = jax.ShapeDtypeStruct((b, n // tm, out_widths[2], tm), BF16)
    return pl.pallas_call(
        kern,
        grid=(b, n // tm),
        in_specs=in_specs,
        out_specs=out_specs,
        out_shape=out_shape,
        compiler_params=_params("parallel", "parallel"),
        name="proj_" + str(layer),
    )(xs, mod3, norm_g.reshape(1, d), *consts, *tables)


def _stack_heads(q_ref, group, dk):
    return jnp.concatenate([q_ref[:, g * dk:(g + 1) * dk] for g in range(group)], axis=0)


def _unstack_heads(o_t, o_ref, group):
    tq = o_ref.shape[0]
    for g in range(group):
        o_ref[:, g * LANES:(g + 1) * LANES] = o_t[:, g * tq:(g + 1) * tq].T.astype(BF16)


def _pv_t(vt_ref, first_slab, p):
    slab = vt_ref.shape[-1]
    acc = None
    for i in range(p.shape[0] // slab):
        t = _dot(vt_ref[first_slab + i], p[i * slab:(i + 1) * slab, :])
        acc = t if acc is None else acc + t
    return acc


def _flash_kernel(q_ref, k_ref, vt_ref, o_ref, sa_ref, sb_ref, *, group, dk, n_lat, n_ctx, bk):
    slab = vt_ref.shape[-1]
    n_chunks = n_lat // bk
    q = _stack_heads(q_ref, group, dk)

    def scores(c, s_ref):
        s = _dot_nt(k_ref[pl.ds(pl.multiple_of(c * bk, bk), bk), :], q)
        s_ref[...] = s
        return jnp.max(s, axis=0, keepdims=True)

    def update(s_ref, s_max, c, m, l, acc):
        m_new = jnp.maximum(m, s_max)
        alpha = jnp.exp2(m - m_new)
        p = jnp.exp2(s_ref[...] - m_new)
        l = alpha * l + jnp.sum(p, axis=0, keepdims=True)
        acc = alpha * acc + _pv_t(vt_ref, c * (bk // slab), p.astype(BF16))
        return m_new, l, acc

    max_a = scores(0, sa_ref)
    s = _dot_nt(k_ref[n_lat:n_lat + n_ctx, :], q)
    m = jnp.max(s, axis=0, keepdims=True)
    p = jnp.exp2(s - m)
    l = jnp.sum(p, axis=0, keepdims=True)
    acc = _pv_t(vt_ref, n_lat // slab, p.astype(BF16))

    def body(i, carry):
        m, l, acc, max_a = carry
        a = 2 * i
        max_b = scores(a + 1, sb_ref)
        m, l, acc = update(sa_ref, max_a, a, m, l, acc)
        max_a = scores(jnp.minimum(a + 2, n_chunks - 2), sa_ref)
        m, l, acc = update(sb_ref, max_b, a + 1, m, l, acc)
        return m, l, acc, max_a

    m, l, acc, _ = lax.fori_loop(0, n_chunks // 2, body, (m, l, acc, max_a))
    _unstack_heads(acc * (1.0 / l), o_ref, group)


def _flash_attention(q, k, vt, *, group, dk, n_lat, layer):
    b, n, _ = q.shape
    n_kv = k.shape[2] // dk
    n_ctx = n - n_lat
    n_slab, slab = vt.shape[1], vt.shape[3]
    tq = min(Q_COLS // group, n_lat)
    bk = min(KV_CHUNK, n_lat // 2)
    assert n_lat % (2 * bk) == 0
    kern = functools.partial(_flash_kernel, group=group, dk=dk, n_lat=n_lat, n_ctx=n_ctx, bk=bk)
    s_buf = pltpu.VMEM((bk, group * tq), F32)
    return pl.pallas_call(
        kern,
        grid=(b, n_kv, n_lat // tq),
        in_specs=[
            pl.BlockSpec((None, tq, group * dk), lambda bi, h, t: (bi, t, h)),
            pl.BlockSpec((None, n, dk), lambda bi, h, t: (bi, 0, h)),
            pl.BlockSpec((None, n_slab, LANES, slab), lambda bi, h, t: (bi, 0, h, 0)),
        ],
        out_specs=pl.BlockSpec((None, tq, group * LANES), lambda bi, h, t: (bi, t, h)),
        out_shape=jax.ShapeDtypeStruct((b, n_lat, n_kv * group * LANES), BF16),
        scratch_shapes=[s_buf, s_buf],
        compiler_params=_params("parallel", "parallel", "parallel"),
        name="flash_" + str(layer),
    )(q, k, vt)


def _ctx_attn_kernel(q_ref, k_ref, vt_ref, o_ref, *, group, dk):
    q = _stack_heads(q_ref, group, dk)
    s = _dot_nt(k_ref[...], q)
    p = jnp.exp2(s - jnp.max(s, axis=0, keepdims=True))
    l = jnp.sum(p, axis=0, keepdims=True)
    _unstack_heads(_dot(vt_ref[...], p.astype(BF16)) * (1.0 / l), o_ref, group)


def _ctx_attention(q, k, vt, *, group, dk, n_lat, layer):
    b, n, _ = q.shape
    n_ctx = n - n_lat
    n_kv = k.shape[2] // dk
    slab = vt.shape[3]
    assert slab == n_ctx
    t = n_lat // n_ctx
    return pl.pallas_call(
        functools.partial(_ctx_attn_kernel, group=group, dk=dk),
        grid=(b, n_kv),
        in_specs=[
            pl.BlockSpec((None, n_ctx, group * dk), lambda bi, h: (bi, t, h)),
            pl.BlockSpec((None, n_ctx, dk), lambda bi, h: (bi, t, h)),
            pl.BlockSpec((None, None, LANES, slab), lambda bi, h: (bi, t, h, 0)),
        ],
        out_specs=pl.BlockSpec((None, n_ctx, group * LANES), lambda bi, h: (bi, 0, h)),
        out_shape=jax.ShapeDtypeStruct((b, n_ctx, n_kv * group * LANES), BF16),
        compiler_params=_params("parallel", "parallel"),
        name="ctx_attn_" + str(layer),
    )(q, k, vt)


def _head_lanes(j):
    lane = lax.broadcasted_iota(jnp.int32, (1, LANES), 1)
    return (lane >= j * NA_HEAD_DIM) & (lane < (j + 1) * NA_HEAD_DIM)


def _na_ctx_kernel(q_ref, k_ref, v_ref, o_ref):
    q, k, v = q_ref[...], k_ref[...], v_ref[...]
    out = jnp.zeros(q.shape, F32)
    for j in range(LANES // NA_HEAD_DIM):
        mine = _head_lanes(j)
        s = _dot_nt(jnp.where(mine, q, jnp.zeros_like(q)), k)
        p = jnp.exp2(s - jnp.max(s, axis=-1, keepdims=True))
        l = jnp.sum(p, axis=-1, keepdims=True)
        out = jnp.where(mine, _dot(p.astype(BF16), v) / l, out)
    o_ref[...] = out.astype(BF16)


def _na_ctx_attention(q, k, v, *, n_lat, layer):
    b, n, width = q.shape
    n_ctx = n - n_lat
    t = n_lat // n_ctx
    blk = pl.BlockSpec((None, n_ctx, LANES), lambda bi, h: (bi, t, h))
    return pl.pallas_call(
        _na_ctx_kernel,
        grid=(b, width // LANES),
        in_specs=[blk, blk, blk],
        out_specs=pl.BlockSpec((None, n_ctx, LANES), lambda bi, h: (bi, 0, h)),
        out_shape=jax.ShapeDtypeStruct((b, n_ctx, width), BF16),
        compiler_params=_params("parallel", "parallel"),
        name="ctx_attn_" + str(layer),
    )(q, k, v)


def _na_kernel(q_ref, k_ref, v_ref, bias_ref, o_ref, *, n_lat, n_ctx, rows):
    rb = pl.program_id(2)
    ks = jnp.clip(rb * NA_Q_ROWS - NA_KH // 2, 0, rows - NA_K_ROWS)
    off = pl.multiple_of(ks * GRID_W, GRID_W)
    nk = NA_K_ROWS * GRID_W
    q = q_ref[...]
    kw = k_ref[pl.ds(off, nk), :]
    vw = v_ref[pl.ds(off, nk), :]
    kc = k_ref[n_lat:n_lat + n_ctx, :]
    vc = v_ref[n_lat:n_lat + n_ctx, :]
    out = jnp.zeros(q.shape, F32)
    for j in range(LANES // NA_HEAD_DIM):
        mine = _head_lanes(j)
        qj = jnp.where(mine, q, jnp.zeros_like(q))
        s_loc = _dot_nt(qj, kw) + bias_ref[0, j]
        s_ctx = _dot_nt(qj, kc)
        m = jnp.maximum(jnp.max(s_loc, axis=-1, keepdims=True), jnp.max(s_ctx, axis=-1, keepdims=True))
        p_loc = jnp.exp2(s_loc - m)
        p_ctx = jnp.exp2(s_ctx - m)
        l = jnp.sum(p_loc, axis=-1, keepdims=True) + jnp.sum(p_ctx, axis=-1, keepdims=True)
        acc = _dot(p_loc.astype(BF16), vw) + _dot(p_ctx.astype(BF16), vc)
        out = jnp.where(mine, acc / l, out)
    o_ref[...] = out.astype(BF16)


def _na_bias(rpb, rows):
    h, n_dr, n_dc = rpb.shape
    n_rb = rows // NA_Q_ROWS
    lead = GRID_W - NA_KW
    wide = 2 * GRID_W
    w = jnp.pad(rpb * LOG2E, ((0, 0), (0, 0), (lead, wide - lead - n_dc)), constant_values=MASK_VALUE)
    flat = jnp.broadcast_to(w[:, :, None, :], (h, n_dr, GRID_W, wide)).reshape(h, n_dr, GRID_W * wide)
    skew = flat[:, :, GRID_W - 1:GRID_W - 1 + GRID_W * (wide - 1)].reshape(h, n_dr, GRID_W, wide - 1)
    qc = jnp.arange(GRID_W)
    cs = jnp.clip(qc - NA_KW // 2, 0, GRID_W - NA_KW)
    col_ok = (qc[None, :] >= cs[:, None]) & (qc[None, :] < cs[:, None] + NA_KW)
    c = jnp.where(col_ok, skew[..., :GRID_W], MASK_VALUE)
    masked = jnp.full((h, GRID_W, GRID_W), MASK_VALUE, F32)
    out = []
    for rb in (0, min(1, n_rb - 1), n_rb - 1):
        r0 = rb * NA_Q_ROWS
        ks = min(max(r0 - NA_KH // 2, 0), rows - NA_K_ROWS)
        strips = []
        for i in range(NA_Q_ROWS):
            r = r0 + i
            rs = min(max(r - NA_KH // 2, 0), rows - NA_KH)
            blocks = [c[:, ks + j - r + NA_KH - 1] if rs <= ks + j < rs + NA_KH else masked
                      for j in range(NA_K_ROWS)]
            strips.append(jnp.concatenate(blocks, axis=-1))
        out.append(jnp.concatenate(strips, axis=1))
    return jnp.stack(out)


def _na_attention(q, k, v, bias, *, n_lat, layer):
    b, n, width = q.shape
    n_ctx = n - n_lat
    rows = n_lat // GRID_W
    n_rb = rows // NA_Q_ROWS
    tq = NA_Q_ROWS * GRID_W
    sub = LANES // NA_HEAD_DIM
    kern = functools.partial(_na_kernel, n_lat=n_lat, n_ctx=n_ctx, rows=rows)

    def bias_idx(hp, bi, rb):
        return (jnp.where(rb == 0, 0, jnp.where(rb == n_rb - 1, 2, 1)), hp, 0, 0)

    return pl.pallas_call(
        kern,
        grid=(width // LANES, b, n_rb),
        in_specs=[
            pl.BlockSpec((None, tq, LANES), lambda hp, bi, rb: (bi, rb, hp)),
            pl.BlockSpec((None, n, LANES), lambda hp, bi, rb: (bi, 0, hp)),
            pl.BlockSpec((None, n, LANES), lambda hp, bi, rb: (bi, 0, hp)),
            pl.BlockSpec((1, sub, tq, NA_K_ROWS * GRID_W), bias_idx),
        ],
        out_specs=pl.BlockSpec((None, tq, LANES), lambda hp, bi, rb: (bi, rb, hp)),
        out_shape=jax.ShapeDtypeStruct((b, n_lat, width), BF16),
        compiler_params=_params("parallel", "parallel", "arbitrary"),
        name="na_attn_" + str(layer),
    )(q, k, v, bias)


def _out_proj_kernel(o_ref, oc_ref, g_ref, w_ref, x_ref, mod_ref, xo_ref, *, n_lat_tiles):
    o = jnp.where(pl.program_id(1) >= n_lat_tiles, oc_ref[...], o_ref[...])
    a = (o.astype(F32) * g_ref[...].astype(F32)).astype(BF16)
    xo_ref[...] = x_ref[...] + mod_ref[0, 2:3, :] * _dot(a, w_ref[...])


def _out_proj_final_kernel(o_ref, g_ref, w_ref, x_ref, mod_ref, fg_ref, y_ref):
    a = (o_ref[...].astype(F32) * g_ref[...].astype(F32)).astype(BF16)
    xn = x_ref[...] + mod_ref[0, 2:3, :] * _dot(a, w_ref[...])
    y_ref[...] = xn * _rms(xn) * fg_ref[...]


def _out_proj(o, o_ctx, gs, w_out, xs, mod3, layer, n_lat, final_g=None):
    b, n, d = xs.shape
    tm = ROW_TILE
    n_lat_tiles = n_lat // tm
    ctx_row = b

    def mod_idx(bi, t):
        return (layer * MOD_ROWS + jnp.where(t < n_lat_tiles, bi, ctx_row), 0, 0)

    row = lambda bi, t: (bi, t, 0)
    w = o.shape[2]
    o_spec = pl.BlockSpec((None, tm, w), lambda bi, t: (bi, jnp.minimum(t, n_lat_tiles - 1), 0))
    tail = [
        pl.BlockSpec((None, tm, w), row),
        pl.BlockSpec(w_out.shape, lambda bi, t: (0, 0)),
        pl.BlockSpec((None, tm, d), row),
        pl.BlockSpec((1, 3, d), mod_idx),
    ]
    if final_g is None:
        return pl.pallas_call(
            functools.partial(_out_proj_kernel, n_lat_tiles=n_lat_tiles),
            grid=(b, n // tm),
            in_specs=[o_spec, pl.BlockSpec((None, tm, w), lambda bi, t: (bi, 0, 0))] + tail,
            out_specs=pl.BlockSpec((None, tm, d), row),
            out_shape=jax.ShapeDtypeStruct(xs.shape, F32),
            input_output_aliases={4: 0},
            compiler_params=_params("parallel", "parallel"),
            name="out_proj_" + str(layer),
        )(o, o_ctx, gs, w_out, xs, mod3)
    return pl.pallas_call(
        _out_proj_final_kernel,
        grid=(b, n_lat_tiles),
        in_specs=[o_spec] + tail + [pl.BlockSpec((1, d), lambda bi, t: (0, 0))],
        out_specs=pl.BlockSpec((None, tm, d), row),
        out_shape=jax.ShapeDtypeStruct((b, n_lat, d), F32),
        compiler_params=_params("parallel", "parallel"),
        name="out_proj_final",
    )(o, gs, w_out, xs, mod3, final_g.reshape(1, d))


def _rope_tables(n_lat, n_ctx, rot_dim):
    n_freq = rot_dim // 4
    inv = ROPE_THETA ** (-jnp.arange(n_freq, dtype=F32) / n_freq)
    t = jnp.arange(n_lat)
    ang_r = (t // GRID_W).astype(F32)[:, None] * inv
    ang_c = (t % GRID_W).astype(F32)[:, None] * inv
    ang = jnp.concatenate([ang_r, ang_r, ang_c, ang_c], axis=-1)
    cos, sin = jnp.cos(ang), jnp.sin(ang)
    first = (jnp.arange(rot_dim) % (2 * n_freq)) < n_freq
    sin_up = jnp.where(first, -sin, 0.0)
    sin_dn = jnp.where(first, 0.0, sin)
    pad = LANES - rot_dim

    def finish(tab, fill):
        tab = jnp.pad(tab, ((0, 0), (0, pad)), constant_values=fill)
        return jnp.pad(tab, ((0, n_ctx), (0, 0)), constant_values=fill)

    return finish(cos, 1.0), finish(sin_up, 0.0), finish(sin_dn, 0.0)


def _mla_weights(w_in, w_uq, w_ukv):
    o_kv = MLA_Q_LORA
    o_kr = o_kv + MLA_KV_LORA
    o_g = o_kr + MLA_ROPE
    k_r = jnp.pad(w_in[:, o_kr:o_g], ((0, 0), (0, LANES - MLA_ROPE)))
    w_perm = jnp.concatenate([w_in[:, :o_kr], w_in[:, o_g:], k_r], axis=1).astype(BF16)
    uq = w_uq.reshape(MLA_Q_LORA, MLA_HEADS, MLA_NOPE + MLA_ROPE)
    uq = jnp.pad(uq, ((0, 0), (0, 0), (0, MLA_QK_PAD - MLA_NOPE - MLA_ROPE)))
    uq = uq.reshape(MLA_Q_LORA, MLA_HEADS * MLA_QK_PAD).astype(BF16)
    ukv = w_ukv.reshape(MLA_KV_LORA, MLA_HEADS, MLA_NOPE + MLA_V)
    k_top = jnp.pad(ukv[:, :, :MLA_NOPE], ((0, 0), (0, 0), (0, MLA_QK_PAD - MLA_NOPE)))
    eye = jnp.pad(jnp.eye(MLA_ROPE, dtype=F32), ((0, LANES - MLA_ROPE), (MLA_NOPE, MLA_QK_PAD - MLA_NOPE - MLA_ROPE)))
    k_bot = jnp.broadcast_to(eye[:, None, :], (LANES, MLA_HEADS, MLA_QK_PAD))
    k_aug = jnp.concatenate([k_top, k_bot], axis=0).reshape(MLA_KV_LORA + LANES, MLA_HEADS * MLA_QK_PAD)
    v_aug = jnp.pad(ukv[:, :, MLA_NOPE:].reshape(MLA_KV_LORA, MLA_HEADS * MLA_V), ((0, LANES), (0, 0)))
    return w_perm, uq, jnp.concatenate([k_aug, v_aug], axis=1).astype(BF16)


def kernel(x, c, ctx, c_ctx, mod_w, mod_b, norm_g, final_g, ga_w_in, ga_q_g, ga_k_g, ga_w_out, na_w_in, na_rpb, na_w_out, mla_w_in, mla_q_g, mla_kv_g, mla_w_uq, mla_w_ukv, mla_w_out):
    b, n_lat, d = x.shape
    n_ctx = ctx.shape[1]
    depth = mod_w.shape[0]
    assert n_lat % ROW_TILE == 0 and n_ctx == ROW_TILE and n_lat % n_ctx == 0
    assert b < MOD_ROWS and (n_lat // GRID_W) % NA_Q_ROWS == 0 and n_lat // GRID_W >= NA_K_ROWS

    xs = jnp.concatenate([x, ctx], axis=1)
    cc = jnp.concatenate([c, c_ctx[None, :], jnp.zeros((MOD_ROWS - b - 1, d), F32)], axis=0)
    mod3 = _modulation(cc, mod_w, mod_b).reshape(depth * MOD_ROWS, 3, d)

    tab_a = _rope_tables(n_lat, n_ctx, GQA_HEAD_DIM)
    tab_m = _rope_tables(n_lat, n_ctx, MLA_ROPE)

    out = None
    for i in range(depth):
        kind, j = i % N_MIXERS, i // N_MIXERS
        need_ctx = i < depth - 1
        if kind == 0:
            n_q = d // GQA_HEAD_DIM
            n_kv = n_q // GQA_GROUP
            kern = functools.partial(_gqa_proj_kernel, n_q=n_q, n_kv=n_kv, scale=GQA_HEAD_DIM ** -0.5 * LOG2E)
            consts = [ga_w_in[j].astype(BF16), ga_q_g[j].reshape(1, -1), ga_k_g[j].reshape(1, -1)]
            kvw = n_kv * GQA_HEAD_DIM
            q, k, vt, gs = _project(kern, xs, mod3, i, norm_g[i], consts, tab_a, [d, kvw, kvw, d], n_lat, True)
            o = _flash_attention(q, k, vt, group=GQA_GROUP, dk=GQA_HEAD_DIM, n_lat=n_lat, layer=i)
            if need_ctx:
                oc = _ctx_attention(q, k, vt, group=GQA_GROUP, dk=GQA_HEAD_DIM, n_lat=n_lat, layer=i)
            w_out = ga_w_out[j]
        elif kind == 1:
            kern = functools.partial(_na_proj_kernel, width=d, scale=NA_HEAD_DIM ** -0.5 * LOG2E)
            q, k, v, gs = _project(kern, xs, mod3, i, norm_g[i], [na_w_in[j].astype(BF16)], (), [d, d, d, d], n_lat, False)
            o = _na_attention(q, k, v, _na_bias(na_rpb[j], n_lat // GRID_W), n_lat=n_lat, layer=i)
            if need_ctx:
                oc = _na_ctx_attention(q, k, v, n_lat=n_lat, layer=i)
            w_out = na_w_out[j]
        else:
            kern = functools.partial(_mla_proj_kernel, width=d, scale=(MLA_NOPE + MLA_ROPE) ** -0.5 * LOG2E)
            w_perm, uq, wkv = _mla_weights(mla_w_in[j], mla_w_uq[j], mla_w_ukv[j])
            consts = [w_perm, mla_q_g[j].reshape(1, -1), mla_kv_g[j].reshape(1, -1), uq, wkv]
            widths = [MLA_HEADS * MLA_QK_PAD, MLA_HEADS * MLA_QK_PAD, MLA_HEADS * MLA_V, d]
            q, k, vt, gs = _project(kern, xs, mod3, i, norm_g[i], consts, tab_m, widths, n_lat, True)
            o = _flash_attention(q, k, vt, group=1, dk=MLA_QK_PAD, n_lat=n_lat, layer=i)
            if need_ctx:
                oc = _ctx_attention(q, k, vt, group=1, dk=MLA_QK_PAD, n_lat=n_lat, layer=i)
            w_out = mla_w_out[j]
        if need_ctx:
            xs = _out_proj(o, oc, gs, w_out.astype(BF16), xs, mod3, i, n_lat)
        else:
            out = _out_proj(o, None, gs, w_out.astype(BF16), xs, mod3, i, n_lat, final_g=final_g)
    return out
```

```python
import functools

import jax
import jax.numpy as jnp
from jax import lax
from jax.experimental import pallas as pl
from jax.experimental.pallas import tpu as pltpu

F32 = jnp.float32
BF16 = jnp.bfloat16

NORM_EPS = 1e-6
ROPE_THETA = 10000.0
GRID_W = 64
N_MIXERS = 3
LANES = 128
MOD_ROWS = 8
MASK_VALUE = -1e30
LOG2E = 1.4426950408889634

GQA_HEAD_DIM = 128
GQA_GROUP = 4
NA_HEAD_DIM = 64
NA_KH = 8
NA_KW = 16
NA_Q_ROWS = 8
NA_K_ROWS = 16
MLA_HEADS = 8
MLA_Q_LORA = 512
MLA_KV_LORA = 256
MLA_NOPE = 128
MLA_ROPE = 64
MLA_V = 128
MLA_QK_PAD = 256

ROW_TILE = 256
Q_COLS = 1024
Q_TILE = 256
KV_CHUNK = 1024
VMEM_LIMIT = 48 * 1024 * 1024


def _params(*sem):
    return pltpu.CompilerParams(dimension_semantics=sem, vmem_limit_bytes=VMEM_LIMIT)


def _silu(v):
    return v * (1.0 / (1.0 + jnp.exp(-v)))


def _rms(v):
    return lax.rsqrt(jnp.mean(v * v, axis=-1, keepdims=True) + NORM_EPS)


def _rope(v, cos, sin_up, sin_dn, half):
    w = v.shape[-1]
    return v * cos + pltpu.roll(v, w - half, 1) * sin_up + pltpu.roll(v, half, 1) * sin_dn


def _dot(a, b):
    return jnp.dot(a, b, preferred_element_type=F32)


def _dot_nt(a, b):
    return lax.dot_general(a, b, (((1,), (1,)), ((), ())), preferred_element_type=F32)


def _mod_kernel(c_ref, w_ref, b_ref, o_ref):
    a = _silu(c_ref[...])
    w = w_ref[0]
    a_hi = a.astype(BF16)
    a_lo = (a - a_hi.astype(F32)).astype(BF16)
    w_hi = w.astype(BF16)
    w_lo = (w - w_hi.astype(F32)).astype(BF16)
    o_ref[0] = _dot(a_hi, w_hi) + _dot(a_lo, w_hi) + _dot(a_hi, w_lo) + b_ref[0]


def _modulation(cc, mod_w, mod_b):
    depth, d, n3 = mod_w.shape
    tn = 1024
    return pl.pallas_call(
        _mod_kernel,
        grid=(depth, n3 // tn),
        in_specs=[
            pl.BlockSpec((MOD_ROWS, d), lambda l, j: (0, 0)),
            pl.BlockSpec((1, d, tn), lambda l, j: (l, 0, j)),
            pl.BlockSpec((1, 1, tn), lambda l, j: (l, 0, j)),
        ],
        out_specs=pl.BlockSpec((1, MOD_ROWS, tn), lambda l, j: (l, 0, j)),
        out_shape=jax.ShapeDtypeStruct((depth, MOD_ROWS, n3), F32),
        compiler_params=_params("parallel", "parallel"),
        name="adaln_modulation",
    )(cc, mod_w, mod_b.reshape(depth, 1, n3))


def _modulated_norm(x_ref, ng_ref, mod_ref):
    x = x_ref[...]
    shift = mod_ref[0, 0:1, :]
    scale = mod_ref[0, 1:2, :]
    return ((x * _rms(x)) * ng_ref[...] * (1.0 + scale) + shift).astype(BF16)


def _gqa_proj_kernel(x_ref, mod_ref, ng_ref, w_ref, qg_ref, kg_ref, cos_ref, su_ref, sd_ref,
                     q_ref, k_ref, vt_ref, g_ref, *, n_q, n_kv, scale):
    h = _modulated_norm(x_ref, ng_ref, mod_ref)
    cos, su, sd = cos_ref[...], su_ref[...], sd_ref[...]
    hd = GQA_HEAD_DIM
    qw, kw = n_q * hd, n_kv * hd
    q = _dot(h, w_ref[:, :qw])
    for i in range(n_q):
        qh = q[:, i * hd:(i + 1) * hd]
        qh = qh * _rms(qh) * qg_ref[...]
        q_ref[:, i * hd:(i + 1) * hd] = (_rope(qh, cos, su, sd, hd // 4) * scale).astype(BF16)
    k = _dot(h, w_ref[:, qw:qw + kw])
    for i in range(n_kv):
        kh = k[:, i * hd:(i + 1) * hd]
        kh = kh * _rms(kh) * kg_ref[...]
        k_ref[:, i * hd:(i + 1) * hd] = _rope(kh, cos, su, sd, hd // 4).astype(BF16)
    v = _dot(h, w_ref[:, qw + kw:qw + 2 * kw])
    for i in range(n_kv):
        vt_ref[i * hd:(i + 1) * hd, :] = v[:, i * hd:(i + 1) * hd].T.astype(BF16)
    g_ref[...] = _silu(_dot(h, w_ref[:, qw + 2 * kw:])).astype(BF16)


def _na_proj_kernel(x_ref, mod_ref, ng_ref, w_ref, q_ref, k_ref, v_ref, g_ref, *, width, scale):
    h = _modulated_norm(x_ref, ng_ref, mod_ref)
    q_ref[...] = (_dot(h, w_ref[:, :width]) * scale).astype(BF16)
    k_ref[...] = _dot(h, w_ref[:, width:2 * width]).astype(BF16)
    v_ref[...] = _dot(h, w_ref[:, 2 * width:3 * width]).astype(BF16)
    g_ref[...] = _silu(_dot(h, w_ref[:, 3 * width:])).astype(BF16)


def _mla_proj_kernel(x_ref, mod_ref, ng_ref, w_ref, qg_ref, kvg_ref, wuq_ref, wkv_ref,
                     cos_ref, su_ref, sd_ref, q_ref, k_ref, vt_ref, g_ref, *, width, scale):
    h = _modulated_norm(x_ref, ng_ref, mod_ref)
    cos, su, sd = cos_ref[...], su_ref[...], sd_ref[...]
    o_kv = MLA_Q_LORA
    o_g = o_kv + MLA_KV_LORA
    o_kr = o_g + width
    c_q = _dot(h, w_ref[:, :o_kv])
    c_q = (c_q * _rms(c_q) * qg_ref[...]).astype(BF16)
    q = _dot(c_q, wuq_ref[...])
    for i in range(MLA_HEADS):
        a = i * MLA_QK_PAD
        q_ref[:, a:a + MLA_NOPE] = (q[:, a:a + MLA_NOPE] * scale).astype(BF16)
        q_ref[:, a + MLA_NOPE:a + MLA_QK_PAD] = (
            _rope(q[:, a + MLA_NOPE:a + MLA_QK_PAD], cos, su, sd, MLA_ROPE // 4) * scale).astype(BF16)
    c_kv = _dot(h, w_ref[:, o_kv:o_g])
    c_kv = (c_kv * _rms(c_kv) * kvg_ref[...]).astype(BF16)
    k_r = _rope(_dot(h, w_ref[:, o_kr:]), cos, su, sd, MLA_ROPE // 4).astype(BF16)
    kv_in = jnp.concatenate([c_kv, k_r], axis=1)
    kw = MLA_HEADS * MLA_QK_PAD
    k_ref[...] = _dot(kv_in, wkv_ref[:, :kw]).astype(BF16)
    v = _dot(kv_in, wkv_ref[:, kw:])
    for i in range(MLA_HEADS):
        vt_ref[i * MLA_V:(i + 1) * MLA_V, :] = v[:, i * MLA_V:(i + 1) * MLA_V].T.astype(BF16)
    g_ref[...] = _silu(_dot(h, w_ref[:, o_g:o_kr])).astype(BF16)


def _project(kern, xs, mod3, layer, norm_g, consts, tables, out_widths, n_lat, v_transposed):
    b, n, d = xs.shape
    tm = ROW_TILE
    n_lat_tiles = n_lat // tm
    ctx_row = b

    def mod_idx(bi, t):
        return (layer * MOD_ROWS + jnp.where(t < n_lat_tiles, bi, ctx_row), 0, 0)

    in_specs = [
        pl.BlockSpec((None, tm, d), lambda bi, t: (bi, t, 0)),
        pl.BlockSpec((1, 3, d), mod_idx),
        pl.BlockSpec((1, d), lambda bi, t: (0, 0)),
    ]
    in_specs += [pl.BlockSpec(a.shape, lambda bi, t: (0, 0)) for a in consts]
    in_specs += [pl.BlockSpec((tm, LANES), lambda bi, t: (t, 0)) for _ in tables]
    out_specs = [pl.BlockSpec((None, tm, w), lambda bi, t: (bi, t, 0)) for w in out_widths]
    out_shape = [jax.ShapeDtypeStruct((b, n, w), BF16) for w in out_widths]
    if v_transposed:
        out_specs[2] = pl.BlockSpec((None, None, out_widths[2], tm), lambda bi, t: (bi, t, 0, 0))
        out_shape[2] = jax.ShapeDtypeStruct((b, n // tm, out_widths[2], tm), BF16)
    return pl.pallas_call(
        kern,
        grid=(b, n // tm),
        in_specs=in_specs,
        out_specs=out_specs,
        out_shape=out_shape,
        compiler_params=_params("parallel", "parallel"),
        name="proj_" + str(layer),
    )(xs, mod3, norm_g.reshape(1, d), *consts, *tables)


def _stack_heads(q_ref, group, dk):
    return jnp.concatenate([q_ref[:, g * dk:(g + 1) * dk] for g in range(group)], axis=0)


def _unstack_heads(o_t, o_ref, group):
    tq = o_ref.shape[0]
    for g in range(group):
        o_ref[:, g * LANES:(g + 1) * LANES] = o_t[:, g * tq:(g + 1) * tq].T.astype(BF16)


def _pv_t(vt_ref, first_slab, p):
    slab = vt_ref.shape[-1]
    acc = None
    for i in range(p.shape[0] // slab):
        t = _dot(vt_ref[first_slab + i], p[i * slab:(i + 1) * slab, :])
        acc = t if acc is None else acc + t
    return acc


def _flash_kernel(q_ref, k_ref, vt_ref, o_ref, sa_ref, sb_ref, acc_ref, m_ref, l_ref,
                  *, group, dk, n_lat, n_ctx, bk):
    slab = vt_ref.shape[-1]
    n_chunks = n_lat // bk
    q = _stack_heads(q_ref, group, dk)
    tiles = [slice(j * Q_TILE, (j + 1) * Q_TILE) for j in range(q.shape[0] // Q_TILE)]

    def scores(c, s_ref, t):
        s = _dot_nt(k_ref[pl.ds(pl.multiple_of(c * bk, bk), bk), :], q[t, :])
        s_ref[:, t] = s
        return jnp.max(s, axis=0, keepdims=True)

    def update(s_ref, s_max, c, t):
        m = m_ref[:, t]
        m_new = jnp.maximum(m, s_max)
        alpha = jnp.exp2(m - m_new)
        p = jnp.exp2(s_ref[:, t] - m_new)
        m_ref[:, t] = m_new
        l_ref[:, t] = alpha * l_ref[:, t] + jnp.sum(p, axis=0, keepdims=True)
        acc_ref[:, t] = alpha * acc_ref[:, t] + _pv_t(vt_ref, c * (bk // slab), p.astype(BF16))

    max_a = []
    for t in tiles:
        max_a.append(scores(0, sa_ref, t))
        s = _dot_nt(k_ref[n_lat:n_lat + n_ctx, :], q[t, :])
        m = jnp.max(s, axis=0, keepdims=True)
        p = jnp.exp2(s - m)
        m_ref[:, t] = m
        l_ref[:, t] = jnp.sum(p, axis=0, keepdims=True)
        acc_ref[:, t] = _pv_t(vt_ref, n_lat // slab, p.astype(BF16))

    def body(i, max_a):
        a = 2 * i
        nxt = jnp.minimum(a + 2, n_chunks - 2)
        max_b, new_a = [], []
        for j, t in enumerate(tiles):
            max_b.append(scores(a + 1, sb_ref, t))
            update(sa_ref, max_a[j], a, t)
        for j, t in enumerate(tiles):
            new_a.append(scores(nxt, sa_ref, t))
            update(sb_ref, max_b[j], a + 1, t)
        return tuple(new_a)

    lax.fori_loop(0, n_chunks // 2, body, tuple(max_a))
    _unstack_heads(acc_ref[...] * (1.0 / l_ref[...]), o_ref, group)


def _flash_attention(q, k, vt, *, group, dk, n_lat, layer):
    b, n, _ = q.shape
    n_kv = k.shape[2] // dk
    n_ctx = n - n_lat
    n_slab, slab = vt.shape[1], vt.shape[3]
    tq = min(Q_COLS // group, n_lat)
    bk = min(KV_CHUNK, n_lat // 2)
    assert n_lat % (2 * bk) == 0
    kern = functools.partial(_flash_kernel, group=group, dk=dk, n_lat=n_lat, n_ctx=n_ctx, bk=bk)
    nq = group * tq
    s_buf = pltpu.VMEM((bk, nq), F32)
    stat = pltpu.VMEM((1, nq), F32)
    return pl.pallas_call(
        kern,
        grid=(b, n_kv, n_lat // tq),
        in_specs=[
            pl.BlockSpec((None, tq, group * dk), lambda bi, h, t: (bi, t, h)),
            pl.BlockSpec((None, n, dk), lambda bi, h, t: (bi, 0, h)),
            pl.BlockSpec((None, n_slab, LANES, slab), lambda bi, h, t: (bi, 0, h, 0)),
        ],
        out_specs=pl.BlockSpec((None, tq, group * LANES), lambda bi, h, t: (bi, t, h)),
        out_shape=jax.ShapeDtypeStruct((b, n_lat, n_kv * group * LANES), BF16),
        scratch_shapes=[s_buf, s_buf, pltpu.VMEM((LANES, nq), F32), stat, stat],
        compiler_params=_params("parallel", "parallel", "parallel"),
        name="flash_" + str(layer),
    )(q, k, vt)


def _ctx_attn_kernel(q_ref, k_ref, vt_ref, o_ref, *, group, dk):
    q = _stack_heads(q_ref, group, dk)
    s = _dot_nt(k_ref[...], q)
    p = jnp.exp2(s - jnp.max(s, axis=0, keepdims=True))
    l = jnp.sum(p, axis=0, keepdims=True)
    _unstack_heads(_dot(vt_ref[...], p.astype(BF16)) * (1.0 / l), o_ref, group)


def _ctx_attention(q, k, vt, *, group, dk, n_lat, layer):
    b, n, _ = q.shape
    n_ctx = n - n_lat
    n_kv = k.shape[2] // dk
    slab = vt.shape[3]
    assert slab == n_ctx
    t = n_lat // n_ctx
    return pl.pallas_call(
        functools.partial(_ctx_attn_kernel, group=group, dk=dk),
        grid=(b, n_kv),
        in_specs=[
            pl.BlockSpec((None, n_ctx, group * dk), lambda bi, h: (bi, t, h)),
            pl.BlockSpec((None, n_ctx, dk), lambda bi, h: (bi, t, h)),
            pl.BlockSpec((None, None, LANES, slab), lambda bi, h: (bi, t, h, 0)),
        ],
        out_specs=pl.BlockSpec((None, n_ctx, group * LANES), lambda bi, h: (bi, 0, h)),
        out_shape=jax.ShapeDtypeStruct((b, n_ctx, n_kv * group * LANES), BF16),
        compiler_params=_params("parallel", "parallel"),
        name="ctx_attn_" + str(layer),
    )(q, k, vt)


def _head_lanes(j):
    lane = lax.broadcasted_iota(jnp.int32, (1, LANES), 1)
    return (lane >= j * NA_HEAD_DIM) & (lane < (j + 1) * NA_HEAD_DIM)


def _na_ctx_kernel(q_ref, k_ref, v_ref, o_ref):
    q, k, v = q_ref[...], k_ref[...], v_ref[...]
    out = jnp.zeros(q.shape, F32)
    for j in range(LANES // NA_HEAD_DIM):
        mine = _head_lanes(j)
        s = _dot_nt(jnp.where(mine, q, jnp.zeros_like(q)), k)
        p = jnp.exp2(s - jnp.max(s, axis=-1, keepdims=True))
        l = jnp.sum(p, axis=-1, keepdims=True)
        out = jnp.where(mine, _dot(p.astype(BF16), v) / l, out)
    o_ref[...] = out.astype(BF16)


def _na_ctx_attention(q, k, v, *, n_lat, layer):
    b, n, width = q.shape
    n_ctx = n - n_lat
    t = n_lat // n_ctx
    blk = pl.BlockSpec((None, n_ctx, LANES), lambda bi, h: (bi, t, h))
    return pl.pallas_call(
        _na_ctx_kernel,
        grid=(b, width // LANES),
        in_specs=[blk, blk, blk],
        out_specs=pl.BlockSpec((None, n_ctx, LANES), lambda bi, h: (bi, 0, h)),
        out_shape=jax.ShapeDtypeStruct((b, n_ctx, width), BF16),
        compiler_params=_params("parallel", "parallel"),
        name="ctx_attn_" + str(layer),
    )(q, k, v)


def _na_kernel(q_ref, k_ref, v_ref, bias_ref, o_ref, *, n_lat, n_ctx, rows):
    rb = pl.program_id(2)
    ks = jnp.clip(rb * NA_Q_ROWS - NA_KH // 2, 0, rows - NA_K_ROWS)
    off = pl.multiple_of(ks * GRID_W, GRID_W)
    nk = NA_K_ROWS * GRID_W
    q = q_ref[...]
    kw = k_ref[pl.ds(off, nk), :]
    vw = v_ref[pl.ds(off, nk), :]
    kc = k_ref[n_lat:n_lat + n_ctx, :]
    vc = v_ref[n_lat:n_lat + n_ctx, :]
    out = jnp.zeros(q.shape, F32)
    for j in range(LANES // NA_HEAD_DIM):
        mine = _head_lanes(j)
        qj = jnp.where(mine, q, jnp.zeros_like(q))
        s_loc = _dot_nt(qj, kw) + bias_ref[0, j]
        s_ctx = _dot_nt(qj, kc)
        m = jnp.maximum(jnp.max(s_loc, axis=-1, keepdims=True), jnp.max(s_ctx, axis=-1, keepdims=True))
        p_loc = jnp.exp2(s_loc - m)
        p_ctx = jnp.exp2(s_ctx - m)
        l = jnp.sum(p_loc, axis=-1, keepdims=True) + jnp.sum(p_ctx, axis=-1, keepdims=True)
        acc = _dot(p_loc.astype(BF16), vw) + _dot(p_ctx.astype(BF16), vc)
        out = jnp.where(mine, acc / l, out)
    o_ref[...] = out.astype(BF16)


def _na_bias(rpb, rows):
    h, n_dr, n_dc = rpb.shape
    n_rb = rows // NA_Q_ROWS
    lead = GRID_W - NA_KW
    wide = 2 * GRID_W
    w = jnp.pad(rpb * LOG2E, ((0, 0), (0, 0), (lead, wide - lead - n_dc)), constant_values=MASK_VALUE)
    flat = jnp.broadcast_to(w[:, :, None, :], (h, n_dr, GRID_W, wide)).reshape(h, n_dr, GRID_W * wide)
    skew = flat[:, :, GRID_W - 1:GRID_W - 1 + GRID_W * (wide - 1)].reshape(h, n_dr, GRID_W, wide - 1)
    qc = jnp.arange(GRID_W)
    cs = jnp.clip(qc - NA_KW // 2, 0, GRID_W - NA_KW)
    col_ok = (qc[None, :] >= cs[:, None]) & (qc[None, :] < cs[:, None] + NA_KW)
    c = jnp.where(col_ok, skew[..., :GRID_W], MASK_VALUE)
    masked = jnp.full((h, GRID_W, GRID_W), MASK_VALUE, F32)
    out = []
    for rb in (0, min(1, n_rb - 1), n_rb - 1):
        r0 = rb * NA_Q_ROWS
        ks = min(max(r0 - NA_KH // 2, 0), rows - NA_K_ROWS)
        strips = []
        for i in range(NA_Q_ROWS):
            r = r0 + i
            rs = min(max(r - NA_KH // 2, 0), rows - NA_KH)
            blocks = [c[:, ks + j - r + NA_KH - 1] if rs <= ks + j < rs + NA_KH else masked
                      for j in range(NA_K_ROWS)]
            strips.append(jnp.concatenate(blocks, axis=-1))
        out.append(jnp.concatenate(strips, axis=1))
    return jnp.stack(out)


def _na_attention(q, k, v, bias, *, n_lat, layer):
    b, n, width = q.shape
    n_ctx = n - n_lat
    rows = n_lat // GRID_W
    n_rb = rows // NA_Q_ROWS
    tq = NA_Q_ROWS * GRID_W
    sub = LANES // NA_HEAD_DIM
    kern = functools.partial(_na_kernel, n_lat=n_lat, n_ctx=n_ctx, rows=rows)

    def bias_idx(hp, bi, rb):
        return (jnp.where(rb == 0, 0, jnp.where(rb == n_rb - 1, 2, 1)), hp, 0, 0)

    return pl.pallas_call(
        kern,
        grid=(width // LANES, b, n_rb),
        in_specs=[
            pl.BlockSpec((None, tq, LANES), lambda hp, bi, rb: (bi, rb, hp)),
            pl.BlockSpec((None, n, LANES), lambda hp, bi, rb: (bi, 0, hp)),
            pl.BlockSpec((None, n, LANES), lambda hp, bi, rb: (bi, 0, hp)),
            pl.BlockSpec((1, sub, tq, NA_K_ROWS * GRID_W), bias_idx),
        ],
        out_specs=pl.BlockSpec((None, tq, LANES), lambda hp, bi, rb: (bi, rb, hp)),
        out_shape=jax.ShapeDtypeStruct((b, n_lat, width), BF16),
        compiler_params=_params("parallel", "parallel", "arbitrary"),
        name="na_attn_" + str(layer),
    )(q, k, v, bias)


def _out_proj_kernel(o_ref, oc_ref, g_ref, w_ref, x_ref, mod_ref, xo_ref, *, n_lat_tiles):
    o = jnp.where(pl.program_id(1) >= n_lat_tiles, oc_ref[...], o_ref[...])
    a = (o.astype(F32) * g_ref[...].astype(F32)).astype(BF16)
    xo_ref[...] = x_ref[...] + mod_ref[0, 2:3, :] * _dot(a, w_ref[...])


def _out_proj_final_kernel(o_ref, g_ref, w_ref, x_ref, mod_ref, fg_ref, y_ref):
    a = (o_ref[...].astype(F32) * g_ref[...].astype(F32)).astype(BF16)
    xn = x_ref[...] + mod_ref[0, 2:3, :] * _dot(a, w_ref[...])
    y_ref[...] = xn * _rms(xn) * fg_ref[...]


def _out_proj(o, o_ctx, gs, w_out, xs, mod3, layer, n_lat, final_g=None):
    b, n, d = xs.shape
    tm = ROW_TILE
    n_lat_tiles = n_lat // tm
    ctx_row = b

    def mod_idx(bi, t):
        return (layer * MOD_ROWS + jnp.where(t < n_lat_tiles, bi, ctx_row), 0, 0)

    row = lambda bi, t: (bi, t, 0)
    w = o.shape[2]
    o_spec = pl.BlockSpec((None, tm, w), lambda bi, t: (bi, jnp.minimum(t, n_lat_tiles - 1), 0))
    tail = [
        pl.BlockSpec((None, tm, w), row),
        pl.BlockSpec(w_out.shape, lambda bi, t: (0, 0)),
        pl.BlockSpec((None, tm, d), row),
        pl.BlockSpec((1, 3, d), mod_idx),
    ]
    if final_g is None:
        return pl.pallas_call(
            functools.partial(_out_proj_kernel, n_lat_tiles=n_lat_tiles),
            grid=(b, n // tm),
            in_specs=[o_spec, pl.BlockSpec((None, tm, w), lambda bi, t: (bi, 0, 0))] + tail,
            out_specs=pl.BlockSpec((None, tm, d), row),
            out_shape=jax.ShapeDtypeStruct(xs.shape, F32),
            input_output_aliases={4: 0},
            compiler_params=_params("parallel", "parallel"),
            name="out_proj_" + str(layer),
        )(o, o_ctx, gs, w_out, xs, mod3)
    return pl.pallas_call(
        _out_proj_final_kernel,
        grid=(b, n_lat_tiles),
        in_specs=[o_spec] + tail + [pl.BlockSpec((1, d), lambda bi, t: (0, 0))],
        out_specs=pl.BlockSpec((None, tm, d), row),
        out_shape=jax.ShapeDtypeStruct((b, n_lat, d), F32),
        compiler_params=_params("parallel", "parallel"),
        name="out_proj_final",
    )(o, gs, w_out, xs, mod3, final_g.reshape(1, d))


def _rope_tables(n_lat, n_ctx, rot_dim):
    n_freq = rot_dim // 4
    inv = ROPE_THETA ** (-jnp.arange(n_freq, dtype=F32) / n_freq)
    t = jnp.arange(n_lat)
    ang_r = (t // GRID_W).astype(F32)[:, None] * inv
    ang_c = (t % GRID_W).astype(F32)[:, None] * inv
    ang = jnp.concatenate([ang_r, ang_r, ang_c, ang_c], axis=-1)
    cos, sin = jnp.cos(ang), jnp.sin(ang)
    first = (jnp.arange(rot_dim) % (2 * n_freq)) < n_freq
    sin_up = jnp.where(first, -sin, 0.0)
    sin_dn = jnp.where(first, 0.0, sin)
    pad = LANES - rot_dim

    def finish(tab, fill):
        tab = jnp.pad(tab, ((0, 0), (0, pad)), constant_values=fill)
        return jnp.pad(tab, ((0, n_ctx), (0, 0)), constant_values=fill)

    return finish(cos, 1.0), finish(sin_up, 0.0), finish(sin_dn, 0.0)


def _mla_weights(w_in, w_uq, w_ukv):
    o_kv = MLA_Q_LORA
    o_kr = o_kv + MLA_KV_LORA
    o_g = o_kr + MLA_ROPE
    k_r = jnp.pad(w_in[:, o_kr:o_g], ((0, 0), (0, LANES - MLA_ROPE)))
    w_perm = jnp.concatenate([w_in[:, :o_kr], w_in[:, o_g:], k_r], axis=1).astype(BF16)
    uq = w_uq.reshape(MLA_Q_LORA, MLA_HEADS, MLA_NOPE + MLA_ROPE)
    uq = jnp.pad(uq, ((0, 0), (0, 0), (0, MLA_QK_PAD - MLA_NOPE - MLA_ROPE)))
    uq = uq.reshape(MLA_Q_LORA, MLA_HEADS * MLA_QK_PAD).astype(BF16)
    ukv = w_ukv.reshape(MLA_KV_LORA, MLA_HEADS, MLA_NOPE + MLA_V)
    k_top = jnp.pad(ukv[:, :, :MLA_NOPE], ((0, 0), (0, 0), (0, MLA_QK_PAD - MLA_NOPE)))
    eye = jnp.pad(jnp.eye(MLA_ROPE, dtype=F32), ((0, LANES - MLA_ROPE), (MLA_NOPE, MLA_QK_PAD - MLA_NOPE - MLA_ROPE)))
    k_bot = jnp.broadcast_to(eye[:, None, :], (LANES, MLA_HEADS, MLA_QK_PAD))
    k_aug = jnp.concatenate([k_top, k_bot], axis=0).reshape(MLA_KV_LORA + LANES, MLA_HEADS * MLA_QK_PAD)
    v_aug = jnp.pad(ukv[:, :, MLA_NOPE:].reshape(MLA_KV_LORA, MLA_HEADS * MLA_V), ((0, LANES), (0, 0)))
    return w_perm, uq, jnp.concatenate([k_aug, v_aug], axis=1).astype(BF16)


def kernel(x, c, ctx, c_ctx, mod_w, mod_b, norm_g, final_g, ga_w_in, ga_q_g, ga_k_g, ga_w_out, na_w_in, na_rpb, na_w_out, mla_w_in, mla_q_g, mla_kv_g, mla_w_uq, mla_w_ukv, mla_w_out):
    b, n_lat, d = x.shape
    n_ctx = ctx.shape[1]
    depth = mod_w.shape[0]
    assert n_lat % ROW_TILE == 0 and n_ctx == ROW_TILE and n_lat % n_ctx == 0
    assert b < MOD_ROWS and (n_lat // GRID_W) % NA_Q_ROWS == 0 and n_lat // GRID_W >= NA_K_ROWS

    xs = jnp.concatenate([x, ctx], axis=1)
    cc = jnp.concatenate([c, c_ctx[None, :], jnp.zeros((MOD_ROWS - b - 1, d), F32)], axis=0)
    mod3 = _modulation(cc, mod_w, mod_b).reshape(depth * MOD_ROWS, 3, d)

    tab_a = _rope_tables(n_lat, n_ctx, GQA_HEAD_DIM)
    tab_m = _rope_tables(n_lat, n_ctx, MLA_ROPE)

    out = None
    for i in range(depth):
        kind, j = i % N_MIXERS, i // N_MIXERS
        need_ctx = i < depth - 1
        if kind == 0:
            n_q = d // GQA_HEAD_DIM
            n_kv = n_q // GQA_GROUP
            kern = functools.partial(_gqa_proj_kernel, n_q=n_q, n_kv=n_kv, scale=GQA_HEAD_DIM ** -0.5 * LOG2E)
            consts = [ga_w_in[j].astype(BF16), ga_q_g[j].reshape(1, -1), ga_k_g[j].reshape(1, -1)]
            kvw = n_kv * GQA_HEAD_DIM
            q, k, vt, gs = _project(kern, xs, mod3, i, norm_g[i], consts, tab_a, [d, kvw, kvw, d], n_lat, True)
            o = _flash_attention(q, k, vt, group=GQA_GROUP, dk=GQA_HEAD_DIM, n_lat=n_lat, layer=i)
            if need_ctx:
                oc = _ctx_attention(q, k, vt, group=GQA_GROUP, dk=GQA_HEAD_DIM, n_lat=n_lat, layer=i)
            w_out = ga_w_out[j]
        elif kind == 1:
            kern = functools.partial(_na_proj_kernel, width=d, scale=NA_HEAD_DIM ** -0.5 * LOG2E)
            q, k, v, gs = _project(kern, xs, mod3, i, norm_g[i], [na_w_in[j].astype(BF16)], (), [d, d, d, d], n_lat, False)
            o = _na_attention(q, k, v, _na_bias(na_rpb[j], n_lat // GRID_W), n_lat=n_lat, layer=i)
            if need_ctx:
                oc = _na_ctx_attention(q, k, v, n_lat=n_lat, layer=i)
            w_out = na_w_out[j]
        else:
            kern = functools.partial(_mla_proj_kernel, width=d, scale=(MLA_NOPE + MLA_ROPE) ** -0.5 * LOG2E)
            w_perm, uq, wkv = _mla_weights(mla_w_in[j], mla_w_uq[j], mla_w_ukv[j])
            consts = [w_perm, mla_q_g[j].reshape(1, -1), mla_kv_g[j].reshape(1, -1), uq, wkv]
            widths = [MLA_HEADS * MLA_QK_PAD, MLA_HEADS * MLA_QK_PAD, MLA_HEADS * MLA_V, d]
            q, k, vt, gs = _project(kern, xs, mod3, i, norm_g[i], consts, tab_m, widths, n_lat, True)
            o = _flash_attention(q, k, vt, group=1, dk=MLA_QK_PAD, n_lat=n_lat, layer=i)
            if need_ctx:
                oc = _ctx_attention(q, k, vt, group=1, dk=MLA_QK_PAD, n_lat=n_lat, layer=i)
            w_out = mla_w_out[j]
        if need_ctx:
            xs = _out_proj(o, oc, gs, w_out.astype(BF16), xs, mod3, i, n_lat)
        else:
            out = _out_proj(o, None, gs, w_out.astype(BF16), xs, mod3, i, n_lat, final_g=final_g)
    return out
```

```python
import functools

import jax
import numpy as np
import jax.numpy as jnp
from jax import lax
from jax.experimental import pallas as pl
from jax.experimental.pallas import tpu as pltpu

F32 = jnp.float32
BF16 = jnp.bfloat16

NORM_EPS = 1e-6
ROPE_THETA = 10000.0
GRID_W = 64
N_MIXERS = 3
LANES = 128
MOD_ROWS = 8
MASK_VALUE = -1e30
LOG2E = 1.4426950408889634

GQA_HEAD_DIM = 128
GQA_GROUP = 4
NA_HEAD_DIM = 64
NA_KH = 8
NA_KW = 16
NA_BLOCK_ROWS = 4
NA_WIN_ROWS = 12
NA_STEP_BLOCKS = 4
MLA_HEADS = 8
MLA_Q_LORA = 512
MLA_KV_LORA = 256
MLA_NOPE = 128
MLA_ROPE = 64
MLA_V = 128
MLA_QK_PAD = 256

ROW_TILE = 256
Q_COLS = 1024
Q_TILE = 256
KV_CHUNKS = (768, 512, 256)
VMEM_LIMIT = 48 * 1024 * 1024


def _params(*sem):
    return pltpu.CompilerParams(dimension_semantics=sem, vmem_limit_bytes=VMEM_LIMIT)


def _silu(v):
    return v * (1.0 / (1.0 + jnp.exp(-v)))


def _rms(v):
    return lax.rsqrt(jnp.mean(v * v, axis=-1, keepdims=True) + NORM_EPS)


def _rope(v, cos, sin_up, sin_dn, half):
    w = v.shape[-1]
    return v * cos + pltpu.roll(v, w - half, 1) * sin_up + pltpu.roll(v, half, 1) * sin_dn


def _dot(a, b):
    return jnp.dot(a, b, preferred_element_type=F32)


def _dot_nt(a, b):
    return lax.dot_general(a, b, (((1,), (1,)), ((), ())), preferred_element_type=F32)


def _mod_kernel(c_ref, w_ref, b_ref, o_ref):
    a = _silu(c_ref[...])
    w = w_ref[0]
    a_hi = a.astype(BF16)
    a_lo = (a - a_hi.astype(F32)).astype(BF16)
    w_hi = w.astype(BF16)
    w_lo = (w - w_hi.astype(F32)).astype(BF16)
    o_ref[0] = _dot(a_hi, w_hi) + _dot(a_lo, w_hi) + _dot(a_hi, w_lo) + b_ref[0]


def _modulation(cc, mod_w, mod_b):
    depth, d, n3 = mod_w.shape
    tn = 1024
    return pl.pallas_call(
        _mod_kernel,
        grid=(depth, n3 // tn),
        in_specs=[
            pl.BlockSpec((MOD_ROWS, d), lambda l, j: (0, 0)),
            pl.BlockSpec((1, d, tn), lambda l, j: (l, 0, j)),
            pl.BlockSpec((1, 1, tn), lambda l, j: (l, 0, j)),
        ],
        out_specs=pl.BlockSpec((1, MOD_ROWS, tn), lambda l, j: (l, 0, j)),
        out_shape=jax.ShapeDtypeStruct((depth, MOD_ROWS, n3), F32),
        compiler_params=_params("parallel", "parallel"),
        name="adaln_modulation",
    )(cc, mod_w, mod_b.reshape(depth, 1, n3))


def _modulated_norm(x_ref, ng_ref, mod_ref):
    x = x_ref[...]
    shift = mod_ref[0, 0:1, :]
    scale = mod_ref[0, 1:2, :]
    return ((x * _rms(x)) * ng_ref[...] * (1.0 + scale) + shift).astype(BF16)


def _gqa_proj_kernel(x_ref, mod_ref, ng_ref, w_ref, qg_ref, kg_ref, cos_ref, su_ref, sd_ref,
                     q_ref, k_ref, vt_ref, g_ref, *, n_q, n_kv, scale):
    h = _modulated_norm(x_ref, ng_ref, mod_ref)
    cos, su, sd = cos_ref[...], su_ref[...], sd_ref[...]
    hd = GQA_HEAD_DIM
    qw, kw = n_q * hd, n_kv * hd
    q = _dot(h, w_ref[:, :qw])
    for i in range(n_q):
        qh = q[:, i * hd:(i + 1) * hd]
        qh = qh * _rms(qh) * qg_ref[...]
        q_ref[:, i * hd:(i + 1) * hd] = (_rope(qh, cos, su, sd, hd // 4) * scale).astype(BF16)
    k = _dot(h, w_ref[:, qw:qw + kw])
    for i in range(n_kv):
        kh = k[:, i * hd:(i + 1) * hd]
        kh = kh * _rms(kh) * kg_ref[...]
        k_ref[:, i * hd:(i + 1) * hd] = _rope(kh, cos, su, sd, hd // 4).astype(BF16)
    v = _dot(h, w_ref[:, qw + kw:qw + 2 * kw])
    for i in range(n_kv):
        vt_ref[i * hd:(i + 1) * hd, :] = v[:, i * hd:(i + 1) * hd].T.astype(BF16)
    g_ref[...] = _silu(_dot(h, w_ref[:, qw + 2 * kw:])).astype(BF16)


def _na_proj_kernel(x_ref, mod_ref, ng_ref, w_ref, q_ref, k_ref, vt_ref, g_ref, *, width, scale):
    h = _modulated_norm(x_ref, ng_ref, mod_ref)
    q_ref[...] = (_dot(h, w_ref[:, :width]) * scale).astype(BF16)
    k_ref[...] = _dot(h, w_ref[:, width:2 * width]).astype(BF16)
    v = _dot(h, w_ref[:, 2 * width:3 * width])
    for i in range(width // LANES):
        vt_ref[i * LANES:(i + 1) * LANES, :] = v[:, i * LANES:(i + 1) * LANES].T.astype(BF16)
    g_ref[...] = _silu(_dot(h, w_ref[:, 3 * width:])).astype(BF16)


def _mla_proj_kernel(x_ref, mod_ref, ng_ref, w_ref, qg_ref, kvg_ref, wuq_ref, wkv_ref,
                     cos_ref, su_ref, sd_ref, q_ref, k_ref, vt_ref, g_ref, *, width, scale):
    h = _modulated_norm(x_ref, ng_ref, mod_ref)
    cos, su, sd = cos_ref[...], su_ref[...], sd_ref[...]
    o_kv = MLA_Q_LORA
    o_g = o_kv + MLA_KV_LORA
    o_kr = o_g + width
    c_q = _dot(h, w_ref[:, :o_kv])
    c_q = (c_q * _rms(c_q) * qg_ref[...]).astype(BF16)
    q = _dot(c_q, wuq_ref[...])
    for i in range(MLA_HEADS):
        a = i * MLA_QK_PAD
        q_ref[:, a:a + MLA_NOPE] = (q[:, a:a + MLA_NOPE] * scale).astype(BF16)
        q_ref[:, a + MLA_NOPE:a + MLA_QK_PAD] = (
            _rope(q[:, a + MLA_NOPE:a + MLA_QK_PAD], cos, su, sd, MLA_ROPE // 4) * scale).astype(BF16)
    c_kv = _dot(h, w_ref[:, o_kv:o_g])
    c_kv = (c_kv * _rms(c_kv) * kvg_ref[...]).astype(BF16)
    k_r = _rope(_dot(h, w_ref[:, o_kr:]), cos, su, sd, MLA_ROPE // 4).astype(BF16)
    kv_in = jnp.concatenate([c_kv, k_r], axis=1)
    kw = MLA_HEADS * MLA_QK_PAD
    k_ref[...] = _dot(kv_in, wkv_ref[:, :kw]).astype(BF16)
    v = _dot(kv_in, wkv_ref[:, kw:])
    for i in range(MLA_HEADS):
        vt_ref[i * MLA_V:(i + 1) * MLA_V, :] = v[:, i * MLA_V:(i + 1) * MLA_V].T.astype(BF16)
    g_ref[...] = _silu(_dot(h, w_ref[:, o_g:o_kr])).astype(BF16)


def _project(kern, xs, mod3, layer, norm_g, consts, tables, out_widths, n_lat):
    b, n, d = xs.shape
    tm = ROW_TILE
    n_lat_tiles = n_lat // tm
    ctx_row = b

    def mod_idx(bi, t):
        return (layer * MOD_ROWS + jnp.where(t < n_lat_tiles, bi, ctx_row), 0, 0)

    in_specs = [
        pl.BlockSpec((None, tm, d), lambda bi, t: (bi, t, 0)),
        pl.BlockSpec((1, 3, d), mod_idx),
        pl.BlockSpec((1, d), lambda bi, t: (0, 0)),
    ]
    in_specs += [pl.BlockSpec(a.shape, lambda bi, t: (0, 0)) for a in consts]
    in_specs += [pl.BlockSpec((tm, LANES), lambda bi, t: (t, 0)) for _ in tables]
    out_specs = [pl.BlockSpec((None, tm, w), lambda bi, t: (bi, t, 0)) for w in out_widths]
    out_shape = [jax.ShapeDtypeStruct((b, n, w), BF16) for w in out_widths]
    out_specs[2] = pl.BlockSpec((None, None, out_widths[2], tm), lambda bi, t: (bi, t, 0, 0))
    out_shape[2] = jax.ShapeDtypeStruct((b, n // tm, out_widths[2], tm), BF16)
    return pl.pallas_call(
        kern,
        grid=(b, n // tm),
        in_specs=in_specs,
        out_specs=out_specs,
        out_shape=out_shape,
        compiler_params=_params("parallel", "parallel"),
        name="proj_" + str(layer),
    )(xs, mod3, norm_g.reshape(1, d), *consts, *tables)


def _stack_heads(q_ref, group, dk):
    return jnp.concatenate([q_ref[:, g * dk:(g + 1) * dk] for g in range(group)], axis=0)


def _unstack_heads(o_t, o_ref, group):
    tq = o_ref.shape[0]
    for g in range(group):
        o_ref[:, g * LANES:(g + 1) * LANES] = o_t[:, g * tq:(g + 1) * tq].T.astype(BF16)


def _pv_t(vt_ref, first_slab, p):
    slab = vt_ref.shape[-1]
    acc = None
    for i in range(p.shape[0] // slab):
        t = _dot(vt_ref[first_slab + i], p[i * slab:(i + 1) * slab, :])
        acc = t if acc is None else acc + t
    return acc


def _flash_kernel(q_ref, k_ref, vt_ref, o_ref, sa_ref, sb_ref, acc_ref, m_ref, l_ref, *, group, dk, bk):
    slab = vt_ref.shape[-1]
    n_chunks = k_ref.shape[0] // bk
    q = _stack_heads(q_ref, group, dk)
    tiles = [slice(j * Q_TILE, (j + 1) * Q_TILE) for j in range(q.shape[0] // Q_TILE)]

    def scores(c, s_ref):
        k = k_ref[pl.ds(pl.multiple_of(c * bk, bk), bk), :]
        out = []
        for t in tiles:
            s = _dot_nt(k, q[t, :])
            s_ref[:, t] = s
            out.append(jnp.max(s, axis=0, keepdims=True))
        return out

    def update(s_ref, s_max, c, t):
        m = m_ref[:, t]
        m_new = jnp.maximum(m, s_max)
        alpha = jnp.exp2(m - m_new)
        p = jnp.exp2(s_ref[:, t] - m_new)
        m_ref[:, t] = m_new
        l_ref[:, t] = alpha * l_ref[:, t] + jnp.sum(p, axis=0, keepdims=True)
        acc_ref[:, t] = alpha * acc_ref[:, t] + _pv_t(vt_ref, c * (bk // slab), p.astype(BF16))

    def step(c, cur_ref, cur_max, nxt_ref):
        k = k_ref[pl.ds(pl.multiple_of((c + 1) * bk, bk), bk), :]
        nxt_max = []
        for j, t in enumerate(tiles):
            s = _dot_nt(k, q[t, :])
            nxt_ref[:, t] = s
            nxt_max.append(jnp.max(s, axis=0, keepdims=True))
            update(cur_ref, cur_max[j], c, t)
        return nxt_max

    m_ref[...] = jnp.full(m_ref.shape, MASK_VALUE, F32)
    l_ref[...] = jnp.zeros(l_ref.shape, F32)
    acc_ref[...] = jnp.zeros(acc_ref.shape, F32)
    max_a = scores(0, sa_ref)

    def body(i, max_a):
        max_b = step(2 * i, sa_ref, max_a, sb_ref)
        return tuple(step(2 * i + 1, sb_ref, max_b, sa_ref))

    pairs = (n_chunks - 1) // 2
    max_a = lax.fori_loop(0, pairs, body, tuple(max_a))
    last = 2 * pairs
    if n_chunks - last == 2:
        max_b = step(last, sa_ref, max_a, sb_ref)
        last, s_ref, s_max = last + 1, sb_ref, max_b
    else:
        s_ref, s_max = sa_ref, max_a
    for j, t in enumerate(tiles):
        update(s_ref, s_max[j], last, t)
    _unstack_heads(acc_ref[...] * (1.0 / l_ref[...]), o_ref, group)


def _flash_attention(q, k, vt, *, group, dk, n_lat, layer):
    b, n, _ = q.shape
    n_kv = k.shape[2] // dk
    n_slab, slab = vt.shape[1], vt.shape[3]
    tq = min(Q_COLS // group, n_lat)
    bk = max(c for c in KV_CHUNKS if n % c == 0)
    assert bk % slab == 0
    kern = functools.partial(_flash_kernel, group=group, dk=dk, bk=bk)
    nq = group * tq
    s_buf = pltpu.VMEM((bk, nq), F32)
    stat = pltpu.VMEM((1, nq), F32)
    return pl.pallas_call(
        kern,
        grid=(b, n_kv, n_lat // tq),
        in_specs=[
            pl.BlockSpec((None, tq, group * dk), lambda bi, h, t: (bi, t, h)),
            pl.BlockSpec((None, n, dk), lambda bi, h, t: (bi, 0, h)),
            pl.BlockSpec((None, n_slab, LANES, slab), lambda bi, h, t: (bi, 0, h, 0)),
        ],
        out_specs=pl.BlockSpec((None, tq, group * LANES), lambda bi, h, t: (bi, t, h)),
        out_shape=jax.ShapeDtypeStruct((b, n_lat, n_kv * group * LANES), BF16),
        scratch_shapes=[s_buf, s_buf, pltpu.VMEM((LANES, nq), F32), stat, stat],
        compiler_params=_params("parallel", "parallel", "parallel"),
        name="flash_" + str(layer),
    )(q, k, vt)


def _ctx_attn_kernel(q_ref, k_ref, vt_ref, o_ref, *, group, dk):
    q = _stack_heads(q_ref, group, dk)
    s = _dot_nt(k_ref[...], q)
    p = jnp.exp2(s - jnp.max(s, axis=0, keepdims=True))
    l = jnp.sum(p, axis=0, keepdims=True)
    _unstack_heads(_dot(vt_ref[...], p.astype(BF16)) * (1.0 / l), o_ref, group)


def _ctx_attention(q, k, vt, *, group, dk, n_lat, layer):
    b, n, _ = q.shape
    n_ctx = n - n_lat
    n_kv = k.shape[2] // dk
    slab = vt.shape[3]
    assert slab == n_ctx
    t = n_lat // n_ctx
    return pl.pallas_call(
        functools.partial(_ctx_attn_kernel, group=group, dk=dk),
        grid=(b, n_kv),
        in_specs=[
            pl.BlockSpec((None, n_ctx, group * dk), lambda bi, h: (bi, t, h)),
            pl.BlockSpec((None, n_ctx, dk), lambda bi, h: (bi, t, h)),
            pl.BlockSpec((None, None, LANES, slab), lambda bi, h: (bi, t, h, 0)),
        ],
        out_specs=pl.BlockSpec((None, n_ctx, group * LANES), lambda bi, h: (bi, 0, h)),
        out_shape=jax.ShapeDtypeStruct((b, n_ctx, n_kv * group * LANES), BF16),
        compiler_params=_params("parallel", "parallel"),
        name="ctx_attn_" + str(layer),
    )(q, k, vt)


def _split_heads(q):
    lane = lax.broadcasted_iota(jnp.int32, (1, LANES), 1)
    zero = jnp.zeros_like(q)
    return jnp.concatenate(
        [jnp.where((lane >= j * NA_HEAD_DIM) & (lane < (j + 1) * NA_HEAD_DIM), q, zero)
         for j in range(LANES // NA_HEAD_DIM)], axis=0)


def _merge_heads(o_t, nq):
    parts = [o_t[j * NA_HEAD_DIM:(j + 1) * NA_HEAD_DIM, j * nq:(j + 1) * nq]
             for j in range(LANES // NA_HEAD_DIM)]
    return jnp.concatenate(parts, axis=0).T


def _na_ctx_kernel(q_ref, k_ref, vt_ref, o_ref):
    q2 = _split_heads(q_ref[...])
    s = _dot_nt(k_ref[...], q2)
    p = jnp.exp2(s - jnp.max(s, axis=0, keepdims=True))
    l = jnp.sum(p, axis=0, keepdims=True)
    o_t = _dot(vt_ref[...], p.astype(BF16)) * (1.0 / l)
    o_ref[...] = _merge_heads(o_t, q_ref.shape[0]).astype(BF16)


def _na_ctx_attention(q, k, vt, *, n_lat, layer):
    b, n, width = q.shape
    n_ctx = n - n_lat
    assert vt.shape[3] == n_ctx
    t = n_lat // n_ctx
    blk = pl.BlockSpec((None, n_ctx, LANES), lambda bi, h: (bi, t, h))
    return pl.pallas_call(
        _na_ctx_kernel,
        grid=(b, width // LANES),
        in_specs=[blk, blk, pl.BlockSpec((None, None, LANES, n_ctx), lambda bi, h: (bi, t, h, 0))],
        out_specs=pl.BlockSpec((None, n_ctx, LANES), lambda bi, h: (bi, 0, h)),
        out_shape=jax.ShapeDtypeStruct((b, n_ctx, width), BF16),
        compiler_params=_params("parallel", "parallel"),
        name="ctx_attn_" + str(layer),
    )(q, k, vt)


def _na_window_start(blk, rows, lib):
    lo = lib.minimum(lib.maximum(NA_BLOCK_ROWS * blk - NA_KH // 2, 0), rows - NA_KH)
    return lib.minimum(lo, rows - NA_WIN_ROWS)


def _na_kernel(q_ref, k_ref, vt_ref, bias_ref, o_ref, *, n_lat, n_ctx, rows):
    slab = vt_ref.shape[-1]
    nq = NA_BLOCK_ROWS * GRID_W
    n_blocks = rows // NA_BLOCK_ROWS
    nk = NA_WIN_ROWS * GRID_W
    kc = k_ref[n_lat:n_lat + n_ctx, :]
    for i in range(q_ref.shape[0] // nq):
        blk = pl.program_id(2) * NA_STEP_BLOCKS + i
        kind = jnp.where(blk == 0, 1, jnp.where(blk == n_blocks - 1, 2, 0))
        ws = _na_window_start(blk, rows, jnp)
        q2 = _split_heads(q_ref[i * nq:(i + 1) * nq, :])
        s_loc = _dot_nt(k_ref[pl.ds(pl.multiple_of(ws * GRID_W, slab), nk), :], q2) + bias_ref[kind]
        s_ctx = _dot_nt(kc, q2)
        m = jnp.maximum(jnp.max(s_loc, axis=0, keepdims=True), jnp.max(s_ctx, axis=0, keepdims=True))
        p_loc = jnp.exp2(s_loc - m)
        p_ctx = jnp.exp2(s_ctx - m)
        l = jnp.sum(p_loc, axis=0, keepdims=True) + jnp.sum(p_ctx, axis=0, keepdims=True)
        o_t = _pv_t(vt_ref, ws * GRID_W // slab, p_loc.astype(BF16))
        o_t = o_t + _pv_t(vt_ref, n_lat // slab, p_ctx.astype(BF16))
        o_ref[i * nq:(i + 1) * nq, :] = _merge_heads(o_t * (1.0 / l), nq).astype(BF16)


def _na_bias(rpb, rows):
    h, n_dr, n_dc = rpb.shape
    n_blocks = rows // NA_BLOCK_ROWS
    lead = GRID_W - NA_KW
    wide = 2 * GRID_W
    w = jnp.pad(rpb * LOG2E, ((0, 0), (0, 0), (lead, wide - lead - n_dc)), constant_values=MASK_VALUE)
    flat = jnp.broadcast_to(w[:, :, None, :], (h, n_dr, GRID_W, wide)).reshape(h, n_dr, GRID_W * wide)
    skew = flat[:, :, GRID_W - 1:GRID_W - 1 + GRID_W * (wide - 1)].reshape(h, n_dr, GRID_W, wide - 1)
    qc = jnp.arange(GRID_W)
    cs = jnp.clip(qc - NA_KW // 2, 0, GRID_W - NA_KW)
    col_ok = (qc[None, :] >= cs[:, None]) & (qc[None, :] < cs[:, None] + NA_KW)
    c_t = jnp.swapaxes(jnp.where(col_ok, skew[..., :GRID_W], MASK_VALUE), -1, -2)
    masked = jnp.full((h, GRID_W, GRID_W), MASK_VALUE, F32)
    kinds = []
    for blk in (1, 0, n_blocks - 1):
        ws = _na_window_start(blk, rows, np)
        strips = []
        for j in range(NA_WIN_ROWS):
            kr = ws + j
            blocks = []
            for i in range(NA_BLOCK_ROWS):
                r = NA_BLOCK_ROWS * blk + i
                rs = min(max(r - NA_KH // 2, 0), rows - NA_KH)
                blocks.append(c_t[:, kr - r + NA_KH - 1] if rs <= kr < rs + NA_KH else masked)
            strips.append(jnp.concatenate(blocks, axis=-1))
        kinds.append(jnp.concatenate(strips, axis=1))
    bias = jnp.stack(kinds, axis=1)
    sub = LANES // NA_HEAD_DIM
    bias = bias.reshape(h // sub, sub, len(kinds), NA_WIN_ROWS * GRID_W, NA_BLOCK_ROWS * GRID_W)
    return jnp.concatenate([bias[:, j] for j in range(sub)], axis=-1)


def _na_attention(q, k, vt, bias, *, n_lat, layer):
    b, n, width = q.shape
    n_ctx = n - n_lat
    rows = n_lat // GRID_W
    n_slab, slab = vt.shape[1], vt.shape[3]
    tq = NA_STEP_BLOCKS * NA_BLOCK_ROWS * GRID_W
    kern = functools.partial(_na_kernel, n_lat=n_lat, n_ctx=n_ctx, rows=rows)
    return pl.pallas_call(
        kern,
        grid=(width // LANES, b, n_lat // tq),
        in_specs=[
            pl.BlockSpec((None, tq, LANES), lambda hp, bi, rb: (bi, rb, hp)),
            pl.BlockSpec((None, n, LANES), lambda hp, bi, rb: (bi, 0, hp)),
            pl.BlockSpec((None, n_slab, LANES, slab), lambda hp, bi, rb: (bi, 0, hp, 0)),
            pl.BlockSpec((None,) + bias.shape[1:], lambda hp, bi, rb: (hp, 0, 0, 0)),
        ],
        out_specs=pl.BlockSpec((None, tq, LANES), lambda hp, bi, rb: (bi, rb, hp)),
        out_shape=jax.ShapeDtypeStruct((b, n_lat, width), BF16),
        compiler_params=_params("parallel", "parallel", "arbitrary"),
        name="na_attn_" + str(layer),
    )(q, k, vt, bias)


def _out_proj_kernel(o_ref, oc_ref, g_ref, w_ref, x_ref, mod_ref, xo_ref, *, n_lat_tiles):
    o = jnp.where(pl.program_id(1) >= n_lat_tiles, oc_ref[...], o_ref[...])
    a = (o.astype(F32) * g_ref[...].astype(F32)).astype(BF16)
    xo_ref[...] = x_ref[...] + mod_ref[0, 2:3, :] * _dot(a, w_ref[...])


def _out_proj_final_kernel(o_ref, g_ref, w_ref, x_ref, mod_ref, fg_ref, y_ref):
    a = (o_ref[...].astype(F32) * g_ref[...].astype(F32)).astype(BF16)
    xn = x_ref[...] + mod_ref[0, 2:3, :] * _dot(a, w_ref[...])
    y_ref[...] = xn * _rms(xn) * fg_ref[...]


def _out_proj(o, o_ctx, gs, w_out, xs, mod3, layer, n_lat, final_g=None):
    b, n, d = xs.shape
    tm = ROW_TILE
    n_lat_tiles = n_lat // tm
    ctx_row = b

    def mod_idx(bi, t):
        return (layer * MOD_ROWS + jnp.where(t < n_lat_tiles, bi, ctx_row), 0, 0)

    row = lambda bi, t: (bi, t, 0)
    w = o.shape[2]
    o_spec = pl.BlockSpec((None, tm, w), lambda bi, t: (bi, jnp.minimum(t, n_lat_tiles - 1), 0))
    tail = [
        pl.BlockSpec((None, tm, w), row),
        pl.BlockSpec(w_out.shape, lambda bi, t: (0, 0)),
        pl.BlockSpec((None, tm, d), row),
        pl.BlockSpec((1, 3, d), mod_idx),
    ]
    if final_g is None:
        return pl.pallas_call(
            functools.partial(_out_proj_kernel, n_lat_tiles=n_lat_tiles),
            grid=(b, n // tm),
            in_specs=[o_spec, pl.BlockSpec((None, tm, w), lambda bi, t: (bi, 0, 0))] + tail,
            out_specs=pl.BlockSpec((None, tm, d), row),
            out_shape=jax.ShapeDtypeStruct(xs.shape, F32),
            input_output_aliases={4: 0},
            compiler_params=_params("parallel", "parallel"),
            name="out_proj_" + str(layer),
        )(o, o_ctx, gs, w_out, xs, mod3)
    return pl.pallas_call(
        _out_proj_final_kernel,
        grid=(b, n_lat_tiles),
        in_specs=[o_spec] + tail + [pl.BlockSpec((1, d), lambda bi, t: (0, 0))],
        out_specs=pl.BlockSpec((None, tm, d), row),
        out_shape=jax.ShapeDtypeStruct((b, n_lat, d), F32),
        compiler_params=_params("parallel", "parallel"),
        name="out_proj_final",
    )(o, gs, w_out, xs, mod3, final_g.reshape(1, d))


def _rope_tables(n_lat, n_ctx, rot_dim):
    n_freq = rot_dim // 4
    inv = ROPE_THETA ** (-jnp.arange(n_freq, dtype=F32) / n_freq)
    t = jnp.arange(n_lat)
    ang_r = (t // GRID_W).astype(F32)[:, None] * inv
    ang_c = (t % GRID_W).astype(F32)[:, None] * inv
    ang = jnp.concatenate([ang_r, ang_r, ang_c, ang_c], axis=-1)
    cos, sin = jnp.cos(ang), jnp.sin(ang)
    first = (jnp.arange(rot_dim) % (2 * n_freq)) < n_freq
    sin_up = jnp.where(first, -sin, 0.0)
    sin_dn = jnp.where(first, 0.0, sin)
    pad = LANES - rot_dim

    def finish(tab, fill):
        tab = jnp.pad(tab, ((0, 0), (0, pad)), constant_values=fill)
        return jnp.pad(tab, ((0, n_ctx), (0, 0)), constant_values=fill)

    return finish(cos, 1.0), finish(sin_up, 0.0), finish(sin_dn, 0.0)


def _mla_weights(w_in, w_uq, w_ukv):
    o_kv = MLA_Q_LORA
    o_kr = o_kv + MLA_KV_LORA
    o_g = o_kr + MLA_ROPE
    k_r = jnp.pad(w_in[:, o_kr:o_g], ((0, 0), (0, LANES - MLA_ROPE)))
    w_perm = jnp.concatenate([w_in[:, :o_kr], w_in[:, o_g:], k_r], axis=1).astype(BF16)
    uq = w_uq.reshape(MLA_Q_LORA, MLA_HEADS, MLA_NOPE + MLA_ROPE)
    uq = jnp.pad(uq, ((0, 0), (0, 0), (0, MLA_QK_PAD - MLA_NOPE - MLA_ROPE)))
    uq = uq.reshape(MLA_Q_LORA, MLA_HEADS * MLA_QK_PAD).astype(BF16)
    ukv = w_ukv.reshape(MLA_KV_LORA, MLA_HEADS, MLA_NOPE + MLA_V)
    k_top = jnp.pad(ukv[:, :, :MLA_NOPE], ((0, 0), (0, 0), (0, MLA_QK_PAD - MLA_NOPE)))
    eye = jnp.pad(jnp.eye(MLA_ROPE, dtype=F32), ((0, LANES - MLA_ROPE), (MLA_NOPE, MLA_QK_PAD - MLA_NOPE - MLA_ROPE)))
    k_bot = jnp.broadcast_to(eye[:, None, :], (LANES, MLA_HEADS, MLA_QK_PAD))
    k_aug = jnp.concatenate([k_top, k_bot], axis=0).reshape(MLA_KV_LORA + LANES, MLA_HEADS * MLA_QK_PAD)
    v_aug = jnp.pad(ukv[:, :, MLA_NOPE:].reshape(MLA_KV_LORA, MLA_HEADS * MLA_V), ((0, LANES), (0, 0)))
    return w_perm, uq, jnp.concatenate([k_aug, v_aug], axis=1).astype(BF16)


def kernel(x, c, ctx, c_ctx, mod_w, mod_b, norm_g, final_g, ga_w_in, ga_q_g, ga_k_g, ga_w_out, na_w_in, na_rpb, na_w_out, mla_w_in, mla_q_g, mla_kv_g, mla_w_uq, mla_w_ukv, mla_w_out):
    b, n_lat, d = x.shape
    n_ctx = ctx.shape[1]
    depth = mod_w.shape[0]
    assert n_lat % ROW_TILE == 0 and n_ctx == ROW_TILE and n_lat % n_ctx == 0
    assert b < MOD_ROWS and n_lat % (NA_STEP_BLOCKS * NA_BLOCK_ROWS * GRID_W) == 0
    assert n_lat // GRID_W >= NA_WIN_ROWS + NA_BLOCK_ROWS

    xs = jnp.concatenate([x, ctx], axis=1)
    cc = jnp.concatenate([c, c_ctx[None, :], jnp.zeros((MOD_ROWS - b - 1, d), F32)], axis=0)
    mod3 = _modulation(cc, mod_w, mod_b).reshape(depth * MOD_ROWS, 3, d)

    tab_a = _rope_tables(n_lat, n_ctx, GQA_HEAD_DIM)
    tab_m = _rope_tables(n_lat, n_ctx, MLA_ROPE)

    out = None
    for i in range(depth):
        kind, j = i % N_MIXERS, i // N_MIXERS
        need_ctx = i < depth - 1
        if kind == 0:
            n_q = d // GQA_HEAD_DIM
            n_kv = n_q // GQA_GROUP
            kern = functools.partial(_gqa_proj_kernel, n_q=n_q, n_kv=n_kv, scale=GQA_HEAD_DIM ** -0.5 * LOG2E)
            consts = [ga_w_in[j].astype(BF16), ga_q_g[j].reshape(1, -1), ga_k_g[j].reshape(1, -1)]
            kvw = n_kv * GQA_HEAD_DIM
            q, k, vt, gs = _project(kern, xs, mod3, i, norm_g[i], consts, tab_a, [d, kvw, kvw, d], n_lat)
            o = _flash_attention(q, k, vt, group=GQA_GROUP, dk=GQA_HEAD_DIM, n_lat=n_lat, layer=i)
            if need_ctx:
                oc = _ctx_attention(q, k, vt, group=GQA_GROUP, dk=GQA_HEAD_DIM, n_lat=n_lat, layer=i)
            w_out = ga_w_out[j]
        elif kind == 1:
            kern = functools.partial(_na_proj_kernel, width=d, scale=NA_HEAD_DIM ** -0.5 * LOG2E)
            q, k, vt, gs = _project(kern, xs, mod3, i, norm_g[i], [na_w_in[j].astype(BF16)], (), [d, d, d, d], n_lat)
            o = _na_attention(q, k, vt, _na_bias(na_rpb[j], n_lat // GRID_W), n_lat=n_lat, layer=i)
            if need_ctx:
                oc = _na_ctx_attention(q, k, vt, n_lat=n_lat, layer=i)
            w_out = na_w_out[j]
        else:
            kern = functools.partial(_mla_proj_kernel, width=d, scale=(MLA_NOPE + MLA_ROPE) ** -0.5 * LOG2E)
            w_perm, uq, wkv = _mla_weights(mla_w_in[j], mla_w_uq[j], mla_w_ukv[j])
            consts = [w_perm, mla_q_g[j].reshape(1, -1), mla_kv_g[j].reshape(1, -1), uq, wkv]
            widths = [MLA_HEADS * MLA_QK_PAD, MLA_HEADS * MLA_QK_PAD, MLA_HEADS * MLA_V, d]
            q, k, vt, gs = _project(kern, xs, mod3, i, norm_g[i], consts, tab_m, widths, n_lat)
            o = _flash_attention(q, k, vt, group=1, dk=MLA_QK_PAD, n_lat=n_lat, layer=i)
            if need_ctx:
                oc = _ctx_attention(q, k, vt, group=1, dk=MLA_QK_PAD, n_lat=n_lat, layer=i)
            w_out = mla_w_out[j]
        if need_ctx:
            xs = _out_proj(o, oc, gs, w_out.astype(BF16), xs, mod3, i, n_lat)
        else:
            out = _out_proj(o, None, gs, w_out.astype(BF16), xs, mod3, i, n_lat, final_g=final_g)
    return out
```

```python
import functools

import jax
import numpy as np
import jax.numpy as jnp
from jax import lax
from jax.experimental import pallas as pl
from jax.experimental.pallas import tpu as pltpu

F32 = jnp.float32
BF16 = jnp.bfloat16

NORM_EPS = 1e-6
ROPE_THETA = 10000.0
GRID_W = 64
N_MIXERS = 3
LANES = 128
MOD_ROWS = 8
MASK_VALUE = -1e30
LOG2E = 1.4426950408889634

GQA_HEAD_DIM = 128
GQA_GROUP = 4
NA_HEAD_DIM = 64
NA_KH = 8
NA_KW = 16
NA_BLOCK_ROWS = 4
NA_WIN_ROWS = 12
NA_STEP_BLOCKS = 4
MLA_HEADS = 8
MLA_Q_LORA = 512
MLA_KV_LORA = 256
MLA_NOPE = 128
MLA_ROPE = 64
MLA_V = 128
MLA_QK_PAD = 256

ROW_TILE = 256
Q_COLS = 1024
Q_TILE = 256
SUM_ROWS = 16
KV_CHUNKS = (768, 512, 256)
VMEM_LIMIT = 48 * 1024 * 1024


def _params(*sem):
    return pltpu.CompilerParams(dimension_semantics=sem, vmem_limit_bytes=VMEM_LIMIT)


def _silu(v):
    return v * (1.0 / (1.0 + jnp.exp(-v)))


def _rms(v):
    return lax.rsqrt(jnp.mean(v * v, axis=-1, keepdims=True) + NORM_EPS)


def _rope(v, cos, sin_up, sin_dn, half):
    w = v.shape[-1]
    return v * cos + pltpu.roll(v, w - half, 1) * sin_up + pltpu.roll(v, half, 1) * sin_dn


def _dot(a, b):
    return jnp.dot(a, b, preferred_element_type=F32)


def _dot_nt(a, b):
    return lax.dot_general(a, b, (((1,), (1,)), ((), ())), preferred_element_type=F32)


def _mod_kernel(c_ref, w_ref, b_ref, o_ref):
    a = _silu(c_ref[...])
    w = w_ref[0]
    a_hi = a.astype(BF16)
    a_lo = (a - a_hi.astype(F32)).astype(BF16)
    w_hi = w.astype(BF16)
    w_lo = (w - w_hi.astype(F32)).astype(BF16)
    o_ref[0] = _dot(a_hi, w_hi) + _dot(a_lo, w_hi) + _dot(a_hi, w_lo) + b_ref[0]


def _modulation(cc, mod_w, mod_b):
    depth, d, n3 = mod_w.shape
    tn = 1024
    return pl.pallas_call(
        _mod_kernel,
        grid=(depth, n3 // tn),
        in_specs=[
            pl.BlockSpec((MOD_ROWS, d), lambda l, j: (0, 0)),
            pl.BlockSpec((1, d, tn), lambda l, j: (l, 0, j)),
            pl.BlockSpec((1, 1, tn), lambda l, j: (l, 0, j)),
        ],
        out_specs=pl.BlockSpec((1, MOD_ROWS, tn), lambda l, j: (l, 0, j)),
        out_shape=jax.ShapeDtypeStruct((depth, MOD_ROWS, n3), F32),
        compiler_params=_params("parallel", "parallel"),
        name="adaln_modulation",
    )(cc, mod_w, mod_b.reshape(depth, 1, n3))


def _modulated_norm(x_ref, ng_ref, mod_ref):
    x = x_ref[...]
    shift = mod_ref[0, 0:1, :]
    scale = mod_ref[0, 1:2, :]
    return ((x * _rms(x)) * ng_ref[...] * (1.0 + scale) + shift).astype(BF16)


def _gqa_proj_kernel(x_ref, mod_ref, ng_ref, w_ref, qg_ref, kg_ref, cos_ref, su_ref, sd_ref,
                     q_ref, k_ref, vt_ref, g_ref, *, n_q, n_kv, scale):
    h = _modulated_norm(x_ref, ng_ref, mod_ref)
    cos, su, sd = cos_ref[...], su_ref[...], sd_ref[...]
    hd = GQA_HEAD_DIM
    qw, kw = n_q * hd, n_kv * hd
    q = _dot(h, w_ref[:, :qw])
    for i in range(n_q):
        qh = q[:, i * hd:(i + 1) * hd]
        qh = qh * _rms(qh) * qg_ref[...]
        q_ref[:, i * hd:(i + 1) * hd] = (_rope(qh, cos, su, sd, hd // 4) * scale).astype(BF16)
    k = _dot(h, w_ref[:, qw:qw + kw])
    for i in range(n_kv):
        kh = k[:, i * hd:(i + 1) * hd]
        kh = kh * _rms(kh) * kg_ref[...]
        k_ref[:, i * hd:(i + 1) * hd] = _rope(kh, cos, su, sd, hd // 4).astype(BF16)
    v = _dot(h, w_ref[:, qw + kw:qw + 2 * kw])
    for i in range(n_kv):
        vt_ref[i * hd:(i + 1) * hd, :] = v[:, i * hd:(i + 1) * hd].T.astype(BF16)
    g_ref[...] = _silu(_dot(h, w_ref[:, qw + 2 * kw:])).astype(BF16)


def _na_proj_kernel(x_ref, mod_ref, ng_ref, w_ref, q_ref, k_ref, vt_ref, g_ref, *, width, scale):
    h = _modulated_norm(x_ref, ng_ref, mod_ref)
    q_ref[...] = (_dot(h, w_ref[:, :width]) * scale).astype(BF16)
    k_ref[...] = _dot(h, w_ref[:, width:2 * width]).astype(BF16)
    v = _dot(h, w_ref[:, 2 * width:3 * width])
    for i in range(width // LANES):
        vt_ref[i * LANES:(i + 1) * LANES, :] = v[:, i * LANES:(i + 1) * LANES].T.astype(BF16)
    g_ref[...] = _silu(_dot(h, w_ref[:, 3 * width:])).astype(BF16)


def _mla_proj_kernel(x_ref, mod_ref, ng_ref, w_ref, qg_ref, kvg_ref, wuq_ref, wkv_ref,
                     cos_ref, su_ref, sd_ref, q_ref, k_ref, vt_ref, g_ref, *, width, scale):
    h = _modulated_norm(x_ref, ng_ref, mod_ref)
    cos, su, sd = cos_ref[...], su_ref[...], sd_ref[...]
    o_kv = MLA_Q_LORA
    o_g = o_kv + MLA_KV_LORA
    o_kr = o_g + width
    c_q = _dot(h, w_ref[:, :o_kv])
    c_q = (c_q * _rms(c_q) * qg_ref[...]).astype(BF16)
    q = _dot(c_q, wuq_ref[...])
    for i in range(MLA_HEADS):
        a = i * MLA_QK_PAD
        q_ref[:, a:a + MLA_NOPE] = (q[:, a:a + MLA_NOPE] * scale).astype(BF16)
        q_ref[:, a + MLA_NOPE:a + MLA_QK_PAD] = (
            _rope(q[:, a + MLA_NOPE:a + MLA_QK_PAD], cos, su, sd, MLA_ROPE // 4) * scale).astype(BF16)
    c_kv = _dot(h, w_ref[:, o_kv:o_g])
    c_kv = (c_kv * _rms(c_kv) * kvg_ref[...]).astype(BF16)
    k_r = _rope(_dot(h, w_ref[:, o_kr:]), cos, su, sd, MLA_ROPE // 4).astype(BF16)
    kv_in = jnp.concatenate([c_kv, k_r], axis=1)
    kw = MLA_HEADS * MLA_QK_PAD
    k_ref[...] = _dot(kv_in, wkv_ref[:, :kw]).astype(BF16)
    v = _dot(kv_in, wkv_ref[:, kw:])
    for i in range(MLA_HEADS):
        vt_ref[i * MLA_V:(i + 1) * MLA_V, :] = v[:, i * MLA_V:(i + 1) * MLA_V].T.astype(BF16)
    g_ref[...] = _silu(_dot(h, w_ref[:, o_g:o_kr])).astype(BF16)


def _project(kern, xs, mod3, layer, norm_g, consts, tables, out_widths, n_lat):
    b, n, d = xs.shape
    tm = ROW_TILE
    n_lat_tiles = n_lat // tm
    ctx_row = b

    def mod_idx(bi, t):
        return (layer * MOD_ROWS + jnp.where(t < n_lat_tiles, bi, ctx_row), 0, 0)

    in_specs = [
        pl.BlockSpec((None, tm, d), lambda bi, t: (bi, t, 0)),
        pl.BlockSpec((1, 3, d), mod_idx),
        pl.BlockSpec((1, d), lambda bi, t: (0, 0)),
    ]
    in_specs += [pl.BlockSpec(a.shape, lambda bi, t: (0, 0)) for a in consts]
    in_specs += [pl.BlockSpec((tm, LANES), lambda bi, t: (t, 0)) for _ in tables]
    out_specs = [pl.BlockSpec((None, tm, w), lambda bi, t: (bi, t, 0)) for w in out_widths]
    out_shape = [jax.ShapeDtypeStruct((b, n, w), BF16) for w in out_widths]
    out_specs[2] = pl.BlockSpec((None, None, out_widths[2], tm), lambda bi, t: (bi, t, 0, 0))
    out_shape[2] = jax.ShapeDtypeStruct((b, n // tm, out_widths[2], tm), BF16)
    return pl.pallas_call(
        kern,
        grid=(b, n // tm),
        in_specs=in_specs,
        out_specs=out_specs,
        out_shape=out_shape,
        compiler_params=_params("parallel", "parallel"),
        name="proj_" + str(layer),
    )(xs, mod3, norm_g.reshape(1, d), *consts, *tables)


def _stack_heads(q_ref, group, dk):
    return jnp.concatenate([q_ref[:, g * dk:(g + 1) * dk] for g in range(group)], axis=0)


def _unstack_heads(o_t, o_ref, group):
    tq = o_ref.shape[0]
    for g in range(group):
        o_ref[:, g * LANES:(g + 1) * LANES] = o_t[:, g * tq:(g + 1) * tq].T.astype(BF16)


def _pv_t(vt_ref, first_slab, p, sum_rows=0):
    slab = vt_ref.shape[-1]
    acc = None
    for i in range(p.shape[0] // slab):
        vt = vt_ref[first_slab + i]
        if sum_rows:
            vt = jnp.concatenate([vt, jnp.ones((sum_rows, slab), vt.dtype)], axis=0)
        t = _dot(vt, p[i * slab:(i + 1) * slab, :])
        acc = t if acc is None else acc + t
    return acc


def _flash_kernel(q_ref, k_ref, vt_ref, o_ref, sa_ref, sb_ref, acc_ref, m_ref, *, group, dk, bk):
    slab = vt_ref.shape[-1]
    n_chunks = k_ref.shape[0] // bk
    q = _stack_heads(q_ref, group, dk)
    tiles = [slice(j * Q_TILE, (j + 1) * Q_TILE) for j in range(q.shape[0] // Q_TILE)]

    def scores(c, s_ref):
        k = k_ref[pl.ds(pl.multiple_of(c * bk, bk), bk), :]
        out = []
        for t in tiles:
            s = _dot_nt(k, q[t, :])
            s_ref[:, t] = s
            out.append(jnp.max(s, axis=0, keepdims=True))
        return out

    def update(s_ref, s_max, c, t):
        m = m_ref[:, t]
        m_new = jnp.maximum(m, s_max)
        alpha = jnp.exp2(m - m_new)
        p = jnp.exp2((s_ref[:, t] - m_new).astype(BF16))
        m_ref[:, t] = m_new
        acc_ref[:, t] = alpha * acc_ref[:, t] + _pv_t(vt_ref, c * (bk // slab), p, SUM_ROWS)

    def step(c, cur_ref, cur_max, nxt_ref):
        k = k_ref[pl.ds(pl.multiple_of((c + 1) * bk, bk), bk), :]
        nxt_max = []
        for j, t in enumerate(tiles):
            s = _dot_nt(k, q[t, :])
            nxt_ref[:, t] = s
            nxt_max.append(jnp.max(s, axis=0, keepdims=True))
            update(cur_ref, cur_max[j], c, t)
        return nxt_max

    m_ref[...] = jnp.full(m_ref.shape, MASK_VALUE, F32)
    acc_ref[...] = jnp.zeros(acc_ref.shape, F32)
    max_a = scores(0, sa_ref)

    def body(i, max_a):
        max_b = step(2 * i, sa_ref, max_a, sb_ref)
        return tuple(step(2 * i + 1, sb_ref, max_b, sa_ref))

    pairs = (n_chunks - 1) // 2
    max_a = lax.fori_loop(0, pairs, body, tuple(max_a))
    last = 2 * pairs
    if n_chunks - last == 2:
        max_b = step(last, sa_ref, max_a, sb_ref)
        last, s_ref, s_max = last + 1, sb_ref, max_b
    else:
        s_ref, s_max = sa_ref, max_a
    for j, t in enumerate(tiles):
        update(s_ref, s_max[j], last, t)
    _unstack_heads(acc_ref[:LANES, :] * (1.0 / acc_ref[LANES:LANES + 1, :]), o_ref, group)


def _flash_attention(q, k, vt, *, group, dk, n_lat, layer):
    b, n, _ = q.shape
    n_kv = k.shape[2] // dk
    n_slab, slab = vt.shape[1], vt.shape[3]
    tq = min(Q_COLS // group, n_lat)
    bk = max(c for c in KV_CHUNKS if n % c == 0)
    assert bk % slab == 0
    kern = functools.partial(_flash_kernel, group=group, dk=dk, bk=bk)
    nq = group * tq
    s_buf = pltpu.VMEM((bk, nq), F32)
    return pl.pallas_call(
        kern,
        grid=(b, n_kv, n_lat // tq),
        in_specs=[
            pl.BlockSpec((None, tq, group * dk), lambda bi, h, t: (bi, t, h)),
            pl.BlockSpec((None, n, dk), lambda bi, h, t: (bi, 0, h)),
            pl.BlockSpec((None, n_slab, LANES, slab), lambda bi, h, t: (bi, 0, h, 0)),
        ],
        out_specs=pl.BlockSpec((None, tq, group * LANES), lambda bi, h, t: (bi, t, h)),
        out_shape=jax.ShapeDtypeStruct((b, n_lat, n_kv * group * LANES), BF16),
        scratch_shapes=[s_buf, s_buf, pltpu.VMEM((LANES + SUM_ROWS, nq), F32), pltpu.VMEM((1, nq), F32)],
        compiler_params=_params("parallel", "parallel", "parallel"),
        name="flash_" + str(layer),
    )(q, k, vt)


def _ctx_attn_kernel(q_ref, k_ref, vt_ref, o_ref, *, group, dk):
    q = _stack_heads(q_ref, group, dk)
    s = _dot_nt(k_ref[...], q)
    p = jnp.exp2(s - jnp.max(s, axis=0, keepdims=True))
    l = jnp.sum(p, axis=0, keepdims=True)
    _unstack_heads(_dot(vt_ref[...], p.astype(BF16)) * (1.0 / l), o_ref, group)


def _ctx_attention(q, k, vt, *, group, dk, n_lat, layer):
    b, n, _ = q.shape
    n_ctx = n - n_lat
    n_kv = k.shape[2] // dk
    slab = vt.shape[3]
    assert slab == n_ctx
    t = n_lat // n_ctx
    return pl.pallas_call(
        functools.partial(_ctx_attn_kernel, group=group, dk=dk),
        grid=(b, n_kv),
        in_specs=[
            pl.BlockSpec((None, n_ctx, group * dk), lambda bi, h: (bi, t, h)),
            pl.BlockSpec((None, n_ctx, dk), lambda bi, h: (bi, t, h)),
            pl.BlockSpec((None, None, LANES, slab), lambda bi, h: (bi, t, h, 0)),
        ],
        out_specs=pl.BlockSpec((None, n_ctx, group * LANES), lambda bi, h: (bi, 0, h)),
        out_shape=jax.ShapeDtypeStruct((b, n_ctx, n_kv * group * LANES), BF16),
        compiler_params=_params("parallel", "parallel"),
        name="ctx_attn_" + str(layer),
    )(q, k, vt)


def _split_heads(q):
    lane = lax.broadcasted_iota(jnp.int32, (1, LANES), 1)
    zero = jnp.zeros_like(q)
    return jnp.concatenate(
        [jnp.where((lane >= j * NA_HEAD_DIM) & (lane < (j + 1) * NA_HEAD_DIM), q, zero)
         for j in range(LANES // NA_HEAD_DIM)], axis=0)


def _merge_heads(o_t, nq):
    parts = [o_t[j * NA_HEAD_DIM:(j + 1) * NA_HEAD_DIM, j * nq:(j + 1) * nq]
             for j in range(LANES // NA_HEAD_DIM)]
    return jnp.concatenate(parts, axis=0).T


def _na_ctx_kernel(q_ref, k_ref, vt_ref, o_ref):
    q2 = _split_heads(q_ref[...])
    s = _dot_nt(k_ref[...], q2)
    p = jnp.exp2(s - jnp.max(s, axis=0, keepdims=True))
    l = jnp.sum(p, axis=0, keepdims=True)
    o_t = _dot(vt_ref[...], p.astype(BF16)) * (1.0 / l)
    o_ref[...] = _merge_heads(o_t, q_ref.shape[0]).astype(BF16)


def _na_ctx_attention(q, k, vt, *, n_lat, layer):
    b, n, width = q.shape
    n_ctx = n - n_lat
    assert vt.shape[3] == n_ctx
    t = n_lat // n_ctx
    blk = pl.BlockSpec((None, n_ctx, LANES), lambda bi, h: (bi, t, h))
    return pl.pallas_call(
        _na_ctx_kernel,
        grid=(b, width // LANES),
        in_specs=[blk, blk, pl.BlockSpec((None, None, LANES, n_ctx), lambda bi, h: (bi, t, h, 0))],
        out_specs=pl.BlockSpec((None, n_ctx, LANES), lambda bi, h: (bi, 0, h)),
        out_shape=jax.ShapeDtypeStruct((b, n_ctx, width), BF16),
        compiler_params=_params("parallel", "parallel"),
        name="ctx_attn_" + str(layer),
    )(q, k, vt)


def _na_window_start(blk, rows, lib):
    lo = lib.minimum(lib.maximum(NA_BLOCK_ROWS * blk - NA_KH // 2, 0), rows - NA_KH)
    return lib.minimum(lo, rows - NA_WIN_ROWS)


def _na_kernel(q_ref, k_ref, vt_ref, bias_ref, o_ref, sl_ref, sc_ref, *, n_lat, n_ctx, rows):
    slab = vt_ref.shape[-1]
    nq = NA_BLOCK_ROWS * GRID_W
    n_blocks = rows // NA_BLOCK_ROWS
    nk = NA_WIN_ROWS * GRID_W
    kc = k_ref[n_lat:n_lat + n_ctx, :]
    n_here = q_ref.shape[0] // nq

    def scores(i):
        blk = pl.program_id(2) * NA_STEP_BLOCKS + i
        kind = jnp.where(blk == 0, 1, jnp.where(blk == n_blocks - 1, 2, 0))
        ws = _na_window_start(blk, rows, jnp)
        q2 = _split_heads(q_ref[i * nq:(i + 1) * nq, :])
        s_loc = _dot_nt(k_ref[pl.ds(pl.multiple_of(ws * GRID_W, slab), nk), :], q2) + bias_ref[kind]
        s_ctx = _dot_nt(kc, q2)
        sl_ref[i % 2] = s_loc
        sc_ref[i % 2] = s_ctx
        return ws, jnp.maximum(jnp.max(s_loc, axis=0, keepdims=True), jnp.max(s_ctx, axis=0, keepdims=True))

    nxt = scores(0)
    for i in range(n_here):
        ws, m = nxt
        if i + 1 < n_here:
            nxt = scores(i + 1)
        parts = []
        for j in range(LANES // NA_HEAD_DIM):
            t = slice(j * nq, (j + 1) * nq)
            p_loc = jnp.exp2((sl_ref[i % 2, :, t] - m[:, t]).astype(BF16))
            p_ctx = jnp.exp2((sc_ref[i % 2, :, t] - m[:, t]).astype(BF16))
            o_t = _pv_t(vt_ref, ws * GRID_W // slab, p_loc, SUM_ROWS)
            o_t = o_t + _pv_t(vt_ref, n_lat // slab, p_ctx, SUM_ROWS)
            parts.append(o_t[j * NA_HEAD_DIM:(j + 1) * NA_HEAD_DIM, :] * (1.0 / o_t[LANES:LANES + 1, :]))
        o_ref[i * nq:(i + 1) * nq, :] = jnp.concatenate(parts, axis=0).T.astype(BF16)


def _na_bias(rpb, rows):
    h, n_dr, n_dc = rpb.shape
    n_blocks = rows // NA_BLOCK_ROWS
    lead = GRID_W - NA_KW
    wide = 2 * GRID_W
    w = jnp.pad(rpb * LOG2E, ((0, 0), (0, 0), (lead, wide - lead - n_dc)), constant_values=MASK_VALUE)
    flat = jnp.broadcast_to(w[:, :, None, :], (h, n_dr, GRID_W, wide)).reshape(h, n_dr, GRID_W * wide)
    skew = flat[:, :, GRID_W - 1:GRID_W - 1 + GRID_W * (wide - 1)].reshape(h, n_dr, GRID_W, wide - 1)
    qc = jnp.arange(GRID_W)
    cs = jnp.clip(qc - NA_KW // 2, 0, GRID_W - NA_KW)
    col_ok = (qc[None, :] >= cs[:, None]) & (qc[None, :] < cs[:, None] + NA_KW)
    c_t = jnp.swapaxes(jnp.where(col_ok, skew[..., :GRID_W], MASK_VALUE), -1, -2)
    masked = jnp.full((h, GRID_W, GRID_W), MASK_VALUE, F32)
    kinds = []
    for blk in (1, 0, n_blocks - 1):
        ws = _na_window_start(blk, rows, np)
        strips = []
        for j in range(NA_WIN_ROWS):
            kr = ws + j
            blocks = []
            for i in range(NA_BLOCK_ROWS):
                r = NA_BLOCK_ROWS * blk + i
                rs = min(max(r - NA_KH // 2, 0), rows - NA_KH)
                blocks.append(c_t[:, kr - r + NA_KH - 1] if rs <= kr < rs + NA_KH else masked)
            strips.append(jnp.concatenate(blocks, axis=-1))
        kinds.append(jnp.concatenate(strips, axis=1))
    bias = jnp.stack(kinds, axis=1)
    sub = LANES // NA_HEAD_DIM
    bias = bias.reshape(h // sub, sub, len(kinds), NA_WIN_ROWS * GRID_W, NA_BLOCK_ROWS * GRID_W)
    return jnp.concatenate([bias[:, j] for j in range(sub)], axis=-1)


def _na_attention(q, k, vt, bias, *, n_lat, layer):
    b, n, width = q.shape
    n_ctx = n - n_lat
    rows = n_lat // GRID_W
    n_slab, slab = vt.shape[1], vt.shape[3]
    tq = NA_STEP_BLOCKS * NA_BLOCK_ROWS * GRID_W
    kern = functools.partial(_na_kernel, n_lat=n_lat, n_ctx=n_ctx, rows=rows)
    return pl.pallas_call(
        kern,
        grid=(width // LANES, b, n_lat // tq),
        in_specs=[
            pl.BlockSpec((None, tq, LANES), lambda hp, bi, rb: (bi, rb, hp)),
            pl.BlockSpec((None, n, LANES), lambda hp, bi, rb: (bi, 0, hp)),
            pl.BlockSpec((None, n_slab, LANES, slab), lambda hp, bi, rb: (bi, 0, hp, 0)),
            pl.BlockSpec((None,) + bias.shape[1:], lambda hp, bi, rb: (hp, 0, 0, 0)),
        ],
        out_specs=pl.BlockSpec((None, tq, LANES), lambda hp, bi, rb: (bi, rb, hp)),
        out_shape=jax.ShapeDtypeStruct((b, n_lat, width), BF16),
        scratch_shapes=[pltpu.VMEM((2,) + bias.shape[2:], F32), pltpu.VMEM((2, n_ctx, bias.shape[3]), F32)],
        compiler_params=_params("parallel", "parallel", "arbitrary"),
        name="na_attn_" + str(layer),
    )(q, k, vt, bias)


def _out_proj_kernel(o_ref, oc_ref, g_ref, w_ref, x_ref, mod_ref, xo_ref, *, n_lat_tiles):
    o = jnp.where(pl.program_id(1) >= n_lat_tiles, oc_ref[...], o_ref[...])
    a = (o.astype(F32) * g_ref[...].astype(F32)).astype(BF16)
    xo_ref[...] = x_ref[...] + mod_ref[0, 2:3, :] * _dot(a, w_ref[...])


def _out_proj_final_kernel(o_ref, g_ref, w_ref, x_ref, mod_ref, fg_ref, y_ref):
    a = (o_ref[...].astype(F32) * g_ref[...].astype(F32)).astype(BF16)
    xn = x_ref[...] + mod_ref[0, 2:3, :] * _dot(a, w_ref[...])
    y_ref[...] = xn * _rms(xn) * fg_ref[...]


def _out_proj(o, o_ctx, gs, w_out, xs, mod3, layer, n_lat, final_g=None):
    b, n, d = xs.shape
    tm = ROW_TILE
    n_lat_tiles = n_lat // tm
    ctx_row = b

    def mod_idx(bi, t):
        return (layer * MOD_ROWS + jnp.where(t < n_lat_tiles, bi, ctx_row), 0, 0)

    row = lambda bi, t: (bi, t, 0)
    w = o.shape[2]
    o_spec = pl.BlockSpec((None, tm, w), lambda bi, t: (bi, jnp.minimum(t, n_lat_tiles - 1), 0))
    tail = [
        pl.BlockSpec((None, tm, w), row),
        pl.BlockSpec(w_out.shape, lambda bi, t: (0, 0)),
        pl.BlockSpec((None, tm, d), row),
        pl.BlockSpec((1, 3, d), mod_idx),
    ]
    if final_g is None:
        return pl.pallas_call(
            functools.partial(_out_proj_kernel, n_lat_tiles=n_lat_tiles),
            grid=(b, n // tm),
            in_specs=[o_spec, pl.BlockSpec((None, tm, w), lambda bi, t: (bi, 0, 0))] + tail,
            out_specs=pl.BlockSpec((None, tm, d), row),
            out_shape=jax.ShapeDtypeStruct(xs.shape, F32),
            input_output_aliases={4: 0},
            compiler_params=_params("parallel", "parallel"),
            name="out_proj_" + str(layer),
        )(o, o_ctx, gs, w_out, xs, mod3)
    return pl.pallas_call(
        _out_proj_final_kernel,
        grid=(b, n_lat_tiles),
        in_specs=[o_spec] + tail + [pl.BlockSpec((1, d), lambda bi, t: (0, 0))],
        out_specs=pl.BlockSpec((None, tm, d), row),
        out_shape=jax.ShapeDtypeStruct((b, n_lat, d), F32),
        compiler_params=_params("parallel", "parallel"),
        name="out_proj_final",
    )(o, gs, w_out, xs, mod3, final_g.reshape(1, d))


def _rope_tables(n_lat, n_ctx, rot_dim):
    n_freq = rot_dim // 4
    inv = ROPE_THETA ** (-jnp.arange(n_freq, dtype=F32) / n_freq)
    t = jnp.arange(n_lat)
    ang_r = (t // GRID_W).astype(F32)[:, None] * inv
    ang_c = (t % GRID_W).astype(F32)[:, None] * inv
    ang = jnp.concatenate([ang_r, ang_r, ang_c, ang_c], axis=-1)
    cos, sin = jnp.cos(ang), jnp.sin(ang)
    first = (jnp.arange(rot_dim) % (2 * n_freq)) < n_freq
    sin_up = jnp.where(first, -sin, 0.0)
    sin_dn = jnp.where(first, 0.0, sin)
    pad = LANES - rot_dim

    def finish(tab, fill):
        tab = jnp.pad(tab, ((0, 0), (0, pad)), constant_values=fill)
        return jnp.pad(tab, ((0, n_ctx), (0, 0)), constant_values=fill)

    return finish(cos, 1.0), finish(sin_up, 0.0), finish(sin_dn, 0.0)


def _mla_weights(w_in, w_uq, w_ukv):
    o_kv = MLA_Q_LORA
    o_kr = o_kv + MLA_KV_LORA
    o_g = o_kr + MLA_ROPE
    k_r = jnp.pad(w_in[:, o_kr:o_g], ((0, 0), (0, LANES - MLA_ROPE)))
    w_perm = jnp.concatenate([w_in[:, :o_kr], w_in[:, o_g:], k_r], axis=1).astype(BF16)
    uq = w_uq.reshape(MLA_Q_LORA, MLA_HEADS, MLA_NOPE + MLA_ROPE)
    uq = jnp.pad(uq, ((0, 0), (0, 0), (0, MLA_QK_PAD - MLA_NOPE - MLA_ROPE)))
    uq = uq.reshape(MLA_Q_LORA, MLA_HEADS * MLA_QK_PAD).astype(BF16)
    ukv = w_ukv.reshape(MLA_KV_LORA, MLA_HEADS, MLA_NOPE + MLA_V)
    k_top = jnp.pad(ukv[:, :, :MLA_NOPE], ((0, 0), (0, 0), (0, MLA_QK_PAD - MLA_NOPE)))
    eye = jnp.pad(jnp.eye(MLA_ROPE, dtype=F32), ((0, LANES - MLA_ROPE), (MLA_NOPE, MLA_QK_PAD - MLA_NOPE - MLA_ROPE)))
    k_bot = jnp.broadcast_to(eye[:, None, :], (LANES, MLA_HEADS, MLA_QK_PAD))
    k_aug = jnp.concatenate([k_top, k_bot], axis=0).reshape(MLA_KV_LORA + LANES, MLA_HEADS * MLA_QK_PAD)
    v_aug = jnp.pad(ukv[:, :, MLA_NOPE:].reshape(MLA_KV_LORA, MLA_HEADS * MLA_V), ((0, LANES), (0, 0)))
    return w_perm, uq, jnp.concatenate([k_aug, v_aug], axis=1).astype(BF16)


def kernel(x, c, ctx, c_ctx, mod_w, mod_b, norm_g, final_g, ga_w_in, ga_q_g, ga_k_g, ga_w_out, na_w_in, na_rpb, na_w_out, mla_w_in, mla_q_g, mla_kv_g, mla_w_uq, mla_w_ukv, mla_w_out):
    b, n_lat, d = x.shape
    n_ctx = ctx.shape[1]
    depth = mod_w.shape[0]
    assert n_lat % ROW_TILE == 0 and n_ctx == ROW_TILE and n_lat % n_ctx == 0
    assert b < MOD_ROWS and n_lat % (NA_STEP_BLOCKS * NA_BLOCK_ROWS * GRID_W) == 0
    assert n_lat // GRID_W >= NA_WIN_ROWS + NA_BLOCK_ROWS

    xs = jnp.concatenate([x, ctx], axis=1)
    cc = jnp.concatenate([c, c_ctx[None, :], jnp.zeros((MOD_ROWS - b - 1, d), F32)], axis=0)
    mod3 = _modulation(cc, mod_w, mod_b).reshape(depth * MOD_ROWS, 3, d)

    tab_a = _rope_tables(n_lat, n_ctx, GQA_HEAD_DIM)
    tab_m = _rope_tables(n_lat, n_ctx, MLA_ROPE)

    out = None
    for i in range(depth):
        kind, j = i % N_MIXERS, i // N_MIXERS
        need_ctx = i < depth - 1
        if kind == 0:
            n_q = d // GQA_HEAD_DIM
            n_kv = n_q // GQA_GROUP
            kern = functools.partial(_gqa_proj_kernel, n_q=n_q, n_kv=n_kv, scale=GQA_HEAD_DIM ** -0.5 * LOG2E)
            consts = [ga_w_in[j].astype(BF16), ga_q_g[j].reshape(1, -1), ga_k_g[j].reshape(1, -1)]
            kvw = n_kv * GQA_HEAD_DIM
            q, k, vt, gs = _project(kern, xs, mod3, i, norm_g[i], consts, tab_a, [d, kvw, kvw, d], n_lat)
            o = _flash_attention(q, k, vt, group=GQA_GROUP, dk=GQA_HEAD_DIM, n_lat=n_lat, layer=i)
            if need_ctx:
                oc = _ctx_attention(q, k, vt, group=GQA_GROUP, dk=GQA_HEAD_DIM, n_lat=n_lat, layer=i)
            w_out = ga_w_out[j]
        elif kind == 1:
            kern = functools.partial(_na_proj_kernel, width=d, scale=NA_HEAD_DIM ** -0.5 * LOG2E)
            q, k, vt, gs = _project(kern, xs, mod3, i, norm_g[i], [na_w_in[j].astype(BF16)], (), [d, d, d, d], n_lat)
            o = _na_attention(q, k, vt, _na_bias(na_rpb[j], n_lat // GRID_W), n_lat=n_lat, layer=i)
            if need_ctx:
                oc = _na_ctx_attention(q, k, vt, n_lat=n_lat, layer=i)
            w_out = na_w_out[j]
        else:
            kern = functools.partial(_mla_proj_kernel, width=d, scale=(MLA_NOPE + MLA_ROPE) ** -0.5 * LOG2E)
            w_perm, uq, wkv = _mla_weights(mla_w_in[j], mla_w_uq[j], mla_w_ukv[j])
            consts = [w_perm, mla_q_g[j].reshape(1, -1), mla_kv_g[j].reshape(1, -1), uq, wkv]
            widths = [MLA_HEADS * MLA_QK_PAD, MLA_HEADS * MLA_QK_PAD, MLA_HEADS * MLA_V, d]
            q, k, vt, gs = _project(kern, xs, mod3, i, norm_g[i], consts, tab_m, widths, n_lat)
            o = _flash_attention(q, k, vt, group=1, dk=MLA_QK_PAD, n_lat=n_lat, layer=i)
            if need_ctx:
                oc = _ctx_attention(q, k, vt, group=1, dk=MLA_QK_PAD, n_lat=n_lat, layer=i)
            w_out = mla_w_out[j]
        if need_ctx:
            xs = _out_proj(o, oc, gs, w_out.astype(BF16), xs, mod3, i, n_lat)
        else:
            out = _out_proj(o, None, gs, w_out.astype(BF16), xs, mod3, i, n_lat, final_g=final_g)
    return out
```

```python
import functools

import jax
import numpy as np
import jax.numpy as jnp
from jax import lax
from jax.experimental import pallas as pl
from jax.experimental.pallas import tpu as pltpu

F32 = jnp.float32
BF16 = jnp.bfloat16

NORM_EPS = 1e-6
ROPE_THETA = 10000.0
GRID_W = 64
N_MIXERS = 3
LANES = 128
MOD_ROWS = 8
MASK_VALUE = -1e30
LOG2E = 1.4426950408889634

GQA_HEAD_DIM = 128
GQA_GROUP = 4
NA_HEAD_DIM = 64
NA_KH = 8
NA_KW = 16
NA_BLOCK_ROWS = 4
NA_WIN_ROWS = 12
NA_STEP_BLOCKS = 4
MLA_HEADS = 8
MLA_Q_LORA = 512
MLA_KV_LORA = 256
MLA_NOPE = 128
MLA_ROPE = 64
MLA_V = 128
MLA_QK_PAD = 256

ROW_TILE = 256
Q_COLS = 1024
Q_TILE = 256
SUM_ROWS = 16
SKEW_LANES = 128
KV_CHUNKS = (768, 512, 256)
VMEM_LIMIT = 48 * 1024 * 1024


def _params(*sem):
    return pltpu.CompilerParams(dimension_semantics=sem, vmem_limit_bytes=VMEM_LIMIT)


def _silu(v):
    return v * (1.0 / (1.0 + jnp.exp(-v)))


def _rms(v):
    return lax.rsqrt(jnp.mean(v * v, axis=-1, keepdims=True) + NORM_EPS)


def _rope(v, cos, sin_up, sin_dn, half):
    w = v.shape[-1]
    return v * cos + pltpu.roll(v, w - half, 1) * sin_up + pltpu.roll(v, half, 1) * sin_dn


def _dot(a, b):
    return jnp.dot(a, b, preferred_element_type=F32)


def _dot_nt(a, b):
    return lax.dot_general(a, b, (((1,), (1,)), ((), ())), preferred_element_type=F32)


def _mod_kernel(c_ref, w_ref, b_ref, o_ref):
    a = _silu(c_ref[...])
    w = w_ref[0]
    a_hi = a.astype(BF16)
    a_lo = (a - a_hi.astype(F32)).astype(BF16)
    w_hi = w.astype(BF16)
    w_lo = (w - w_hi.astype(F32)).astype(BF16)
    o_ref[0] = _dot(a_hi, w_hi) + _dot(a_lo, w_hi) + _dot(a_hi, w_lo) + b_ref[0]


def _modulation(cc, mod_w, mod_b):
    depth, d, n3 = mod_w.shape
    tn = 1024
    return pl.pallas_call(
        _mod_kernel,
        grid=(depth, n3 // tn),
        in_specs=[
            pl.BlockSpec((MOD_ROWS, d), lambda l, j: (0, 0)),
            pl.BlockSpec((1, d, tn), lambda l, j: (l, 0, j)),
            pl.BlockSpec((1, 1, tn), lambda l, j: (l, 0, j)),
        ],
        out_specs=pl.BlockSpec((1, MOD_ROWS, tn), lambda l, j: (l, 0, j)),
        out_shape=jax.ShapeDtypeStruct((depth, MOD_ROWS, n3), F32),
        compiler_params=_params("parallel", "parallel"),
        name="adaln_modulation",
    )(cc, mod_w, mod_b.reshape(depth, 1, n3))


def _modulated_norm(x_ref, ng_ref, mod_ref):
    x = x_ref[...]
    shift = mod_ref[0, 0:1, :]
    scale = mod_ref[0, 1:2, :]
    return ((x * _rms(x)) * ng_ref[...] * (1.0 + scale) + shift).astype(BF16)


def _gqa_proj_kernel(x_ref, mod_ref, ng_ref, w_ref, qg_ref, kg_ref, cos_ref, su_ref, sd_ref,
                     q_ref, k_ref, vt_ref, g_ref, *, n_q, n_kv, scale):
    h = _modulated_norm(x_ref, ng_ref, mod_ref)
    cos, su, sd = cos_ref[...], su_ref[...], sd_ref[...]
    hd = GQA_HEAD_DIM
    qw, kw = n_q * hd, n_kv * hd
    q = _dot(h, w_ref[:, :qw])
    for i in range(n_q):
        qh = q[:, i * hd:(i + 1) * hd]
        qh = qh * _rms(qh) * qg_ref[...]
        q_ref[:, i * hd:(i + 1) * hd] = (_rope(qh, cos, su, sd, hd // 4) * scale).astype(BF16)
    k = _dot(h, w_ref[:, qw:qw + kw])
    for i in range(n_kv):
        kh = k[:, i * hd:(i + 1) * hd]
        kh = kh * _rms(kh) * kg_ref[...]
        k_ref[:, i * hd:(i + 1) * hd] = _rope(kh, cos, su, sd, hd // 4).astype(BF16)
    v = _dot(h, w_ref[:, qw + kw:qw + 2 * kw])
    for i in range(n_kv):
        vt_ref[i * hd:(i + 1) * hd, :] = v[:, i * hd:(i + 1) * hd].T.astype(BF16)
    g_ref[...] = _silu(_dot(h, w_ref[:, qw + 2 * kw:])).astype(BF16)


def _na_proj_kernel(x_ref, mod_ref, ng_ref, w_ref, q_ref, k_ref, vt_ref, g_ref, *, width, scale):
    h = _modulated_norm(x_ref, ng_ref, mod_ref)
    q_ref[...] = (_dot(h, w_ref[:, :width]) * scale).astype(BF16)
    k_ref[...] = _dot(h, w_ref[:, width:2 * width]).astype(BF16)
    v = _dot(h, w_ref[:, 2 * width:3 * width])
    for i in range(width // LANES):
        vt_ref[i * LANES:(i + 1) * LANES, :] = v[:, i * LANES:(i + 1) * LANES].T.astype(BF16)
    g_ref[...] = _silu(_dot(h, w_ref[:, 3 * width:])).astype(BF16)


def _mla_proj_kernel(x_ref, mod_ref, ng_ref, w_ref, qg_ref, kvg_ref, wuq_ref, wkv_ref,
                     cos_ref, su_ref, sd_ref, q_ref, k_ref, vt_ref, g_ref, *, width, scale):
    h = _modulated_norm(x_ref, ng_ref, mod_ref)
    cos, su, sd = cos_ref[...], su_ref[...], sd_ref[...]
    o_kv = MLA_Q_LORA
    o_g = o_kv + MLA_KV_LORA
    o_kr = o_g + width
    c_q = _dot(h, w_ref[:, :o_kv])
    c_q = (c_q * _rms(c_q) * qg_ref[...]).astype(BF16)
    q = _dot(c_q, wuq_ref[...])
    for i in range(MLA_HEADS):
        a = i * MLA_QK_PAD
        q_ref[:, a:a + MLA_NOPE] = (q[:, a:a + MLA_NOPE] * scale).astype(BF16)
        q_ref[:, a + MLA_NOPE:a + MLA_QK_PAD] = (
            _rope(q[:, a + MLA_NOPE:a + MLA_QK_PAD], cos, su, sd, MLA_ROPE // 4) * scale).astype(BF16)
    c_kv = _dot(h, w_ref[:, o_kv:o_g])
    c_kv = (c_kv * _rms(c_kv) * kvg_ref[...]).astype(BF16)
    k_r = _rope(_dot(h, w_ref[:, o_kr:]), cos, su, sd, MLA_ROPE // 4).astype(BF16)
    kv_in = jnp.concatenate([c_kv, k_r], axis=1)
    kw = MLA_HEADS * MLA_QK_PAD
    k_ref[...] = _dot(kv_in, wkv_ref[:, :kw]).astype(BF16)
    v = _dot(kv_in, wkv_ref[:, kw:])
    for i in range(MLA_HEADS):
        vt_ref[i * MLA_V:(i + 1) * MLA_V, :] = v[:, i * MLA_V:(i + 1) * MLA_V].T.astype(BF16)
    g_ref[...] = _silu(_dot(h, w_ref[:, o_g:o_kr])).astype(BF16)


def _project(kern, xs, mod3, layer, norm_g, consts, tables, out_widths, n_lat):
    b, n, d = xs.shape
    tm = ROW_TILE
    n_lat_tiles = n_lat // tm
    ctx_row = b

    def mod_idx(bi, t):
        return (layer * MOD_ROWS + jnp.where(t < n_lat_tiles, bi, ctx_row), 0, 0)

    in_specs = [
        pl.BlockSpec((None, tm, d), lambda bi, t: (bi, t, 0)),
        pl.BlockSpec((1, 3, d), mod_idx),
        pl.BlockSpec((1, d), lambda bi, t: (0, 0)),
    ]
    in_specs += [pl.BlockSpec(a.shape, lambda bi, t: (0, 0)) for a in consts]
    in_specs += [pl.BlockSpec((tm, LANES), lambda bi, t: (t, 0)) for _ in tables]
    out_specs = [pl.BlockSpec((None, tm, w), lambda bi, t: (bi, t, 0)) for w in out_widths]
    out_shape = [jax.ShapeDtypeStruct((b, n, w), BF16) for w in out_widths]
    out_specs[2] = pl.BlockSpec((None, None, out_widths[2], tm), lambda bi, t: (bi, t, 0, 0))
    out_shape[2] = jax.ShapeDtypeStruct((b, n // tm, out_widths[2], tm), BF16)
    return pl.pallas_call(
        kern,
        grid=(b, n // tm),
        in_specs=in_specs,
        out_specs=out_specs,
        out_shape=out_shape,
        compiler_params=_params("parallel", "parallel"),
        name="proj_" + str(layer),
    )(xs, mod3, norm_g.reshape(1, d), *consts, *tables)


def _stack_heads(q_ref, group, dk):
    return jnp.concatenate([q_ref[:, g * dk:(g + 1) * dk] for g in range(group)], axis=0)


def _unstack_heads(o_t, o_ref, group):
    tq = o_ref.shape[0]
    for g in range(group):
        o_ref[:, g * LANES:(g + 1) * LANES] = o_t[:, g * tq:(g + 1) * tq].T.astype(BF16)


def _pv_t(vt_ref, first_slab, p, sum_rows=0):
    slab = vt_ref.shape[-1]
    acc = None
    for i in range(p.shape[0] // slab):
        vt = vt_ref[first_slab + i]
        if sum_rows:
            vt = jnp.concatenate([vt, jnp.ones((sum_rows, slab), vt.dtype)], axis=0)
        t = _dot(vt, p[i * slab:(i + 1) * slab, :])
        acc = t if acc is None else acc + t
    return acc


def _flash_kernel(q_ref, k_ref, vt_ref, o_ref, sa_ref, sb_ref, acc_ref, m_ref, *, group, dk, bk):
    slab = vt_ref.shape[-1]
    n_chunks = k_ref.shape[0] // bk
    q = _stack_heads(q_ref, group, dk)
    tiles = [slice(j * Q_TILE, (j + 1) * Q_TILE) for j in range(q.shape[0] // Q_TILE)]

    def scores(c, s_ref):
        k = k_ref[pl.ds(pl.multiple_of(c * bk, bk), bk), :]
        out = []
        for t in tiles:
            s = _dot_nt(k, q[t, :])
            s_ref[:, t] = s
            out.append(jnp.max(s, axis=0, keepdims=True))
        return out

    def update(s_ref, s_max, c, t):
        m = m_ref[:, t]
        m_new = jnp.maximum(m, s_max)
        alpha = jnp.exp2(m - m_new)
        p = jnp.exp2((s_ref[:, t] - m_new).astype(BF16))
        m_ref[:, t] = m_new
        acc_ref[:, t] = alpha * acc_ref[:, t] + _pv_t(vt_ref, c * (bk // slab), p, SUM_ROWS)

    def step(c, cur_ref, cur_max, nxt_ref):
        k = k_ref[pl.ds(pl.multiple_of((c + 1) * bk, bk), bk), :]
        nxt_max = []
        for j, t in enumerate(tiles):
            s = _dot_nt(k, q[t, :])
            nxt_ref[:, t] = s
            nxt_max.append(jnp.max(s, axis=0, keepdims=True))
            update(cur_ref, cur_max[j], c, t)
        return nxt_max

    m_ref[...] = jnp.full(m_ref.shape, MASK_VALUE, F32)
    acc_ref[...] = jnp.zeros(acc_ref.shape, F32)
    max_a = scores(0, sa_ref)

    def body(i, max_a):
        max_b = step(2 * i, sa_ref, max_a, sb_ref)
        return tuple(step(2 * i + 1, sb_ref, max_b, sa_ref))

    pairs = (n_chunks - 1) // 2
    max_a = lax.fori_loop(0, pairs, body, tuple(max_a))
    last = 2 * pairs
    if n_chunks - last == 2:
        max_b = step(last, sa_ref, max_a, sb_ref)
        last, s_ref, s_max = last + 1, sb_ref, max_b
    else:
        s_ref, s_max = sa_ref, max_a
    for j, t in enumerate(tiles):
        update(s_ref, s_max[j], last, t)
    _unstack_heads(acc_ref[:LANES, :] * (1.0 / acc_ref[LANES:LANES + 1, :]), o_ref, group)


def _flash_attention(q, k, vt, *, group, dk, n_lat, layer):
    b, n, _ = q.shape
    n_kv = k.shape[2] // dk
    n_slab, slab = vt.shape[1], vt.shape[3]
    tq = min(Q_COLS // group, n_lat)
    bk = max(c for c in KV_CHUNKS if n % c == 0)
    assert bk % slab == 0
    kern = functools.partial(_flash_kernel, group=group, dk=dk, bk=bk)
    nq = group * tq
    s_buf = pltpu.VMEM((bk, nq + SKEW_LANES), F32)
    return pl.pallas_call(
        kern,
        grid=(b, n_kv, n_lat // tq),
        in_specs=[
            pl.BlockSpec((None, tq, group * dk), lambda bi, h, t: (bi, t, h)),
            pl.BlockSpec((None, n, dk), lambda bi, h, t: (bi, 0, h)),
            pl.BlockSpec((None, n_slab, LANES, slab), lambda bi, h, t: (bi, 0, h, 0)),
        ],
        out_specs=pl.BlockSpec((None, tq, group * LANES), lambda bi, h, t: (bi, t, h)),
        out_shape=jax.ShapeDtypeStruct((b, n_lat, n_kv * group * LANES), BF16),
        scratch_shapes=[s_buf, s_buf, pltpu.VMEM((LANES + SUM_ROWS, nq), F32), pltpu.VMEM((1, nq), F32)],
        compiler_params=_params("parallel", "parallel", "parallel"),
        name="flash_" + str(layer),
    )(q, k, vt)


def _ctx_attn_kernel(q_ref, k_ref, vt_ref, o_ref, *, group, dk):
    q = _stack_heads(q_ref, group, dk)
    s = _dot_nt(k_ref[...], q)
    p = jnp.exp2(s - jnp.max(s, axis=0, keepdims=True))
    l = jnp.sum(p, axis=0, keepdims=True)
    _unstack_heads(_dot(vt_ref[...], p.astype(BF16)) * (1.0 / l), o_ref, group)


def _ctx_attention(q, k, vt, *, group, dk, n_lat, layer):
    b, n, _ = q.shape
    n_ctx = n - n_lat
    n_kv = k.shape[2] // dk
    slab = vt.shape[3]
    assert slab == n_ctx
    t = n_lat // n_ctx
    return pl.pallas_call(
        functools.partial(_ctx_attn_kernel, group=group, dk=dk),
        grid=(b, n_kv),
        in_specs=[
            pl.BlockSpec((None, n_ctx, group * dk), lambda bi, h: (bi, t, h)),
            pl.BlockSpec((None, n_ctx, dk), lambda bi, h: (bi, t, h)),
            pl.BlockSpec((None, None, LANES, slab), lambda bi, h: (bi, t, h, 0)),
        ],
        out_specs=pl.BlockSpec((None, n_ctx, group * LANES), lambda bi, h: (bi, 0, h)),
        out_shape=jax.ShapeDtypeStruct((b, n_ctx, n_kv * group * LANES), BF16),
        compiler_params=_params("parallel", "parallel"),
        name="ctx_attn_" + str(layer),
    )(q, k, vt)


def _split_heads(q):
    lane = lax.broadcasted_iota(jnp.int32, (1, LANES), 1)
    zero = jnp.zeros_like(q)
    return jnp.concatenate(
        [jnp.where((lane >= j * NA_HEAD_DIM) & (lane < (j + 1) * NA_HEAD_DIM), q, zero)
         for j in range(LANES // NA_HEAD_DIM)], axis=0)


def _merge_heads(o_t, nq):
    parts = [o_t[j * NA_HEAD_DIM:(j + 1) * NA_HEAD_DIM, j * nq:(j + 1) * nq]
             for j in range(LANES // NA_HEAD_DIM)]
    return jnp.concatenate(parts, axis=0).T


def _na_ctx_kernel(q_ref, k_ref, vt_ref, o_ref):
    q2 = _split_heads(q_ref[...])
    s = _dot_nt(k_ref[...], q2)
    p = jnp.exp2(s - jnp.max(s, axis=0, keepdims=True))
    l = jnp.sum(p, axis=0, keepdims=True)
    o_t = _dot(vt_ref[...], p.astype(BF16)) * (1.0 / l)
    o_ref[...] = _merge_heads(o_t, q_ref.shape[0]).astype(BF16)


def _na_ctx_attention(q, k, vt, *, n_lat, layer):
    b, n, width = q.shape
    n_ctx = n - n_lat
    assert vt.shape[3] == n_ctx
    t = n_lat // n_ctx
    blk = pl.BlockSpec((None, n_ctx, LANES), lambda bi, h: (bi, t, h))
    return pl.pallas_call(
        _na_ctx_kernel,
        grid=(b, width // LANES),
        in_specs=[blk, blk, pl.BlockSpec((None, None, LANES, n_ctx), lambda bi, h: (bi, t, h, 0))],
        out_specs=pl.BlockSpec((None, n_ctx, LANES), lambda bi, h: (bi, 0, h)),
        out_shape=jax.ShapeDtypeStruct((b, n_ctx, width), BF16),
        compiler_params=_params("parallel", "parallel"),
        name="ctx_attn_" + str(layer),
    )(q, k, vt)


def _na_window_start(blk, rows, lib):
    lo = lib.minimum(lib.maximum(NA_BLOCK_ROWS * blk - NA_KH // 2, 0), rows - NA_KH)
    return lib.minimum(lo, rows - NA_WIN_ROWS)


def _na_kernel(q_ref, k_ref, vt_ref, bias_ref, o_ref, sl_ref, sc_ref, *, n_lat, n_ctx, rows):
    slab = vt_ref.shape[-1]
    nq = NA_BLOCK_ROWS * GRID_W
    n_blocks = rows // NA_BLOCK_ROWS
    nk = NA_WIN_ROWS * GRID_W
    kc = k_ref[n_lat:n_lat + n_ctx, :]
    n_here = q_ref.shape[0] // nq

    def scores(i):
        blk = pl.program_id(2) * NA_STEP_BLOCKS + i
        kind = jnp.where(blk == 0, 1, jnp.where(blk == n_blocks - 1, 2, 0))
        ws = _na_window_start(blk, rows, jnp)
        q2 = _split_heads(q_ref[i * nq:(i + 1) * nq, :])
        s_loc = _dot_nt(k_ref[pl.ds(pl.multiple_of(ws * GRID_W, slab), nk), :], q2) + bias_ref[kind]
        s_ctx = _dot_nt(kc, q2)
        sl_ref[i % 2, :, :s_loc.shape[1]] = s_loc
        sc_ref[i % 2, :, :s_ctx.shape[1]] = s_ctx
        return ws, jnp.maximum(jnp.max(s_loc, axis=0, keepdims=True), jnp.max(s_ctx, axis=0, keepdims=True))

    nxt = scores(0)
    for i in range(n_here):
        ws, m = nxt
        if i + 1 < n_here:
            nxt = scores(i + 1)
        parts = []
        for j in range(LANES // NA_HEAD_DIM):
            t = slice(j * nq, (j + 1) * nq)
            p_loc = jnp.exp2((sl_ref[i % 2, :, t] - m[:, t]).astype(BF16))
            p_ctx = jnp.exp2((sc_ref[i % 2, :, t] - m[:, t]).astype(BF16))
            o_t = _pv_t(vt_ref, ws * GRID_W // slab, p_loc, SUM_ROWS)
            o_t = o_t + _pv_t(vt_ref, n_lat // slab, p_ctx, SUM_ROWS)
            parts.append(o_t[j * NA_HEAD_DIM:(j + 1) * NA_HEAD_DIM, :] * (1.0 / o_t[LANES:LANES + 1, :]))
        o_ref[i * nq:(i + 1) * nq, :] = jnp.concatenate(parts, axis=0).T.astype(BF16)


def _na_bias(rpb, rows):
    h, n_dr, n_dc = rpb.shape
    n_blocks = rows // NA_BLOCK_ROWS
    lead = GRID_W - NA_KW
    wide = 2 * GRID_W
    w = jnp.pad(rpb * LOG2E, ((0, 0), (0, 0), (lead, wide - lead - n_dc)), constant_values=MASK_VALUE)
    flat = jnp.broadcast_to(w[:, :, None, :], (h, n_dr, GRID_W, wide)).reshape(h, n_dr, GRID_W * wide)
    skew = flat[:, :, GRID_W - 1:GRID_W - 1 + GRID_W * (wide - 1)].reshape(h, n_dr, GRID_W, wide - 1)
    qc = jnp.arange(GRID_W)
    cs = jnp.clip(qc - NA_KW // 2, 0, GRID_W - NA_KW)
    col_ok = (qc[None, :] >= cs[:, None]) & (qc[None, :] < cs[:, None] + NA_KW)
    c_t = jnp.swapaxes(jnp.where(col_ok, skew[..., :GRID_W], MASK_VALUE), -1, -2)
    masked = jnp.full((h, GRID_W, GRID_W), MASK_VALUE, F32)
    kinds = []
    for blk in (1, 0, n_blocks - 1):
        ws = _na_window_start(blk, rows, np)
        strips = []
        for j in range(NA_WIN_ROWS):
            kr = ws + j
            blocks = []
            for i in range(NA_BLOCK_ROWS):
                r = NA_BLOCK_ROWS * blk + i
                rs = min(max(r - NA_KH // 2, 0), rows - NA_KH)
                blocks.append(c_t[:, kr - r + NA_KH - 1] if rs <= kr < rs + NA_KH else masked)
            strips.append(jnp.concatenate(blocks, axis=-1))
        kinds.append(jnp.concatenate(strips, axis=1))
    bias = jnp.stack(kinds, axis=1)
    sub = LANES // NA_HEAD_DIM
    bias = bias.reshape(h // sub, sub, len(kinds), NA_WIN_ROWS * GRID_W, NA_BLOCK_ROWS * GRID_W)
    return jnp.concatenate([bias[:, j] for j in range(sub)], axis=-1)


def _na_attention(q, k, vt, bias, *, n_lat, layer):
    b, n, width = q.shape
    n_ctx = n - n_lat
    rows = n_lat // GRID_W
    n_slab, slab = vt.shape[1], vt.shape[3]
    tq = NA_STEP_BLOCKS * NA_BLOCK_ROWS * GRID_W
    kern = functools.partial(_na_kernel, n_lat=n_lat, n_ctx=n_ctx, rows=rows)
    return pl.pallas_call(
        kern,
        grid=(width // LANES, b, n_lat // tq),
        in_specs=[
            pl.BlockSpec((None, tq, LANES), lambda hp, bi, rb: (bi, rb, hp)),
            pl.BlockSpec((None, n, LANES), lambda hp, bi, rb: (bi, 0, hp)),
            pl.BlockSpec((None, n_slab, LANES, slab), lambda hp, bi, rb: (bi, 0, hp, 0)),
            pl.BlockSpec((None,) + bias.shape[1:], lambda hp, bi, rb: (hp, 0, 0, 0)),
        ],
        out_specs=pl.BlockSpec((None, tq, LANES), lambda hp, bi, rb: (bi, rb, hp)),
        out_shape=jax.ShapeDtypeStruct((b, n_lat, width), BF16),
        scratch_shapes=[pltpu.VMEM((2, bias.shape[2], bias.shape[3] + SKEW_LANES), F32),
                        pltpu.VMEM((2, n_ctx, bias.shape[3] + SKEW_LANES), F32)],
        compiler_params=_params("parallel", "parallel", "arbitrary"),
        name="na_attn_" + str(layer),
    )(q, k, vt, bias)


def _out_proj_kernel(o_ref, oc_ref, g_ref, w_ref, x_ref, mod_ref, xo_ref, *, n_lat_tiles):
    o = jnp.where(pl.program_id(1) >= n_lat_tiles, oc_ref[...], o_ref[...])
    a = (o.astype(F32) * g_ref[...].astype(F32)).astype(BF16)
    xo_ref[...] = x_ref[...] + mod_ref[0, 2:3, :] * _dot(a, w_ref[...])


def _out_proj_final_kernel(o_ref, g_ref, w_ref, x_ref, mod_ref, fg_ref, y_ref):
    a = (o_ref[...].astype(F32) * g_ref[...].astype(F32)).astype(BF16)
    xn = x_ref[...] + mod_ref[0, 2:3, :] * _dot(a, w_ref[...])
    y_ref[...] = xn * _rms(xn) * fg_ref[...]


def _out_proj(o, o_ctx, gs, w_out, xs, mod3, layer, n_lat, final_g=None):
    b, n, d = xs.shape
    tm = ROW_TILE
    n_lat_tiles = n_lat // tm
    ctx_row = b

    def mod_idx(bi, t):
        return (layer * MOD_ROWS + jnp.where(t < n_lat_tiles, bi, ctx_row), 0, 0)

    row = lambda bi, t: (bi, t, 0)
    w = o.shape[2]
    o_spec = pl.BlockSpec((None, tm, w), lambda bi, t: (bi, jnp.minimum(t, n_lat_tiles - 1), 0))
    tail = [
        pl.BlockSpec((None, tm, w), row),
        pl.BlockSpec(w_out.shape, lambda bi, t: (0, 0)),
        pl.BlockSpec((None, tm, d), row),
        pl.BlockSpec((1, 3, d), mod_idx),
    ]
    if final_g is None:
        return pl.pallas_call(
            functools.partial(_out_proj_kernel, n_lat_tiles=n_lat_tiles),
            grid=(b, n // tm),
            in_specs=[o_spec, pl.BlockSpec((None, tm, w), lambda bi, t: (bi, 0, 0))] + tail,
            out_specs=pl.BlockSpec((None, tm, d), row),
            out_shape=jax.ShapeDtypeStruct(xs.shape, F32),
            input_output_aliases={4: 0},
            compiler_params=_params("parallel", "parallel"),
            name="out_proj_" + str(layer),
        )(o, o_ctx, gs, w_out, xs, mod3)
    return pl.pallas_call(
        _out_proj_final_kernel,
        grid=(b, n_lat_tiles),
        in_specs=[o_spec] + tail + [pl.BlockSpec((1, d), lambda bi, t: (0, 0))],
        out_specs=pl.BlockSpec((None, tm, d), row),
        out_shape=jax.ShapeDtypeStruct((b, n_lat, d), F32),
        compiler_params=_params("parallel", "parallel"),
        name="out_proj_final",
    )(o, gs, w_out, xs, mod3, final_g.reshape(1, d))


def _rope_tables(n_lat, n_ctx, rot_dim):
    n_freq = rot_dim // 4
    inv = ROPE_THETA ** (-jnp.arange(n_freq, dtype=F32) / n_freq)
    t = jnp.arange(n_lat)
    ang_r = (t // GRID_W).astype(F32)[:, None] * inv
    ang_c = (t % GRID_W).astype(F32)[:, None] * inv
    ang = jnp.concatenate([ang_r, ang_r, ang_c, ang_c], axis=-1)
    cos, sin = jnp.cos(ang), jnp.sin(ang)
    first = (jnp.arange(rot_dim) % (2 * n_freq)) < n_freq
    sin_up = jnp.where(first, -sin, 0.0)
    sin_dn = jnp.where(first, 0.0, sin)
    pad = LANES - rot_dim

    def finish(tab, fill):
        tab = jnp.pad(tab, ((0, 0), (0, pad)), constant_values=fill)
        return jnp.pad(tab, ((0, n_ctx), (0, 0)), constant_values=fill)

    return finish(cos, 1.0), finish(sin_up, 0.0), finish(sin_dn, 0.0)


def _mla_weights(w_in, w_uq, w_ukv):
    o_kv = MLA_Q_LORA
    o_kr = o_kv + MLA_KV_LORA
    o_g = o_kr + MLA_ROPE
    k_r = jnp.pad(w_in[:, o_kr:o_g], ((0, 0), (0, LANES - MLA_ROPE)))
    w_perm = jnp.concatenate([w_in[:, :o_kr], w_in[:, o_g:], k_r], axis=1).astype(BF16)
    uq = w_uq.reshape(MLA_Q_LORA, MLA_HEADS, MLA_NOPE + MLA_ROPE)
    uq = jnp.pad(uq, ((0, 0), (0, 0), (0, MLA_QK_PAD - MLA_NOPE - MLA_ROPE)))
    uq = uq.reshape(MLA_Q_LORA, MLA_HEADS * MLA_QK_PAD).astype(BF16)
    ukv = w_ukv.reshape(MLA_KV_LORA, MLA_HEADS, MLA_NOPE + MLA_V)
    k_top = jnp.pad(ukv[:, :, :MLA_NOPE], ((0, 0), (0, 0), (0, MLA_QK_PAD - MLA_NOPE)))
    eye = jnp.pad(jnp.eye(MLA_ROPE, dtype=F32), ((0, LANES - MLA_ROPE), (MLA_NOPE, MLA_QK_PAD - MLA_NOPE - MLA_ROPE)))
    k_bot = jnp.broadcast_to(eye[:, None, :], (LANES, MLA_HEADS, MLA_QK_PAD))
    k_aug = jnp.concatenate([k_top, k_bot], axis=0).reshape(MLA_KV_LORA + LANES, MLA_HEADS * MLA_QK_PAD)
    v_aug = jnp.pad(ukv[:, :, MLA_NOPE:].reshape(MLA_KV_LORA, MLA_HEADS * MLA_V), ((0, LANES), (0, 0)))
    return w_perm, uq, jnp.concatenate([k_aug, v_aug], axis=1).astype(BF16)


def kernel(x, c, ctx, c_ctx, mod_w, mod_b, norm_g, final_g, ga_w_in, ga_q_g, ga_k_g, ga_w_out, na_w_in, na_rpb, na_w_out, mla_w_in, mla_q_g, mla_kv_g, mla_w_uq, mla_w_ukv, mla_w_out):
    b, n_lat, d = x.shape
    n_ctx = ctx.shape[1]
    depth = mod_w.shape[0]
    assert n_lat % ROW_TILE == 0 and n_ctx == ROW_TILE and n_lat % n_ctx == 0
    assert b < MOD_ROWS and n_lat % (NA_STEP_BLOCKS * NA_BLOCK_ROWS * GRID_W) == 0
    assert n_lat // GRID_W >= NA_WIN_ROWS + NA_BLOCK_ROWS

    xs = jnp.concatenate([x, ctx], axis=1)
    cc = jnp.concatenate([c, c_ctx[None, :], jnp.zeros((MOD_ROWS - b - 1, d), F32)], axis=0)
    mod3 = _modulation(cc, mod_w, mod_b).reshape(depth * MOD_ROWS, 3, d)

    tab_a = _rope_tables(n_lat, n_ctx, GQA_HEAD_DIM)
    tab_m = _rope_tables(n_lat, n_ctx, MLA_ROPE)

    out = None
    for i in range(depth):
        kind, j = i % N_MIXERS, i // N_MIXERS
        need_ctx = i < depth - 1
        if kind == 0:
            n_q = d // GQA_HEAD_DIM
            n_kv = n_q // GQA_GROUP
            kern = functools.partial(_gqa_proj_kernel, n_q=n_q, n_kv=n_kv, scale=GQA_HEAD_DIM ** -0.5 * LOG2E)
            consts = [ga_w_in[j].astype(BF16), ga_q_g[j].reshape(1, -1), ga_k_g[j].reshape(1, -1)]
            kvw = n_kv * GQA_HEAD_DIM
            q, k, vt, gs = _project(kern, xs, mod3, i, norm_g[i], consts, tab_a, [d, kvw, kvw, d], n_lat)
            o = _flash_attention(q, k, vt, group=GQA_GROUP, dk=GQA_HEAD_DIM, n_lat=n_lat, layer=i)
            if need_ctx:
                oc = _ctx_attention(q, k, vt, group=GQA_GROUP, dk=GQA_HEAD_DIM, n_lat=n_lat, layer=i)
            w_out = ga_w_out[j]
        elif kind == 1:
            kern = functools.partial(_na_proj_kernel, width=d, scale=NA_HEAD_DIM ** -0.5 * LOG2E)
            q, k, vt, gs = _project(kern, xs, mod3, i, norm_g[i], [na_w_in[j].astype(BF16)], (), [d, d, d, d], n_lat)
            o = _na_attention(q, k, vt, _na_bias(na_rpb[j], n_lat // GRID_W), n_lat=n_lat, layer=i)
            if need_ctx:
                oc = _na_ctx_attention(q, k, vt, n_lat=n_lat, layer=i)
            w_out = na_w_out[j]
        else:
            kern = functools.partial(_mla_proj_kernel, width=d, scale=(MLA_NOPE + MLA_ROPE) ** -0.5 * LOG2E)
            w_perm, uq, wkv = _mla_weights(mla_w_in[j], mla_w_uq[j], mla_w_ukv[j])
            consts = [w_perm, mla_q_g[j].reshape(1, -1), mla_kv_g[j].reshape(1, -1), uq, wkv]
            widths = [MLA_HEADS * MLA_QK_PAD, MLA_HEADS * MLA_QK_PAD, MLA_HEADS * MLA_V, d]
            q, k, vt, gs = _project(kern, xs, mod3, i, norm_g[i], consts, tab_m, widths, n_lat)
            o = _flash_attention(q, k, vt, group=1, dk=MLA_QK_PAD, n_lat=n_lat, layer=i)
            if need_ctx:
                oc = _ctx_attention(q, k, vt, group=1, dk=MLA_QK_PAD, n_lat=n_lat, layer=i)
            w_out = mla_w_out[j]
        if need_ctx:
            xs = _out_proj(o, oc, gs, w_out.astype(BF16), xs, mod3, i, n_lat)
        else:
            out = _out_proj(o, None, gs, w_out.astype(BF16), xs, mod3, i, n_lat, final_g=final_g)
    return out
```

```python
import functools

import jax
import numpy as np
import jax.numpy as jnp
from jax import lax
from jax.experimental import pallas as pl
from jax.experimental.pallas import tpu as pltpu

F32 = jnp.float32
BF16 = jnp.bfloat16

NORM_EPS = 1e-6
ROPE_THETA = 10000.0
GRID_W = 64
N_MIXERS = 3
LANES = 128
MOD_ROWS = 8
MASK_VALUE = -1e30
LOG2E = 1.4426950408889634

GQA_HEAD_DIM = 128
GQA_GROUP = 4
NA_HEAD_DIM = 64
NA_KH = 8
NA_KW = 16
NA_BLOCK_ROWS = 4
NA_WIN_ROWS = 12
NA_STEP_BLOCKS = 4
MLA_HEADS = 8
MLA_Q_LORA = 512
MLA_KV_LORA = 256
MLA_NOPE = 128
MLA_ROPE = 64
MLA_V = 128
MLA_QK_PAD = 256

ROW_TILE = 256
Q_COLS = 1024
Q_TILE = 256
SUM_ROWS = 16
SKEW_LANES = 128
KV_CHUNKS = (768, 512, 256)
VMEM_LIMIT = 48 * 1024 * 1024


def _params(*sem):
    return pltpu.CompilerParams(dimension_semantics=sem, vmem_limit_bytes=VMEM_LIMIT)


def _silu(v):
    return v * (1.0 / (1.0 + jnp.exp(-v)))


def _rms(v):
    return lax.rsqrt(jnp.mean(v * v, axis=-1, keepdims=True) + NORM_EPS)


def _rope(v, cos, sin_up, sin_dn, half):
    w = v.shape[-1]
    return v * cos + pltpu.roll(v, w - half, 1) * sin_up + pltpu.roll(v, half, 1) * sin_dn


def _dot(a, b):
    return jnp.dot(a, b, preferred_element_type=F32)


def _dot_nt(a, b):
    return lax.dot_general(a, b, (((1,), (1,)), ((), ())), preferred_element_type=F32)


def _mod_kernel(c_ref, w_ref, b_ref, o_ref):
    a = _silu(c_ref[...])
    w = w_ref[0]
    a_hi = a.astype(BF16)
    a_lo = (a - a_hi.astype(F32)).astype(BF16)
    w_hi = w.astype(BF16)
    w_lo = (w - w_hi.astype(F32)).astype(BF16)
    o_ref[0] = _dot(a_hi, w_hi) + _dot(a_lo, w_hi) + _dot(a_hi, w_lo) + b_ref[0]


def _modulation(cc, mod_w, mod_b):
    depth, d, n3 = mod_w.shape
    tn = 1024
    return pl.pallas_call(
        _mod_kernel,
        grid=(depth, n3 // tn),
        in_specs=[
            pl.BlockSpec((MOD_ROWS, d), lambda l, j: (0, 0)),
            pl.BlockSpec((1, d, tn), lambda l, j: (l, 0, j)),
            pl.BlockSpec((1, 1, tn), lambda l, j: (l, 0, j)),
        ],
        out_specs=pl.BlockSpec((1, MOD_ROWS, tn), lambda l, j: (l, 0, j)),
        out_shape=jax.ShapeDtypeStruct((depth, MOD_ROWS, n3), F32),
        compiler_params=_params("parallel", "parallel"),
        name="adaln_modulation",
    )(cc, mod_w, mod_b.reshape(depth, 1, n3))


def _modulated_norm(x_ref, ng_ref, mod_ref):
    x = x_ref[...]
    shift = mod_ref[0, 0:1, :]
    scale = mod_ref[0, 1:2, :]
    return ((x * _rms(x)) * ng_ref[...] * (1.0 + scale) + shift).astype(BF16)


def _gqa_proj_kernel(x_ref, mod_ref, ng_ref, w_ref, qg_ref, kg_ref, cos_ref, su_ref, sd_ref,
                     q_ref, k_ref, vt_ref, g_ref, *, n_q, n_kv, scale):
    h = _modulated_norm(x_ref, ng_ref, mod_ref)
    cos, su, sd = cos_ref[...], su_ref[...], sd_ref[...]
    hd = GQA_HEAD_DIM
    qw, kw = n_q * hd, n_kv * hd
    q = _dot(h, w_ref[:, :qw])
    for i in range(n_q):
        qh = q[:, i * hd:(i + 1) * hd]
        qh = qh * _rms(qh) * qg_ref[...]
        q_ref[:, i * hd:(i + 1) * hd] = (_rope(qh, cos, su, sd, hd // 4) * scale).astype(BF16)
    k = _dot(h, w_ref[:, qw:qw + kw])
    for i in range(n_kv):
        kh = k[:, i * hd:(i + 1) * hd]
        kh = kh * _rms(kh) * kg_ref[...]
        k_ref[:, i * hd:(i + 1) * hd] = _rope(kh, cos, su, sd, hd // 4).astype(BF16)
    v = _dot(h, w_ref[:, qw + kw:qw + 2 * kw])
    for i in range(n_kv):
        vt_ref[i * hd:(i + 1) * hd, :] = v[:, i * hd:(i + 1) * hd].T.astype(BF16)
    g_ref[...] = _silu(_dot(h, w_ref[:, qw + 2 * kw:])).astype(BF16)


def _na_proj_kernel(x_ref, mod_ref, ng_ref, w_ref, q_ref, k_ref, vt_ref, g_ref, *, width, scale):
    h = _modulated_norm(x_ref, ng_ref, mod_ref)
    q_ref[...] = (_dot(h, w_ref[:, :width]) * scale).astype(BF16)
    k_ref[...] = _dot(h, w_ref[:, width:2 * width]).astype(BF16)
    v = _dot(h, w_ref[:, 2 * width:3 * width])
    for i in range(width // LANES):
        vt_ref[i * LANES:(i + 1) * LANES, :] = v[:, i * LANES:(i + 1) * LANES].T.astype(BF16)
    g_ref[...] = _silu(_dot(h, w_ref[:, 3 * width:])).astype(BF16)


def _mla_proj_kernel(x_ref, mod_ref, ng_ref, w_ref, qg_ref, kvg_ref, wuq_ref, wkv_ref,
                     cos_ref, su_ref, sd_ref, q_ref, k_ref, vt_ref, g_ref, *, width, scale):
    h = _modulated_norm(x_ref, ng_ref, mod_ref)
    cos, su, sd = cos_ref[...], su_ref[...], sd_ref[...]
    o_kv = MLA_Q_LORA
    o_g = o_kv + MLA_KV_LORA
    o_kr = o_g + width
    c_q = _dot(h, w_ref[:, :o_kv])
    c_q = (c_q * _rms(c_q) * qg_ref[...]).astype(BF16)
    q = _dot(c_q, wuq_ref[...])
    for i in range(MLA_HEADS):
        a = i * MLA_QK_PAD
        q_ref[:, a:a + MLA_NOPE] = (q[:, a:a + MLA_NOPE] * scale).astype(BF16)
        q_ref[:, a + MLA_NOPE:a + MLA_QK_PAD] = (
            _rope(q[:, a + MLA_NOPE:a + MLA_QK_PAD], cos, su, sd, MLA_ROPE // 4) * scale).astype(BF16)
    c_kv = _dot(h, w_ref[:, o_kv:o_g])
    c_kv = (c_kv * _rms(c_kv) * kvg_ref[...]).astype(BF16)
    k_r = _rope(_dot(h, w_ref[:, o_kr:]), cos, su, sd, MLA_ROPE // 4).astype(BF16)
    kv_in = jnp.concatenate([c_kv, k_r], axis=1)
    kw = MLA_HEADS * MLA_QK_PAD
    k_ref[...] = _dot(kv_in, wkv_ref[:, :kw]).astype(BF16)
    v = _dot(kv_in, wkv_ref[:, kw:])
    for i in range(MLA_HEADS):
        vt_ref[i * MLA_V:(i + 1) * MLA_V, :] = v[:, i * MLA_V:(i + 1) * MLA_V].T.astype(BF16)
    g_ref[...] = _silu(_dot(h, w_ref[:, o_g:o_kr])).astype(BF16)


def _project(kern, xs, mod3, layer, norm_g, consts, tables, out_widths, n_lat):
    b, n, d = xs.shape
    tm = ROW_TILE
    n_lat_tiles = n_lat // tm
    ctx_row = b

    def mod_idx(bi, t):
        return (layer * MOD_ROWS + jnp.where(t < n_lat_tiles, bi, ctx_row), 0, 0)

    in_specs = [
        pl.BlockSpec((None, tm, d), lambda bi, t: (bi, t, 0)),
        pl.BlockSpec((1, 3, d), mod_idx),
        pl.BlockSpec((1, d), lambda bi, t: (0, 0)),
    ]
    in_specs += [pl.BlockSpec(a.shape, lambda bi, t: (0, 0)) for a in consts]
    in_specs += [pl.BlockSpec((tm, LANES), lambda bi, t: (t, 0)) for _ in tables]
    out_specs = [pl.BlockSpec((None, tm, w), lambda bi, t: (bi, t, 0)) for w in out_widths]
    out_shape = [jax.ShapeDtypeStruct((b, n, w), BF16) for w in out_widths]
    out_specs[2] = pl.BlockSpec((None, None, out_widths[2], tm), lambda bi, t: (bi, t, 0, 0))
    out_shape[2] = jax.ShapeDtypeStruct((b, n // tm, out_widths[2], tm), BF16)
    return pl.pallas_call(
        kern,
        grid=(b, n // tm),
        in_specs=in_specs,
        out_specs=out_specs,
        out_shape=out_shape,
        compiler_params=_params("parallel", "parallel"),
        name="proj_" + str(layer),
    )(xs, mod3, norm_g.reshape(1, d), *consts, *tables)


def _stack_heads(q_ref, group, dk):
    return jnp.concatenate([q_ref[:, g * dk:(g + 1) * dk] for g in range(group)], axis=0)


def _unstack_heads(o_t, o_ref, group):
    tq = o_ref.shape[0]
    for g in range(group):
        o_ref[:, g * LANES:(g + 1) * LANES] = o_t[:, g * tq:(g + 1) * tq].T.astype(BF16)


def _pv_t(vt_ref, first_slab, p, sum_rows=0):
    slab = vt_ref.shape[-1]
    acc = None
    for i in range(p.shape[0] // slab):
        vt = vt_ref[first_slab + i]
        if sum_rows:
            vt = jnp.concatenate([vt, jnp.ones((sum_rows, slab), vt.dtype)], axis=0)
        t = _dot(vt, p[i * slab:(i + 1) * slab, :])
        acc = t if acc is None else acc + t
    return acc


def _flash_kernel(q_ref, k_ref, vt_ref, o_ref, sa_ref, sb_ref, acc_ref, m_ref, *, group, dk, bk):
    slab = vt_ref.shape[-1]
    n_chunks = k_ref.shape[0] // bk
    q = _stack_heads(q_ref, group, dk)
    tiles = [slice(j * Q_TILE, (j + 1) * Q_TILE) for j in range(q.shape[0] // Q_TILE)]

    def scores(c, s_ref):
        k = k_ref[pl.ds(pl.multiple_of(c * bk, bk), bk), :]
        out = []
        for t in tiles:
            s = _dot_nt(k, q[t, :])
            s_ref[:, t] = s
            out.append(jnp.max(s, axis=0, keepdims=True))
        return out

    def update(s_ref, s_max, c, t):
        m = m_ref[:, t]
        m_new = jnp.maximum(m, s_max)
        alpha = jnp.exp2(m - m_new)
        p = jnp.exp2(s_ref[:, t] - m_new).astype(BF16)
        m_ref[:, t] = m_new
        acc_ref[:, t] = alpha * acc_ref[:, t] + _pv_t(vt_ref, c * (bk // slab), p, SUM_ROWS)

    def step(c, cur_ref, cur_max, nxt_ref):
        k = k_ref[pl.ds(pl.multiple_of((c + 1) * bk, bk), bk), :]
        nxt_max = []
        for j, t in enumerate(tiles):
            s = _dot_nt(k, q[t, :])
            nxt_ref[:, t] = s
            nxt_max.append(jnp.max(s, axis=0, keepdims=True))
            update(cur_ref, cur_max[j], c, t)
        return nxt_max

    m_ref[...] = jnp.full(m_ref.shape, MASK_VALUE, F32)
    acc_ref[...] = jnp.zeros(acc_ref.shape, F32)
    max_a = scores(0, sa_ref)

    def body(i, max_a):
        max_b = step(2 * i, sa_ref, max_a, sb_ref)
        return tuple(step(2 * i + 1, sb_ref, max_b, sa_ref))

    pairs = (n_chunks - 1) // 2
    max_a = lax.fori_loop(0, pairs, body, tuple(max_a))
    last = 2 * pairs
    if n_chunks - last == 2:
        max_b = step(last, sa_ref, max_a, sb_ref)
        last, s_ref, s_max = last + 1, sb_ref, max_b
    else:
        s_ref, s_max = sa_ref, max_a
    for j, t in enumerate(tiles):
        update(s_ref, s_max[j], last, t)
    _unstack_heads(acc_ref[:LANES, :] * (1.0 / acc_ref[LANES:LANES + 1, :]), o_ref, group)


def _flash_attention(q, k, vt, *, group, dk, n_lat, layer):
    b, n, _ = q.shape
    n_kv = k.shape[2] // dk
    n_slab, slab = vt.shape[1], vt.shape[3]
    tq = min(Q_COLS // group, n_lat)
    bk = max(c for c in KV_CHUNKS if n % c == 0)
    assert bk % slab == 0
    kern = functools.partial(_flash_kernel, group=group, dk=dk, bk=bk)
    nq = group * tq
    s_buf = pltpu.VMEM((bk, nq + SKEW_LANES), F32)
    return pl.pallas_call(
        kern,
        grid=(b, n_kv, n_lat // tq),
        in_specs=[
            pl.BlockSpec((None, tq, group * dk), lambda bi, h, t: (bi, t, h)),
            pl.BlockSpec((None, n, dk), lambda bi, h, t: (bi, 0, h)),
            pl.BlockSpec((None, n_slab, LANES, slab), lambda bi, h, t: (bi, 0, h, 0)),
        ],
        out_specs=pl.BlockSpec((None, tq, group * LANES), lambda bi, h, t: (bi, t, h)),
        out_shape=jax.ShapeDtypeStruct((b, n_lat, n_kv * group * LANES), BF16),
        scratch_shapes=[s_buf, s_buf, pltpu.VMEM((LANES + SUM_ROWS, nq), F32), pltpu.VMEM((1, nq), F32)],
        compiler_params=_params("parallel", "parallel", "parallel"),
        name="flash_" + str(layer),
    )(q, k, vt)


def _ctx_attn_kernel(q_ref, k_ref, vt_ref, o_ref, *, group, dk):
    q = _stack_heads(q_ref, group, dk)
    s = _dot_nt(k_ref[...], q)
    p = jnp.exp2(s - jnp.max(s, axis=0, keepdims=True))
    l = jnp.sum(p, axis=0, keepdims=True)
    _unstack_heads(_dot(vt_ref[...], p.astype(BF16)) * (1.0 / l), o_ref, group)


def _ctx_attention(q, k, vt, *, group, dk, n_lat, layer):
    b, n, _ = q.shape
    n_ctx = n - n_lat
    n_kv = k.shape[2] // dk
    slab = vt.shape[3]
    assert slab == n_ctx
    t = n_lat // n_ctx
    return pl.pallas_call(
        functools.partial(_ctx_attn_kernel, group=group, dk=dk),
        grid=(b, n_kv),
        in_specs=[
            pl.BlockSpec((None, n_ctx, group * dk), lambda bi, h: (bi, t, h)),
            pl.BlockSpec((None, n_ctx, dk), lambda bi, h: (bi, t, h)),
            pl.BlockSpec((None, None, LANES, slab), lambda bi, h: (bi, t, h, 0)),
        ],
        out_specs=pl.BlockSpec((None, n_ctx, group * LANES), lambda bi, h: (bi, 0, h)),
        out_shape=jax.ShapeDtypeStruct((b, n_ctx, n_kv * group * LANES), BF16),
        compiler_params=_params("parallel", "parallel"),
        name="ctx_attn_" + str(layer),
    )(q, k, vt)


def _split_heads(q):
    lane = lax.broadcasted_iota(jnp.int32, (1, LANES), 1)
    zero = jnp.zeros_like(q)
    return jnp.concatenate(
        [jnp.where((lane >= j * NA_HEAD_DIM) & (lane < (j + 1) * NA_HEAD_DIM), q, zero)
         for j in range(LANES // NA_HEAD_DIM)], axis=0)


def _merge_heads(o_t, nq):
    parts = [o_t[j * NA_HEAD_DIM:(j + 1) * NA_HEAD_DIM, j * nq:(j + 1) * nq]
             for j in range(LANES // NA_HEAD_DIM)]
    return jnp.concatenate(parts, axis=0).T


def _na_ctx_kernel(q_ref, k_ref, vt_ref, o_ref):
    q2 = _split_heads(q_ref[...])
    s = _dot_nt(k_ref[...], q2)
    p = jnp.exp2(s - jnp.max(s, axis=0, keepdims=True))
    l = jnp.sum(p, axis=0, keepdims=True)
    o_t = _dot(vt_ref[...], p.astype(BF16)) * (1.0 / l)
    o_ref[...] = _merge_heads(o_t, q_ref.shape[0]).astype(BF16)


def _na_ctx_attention(q, k, vt, *, n_lat, layer):
    b, n, width = q.shape
    n_ctx = n - n_lat
    assert vt.shape[3] == n_ctx
    t = n_lat // n_ctx
    blk = pl.BlockSpec((None, n_ctx, LANES), lambda bi, h: (bi, t, h))
    return pl.pallas_call(
        _na_ctx_kernel,
        grid=(b, width // LANES),
        in_specs=[blk, blk, pl.BlockSpec((None, None, LANES, n_ctx), lambda bi, h: (bi, t, h, 0))],
        out_specs=pl.BlockSpec((None, n_ctx, LANES), lambda bi, h: (bi, 0, h)),
        out_shape=jax.ShapeDtypeStruct((b, n_ctx, width), BF16),
        compiler_params=_params("parallel", "parallel"),
        name="ctx_attn_" + str(layer),
    )(q, k, vt)


def _na_window_start(blk, rows, lib):
    lo = lib.minimum(lib.maximum(NA_BLOCK_ROWS * blk - NA_KH // 2, 0), rows - NA_KH)
    return lib.minimum(lo, rows - NA_WIN_ROWS)


def _na_kernel(q_ref, k_ref, vt_ref, bias_ref, o_ref, sl_ref, sc_ref, *, n_lat, n_ctx, rows):
    slab = vt_ref.shape[-1]
    nq = NA_BLOCK_ROWS * GRID_W
    n_blocks = rows // NA_BLOCK_ROWS
    nk = NA_WIN_ROWS * GRID_W
    kc = k_ref[n_lat:n_lat + n_ctx, :]
    n_here = q_ref.shape[0] // nq

    def scores(i):
        blk = pl.program_id(2) * NA_STEP_BLOCKS + i
        kind = jnp.where(blk == 0, 1, jnp.where(blk == n_blocks - 1, 2, 0))
        ws = _na_window_start(blk, rows, jnp)
        q2 = _split_heads(q_ref[i * nq:(i + 1) * nq, :])
        s_loc = _dot_nt(k_ref[pl.ds(pl.multiple_of(ws * GRID_W, slab), nk), :], q2) + bias_ref[kind]
        s_ctx = _dot_nt(kc, q2)
        sl_ref[i % 2, :, :s_loc.shape[1]] = s_loc
        sc_ref[i % 2, :, :s_ctx.shape[1]] = s_ctx
        return ws, jnp.maximum(jnp.max(s_loc, axis=0, keepdims=True), jnp.max(s_ctx, axis=0, keepdims=True))

    nxt = scores(0)
    for i in range(n_here):
        ws, m = nxt
        if i + 1 < n_here:
            nxt = scores(i + 1)
        parts = []
        for j in range(LANES // NA_HEAD_DIM):
            t = slice(j * nq, (j + 1) * nq)
            p_loc = jnp.exp2((sl_ref[i % 2, :, t] - m[:, t]).astype(BF16))
            p_ctx = jnp.exp2((sc_ref[i % 2, :, t] - m[:, t]).astype(BF16))
            o_t = _pv_t(vt_ref, ws * GRID_W // slab, p_loc, SUM_ROWS)
            o_t = o_t + _pv_t(vt_ref, n_lat // slab, p_ctx, SUM_ROWS)
            parts.append(o_t[j * NA_HEAD_DIM:(j + 1) * NA_HEAD_DIM, :] * (1.0 / o_t[LANES:LANES + 1, :]))
        o_ref[i * nq:(i + 1) * nq, :] = jnp.concatenate(parts, axis=0).T.astype(BF16)


def _na_bias(rpb, rows):
    h, n_dr, n_dc = rpb.shape
    n_blocks = rows // NA_BLOCK_ROWS
    lead = GRID_W - NA_KW
    wide = 2 * GRID_W
    w = jnp.pad(rpb * LOG2E, ((0, 0), (0, 0), (lead, wide - lead - n_dc)), constant_values=MASK_VALUE)
    flat = jnp.broadcast_to(w[:, :, None, :], (h, n_dr, GRID_W, wide)).reshape(h, n_dr, GRID_W * wide)
    skew = flat[:, :, GRID_W - 1:GRID_W - 1 + GRID_W * (wide - 1)].reshape(h, n_dr, GRID_W, wide - 1)
    qc = jnp.arange(GRID_W)
    cs = jnp.clip(qc - NA_KW // 2, 0, GRID_W - NA_KW)
    col_ok = (qc[None, :] >= cs[:, None]) & (qc[None, :] < cs[:, None] + NA_KW)
    c_t = jnp.swapaxes(jnp.where(col_ok, skew[..., :GRID_W], MASK_VALUE), -1, -2)
    masked = jnp.full((h, GRID_W, GRID_W), MASK_VALUE, F32)
    kinds = []
    for blk in (1, 0, n_blocks - 1):
        ws = _na_window_start(blk, rows, np)
        strips = []
        for j in range(NA_WIN_ROWS):
            kr = ws + j
            blocks = []
            for i in range(NA_BLOCK_ROWS):
                r = NA_BLOCK_ROWS * blk + i
                rs = min(max(r - NA_KH // 2, 0), rows - NA_KH)
                blocks.append(c_t[:, kr - r + NA_KH - 1] if rs <= kr < rs + NA_KH else masked)
            strips.append(jnp.concatenate(blocks, axis=-1))
        kinds.append(jnp.concatenate(strips, axis=1))
    bias = jnp.stack(kinds, axis=1)
    sub = LANES // NA_HEAD_DIM
    bias = bias.reshape(h // sub, sub, len(kinds), NA_WIN_ROWS * GRID_W, NA_BLOCK_ROWS * GRID_W)
    return jnp.concatenate([bias[:, j] for j in range(sub)], axis=-1)


def _na_attention(q, k, vt, bias, *, n_lat, layer):
    b, n, width = q.shape
    n_ctx = n - n_lat
    rows = n_lat // GRID_W
    n_slab, slab = vt.shape[1], vt.shape[3]
    tq = NA_STEP_BLOCKS * NA_BLOCK_ROWS * GRID_W
    kern = functools.partial(_na_kernel, n_lat=n_lat, n_ctx=n_ctx, rows=rows)
    return pl.pallas_call(
        kern,
        grid=(width // LANES, b, n_lat // tq),
        in_specs=[
            pl.BlockSpec((None, tq, LANES), lambda hp, bi, rb: (bi, rb, hp)),
            pl.BlockSpec((None, n, LANES), lambda hp, bi, rb: (bi, 0, hp)),
            pl.BlockSpec((None, n_slab, LANES, slab), lambda hp, bi, rb: (bi, 0, hp, 0)),
            pl.BlockSpec((None,) + bias.shape[1:], lambda hp, bi, rb: (hp, 0, 0, 0)),
        ],
        out_specs=pl.BlockSpec((None, tq, LANES), lambda hp, bi, rb: (bi, rb, hp)),
        out_shape=jax.ShapeDtypeStruct((b, n_lat, width), BF16),
        scratch_shapes=[pltpu.VMEM((2, bias.shape[2], bias.shape[3] + SKEW_LANES), F32),
                        pltpu.VMEM((2, n_ctx, bias.shape[3] + SKEW_LANES), F32)],
        compiler_params=_params("parallel", "parallel", "arbitrary"),
        name="na_attn_" + str(layer),
    )(q, k, vt, bias)


def _out_proj_kernel(o_ref, oc_ref, g_ref, w_ref, x_ref, mod_ref, xo_ref, *, n_lat_tiles):
    o = jnp.where(pl.program_id(1) >= n_lat_tiles, oc_ref[...], o_ref[...])
    a = (o.astype(F32) * g_ref[...].astype(F32)).astype(BF16)
    xo_ref[...] = x_ref[...] + mod_ref[0, 2:3, :] * _dot(a, w_ref[...])


def _out_proj_final_kernel(o_ref, g_ref, w_ref, x_ref, mod_ref, fg_ref, y_ref):
    a = (o_ref[...].astype(F32) * g_ref[...].astype(F32)).astype(BF16)
    xn = x_ref[...] + mod_ref[0, 2:3, :] * _dot(a, w_ref[...])
    y_ref[...] = xn * _rms(xn) * fg_ref[...]


def _out_proj(o, o_ctx, gs, w_out, xs, mod3, layer, n_lat, final_g=None):
    b, n, d = xs.shape
    tm = ROW_TILE
    n_lat_tiles = n_lat // tm
    ctx_row = b

    def mod_idx(bi, t):
        return (layer * MOD_ROWS + jnp.where(t < n_lat_tiles, bi, ctx_row), 0, 0)

    row = lambda bi, t: (bi, t, 0)
    w = o.shape[2]
    o_spec = pl.BlockSpec((None, tm, w), lambda bi, t: (bi, jnp.minimum(t, n_lat_tiles - 1), 0))
    tail = [
        pl.BlockSpec((None, tm, w), row),
        pl.BlockSpec(w_out.shape, lambda bi, t: (0, 0)),
        pl.BlockSpec((None, tm, d), row),
        pl.BlockSpec((1, 3, d), mod_idx),
    ]
    if final_g is None:
        return pl.pallas_call(
            functools.partial(_out_proj_kernel, n_lat_tiles=n_lat_tiles),
            grid=(b, n // tm),
            in_specs=[o_spec, pl.BlockSpec((None, tm, w), lambda bi, t: (bi, 0, 0))] + tail,
            out_specs=pl.BlockSpec((None, tm, d), row),
            out_shape=jax.ShapeDtypeStruct(xs.shape, F32),
            input_output_aliases={4: 0},
            compiler_params=_params("parallel", "parallel"),
            name="out_proj_" + str(layer),
        )(o, o_ctx, gs, w_out, xs, mod3)
    return pl.pallas_call(
        _out_proj_final_kernel,
        grid=(b, n_lat_tiles),
        in_specs=[o_spec] + tail + [pl.BlockSpec((1, d), lambda bi, t: (0, 0))],
        out_specs=pl.BlockSpec((None, tm, d), row),
        out_shape=jax.ShapeDtypeStruct((b, n_lat, d), F32),
        compiler_params=_params("parallel", "parallel"),
        name="out_proj_final",
    )(o, gs, w_out, xs, mod3, final_g.reshape(1, d))


def _rope_tables(n_lat, n_ctx, rot_dim):
    n_freq = rot_dim // 4
    inv = ROPE_THETA ** (-jnp.arange(n_freq, dtype=F32) / n_freq)
    t = jnp.arange(n_lat)
    ang_r = (t // GRID_W).astype(F32)[:, None] * inv
    ang_c = (t % GRID_W).astype(F32)[:, None] * inv
    ang = jnp.concatenate([ang_r, ang_r, ang_c, ang_c], axis=-1)
    cos, sin = jnp.cos(ang), jnp.sin(ang)
    first = (jnp.arange(rot_dim) % (2 * n_freq)) < n_freq
    sin_up = jnp.where(first, -sin, 0.0)
    sin_dn = jnp.where(first, 0.0, sin)
    pad = LANES - rot_dim

    def finish(tab, fill):
        tab = jnp.pad(tab, ((0, 0), (0, pad)), constant_values=fill)
        return jnp.pad(tab, ((0, n_ctx), (0, 0)), constant_values=fill)

    return finish(cos, 1.0), finish(sin_up, 0.0), finish(sin_dn, 0.0)


def _mla_weights(w_in, w_uq, w_ukv):
    o_kv = MLA_Q_LORA
    o_kr = o_kv + MLA_KV_LORA
    o_g = o_kr + MLA_ROPE
    k_r = jnp.pad(w_in[:, o_kr:o_g], ((0, 0), (0, LANES - MLA_ROPE)))
    w_perm = jnp.concatenate([w_in[:, :o_kr], w_in[:, o_g:], k_r], axis=1).astype(BF16)
    uq = w_uq.reshape(MLA_Q_LORA, MLA_HEADS, MLA_NOPE + MLA_ROPE)
    uq = jnp.pad(uq, ((0, 0), (0, 0), (0, MLA_QK_PAD - MLA_NOPE - MLA_ROPE)))
    uq = uq.reshape(MLA_Q_LORA, MLA_HEADS * MLA_QK_PAD).astype(BF16)
    ukv = w_ukv.reshape(MLA_KV_LORA, MLA_HEADS, MLA_NOPE + MLA_V)
    k_top = jnp.pad(ukv[:, :, :MLA_NOPE], ((0, 0), (0, 0), (0, MLA_QK_PAD - MLA_NOPE)))
    eye = jnp.pad(jnp.eye(MLA_ROPE, dtype=F32), ((0, LANES - MLA_ROPE), (MLA_NOPE, MLA_QK_PAD - MLA_NOPE - MLA_ROPE)))
    k_bot = jnp.broadcast_to(eye[:, None, :], (LANES, MLA_HEADS, MLA_QK_PAD))
    k_aug = jnp.concatenate([k_top, k_bot], axis=0).reshape(MLA_KV_LORA + LANES, MLA_HEADS * MLA_QK_PAD)
    v_aug = jnp.pad(ukv[:, :, MLA_NOPE:].reshape(MLA_KV_LORA, MLA_HEADS * MLA_V), ((0, LANES), (0, 0)))
    return w_perm, uq, jnp.concatenate([k_aug, v_aug], axis=1).astype(BF16)


def kernel(x, c, ctx, c_ctx, mod_w, mod_b, norm_g, final_g, ga_w_in, ga_q_g, ga_k_g, ga_w_out, na_w_in, na_rpb, na_w_out, mla_w_in, mla_q_g, mla_kv_g, mla_w_uq, mla_w_ukv, mla_w_out):
    b, n_lat, d = x.shape
    n_ctx = ctx.shape[1]
    depth = mod_w.shape[0]
    assert n_lat % ROW_TILE == 0 and n_ctx == ROW_TILE and n_lat % n_ctx == 0
    assert b < MOD_ROWS and n_lat % (NA_STEP_BLOCKS * NA_BLOCK_ROWS * GRID_W) == 0
    assert n_lat // GRID_W >= NA_WIN_ROWS + NA_BLOCK_ROWS

    xs = jnp.concatenate([x, ctx], axis=1)
    cc = jnp.concatenate([c, c_ctx[None, :], jnp.zeros((MOD_ROWS - b - 1, d), F32)], axis=0)
    mod3 = _modulation(cc, mod_w, mod_b).reshape(depth * MOD_ROWS, 3, d)

    tab_a = _rope_tables(n_lat, n_ctx, GQA_HEAD_DIM)
    tab_m = _rope_tables(n_lat, n_ctx, MLA_ROPE)

    out = None
    for i in range(depth):
        kind, j = i % N_MIXERS, i // N_MIXERS
        need_ctx = i < depth - 1
        if kind == 0:
            n_q = d // GQA_HEAD_DIM
            n_kv = n_q // GQA_GROUP
            kern = functools.partial(_gqa_proj_kernel, n_q=n_q, n_kv=n_kv, scale=GQA_HEAD_DIM ** -0.5 * LOG2E)
            consts = [ga_w_in[j].astype(BF16), ga_q_g[j].reshape(1, -1), ga_k_g[j].reshape(1, -1)]
            kvw = n_kv * GQA_HEAD_DIM
            q, k, vt, gs = _project(kern, xs, mod3, i, norm_g[i], consts, tab_a, [d, kvw, kvw, d], n_lat)
            o = _flash_attention(q, k, vt, group=GQA_GROUP, dk=GQA_HEAD_DIM, n_lat=n_lat, layer=i)
            if need_ctx:
                oc = _ctx_attention(q, k, vt, group=GQA_GROUP, dk=GQA_HEAD_DIM, n_lat=n_lat, layer=i)
            w_out = ga_w_out[j]
        elif kind == 1:
            kern = functools.partial(_na_proj_kernel, width=d, scale=NA_HEAD_DIM ** -0.5 * LOG2E)
            q, k, vt, gs = _project(kern, xs, mod3, i, norm_g[i], [na_w_in[j].astype(BF16)], (), [d, d, d, d], n_lat)
            o = _na_attention(q, k, vt, _na_bias(na_rpb[j], n_lat // GRID_W), n_lat=n_lat, layer=i)
            if need_ctx:
                oc = _na_ctx_attention(q, k, vt, n_lat=n_lat, layer=i)
            w_out = na_w_out[j]
        else:
            kern = functools.partial(_mla_proj_kernel, width=d, scale=(MLA_NOPE + MLA_ROPE) ** -0.5 * LOG2E)
            w_perm, uq, wkv = _mla_weights(mla_w_in[j], mla_w_uq[j], mla_w_ukv[j])
            consts = [w_perm, mla_q_g[j].reshape(1, -1), mla_kv_g[j].reshape(1, -1), uq, wkv]
            widths = [MLA_HEADS * MLA_QK_PAD, MLA_HEADS * MLA_QK_PAD, MLA_HEADS * MLA_V, d]
            q, k, vt, gs = _project(kern, xs, mod3, i, norm_g[i], consts, tab_m, widths, n_lat)
            o = _flash_attention(q, k, vt, group=1, dk=MLA_QK_PAD, n_lat=n_lat, layer=i)
            if need_ctx:
                oc = _ctx_attention(q, k, vt, group=1, dk=MLA_QK_PAD, n_lat=n_lat, layer=i)
            w_out = mla_w_out[j]
        if need_ctx:
            xs = _out_proj(o, oc, gs, w_out.astype(BF16), xs, mod3, i, n_lat)
        else:
            out = _out_proj(o, None, gs, w_out.astype(BF16), xs, mod3, i, n_lat, final_g=final_g)
    return out
```

```python
import functools

import jax
import numpy as np
import jax.numpy as jnp
from jax import lax
from jax.experimental import pallas as pl
from jax.experimental.pallas import tpu as pltpu

F32 = jnp.float32
BF16 = jnp.bfloat16

NORM_EPS = 1e-6
ROPE_THETA = 10000.0
GRID_W = 64
N_MIXERS = 3
LANES = 128
MOD_ROWS = 8
MASK_VALUE = -1e30
LOG2E = 1.4426950408889634

GQA_HEAD_DIM = 128
GQA_GROUP = 4
NA_HEAD_DIM = 64
NA_KH = 8
NA_KW = 16
NA_BLOCK_ROWS = 4
NA_WIN_ROWS = 12
NA_STEP_BLOCKS = 4
MLA_HEADS = 8
MLA_Q_LORA = 512
MLA_KV_LORA = 256
MLA_NOPE = 128
MLA_ROPE = 64
MLA_V = 128
MLA_QK_PAD = 256

ROW_TILE = 256
Q_COLS = 1024
Q_TILE = 256
SUM_ROWS = 16
SKEW_LANES = 128
KV_CHUNKS = (768, 512, 256)
VMEM_LIMIT = 48 * 1024 * 1024


def _params(*sem):
    return pltpu.CompilerParams(dimension_semantics=sem, vmem_limit_bytes=VMEM_LIMIT)


def _silu(v):
    return v * (1.0 / (1.0 + jnp.exp(-v)))


def _rms(v):
    return lax.rsqrt(jnp.mean(v * v, axis=-1, keepdims=True) + NORM_EPS)


def _rope(v, cos, sin_up, sin_dn, half):
    w = v.shape[-1]
    return v * cos + pltpu.roll(v, w - half, 1) * sin_up + pltpu.roll(v, half, 1) * sin_dn


def _dot(a, b):
    return jnp.dot(a, b, preferred_element_type=F32)


def _dot_nt(a, b):
    return lax.dot_general(a, b, (((1,), (1,)), ((), ())), preferred_element_type=F32)


def _mod_kernel(c_ref, w_ref, b_ref, o_ref):
    a = _silu(c_ref[...])
    w = w_ref[0]
    a_hi = a.astype(BF16)
    a_lo = (a - a_hi.astype(F32)).astype(BF16)
    w_hi = w.astype(BF16)
    w_lo = (w - w_hi.astype(F32)).astype(BF16)
    o_ref[0] = _dot(a_hi, w_hi) + _dot(a_lo, w_hi) + _dot(a_hi, w_lo) + b_ref[0]


def _modulation(cc, mod_w, mod_b):
    depth, d, n3 = mod_w.shape
    tn = 1024
    return pl.pallas_call(
        _mod_kernel,
        grid=(depth, n3 // tn),
        in_specs=[
            pl.BlockSpec((MOD_ROWS, d), lambda l, j: (0, 0)),
            pl.BlockSpec((1, d, tn), lambda l, j: (l, 0, j)),
            pl.BlockSpec((1, 1, tn), lambda l, j: (l, 0, j)),
        ],
        out_specs=pl.BlockSpec((1, MOD_ROWS, tn), lambda l, j: (l, 0, j)),
        out_shape=jax.ShapeDtypeStruct((depth, MOD_ROWS, n3), F32),
        compiler_params=_params("parallel", "parallel"),
        name="adaln_modulation",
    )(cc, mod_w, mod_b.reshape(depth, 1, n3))


def _modulated_norm(x_ref, xc_ref, ng_ref, mod_ref, n_lat_tiles):
    x = jnp.where(pl.program_id(1) >= n_lat_tiles, xc_ref[...], x_ref[...])
    shift = mod_ref[0, 0:1, :]
    scale = mod_ref[0, 1:2, :]
    return ((x * _rms(x)) * ng_ref[...] * (1.0 + scale) + shift).astype(BF16)


def _gqa_proj_kernel(x_ref, xc_ref, mod_ref, ng_ref, w_ref, qg_ref, kg_ref, cos_ref, su_ref, sd_ref,
                     q_ref, k_ref, vt_ref, g_ref, *, n_lat_tiles, n_q, n_kv, scale):
    h = _modulated_norm(x_ref, xc_ref, ng_ref, mod_ref, n_lat_tiles)
    cos, su, sd = cos_ref[...], su_ref[...], sd_ref[...]
    hd = GQA_HEAD_DIM
    qw, kw = n_q * hd, n_kv * hd
    q = _dot(h, w_ref[:, :qw])
    for i in range(n_q):
        qh = q[:, i * hd:(i + 1) * hd]
        qh = qh * _rms(qh) * qg_ref[...]
        q_ref[:, i * hd:(i + 1) * hd] = (_rope(qh, cos, su, sd, hd // 4) * scale).astype(BF16)
    k = _dot(h, w_ref[:, qw:qw + kw])
    for i in range(n_kv):
        kh = k[:, i * hd:(i + 1) * hd]
        kh = kh * _rms(kh) * kg_ref[...]
        k_ref[:, i * hd:(i + 1) * hd] = _rope(kh, cos, su, sd, hd // 4).astype(BF16)
    v = _dot(h, w_ref[:, qw + kw:qw + 2 * kw])
    for i in range(n_kv):
        vt_ref[i * hd:(i + 1) * hd, :] = v[:, i * hd:(i + 1) * hd].T.astype(BF16)
    g_ref[...] = _silu(_dot(h, w_ref[:, qw + 2 * kw:])).astype(BF16)


def _na_proj_kernel(x_ref, xc_ref, mod_ref, ng_ref, w_ref, q_ref, k_ref, vt_ref, g_ref,
                    *, n_lat_tiles, width, scale):
    h = _modulated_norm(x_ref, xc_ref, ng_ref, mod_ref, n_lat_tiles)
    q_ref[...] = (_dot(h, w_ref[:, :width]) * scale).astype(BF16)
    k_ref[...] = _dot(h, w_ref[:, width:2 * width]).astype(BF16)
    v = _dot(h, w_ref[:, 2 * width:3 * width])
    for i in range(width // LANES):
        vt_ref[i * LANES:(i + 1) * LANES, :] = v[:, i * LANES:(i + 1) * LANES].T.astype(BF16)
    g_ref[...] = _silu(_dot(h, w_ref[:, 3 * width:])).astype(BF16)


def _mla_proj_kernel(x_ref, xc_ref, mod_ref, ng_ref, w_ref, qg_ref, kvg_ref, wuq_ref, wuk_ref, wuv_ref,
                     cos_ref, su_ref, sd_ref, q_ref, k_ref, vt_ref, g_ref, *, n_lat_tiles, width, scale):
    h = _modulated_norm(x_ref, xc_ref, ng_ref, mod_ref, n_lat_tiles)
    cos, su, sd = cos_ref[...], su_ref[...], sd_ref[...]
    o_kv = MLA_Q_LORA
    o_g = o_kv + MLA_KV_LORA
    o_kr = o_g + width
    c_q = _dot(h, w_ref[:, :o_kv])
    c_q = (c_q * _rms(c_q) * qg_ref[...]).astype(BF16)
    q = _dot(c_q, wuq_ref[...])
    for i in range(MLA_HEADS):
        a = i * MLA_QK_PAD
        q_ref[:, a:a + MLA_NOPE] = (q[:, a:a + MLA_NOPE] * scale).astype(BF16)
        q_ref[:, a + MLA_NOPE:a + MLA_QK_PAD] = (
            _rope(q[:, a + MLA_NOPE:a + MLA_QK_PAD], cos, su, sd, MLA_ROPE // 4) * scale).astype(BF16)
    c_kv = _dot(h, w_ref[:, o_kv:o_g])
    c_kv = (c_kv * _rms(c_kv) * kvg_ref[...]).astype(BF16)
    k_r = _rope(_dot(h, w_ref[:, o_kr:]), cos, su, sd, MLA_ROPE // 4).astype(BF16)
    k_nope = _dot(c_kv, wuk_ref[...])
    for i in range(MLA_HEADS):
        a = i * MLA_QK_PAD
        k_ref[:, a:a + MLA_NOPE] = k_nope[:, i * MLA_NOPE:(i + 1) * MLA_NOPE].astype(BF16)
        k_ref[:, a + MLA_NOPE:a + MLA_QK_PAD] = k_r
    v = _dot(c_kv, wuv_ref[...])
    for i in range(MLA_HEADS):
        vt_ref[i * MLA_V:(i + 1) * MLA_V, :] = v[:, i * MLA_V:(i + 1) * MLA_V].T.astype(BF16)
    g_ref[...] = _silu(_dot(h, w_ref[:, o_g:o_kr])).astype(BF16)


def _project(kern, stream, mod3, layer, norm_g, consts, tables, out_widths, n_lat):
    x_lat, x_ctx, ctx_tile = stream
    b, _, d = x_lat.shape
    tm = ROW_TILE
    n_lat_tiles = n_lat // tm
    n = n_lat + tm
    ctx_row = b

    def mod_idx(bi, t):
        return (layer * MOD_ROWS + jnp.where(t < n_lat_tiles, bi, ctx_row), 0, 0)

    in_specs = [
        pl.BlockSpec((None, tm, d), lambda bi, t: (bi, jnp.minimum(t, n_lat_tiles - 1), 0)),
        pl.BlockSpec((None, tm, d), lambda bi, t: (bi, ctx_tile, 0)),
        pl.BlockSpec((1, 3, d), mod_idx),
        pl.BlockSpec((1, d), lambda bi, t: (0, 0)),
    ]
    in_specs += [pl.BlockSpec(a.shape, lambda bi, t: (0, 0)) for a in consts]
    in_specs += [pl.BlockSpec((tm, LANES), lambda bi, t: (t, 0)) for _ in tables]
    out_specs = [pl.BlockSpec((None, tm, w), lambda bi, t: (bi, t, 0)) for w in out_widths]
    out_shape = [jax.ShapeDtypeStruct((b, n, w), BF16) for w in out_widths]
    out_specs[2] = pl.BlockSpec((None, None, out_widths[2], tm), lambda bi, t: (bi, t, 0, 0))
    out_shape[2] = jax.ShapeDtypeStruct((b, n // tm, out_widths[2], tm), BF16)
    return pl.pallas_call(
        functools.partial(kern, n_lat_tiles=n_lat_tiles),
        grid=(b, n // tm),
        in_specs=in_specs,
        out_specs=out_specs,
        out_shape=out_shape,
        compiler_params=_params("parallel", "parallel"),
        name="proj_" + str(layer),
    )(x_lat, x_ctx, mod3, norm_g.reshape(1, d), *consts, *tables)


def _stack_heads(q_ref, group, dk):
    return jnp.concatenate([q_ref[:, g * dk:(g + 1) * dk] for g in range(group)], axis=0)


def _unstack_heads(o_t, o_ref, group):
    tq = o_ref.shape[0]
    for g in range(group):
        o_ref[:, g * LANES:(g + 1) * LANES] = o_t[:, g * tq:(g + 1) * tq].T.astype(BF16)


def _pv_t(vt_ref, first_slab, p, sum_rows=0):
    slab = vt_ref.shape[-1]
    acc = None
    for i in range(p.shape[0] // slab):
        vt = vt_ref[first_slab + i]
        if sum_rows:
            vt = jnp.concatenate([vt, jnp.ones((sum_rows, slab), vt.dtype)], axis=0)
        t = _dot(vt, p[i * slab:(i + 1) * slab, :])
        acc = t if acc is None else acc + t
    return acc


def _flash_kernel(q_ref, k_ref, vt_ref, o_ref, sa_ref, sb_ref, acc_ref, m_ref, *, group, dk, bk):
    slab = vt_ref.shape[-1]
    n_chunks = k_ref.shape[0] // bk
    q = _stack_heads(q_ref, group, dk)
    tiles = [slice(j * Q_TILE, (j + 1) * Q_TILE) for j in range(q.shape[0] // Q_TILE)]

    def scores(c, s_ref):
        k = k_ref[pl.ds(pl.multiple_of(c * bk, bk), bk), :]
        out = []
        for t in tiles:
            s = _dot_nt(k, q[t, :])
            s_ref[:, t] = s
            out.append(jnp.max(s, axis=0, keepdims=True))
        return out

    def update(s_ref, s_max, c, t):
        m = m_ref[:, t]
        m_new = jnp.maximum(m, s_max)
        alpha = jnp.exp2(m - m_new)
        p = jnp.exp2(s_ref[:, t] - m_new).astype(BF16)
        m_ref[:, t] = m_new
        acc_ref[:, t] = alpha * acc_ref[:, t] + _pv_t(vt_ref, c * (bk // slab), p, SUM_ROWS)

    def step(c, cur_ref, cur_max, nxt_ref):
        k = k_ref[pl.ds(pl.multiple_of((c + 1) * bk, bk), bk), :]
        nxt_max = []
        for j, t in enumerate(tiles):
            s = _dot_nt(k, q[t, :])
            nxt_ref[:, t] = s
            nxt_max.append(jnp.max(s, axis=0, keepdims=True))
            update(cur_ref, cur_max[j], c, t)
        return nxt_max

    m_ref[...] = jnp.full(m_ref.shape, MASK_VALUE, F32)
    acc_ref[...] = jnp.zeros(acc_ref.shape, F32)
    max_a = scores(0, sa_ref)

    def body(i, max_a):
        max_b = step(2 * i, sa_ref, max_a, sb_ref)
        return tuple(step(2 * i + 1, sb_ref, max_b, sa_ref))

    pairs = (n_chunks - 1) // 2
    max_a = lax.fori_loop(0, pairs, body, tuple(max_a))
    last = 2 * pairs
    if n_chunks - last == 2:
        max_b = step(last, sa_ref, max_a, sb_ref)
        last, s_ref, s_max = last + 1, sb_ref, max_b
    else:
        s_ref, s_max = sa_ref, max_a
    for j, t in enumerate(tiles):
        update(s_ref, s_max[j], last, t)
    _unstack_heads(acc_ref[:LANES, :] * (1.0 / acc_ref[LANES:LANES + 1, :]), o_ref, group)


def _flash_attention(q, k, vt, *, group, dk, n_lat, layer):
    b, n, _ = q.shape
    n_kv = k.shape[2] // dk
    n_slab, slab = vt.shape[1], vt.shape[3]
    tq = min(Q_COLS // group, n_lat)
    bk = max(c for c in KV_CHUNKS if n % c == 0)
    assert bk % slab == 0
    kern = functools.partial(_flash_kernel, group=group, dk=dk, bk=bk)
    nq = group * tq
    s_buf = pltpu.VMEM((bk, nq + SKEW_LANES), F32)
    return pl.pallas_call(
        kern,
        grid=(b, n_kv, n_lat // tq),
        in_specs=[
            pl.BlockSpec((None, tq, group * dk), lambda bi, h, t: (bi, t, h)),
            pl.BlockSpec((None, n, dk), lambda bi, h, t: (bi, 0, h)),
            pl.BlockSpec((None, n_slab, LANES, slab), lambda bi, h, t: (bi, 0, h, 0)),
        ],
        out_specs=pl.BlockSpec((None, tq, group * LANES), lambda bi, h, t: (bi, t, h)),
        out_shape=jax.ShapeDtypeStruct((b, n_lat, n_kv * group * LANES), BF16),
        scratch_shapes=[s_buf, s_buf, pltpu.VMEM((LANES + SUM_ROWS, nq), F32), pltpu.VMEM((1, nq), F32)],
        compiler_params=_params("parallel", "parallel", "parallel"),
        name="flash_" + str(layer),
    )(q, k, vt)


def _ctx_attn_kernel(q_ref, k_ref, vt_ref, o_ref, *, group, dk):
    q = _stack_heads(q_ref, group, dk)
    s = _dot_nt(k_ref[...], q)
    p = jnp.exp2(s - jnp.max(s, axis=0, keepdims=True))
    l = jnp.sum(p, axis=0, keepdims=True)
    _unstack_heads(_dot(vt_ref[...], p.astype(BF16)) * (1.0 / l), o_ref, group)


def _ctx_attention(q, k, vt, *, group, dk, n_lat, layer):
    b, n, _ = q.shape
    n_ctx = n - n_lat
    n_kv = k.shape[2] // dk
    slab = vt.shape[3]
    assert slab == n_ctx
    t = n_lat // n_ctx
    return pl.pallas_call(
        functools.partial(_ctx_attn_kernel, group=group, dk=dk),
        grid=(b, n_kv),
        in_specs=[
            pl.BlockSpec((None, n_ctx, group * dk), lambda bi, h: (bi, t, h)),
            pl.BlockSpec((None, n_ctx, dk), lambda bi, h: (bi, t, h)),
            pl.BlockSpec((None, None, LANES, slab), lambda bi, h: (bi, t, h, 0)),
        ],
        out_specs=pl.BlockSpec((None, n_ctx, group * LANES), lambda bi, h: (bi, 0, h)),
        out_shape=jax.ShapeDtypeStruct((b, n_ctx, n_kv * group * LANES), BF16),
        compiler_params=_params("parallel", "parallel"),
        name="ctx_attn_" + str(layer),
    )(q, k, vt)


def _split_heads(q):
    lane = lax.broadcasted_iota(jnp.int32, (1, LANES), 1)
    zero = jnp.zeros_like(q)
    return jnp.concatenate(
        [jnp.where((lane >= j * NA_HEAD_DIM) & (lane < (j + 1) * NA_HEAD_DIM), q, zero)
         for j in range(LANES // NA_HEAD_DIM)], axis=0)


def _merge_heads(o_t, nq):
    parts = [o_t[j * NA_HEAD_DIM:(j + 1) * NA_HEAD_DIM, j * nq:(j + 1) * nq]
             for j in range(LANES // NA_HEAD_DIM)]
    return jnp.concatenate(parts, axis=0).T


def _na_ctx_kernel(q_ref, k_ref, vt_ref, o_ref):
    q2 = _split_heads(q_ref[...])
    s = _dot_nt(k_ref[...], q2)
    p = jnp.exp2(s - jnp.max(s, axis=0, keepdims=True))
    l = jnp.sum(p, axis=0, keepdims=True)
    o_t = _dot(vt_ref[...], p.astype(BF16)) * (1.0 / l)
    o_ref[...] = _merge_heads(o_t, q_ref.shape[0]).astype(BF16)


def _na_ctx_attention(q, k, vt, *, n_lat, layer):
    b, n, width = q.shape
    n_ctx = n - n_lat
    assert vt.shape[3] == n_ctx
    t = n_lat // n_ctx
    blk = pl.BlockSpec((None, n_ctx, LANES), lambda bi, h: (bi, t, h))
    return pl.pallas_call(
        _na_ctx_kernel,
        grid=(b, width // LANES),
        in_specs=[blk, blk, pl.BlockSpec((None, None, LANES, n_ctx), lambda bi, h: (bi, t, h, 0))],
        out_specs=pl.BlockSpec((None, n_ctx, LANES), lambda bi, h: (bi, 0, h)),
        out_shape=jax.ShapeDtypeStruct((b, n_ctx, width), BF16),
        compiler_params=_params("parallel", "parallel"),
        name="ctx_attn_" + str(layer),
    )(q, k, vt)


def _na_window_start(blk, rows, lib):
    lo = lib.minimum(lib.maximum(NA_BLOCK_ROWS * blk - NA_KH // 2, 0), rows - NA_KH)
    return lib.minimum(lo, rows - NA_WIN_ROWS)


def _na_kernel(q_ref, k_ref, vt_ref, bias_ref, o_ref, sl_ref, sc_ref, *, n_lat, n_ctx, rows):
    slab = vt_ref.shape[-1]
    nq = NA_BLOCK_ROWS * GRID_W
    n_blocks = rows // NA_BLOCK_ROWS
    nk = NA_WIN_ROWS * GRID_W
    kc = k_ref[n_lat:n_lat + n_ctx, :]
    n_here = q_ref.shape[0] // nq

    def scores(i):
        blk = pl.program_id(2) * NA_STEP_BLOCKS + i
        kind = jnp.where(blk == 0, 1, jnp.where(blk == n_blocks - 1, 2, 0))
        ws = _na_window_start(blk, rows, jnp)
        q2 = _split_heads(q_ref[i * nq:(i + 1) * nq, :])
        s_loc = _dot_nt(k_ref[pl.ds(pl.multiple_of(ws * GRID_W, slab), nk), :], q2) + bias_ref[kind]
        s_ctx = _dot_nt(kc, q2)
        sl_ref[i % 2, :, :s_loc.shape[1]] = s_loc
        sc_ref[i % 2, :, :s_ctx.shape[1]] = s_ctx
        return ws, jnp.maximum(jnp.max(s_loc, axis=0, keepdims=True), jnp.max(s_ctx, axis=0, keepdims=True))

    nxt = scores(0)
    for i in range(n_here):
        ws, m = nxt
        if i + 1 < n_here:
            nxt = scores(i + 1)
        parts = []
        for j in range(LANES // NA_HEAD_DIM):
            t = slice(j * nq, (j + 1) * nq)
            p_loc = jnp.exp2(sl_ref[i % 2, :, t] - m[:, t]).astype(BF16)
            p_ctx = jnp.exp2(sc_ref[i % 2, :, t] - m[:, t]).astype(BF16)
            o_t = _pv_t(vt_ref, ws * GRID_W // slab, p_loc, SUM_ROWS)
            o_t = o_t + _pv_t(vt_ref, n_lat // slab, p_ctx, SUM_ROWS)
            parts.append(o_t[j * NA_HEAD_DIM:(j + 1) * NA_HEAD_DIM, :] * (1.0 / o_t[LANES:LANES + 1, :]))
        o_ref[i * nq:(i + 1) * nq, :] = jnp.concatenate(parts, axis=0).T.astype(BF16)


def _na_bias(rpb, rows):
    h, n_dr, n_dc = rpb.shape
    n_blocks = rows // NA_BLOCK_ROWS
    lead = GRID_W - NA_KW
    wide = 2 * GRID_W
    w = jnp.pad(rpb * LOG2E, ((0, 0), (0, 0), (lead, wide - lead - n_dc)), constant_values=MASK_VALUE)
    flat = jnp.broadcast_to(w[:, :, None, :], (h, n_dr, GRID_W, wide)).reshape(h, n_dr, GRID_W * wide)
    skew = flat[:, :, GRID_W - 1:GRID_W - 1 + GRID_W * (wide - 1)].reshape(h, n_dr, GRID_W, wide - 1)
    qc = jnp.arange(GRID_W)
    cs = jnp.clip(qc - NA_KW // 2, 0, GRID_W - NA_KW)
    col_ok = (qc[None, :] >= cs[:, None]) & (qc[None, :] < cs[:, None] + NA_KW)
    c_t = jnp.swapaxes(jnp.where(col_ok, skew[..., :GRID_W], MASK_VALUE), -1, -2)
    masked = jnp.full((h, GRID_W, GRID_W), MASK_VALUE, F32)
    kinds = []
    for blk in (1, 0, n_blocks - 1):
        ws = _na_window_start(blk, rows, np)
        strips = []
        for j in range(NA_WIN_ROWS):
            kr = ws + j
            blocks = []
            for i in range(NA_BLOCK_ROWS):
                r = NA_BLOCK_ROWS * blk + i
                rs = min(max(r - NA_KH // 2, 0), rows - NA_KH)
                blocks.append(c_t[:, kr - r + NA_KH - 1] if rs <= kr < rs + NA_KH else masked)
            strips.append(jnp.concatenate(blocks, axis=-1))
        kinds.append(jnp.concatenate(strips, axis=1))
    bias = jnp.stack(kinds, axis=1)
    sub = LANES // NA_HEAD_DIM
    bias = bias.reshape(h // sub, sub, len(kinds), NA_WIN_ROWS * GRID_W, NA_BLOCK_ROWS * GRID_W)
    return jnp.concatenate([bias[:, j] for j in range(sub)], axis=-1)


def _na_attention(q, k, vt, bias, *, n_lat, layer):
    b, n, width = q.shape
    n_ctx = n - n_lat
    rows = n_lat // GRID_W
    n_slab, slab = vt.shape[1], vt.shape[3]
    tq = NA_STEP_BLOCKS * NA_BLOCK_ROWS * GRID_W
    kern = functools.partial(_na_kernel, n_lat=n_lat, n_ctx=n_ctx, rows=rows)
    return pl.pallas_call(
        kern,
        grid=(width // LANES, b, n_lat // tq),
        in_specs=[
            pl.BlockSpec((None, tq, LANES), lambda hp, bi, rb: (bi, rb, hp)),
            pl.BlockSpec((None, n, LANES), lambda hp, bi, rb: (bi, 0, hp)),
            pl.BlockSpec((None, n_slab, LANES, slab), lambda hp, bi, rb: (bi, 0, hp, 0)),
            pl.BlockSpec((None,) + bias.shape[1:], lambda hp, bi, rb: (hp, 0, 0, 0)),
        ],
        out_specs=pl.BlockSpec((None, tq, LANES), lambda hp, bi, rb: (bi, rb, hp)),
        out_shape=jax.ShapeDtypeStruct((b, n_lat, width), BF16),
        scratch_shapes=[pltpu.VMEM((2, bias.shape[2], bias.shape[3] + SKEW_LANES), F32),
                        pltpu.VMEM((2, n_ctx, bias.shape[3] + SKEW_LANES), F32)],
        compiler_params=_params("parallel", "parallel", "arbitrary"),
        name="na_attn_" + str(layer),
    )(q, k, vt, bias)


def _out_proj_kernel(o_ref, oc_ref, g_ref, w_ref, x_ref, xc_ref, mod_ref, xo_ref, *, n_lat_tiles):
    is_ctx = pl.program_id(1) >= n_lat_tiles
    o = jnp.where(is_ctx, oc_ref[...], o_ref[...])
    x = jnp.where(is_ctx, xc_ref[...], x_ref[...])
    a = (o.astype(F32) * g_ref[...].astype(F32)).astype(BF16)
    xo_ref[...] = x + mod_ref[0, 2:3, :] * _dot(a, w_ref[...])


def _out_proj_final_kernel(o_ref, g_ref, w_ref, x_ref, mod_ref, fg_ref, y_ref):
    a = (o_ref[...].astype(F32) * g_ref[...].astype(F32)).astype(BF16)
    xn = x_ref[...] + mod_ref[0, 2:3, :] * _dot(a, w_ref[...])
    y_ref[...] = xn * _rms(xn) * fg_ref[...]


def _out_proj(o, o_ctx, gs, w_out, stream, mod3, layer, n_lat, final_g=None):
    x_lat, x_ctx, ctx_tile = stream
    b, _, d = x_lat.shape
    tm = ROW_TILE
    n_lat_tiles = n_lat // tm
    n = n_lat + tm
    ctx_row = b

    def mod_idx(bi, t):
        return (layer * MOD_ROWS + jnp.where(t < n_lat_tiles, bi, ctx_row), 0, 0)

    row = lambda bi, t: (bi, t, 0)
    lat_row = lambda bi, t: (bi, jnp.minimum(t, n_lat_tiles - 1), 0)
    w = o.shape[2]
    head = [pl.BlockSpec((None, tm, w), lat_row)]
    mid = [pl.BlockSpec((None, tm, w), row), pl.BlockSpec(w_out.shape, lambda bi, t: (0, 0)),
           pl.BlockSpec((None, tm, d), lat_row)]
    if final_g is None:
        return pl.pallas_call(
            functools.partial(_out_proj_kernel, n_lat_tiles=n_lat_tiles),
            grid=(b, n // tm),
            in_specs=head + [pl.BlockSpec((None, tm, w), lambda bi, t: (bi, 0, 0))] + mid
            + [pl.BlockSpec((None, tm, d), lambda bi, t: (bi, ctx_tile, 0)), pl.BlockSpec((1, 3, d), mod_idx)],
            out_specs=pl.BlockSpec((None, tm, d), row),
            out_shape=jax.ShapeDtypeStruct((b, n, d), F32),
            input_output_aliases={4: 0} if x_lat is x_ctx else {},
            compiler_params=_params("parallel", "parallel"),
            name="out_proj_" + str(layer),
        )(o, o_ctx, gs, w_out, x_lat, x_ctx, mod3)
    return pl.pallas_call(
        _out_proj_final_kernel,
        grid=(b, n_lat_tiles),
        in_specs=head + mid + [pl.BlockSpec((1, 3, d), mod_idx), pl.BlockSpec((1, d), lambda bi, t: (0, 0))],
        out_specs=pl.BlockSpec((None, tm, d), row),
        out_shape=jax.ShapeDtypeStruct((b, n_lat, d), F32),
        compiler_params=_params("parallel", "parallel"),
        name="out_proj_final",
    )(o, gs, w_out, x_lat, mod3, final_g.reshape(1, d))


def _rope_tables(n_lat, n_ctx, rot_dim):
    n_freq = rot_dim // 4
    inv = ROPE_THETA ** (-jnp.arange(n_freq, dtype=F32) / n_freq)
    t = jnp.arange(n_lat)
    ang_r = (t // GRID_W).astype(F32)[:, None] * inv
    ang_c = (t % GRID_W).astype(F32)[:, None] * inv
    ang = jnp.concatenate([ang_r, ang_r, ang_c, ang_c], axis=-1)
    cos, sin = jnp.cos(ang), jnp.sin(ang)
    first = (jnp.arange(rot_dim) % (2 * n_freq)) < n_freq
    sin_up = jnp.where(first, -sin, 0.0)
    sin_dn = jnp.where(first, 0.0, sin)
    pad = LANES - rot_dim

    def finish(tab, fill):
        tab = jnp.pad(tab, ((0, 0), (0, pad)), constant_values=fill)
        return jnp.pad(tab, ((0, n_ctx), (0, 0)), constant_values=fill)

    return finish(cos, 1.0), finish(sin_up, 0.0), finish(sin_dn, 0.0)


def _mla_weights(w_in, w_uq, w_ukv):
    o_kv = MLA_Q_LORA
    o_kr = o_kv + MLA_KV_LORA
    o_g = o_kr + MLA_ROPE
    k_r = jnp.pad(w_in[:, o_kr:o_g], ((0, 0), (0, LANES - MLA_ROPE)))
    w_perm = jnp.concatenate([w_in[:, :o_kr], w_in[:, o_g:], k_r], axis=1).astype(BF16)
    uq = w_uq.reshape(MLA_Q_LORA, MLA_HEADS, MLA_NOPE + MLA_ROPE)
    uq = jnp.pad(uq, ((0, 0), (0, 0), (0, MLA_QK_PAD - MLA_NOPE - MLA_ROPE)))
    uq = uq.reshape(MLA_Q_LORA, MLA_HEADS * MLA_QK_PAD).astype(BF16)
    ukv = w_ukv.reshape(MLA_KV_LORA, MLA_HEADS, MLA_NOPE + MLA_V).astype(BF16)
    uk = ukv[:, :, :MLA_NOPE].reshape(MLA_KV_LORA, MLA_HEADS * MLA_NOPE)
    uv = ukv[:, :, MLA_NOPE:].reshape(MLA_KV_LORA, MLA_HEADS * MLA_V)
    return w_perm, uq, uk, uv


def kernel(x, c, ctx, c_ctx, mod_w, mod_b, norm_g, final_g, ga_w_in, ga_q_g, ga_k_g, ga_w_out, na_w_in, na_rpb, na_w_out, mla_w_in, mla_q_g, mla_kv_g, mla_w_uq, mla_w_ukv, mla_w_out):
    b, n_lat, d = x.shape
    n_ctx = ctx.shape[1]
    depth = mod_w.shape[0]
    assert n_lat % ROW_TILE == 0 and n_ctx == ROW_TILE and n_lat % n_ctx == 0
    assert b < MOD_ROWS and n_lat % (NA_STEP_BLOCKS * NA_BLOCK_ROWS * GRID_W) == 0
    assert n_lat // GRID_W >= NA_WIN_ROWS + NA_BLOCK_ROWS

    stream = (x, ctx, 0)
    cc = jnp.concatenate([c, c_ctx[None, :], jnp.zeros((MOD_ROWS - b - 1, d), F32)], axis=0)
    mod3 = _modulation(cc, mod_w, mod_b).reshape(depth * MOD_ROWS, 3, d)

    tab_a = _rope_tables(n_lat, n_ctx, GQA_HEAD_DIM)
    tab_m = _rope_tables(n_lat, n_ctx, MLA_ROPE)

    out = None
    for i in range(depth):
        kind, j = i % N_MIXERS, i // N_MIXERS
        need_ctx = i < depth - 1
        if kind == 0:
            n_q = d // GQA_HEAD_DIM
            n_kv = n_q // GQA_GROUP
            kern = functools.partial(_gqa_proj_kernel, n_q=n_q, n_kv=n_kv, scale=GQA_HEAD_DIM ** -0.5 * LOG2E)
            consts = [ga_w_in[j].astype(BF16), ga_q_g[j].reshape(1, -1), ga_k_g[j].reshape(1, -1)]
            kvw = n_kv * GQA_HEAD_DIM
            q, k, vt, gs = _project(kern, stream, mod3, i, norm_g[i], consts, tab_a, [d, kvw, kvw, d], n_lat)
            o = _flash_attention(q, k, vt, group=GQA_GROUP, dk=GQA_HEAD_DIM, n_lat=n_lat, layer=i)
            if need_ctx:
                oc = _ctx_attention(q, k, vt, group=GQA_GROUP, dk=GQA_HEAD_DIM, n_lat=n_lat, layer=i)
            w_out = ga_w_out[j]
        elif kind == 1:
            kern = functools.partial(_na_proj_kernel, width=d, scale=NA_HEAD_DIM ** -0.5 * LOG2E)
            q, k, vt, gs = _project(kern, stream, mod3, i, norm_g[i], [na_w_in[j].astype(BF16)], (), [d, d, d, d], n_lat)
            o = _na_attention(q, k, vt, _na_bias(na_rpb[j], n_lat // GRID_W), n_lat=n_lat, layer=i)
            if need_ctx:
                oc = _na_ctx_attention(q, k, vt, n_lat=n_lat, layer=i)
            w_out = na_w_out[j]
        else:
            kern = functools.partial(_mla_proj_kernel, width=d, scale=(MLA_NOPE + MLA_ROPE) ** -0.5 * LOG2E)
            w_perm, uq, uk, uv = _mla_weights(mla_w_in[j], mla_w_uq[j], mla_w_ukv[j])
            consts = [w_perm, mla_q_g[j].reshape(1, -1), mla_kv_g[j].reshape(1, -1), uq, uk, uv]
            widths = [MLA_HEADS * MLA_QK_PAD, MLA_HEADS * MLA_QK_PAD, MLA_HEADS * MLA_V, d]
            q, k, vt, gs = _project(kern, stream, mod3, i, norm_g[i], consts, tab_m, widths, n_lat)
            o = _flash_attention(q, k, vt, group=1, dk=MLA_QK_PAD, n_lat=n_lat, layer=i)
            if need_ctx:
                oc = _ctx_attention(q, k, vt, group=1, dk=MLA_QK_PAD, n_lat=n_lat, layer=i)
            w_out = mla_w_out[j]
        if need_ctx:
            xs = _out_proj(o, oc, gs, w_out.astype(BF16), stream, mod3, i, n_lat)
            stream = (xs, xs, n_lat // ROW_TILE)
        else:
            out = _out_proj(o, None, gs, w_out.astype(BF16), stream, mod3, i, n_lat, final_g=final_g)
    return out
```

```python
import functools

import jax
import numpy as np
import jax.numpy as jnp
from jax import lax
from jax.experimental import pallas as pl
from jax.experimental.pallas import tpu as pltpu

F32 = jnp.float32
BF16 = jnp.bfloat16

NORM_EPS = 1e-6
ROPE_THETA = 10000.0
GRID_W = 64
N_MIXERS = 3
LANES = 128
MOD_ROWS = 8
MASK_VALUE = -1e30
LOG2E = 1.4426950408889634

GQA_HEAD_DIM = 128
GQA_GROUP = 4
NA_HEAD_DIM = 64
NA_KH = 8
NA_KW = 16
NA_BLOCK_ROWS = 4
NA_WIN_ROWS = 12
NA_STEP_BLOCKS = 4
MLA_HEADS = 8
MLA_Q_LORA = 512
MLA_KV_LORA = 256
MLA_NOPE = 128
MLA_ROPE = 64
MLA_V = 128
MLA_QK_PAD = 256

ROW_TILE = 256
Q_COLS = 1024
Q_TILE = 256
SUM_ROWS = 16
SKEW_LANES = 128
SAFE_SCORE_BOUND = 60.0
BOUNDED_UNROLL = 5
BOUND_SLACK = 1.02
KV_CHUNKS = (768, 512, 256)
VMEM_LIMIT = 48 * 1024 * 1024


def _params(*sem):
    return pltpu.CompilerParams(dimension_semantics=sem, vmem_limit_bytes=VMEM_LIMIT)


def _silu(v):
    return v * (1.0 / (1.0 + jnp.exp(-v)))


def _rms(v):
    return lax.rsqrt(jnp.mean(v * v, axis=-1, keepdims=True) + NORM_EPS)


def _rope(v, cos, sin_up, sin_dn, half):
    w = v.shape[-1]
    return v * cos + pltpu.roll(v, w - half, 1) * sin_up + pltpu.roll(v, half, 1) * sin_dn


def _dot(a, b):
    return jnp.dot(a, b, preferred_element_type=F32)


def _dot_nt(a, b):
    return lax.dot_general(a, b, (((1,), (1,)), ((), ())), preferred_element_type=F32)


def _mod_kernel(c_ref, w_ref, b_ref, o_ref):
    a = _silu(c_ref[...])
    w = w_ref[0]
    a_hi = a.astype(BF16)
    a_lo = (a - a_hi.astype(F32)).astype(BF16)
    w_hi = w.astype(BF16)
    w_lo = (w - w_hi.astype(F32)).astype(BF16)
    o_ref[0] = _dot(a_hi, w_hi) + _dot(a_lo, w_hi) + _dot(a_hi, w_lo) + b_ref[0]


def _modulation(cc, mod_w, mod_b):
    depth, d, n3 = mod_w.shape
    tn = 1024
    return pl.pallas_call(
        _mod_kernel,
        grid=(depth, n3 // tn),
        in_specs=[
            pl.BlockSpec((MOD_ROWS, d), lambda l, j: (0, 0)),
            pl.BlockSpec((1, d, tn), lambda l, j: (l, 0, j)),
            pl.BlockSpec((1, 1, tn), lambda l, j: (l, 0, j)),
        ],
        out_specs=pl.BlockSpec((1, MOD_ROWS, tn), lambda l, j: (l, 0, j)),
        out_shape=jax.ShapeDtypeStruct((depth, MOD_ROWS, n3), F32),
        compiler_params=_params("parallel", "parallel"),
        name="adaln_modulation",
    )(cc, mod_w, mod_b.reshape(depth, 1, n3))


def _modulated_norm(x_ref, xc_ref, ng_ref, mod_ref, n_lat_tiles):
    x = jnp.where(pl.program_id(1) >= n_lat_tiles, xc_ref[...], x_ref[...])
    shift = mod_ref[0, 0:1, :]
    scale = mod_ref[0, 1:2, :]
    return ((x * _rms(x)) * ng_ref[...] * (1.0 + scale) + shift).astype(BF16)


def _gqa_proj_kernel(x_ref, xc_ref, mod_ref, ng_ref, w_ref, qg_ref, kg_ref, cos_ref, su_ref, sd_ref,
                     q_ref, k_ref, vt_ref, g_ref, *, n_lat_tiles, n_q, n_kv, scale):
    h = _modulated_norm(x_ref, xc_ref, ng_ref, mod_ref, n_lat_tiles)
    cos, su, sd = cos_ref[...], su_ref[...], sd_ref[...]
    hd = GQA_HEAD_DIM
    qw, kw = n_q * hd, n_kv * hd
    q = _dot(h, w_ref[:, :qw])
    for i in range(n_q):
        qh = q[:, i * hd:(i + 1) * hd]
        qh = qh * _rms(qh) * qg_ref[...]
        q_ref[:, i * hd:(i + 1) * hd] = (_rope(qh, cos, su, sd, hd // 4) * scale).astype(BF16)
    k = _dot(h, w_ref[:, qw:qw + kw])
    for i in range(n_kv):
        kh = k[:, i * hd:(i + 1) * hd]
        kh = kh * _rms(kh) * kg_ref[...]
        k_ref[:, i * hd:(i + 1) * hd] = _rope(kh, cos, su, sd, hd // 4).astype(BF16)
    v = _dot(h, w_ref[:, qw + kw:qw + 2 * kw])
    for i in range(n_kv):
        vt_ref[i * hd:(i + 1) * hd, :] = v[:, i * hd:(i + 1) * hd].T.astype(BF16)
    g_ref[...] = _silu(_dot(h, w_ref[:, qw + 2 * kw:])).astype(BF16)


def _na_proj_kernel(x_ref, xc_ref, mod_ref, ng_ref, w_ref, q_ref, k_ref, vt_ref, g_ref,
                    *, n_lat_tiles, width, scale):
    h = _modulated_norm(x_ref, xc_ref, ng_ref, mod_ref, n_lat_tiles)
    q_ref[...] = (_dot(h, w_ref[:, :width]) * scale).astype(BF16)
    k_ref[...] = _dot(h, w_ref[:, width:2 * width]).astype(BF16)
    v = _dot(h, w_ref[:, 2 * width:3 * width])
    for i in range(width // LANES):
        vt_ref[i * LANES:(i + 1) * LANES, :] = v[:, i * LANES:(i + 1) * LANES].T.astype(BF16)
    g_ref[...] = _silu(_dot(h, w_ref[:, 3 * width:])).astype(BF16)


def _mla_proj_kernel(x_ref, xc_ref, mod_ref, ng_ref, w_ref, qg_ref, kvg_ref, wuq_ref, wuk_ref, wuv_ref,
                     cos_ref, su_ref, sd_ref, q_ref, k_ref, vt_ref, g_ref, *, n_lat_tiles, width, scale):
    h = _modulated_norm(x_ref, xc_ref, ng_ref, mod_ref, n_lat_tiles)
    cos, su, sd = cos_ref[...], su_ref[...], sd_ref[...]
    o_kv = MLA_Q_LORA
    o_g = o_kv + MLA_KV_LORA
    o_kr = o_g + width
    c_q = _dot(h, w_ref[:, :o_kv])
    c_q = (c_q * _rms(c_q) * qg_ref[...]).astype(BF16)
    q = _dot(c_q, wuq_ref[...])
    for i in range(MLA_HEADS):
        a = i * MLA_QK_PAD
        q_ref[:, a:a + MLA_NOPE] = (q[:, a:a + MLA_NOPE] * scale).astype(BF16)
        q_ref[:, a + MLA_NOPE:a + MLA_QK_PAD] = (
            _rope(q[:, a + MLA_NOPE:a + MLA_QK_PAD], cos, su, sd, MLA_ROPE // 4) * scale).astype(BF16)
    c_kv = _dot(h, w_ref[:, o_kv:o_g])
    c_kv = (c_kv * _rms(c_kv) * kvg_ref[...]).astype(BF16)
    k_r = _rope(_dot(h, w_ref[:, o_kr:]), cos, su, sd, MLA_ROPE // 4).astype(BF16)
    k_nope = _dot(c_kv, wuk_ref[...])
    for i in range(MLA_HEADS):
        a = i * MLA_QK_PAD
        k_ref[:, a:a + MLA_NOPE] = k_nope[:, i * MLA_NOPE:(i + 1) * MLA_NOPE].astype(BF16)
        k_ref[:, a + MLA_NOPE:a + MLA_QK_PAD] = k_r
    v = _dot(c_kv, wuv_ref[...])
    for i in range(MLA_HEADS):
        vt_ref[i * MLA_V:(i + 1) * MLA_V, :] = v[:, i * MLA_V:(i + 1) * MLA_V].T.astype(BF16)
    g_ref[...] = _silu(_dot(h, w_ref[:, o_g:o_kr])).astype(BF16)


def _project(kern, stream, mod3, layer, norm_g, consts, tables, out_widths, n_lat):
    x_lat, x_ctx, ctx_tile = stream
    b, _, d = x_lat.shape
    tm = ROW_TILE
    n_lat_tiles = n_lat // tm
    n = n_lat + tm
    ctx_row = b

    def mod_idx(bi, t):
        return (layer * MOD_ROWS + jnp.where(t < n_lat_tiles, bi, ctx_row), 0, 0)

    in_specs = [
        pl.BlockSpec((None, tm, d), lambda bi, t: (bi, jnp.minimum(t, n_lat_tiles - 1), 0)),
        pl.BlockSpec((None, tm, d), lambda bi, t: (bi, ctx_tile, 0)),
        pl.BlockSpec((1, 3, d), mod_idx),
        pl.BlockSpec((1, d), lambda bi, t: (0, 0)),
    ]
    in_specs += [pl.BlockSpec(a.shape, lambda bi, t: (0, 0)) for a in consts]
    in_specs += [pl.BlockSpec((tm, LANES), lambda bi, t: (t, 0)) for _ in tables]
    out_specs = [pl.BlockSpec((None, tm, w), lambda bi, t: (bi, t, 0)) for w in out_widths]
    out_shape = [jax.ShapeDtypeStruct((b, n, w), BF16) for w in out_widths]
    out_specs[2] = pl.BlockSpec((None, None, out_widths[2], tm), lambda bi, t: (bi, t, 0, 0))
    out_shape[2] = jax.ShapeDtypeStruct((b, n // tm, out_widths[2], tm), BF16)
    return pl.pallas_call(
        functools.partial(kern, n_lat_tiles=n_lat_tiles),
        grid=(b, n // tm),
        in_specs=in_specs,
        out_specs=out_specs,
        out_shape=out_shape,
        compiler_params=_params("parallel", "parallel"),
        name="proj_" + str(layer),
    )(x_lat, x_ctx, mod3, norm_g.reshape(1, d), *consts, *tables)


def _stack_heads(q_ref, group, dk):
    return jnp.concatenate([q_ref[:, g * dk:(g + 1) * dk] for g in range(group)], axis=0)


def _unstack_heads(o_t, o_ref, group):
    tq = o_ref.shape[0]
    for g in range(group):
        o_ref[:, g * LANES:(g + 1) * LANES] = o_t[:, g * tq:(g + 1) * tq].T.astype(BF16)


def _pv_t(vt_ref, first_slab, p, sum_rows=0):
    slab = vt_ref.shape[-1]
    acc = None
    for i in range(p.shape[0] // slab):
        vt = vt_ref[first_slab + i]
        if sum_rows:
            vt = jnp.concatenate([vt, jnp.ones((sum_rows, slab), vt.dtype)], axis=0)
        t = _dot(vt, p[i * slab:(i + 1) * slab, :])
        acc = t if acc is None else acc + t
    return acc


def _flash_kernel(q_ref, k_ref, vt_ref, o_ref, sa_ref, sb_ref, acc_ref, m_ref, *, group, dk, bk):
    slab = vt_ref.shape[-1]
    n_chunks = k_ref.shape[0] // bk
    q = _stack_heads(q_ref, group, dk)
    tiles = [slice(j * Q_TILE, (j + 1) * Q_TILE) for j in range(q.shape[0] // Q_TILE)]

    def scores(c, s_ref):
        k = k_ref[pl.ds(pl.multiple_of(c * bk, bk), bk), :]
        out = []
        for t in tiles:
            s = _dot_nt(k, q[t, :])
            s_ref[:, t] = s
            out.append(jnp.max(s, axis=0, keepdims=True))
        return out

    def update(s_ref, s_max, c, t):
        m = m_ref[:, t]
        m_new = jnp.maximum(m, s_max)
        alpha = jnp.exp2(m - m_new)
        p = jnp.exp2(s_ref[:, t] - m_new).astype(BF16)
        m_ref[:, t] = m_new
        acc_ref[:, t] = alpha * acc_ref[:, t] + _pv_t(vt_ref, c * (bk // slab), p, SUM_ROWS)

    def step(c, cur_ref, cur_max, nxt_ref):
        k = k_ref[pl.ds(pl.multiple_of((c + 1) * bk, bk), bk), :]
        nxt_max = []
        for j, t in enumerate(tiles):
            s = _dot_nt(k, q[t, :])
            nxt_ref[:, t] = s
            nxt_max.append(jnp.max(s, axis=0, keepdims=True))
            update(cur_ref, cur_max[j], c, t)
        return nxt_max

    m_ref[...] = jnp.full(m_ref.shape, MASK_VALUE, F32)
    acc_ref[...] = jnp.zeros(acc_ref.shape, F32)
    max_a = scores(0, sa_ref)

    def body(i, max_a):
        max_b = step(2 * i, sa_ref, max_a, sb_ref)
        return tuple(step(2 * i + 1, sb_ref, max_b, sa_ref))

    pairs = (n_chunks - 1) // 2
    max_a = lax.fori_loop(0, pairs, body, tuple(max_a))
    last = 2 * pairs
    if n_chunks - last == 2:
        max_b = step(last, sa_ref, max_a, sb_ref)
        last, s_ref, s_max = last + 1, sb_ref, max_b
    else:
        s_ref, s_max = sa_ref, max_a
    for j, t in enumerate(tiles):
        update(s_ref, s_max[j], last, t)
    _unstack_heads(acc_ref[:LANES, :] * (1.0 / acc_ref[LANES:LANES + 1, :]), o_ref, group)


def _flash_bounded_kernel(q_ref, k_ref, vt_ref, bound_ref, o_ref, acc_ref, *, group, dk, bk):
    slab = vt_ref.shape[-1]
    n_chunks = k_ref.shape[0] // bk
    q = _stack_heads(q_ref, group, dk)
    tiles = [slice(0, q.shape[0])]
    bound = bound_ref[:, :1]
    acc_ref[...] = jnp.zeros(acc_ref.shape, F32)

    def chunk(c):
        k = k_ref[pl.ds(pl.multiple_of(c * bk, bk), bk), :]
        for t in tiles:
            p = jnp.exp2(_dot_nt(k, q[t, :]) - bound).astype(BF16)
            acc_ref[:, t] += _pv_t(vt_ref, c * (bk // slab), p, SUM_ROWS)

    def body(i, carry):
        for u in range(BOUNDED_UNROLL):
            chunk(BOUNDED_UNROLL * i + u)
        return carry

    lax.fori_loop(0, n_chunks // BOUNDED_UNROLL, body, 0)
    for c in range(n_chunks - n_chunks % BOUNDED_UNROLL, n_chunks):
        chunk(c)
    _unstack_heads(acc_ref[:LANES, :] * (1.0 / acc_ref[LANES:LANES + 1, :]), o_ref, group)


def _flash_attention(q, k, vt, *, group, dk, n_lat, layer, bound=None):
    b, n, _ = q.shape
    n_kv = k.shape[2] // dk
    n_slab, slab = vt.shape[1], vt.shape[3]
    tq = min(Q_COLS // group, n_lat)
    bk = max(c for c in KV_CHUNKS if n % c == 0)
    assert bk % slab == 0
    nq = group * tq
    in_specs = [
        pl.BlockSpec((None, tq, group * dk), lambda bi, h, t: (bi, t, h)),
        pl.BlockSpec((None, n, dk), lambda bi, h, t: (bi, 0, h)),
        pl.BlockSpec((None, n_slab, LANES, slab), lambda bi, h, t: (bi, 0, h, 0)),
    ]
    common = dict(
        grid=(b, n_kv, n_lat // tq),
        out_specs=pl.BlockSpec((None, tq, group * LANES), lambda bi, h, t: (bi, t, h)),
        out_shape=jax.ShapeDtypeStruct((b, n_lat, n_kv * group * LANES), BF16),
        compiler_params=_params("parallel", "parallel", "parallel"),
    )
    acc = pltpu.VMEM((LANES + SUM_ROWS, nq), F32)

    def general():
        s_buf = pltpu.VMEM((bk, nq + SKEW_LANES), F32)
        return pl.pallas_call(
            functools.partial(_flash_kernel, group=group, dk=dk, bk=bk),
            in_specs=in_specs,
            scratch_shapes=[s_buf, s_buf, acc, pltpu.VMEM((1, nq), F32)],
            name="flash_" + str(layer), **common,
        )(q, k, vt)

    if bound is None:
        return general()

    def bounded():
        return pl.pallas_call(
            functools.partial(_flash_bounded_kernel, group=group, dk=dk, bk=bk),
            in_specs=in_specs + [pl.BlockSpec((1, LANES), lambda bi, h, t: (0, 0))],
            scratch_shapes=[acc],
            name="flash_bounded_" + str(layer), **common,
        )(q, k, vt, jnp.full((1, LANES), bound, F32))

    return lax.cond(bound <= SAFE_SCORE_BOUND, bounded, general)


def _ctx_attn_kernel(q_ref, k_ref, vt_ref, o_ref, *, group, dk):
    q = _stack_heads(q_ref, group, dk)
    s = _dot_nt(k_ref[...], q)
    p = jnp.exp2(s - jnp.max(s, axis=0, keepdims=True))
    l = jnp.sum(p, axis=0, keepdims=True)
    _unstack_heads(_dot(vt_ref[...], p.astype(BF16)) * (1.0 / l), o_ref, group)


def _ctx_attention(q, k, vt, *, group, dk, n_lat, layer):
    b, n, _ = q.shape
    n_ctx = n - n_lat
    n_kv = k.shape[2] // dk
    slab = vt.shape[3]
    assert slab == n_ctx
    t = n_lat // n_ctx
    return pl.pallas_call(
        functools.partial(_ctx_attn_kernel, group=group, dk=dk),
        grid=(b, n_kv),
        in_specs=[
            pl.BlockSpec((None, n_ctx, group * dk), lambda bi, h: (bi, t, h)),
            pl.BlockSpec((None, n_ctx, dk), lambda bi, h: (bi, t, h)),
            pl.BlockSpec((None, None, LANES, slab), lambda bi, h: (bi, t, h, 0)),
        ],
        out_specs=pl.BlockSpec((None, n_ctx, group * LANES), lambda bi, h: (bi, 0, h)),
        out_shape=jax.ShapeDtypeStruct((b, n_ctx, n_kv * group * LANES), BF16),
        compiler_params=_params("parallel", "parallel"),
        name="ctx_attn_" + str(layer),
    )(q, k, vt)


def _split_heads(q):
    lane = lax.broadcasted_iota(jnp.int32, (1, LANES), 1)
    zero = jnp.zeros_like(q)
    return jnp.concatenate(
        [jnp.where((lane >= j * NA_HEAD_DIM) & (lane < (j + 1) * NA_HEAD_DIM), q, zero)
         for j in range(LANES // NA_HEAD_DIM)], axis=0)


def _merge_heads(o_t, nq):
    parts = [o_t[j * NA_HEAD_DIM:(j + 1) * NA_HEAD_DIM, j * nq:(j + 1) * nq]
             for j in range(LANES // NA_HEAD_DIM)]
    return jnp.concatenate(parts, axis=0).T


def _na_ctx_kernel(q_ref, k_ref, vt_ref, o_ref):
    q2 = _split_heads(q_ref[...])
    s = _dot_nt(k_ref[...], q2)
    p = jnp.exp2(s - jnp.max(s, axis=0, keepdims=True))
    l = jnp.sum(p, axis=0, keepdims=True)
    o_t = _dot(vt_ref[...], p.astype(BF16)) * (1.0 / l)
    o_ref[...] = _merge_heads(o_t, q_ref.shape[0]).astype(BF16)


def _na_ctx_attention(q, k, vt, *, n_lat, layer):
    b, n, width = q.shape
    n_ctx = n - n_lat
    assert vt.shape[3] == n_ctx
    t = n_lat // n_ctx
    blk = pl.BlockSpec((None, n_ctx, LANES), lambda bi, h: (bi, t, h))
    return pl.pallas_call(
        _na_ctx_kernel,
        grid=(b, width // LANES),
        in_specs=[blk, blk, pl.BlockSpec((None, None, LANES, n_ctx), lambda bi, h: (bi, t, h, 0))],
        out_specs=pl.BlockSpec((None, n_ctx, LANES), lambda bi, h: (bi, 0, h)),
        out_shape=jax.ShapeDtypeStruct((b, n_ctx, width), BF16),
        compiler_params=_params("parallel", "parallel"),
        name="ctx_attn_" + str(layer),
    )(q, k, vt)


def _na_window_start(blk, rows, lib):
    lo = lib.minimum(lib.maximum(NA_BLOCK_ROWS * blk - NA_KH // 2, 0), rows - NA_KH)
    return lib.minimum(lo, rows - NA_WIN_ROWS)


def _na_kernel(q_ref, k_ref, vt_ref, bias_ref, o_ref, sl_ref, sc_ref, *, n_lat, n_ctx, rows):
    slab = vt_ref.shape[-1]
    nq = NA_BLOCK_ROWS * GRID_W
    n_blocks = rows // NA_BLOCK_ROWS
    nk = NA_WIN_ROWS * GRID_W
    kc = k_ref[n_lat:n_lat + n_ctx, :]
    n_here = q_ref.shape[0] // nq

    def scores(i):
        blk = pl.program_id(2) * NA_STEP_BLOCKS + i
        kind = jnp.where(blk == 0, 1, jnp.where(blk == n_blocks - 1, 2, 0))
        ws = _na_window_start(blk, rows, jnp)
        q2 = _split_heads(q_ref[i * nq:(i + 1) * nq, :])
        s_loc = _dot_nt(k_ref[pl.ds(pl.multiple_of(ws * GRID_W, slab), nk), :], q2) + bias_ref[kind]
        s_ctx = _dot_nt(kc, q2)
        sl_ref[i % 2, :, :s_loc.shape[1]] = s_loc
        sc_ref[i % 2, :, :s_ctx.shape[1]] = s_ctx
        return ws, jnp.maximum(jnp.max(s_loc, axis=0, keepdims=True), jnp.max(s_ctx, axis=0, keepdims=True))

    nxt = scores(0)
    for i in range(n_here):
        ws, m = nxt
        if i + 1 < n_here:
            nxt = scores(i + 1)
        parts = []
        for j in range(LANES // NA_HEAD_DIM):
            t = slice(j * nq, (j + 1) * nq)
            p_loc = jnp.exp2(sl_ref[i % 2, :, t] - m[:, t]).astype(BF16)
            p_ctx = jnp.exp2(sc_ref[i % 2, :, t] - m[:, t]).astype(BF16)
            o_t = _pv_t(vt_ref, ws * GRID_W // slab, p_loc, SUM_ROWS)
            o_t = o_t + _pv_t(vt_ref, n_lat // slab, p_ctx, SUM_ROWS)
            parts.append(o_t[j * NA_HEAD_DIM:(j + 1) * NA_HEAD_DIM, :] * (1.0 / o_t[LANES:LANES + 1, :]))
        o_ref[i * nq:(i + 1) * nq, :] = jnp.concatenate(parts, axis=0).T.astype(BF16)


def _na_bias(rpb, rows):
    h, n_dr, n_dc = rpb.shape
    n_blocks = rows // NA_BLOCK_ROWS
    lead = GRID_W - NA_KW
    wide = 2 * GRID_W
    w = jnp.pad(rpb * LOG2E, ((0, 0), (0, 0), (lead, wide - lead - n_dc)), constant_values=MASK_VALUE)
    flat = jnp.broadcast_to(w[:, :, None, :], (h, n_dr, GRID_W, wide)).reshape(h, n_dr, GRID_W * wide)
    skew = flat[:, :, GRID_W - 1:GRID_W - 1 + GRID_W * (wide - 1)].reshape(h, n_dr, GRID_W, wide - 1)
    qc = jnp.arange(GRID_W)
    cs = jnp.clip(qc - NA_KW // 2, 0, GRID_W - NA_KW)
    col_ok = (qc[None, :] >= cs[:, None]) & (qc[None, :] < cs[:, None] + NA_KW)
    c_t = jnp.swapaxes(jnp.where(col_ok, skew[..., :GRID_W], MASK_VALUE), -1, -2)
    masked = jnp.full((h, GRID_W, GRID_W), MASK_VALUE, F32)
    kinds = []
    for blk in (1, 0, n_blocks - 1):
        ws = _na_window_start(blk, rows, np)
        strips = []
        for j in range(NA_WIN_ROWS):
            kr = ws + j
            blocks = []
            for i in range(NA_BLOCK_ROWS):
                r = NA_BLOCK_ROWS * blk + i
                rs = min(max(r - NA_KH // 2, 0), rows - NA_KH)
                blocks.append(c_t[:, kr - r + NA_KH - 1] if rs <= kr < rs + NA_KH else masked)
            strips.append(jnp.concatenate(blocks, axis=-1))
        kinds.append(jnp.concatenate(strips, axis=1))
    bias = jnp.stack(kinds, axis=1)
    sub = LANES // NA_HEAD_DIM
    bias = bias.reshape(h // sub, sub, len(kinds), NA_WIN_ROWS * GRID_W, NA_BLOCK_ROWS * GRID_W)
    return jnp.concatenate([bias[:, j] for j in range(sub)], axis=-1)


def _na_attention(q, k, vt, bias, *, n_lat, layer):
    b, n, width = q.shape
    n_ctx = n - n_lat
    rows = n_lat // GRID_W
    n_slab, slab = vt.shape[1], vt.shape[3]
    tq = NA_STEP_BLOCKS * NA_BLOCK_ROWS * GRID_W
    kern = functools.partial(_na_kernel, n_lat=n_lat, n_ctx=n_ctx, rows=rows)
    return pl.pallas_call(
        kern,
        grid=(width // LANES, b, n_lat // tq),
        in_specs=[
            pl.BlockSpec((None, tq, LANES), lambda hp, bi, rb: (bi, rb, hp)),
            pl.BlockSpec((None, n, LANES), lambda hp, bi, rb: (bi, 0, hp)),
            pl.BlockSpec((None, n_slab, LANES, slab), lambda hp, bi, rb: (bi, 0, hp, 0)),
            pl.BlockSpec((None,) + bias.shape[1:], lambda hp, bi, rb: (hp, 0, 0, 0)),
        ],
        out_specs=pl.BlockSpec((None, tq, LANES), lambda hp, bi, rb: (bi, rb, hp)),
        out_shape=jax.ShapeDtypeStruct((b, n_lat, width), BF16),
        scratch_shapes=[pltpu.VMEM((2, bias.shape[2], bias.shape[3] + SKEW_LANES), F32),
                        pltpu.VMEM((2, n_ctx, bias.shape[3] + SKEW_LANES), F32)],
        compiler_params=_params("parallel", "parallel", "arbitrary"),
        name="na_attn_" + str(layer),
    )(q, k, vt, bias)


def _out_proj_kernel(o_ref, oc_ref, g_ref, w_ref, x_ref, *refs, n_lat_tiles):
    mod_ref, xo_ref = refs[-2:]
    is_ctx = pl.program_id(1) >= n_lat_tiles
    o = jnp.where(is_ctx, oc_ref[...], o_ref[...])
    x = x_ref[...] if len(refs) == 2 else jnp.where(is_ctx, refs[0][...], x_ref[...])
    a = (o.astype(F32) * g_ref[...].astype(F32)).astype(BF16)
    xo_ref[...] = x + mod_ref[0, 2:3, :] * _dot(a, w_ref[...])


def _out_proj_final_kernel(o_ref, g_ref, w_ref, x_ref, mod_ref, fg_ref, y_ref):
    a = (o_ref[...].astype(F32) * g_ref[...].astype(F32)).astype(BF16)
    xn = x_ref[...] + mod_ref[0, 2:3, :] * _dot(a, w_ref[...])
    y_ref[...] = xn * _rms(xn) * fg_ref[...]


def _out_proj(o, o_ctx, gs, w_out, stream, mod3, layer, n_lat, final_g=None):
    x_lat, x_ctx, ctx_tile = stream
    b, _, d = x_lat.shape
    tm = ROW_TILE
    n_lat_tiles = n_lat // tm
    n = n_lat + tm
    ctx_row = b

    def mod_idx(bi, t):
        return (layer * MOD_ROWS + jnp.where(t < n_lat_tiles, bi, ctx_row), 0, 0)

    row = lambda bi, t: (bi, t, 0)
    lat_row = lambda bi, t: (bi, jnp.minimum(t, n_lat_tiles - 1), 0)
    w = o.shape[2]
    head = [pl.BlockSpec((None, tm, w), lat_row)]
    mid = [pl.BlockSpec((None, tm, w), row), pl.BlockSpec(w_out.shape, lambda bi, t: (0, 0)),
           pl.BlockSpec((None, tm, d), lat_row)]
    if final_g is None:
        combined = x_lat is x_ctx
        specs = head + [pl.BlockSpec((None, tm, w), lambda bi, t: (bi, 0, 0))] + mid[:2]
        if combined:
            specs, xs = specs + [pl.BlockSpec((None, tm, d), row)], [x_lat]
        else:
            specs += [mid[2], pl.BlockSpec((None, tm, d), lambda bi, t: (bi, ctx_tile, 0))]
            xs = [x_lat, x_ctx]
        return pl.pallas_call(
            functools.partial(_out_proj_kernel, n_lat_tiles=n_lat_tiles),
            grid=(b, n // tm),
            in_specs=specs + [pl.BlockSpec((1, 3, d), mod_idx)],
            out_specs=pl.BlockSpec((None, tm, d), row),
            out_shape=jax.ShapeDtypeStruct((b, n, d), F32),
            input_output_aliases={4: 0} if combined else {},
            compiler_params=_params("parallel", "parallel"),
            name="out_proj_" + str(layer),
        )(o, o_ctx, gs, w_out, *xs, mod3)
    return pl.pallas_call(
        _out_proj_final_kernel,
        grid=(b, n_lat_tiles),
        in_specs=head + mid + [pl.BlockSpec((1, 3, d), mod_idx), pl.BlockSpec((1, d), lambda bi, t: (0, 0))],
        out_specs=pl.BlockSpec((None, tm, d), row),
        out_shape=jax.ShapeDtypeStruct((b, n_lat, d), F32),
        compiler_params=_params("parallel", "parallel"),
        name="out_proj_final",
    )(o, gs, w_out, x_lat, mod3, final_g.reshape(1, d))


def _rope_tables(n_lat, n_ctx, rot_dim):
    n_freq = rot_dim // 4
    inv = ROPE_THETA ** (-jnp.arange(n_freq, dtype=F32) / n_freq)
    t = jnp.arange(n_lat)
    ang_r = (t // GRID_W).astype(F32)[:, None] * inv
    ang_c = (t % GRID_W).astype(F32)[:, None] * inv
    ang = jnp.concatenate([ang_r, ang_r, ang_c, ang_c], axis=-1)
    cos, sin = jnp.cos(ang), jnp.sin(ang)
    first = (jnp.arange(rot_dim) % (2 * n_freq)) < n_freq
    sin_up = jnp.where(first, -sin, 0.0)
    sin_dn = jnp.where(first, 0.0, sin)
    pad = LANES - rot_dim

    def finish(tab, fill):
        tab = jnp.pad(tab, ((0, 0), (0, pad)), constant_values=fill)
        return jnp.pad(tab, ((0, n_ctx), (0, 0)), constant_values=fill)

    return finish(cos, 1.0), finish(sin_up, 0.0), finish(sin_dn, 0.0)


def _mla_weights(w_in, w_uq, w_ukv):
    o_kv = MLA_Q_LORA
    o_kr = o_kv + MLA_KV_LORA
    o_g = o_kr + MLA_ROPE
    k_r = jnp.pad(w_in[:, o_kr:o_g], ((0, 0), (0, LANES - MLA_ROPE)))
    w_perm = jnp.concatenate([w_in[:, :o_kr], w_in[:, o_g:], k_r], axis=1).astype(BF16)
    uq = w_uq.reshape(MLA_Q_LORA, MLA_HEADS, MLA_NOPE + MLA_ROPE)
    uq = jnp.pad(uq, ((0, 0), (0, 0), (0, MLA_QK_PAD - MLA_NOPE - MLA_ROPE)))
    uq = uq.reshape(MLA_Q_LORA, MLA_HEADS * MLA_QK_PAD).astype(BF16)
    ukv = w_ukv.reshape(MLA_KV_LORA, MLA_HEADS, MLA_NOPE + MLA_V).astype(BF16)
    uk = ukv[:, :, :MLA_NOPE].reshape(MLA_KV_LORA, MLA_HEADS * MLA_NOPE)
    uv = ukv[:, :, MLA_NOPE:].reshape(MLA_KV_LORA, MLA_HEADS * MLA_V)
    return w_perm, uq, uk, uv


def kernel(x, c, ctx, c_ctx, mod_w, mod_b, norm_g, final_g, ga_w_in, ga_q_g, ga_k_g, ga_w_out, na_w_in, na_rpb, na_w_out, mla_w_in, mla_q_g, mla_kv_g, mla_w_uq, mla_w_ukv, mla_w_out):
    b, n_lat, d = x.shape
    n_ctx = ctx.shape[1]
    depth = mod_w.shape[0]
    assert n_lat % ROW_TILE == 0 and n_ctx == ROW_TILE and n_lat % n_ctx == 0
    assert b < MOD_ROWS and n_lat % (NA_STEP_BLOCKS * NA_BLOCK_ROWS * GRID_W) == 0
    assert n_lat // GRID_W >= NA_WIN_ROWS + NA_BLOCK_ROWS

    stream = (x, ctx, 0)
    cc = jnp.concatenate([c, c_ctx[None, :], jnp.zeros((MOD_ROWS - b - 1, d), F32)], axis=0)
    mod3 = _modulation(cc, mod_w, mod_b).reshape(depth * MOD_ROWS, 3, d)

    tab_a = _rope_tables(n_lat, n_ctx, GQA_HEAD_DIM)
    tab_m = _rope_tables(n_lat, n_ctx, MLA_ROPE)

    out = None
    for i in range(depth):
        kind, j = i % N_MIXERS, i // N_MIXERS
        need_ctx = i < depth - 1
        if kind == 0:
            n_q = d // GQA_HEAD_DIM
            n_kv = n_q // GQA_GROUP
            kern = functools.partial(_gqa_proj_kernel, n_q=n_q, n_kv=n_kv, scale=GQA_HEAD_DIM ** -0.5 * LOG2E)
            consts = [ga_w_in[j].astype(BF16), ga_q_g[j].reshape(1, -1), ga_k_g[j].reshape(1, -1)]
            kvw = n_kv * GQA_HEAD_DIM
            q, k, vt, gs = _project(kern, stream, mod3, i, norm_g[i], consts, tab_a, [d, kvw, kvw, d], n_lat)
            bound = (BOUND_SLACK * GQA_HEAD_DIM ** 0.5 * LOG2E
                     * jnp.max(jnp.abs(ga_q_g[j])) * jnp.max(jnp.abs(ga_k_g[j])))
            o = _flash_attention(q, k, vt, group=GQA_GROUP, dk=GQA_HEAD_DIM, n_lat=n_lat, layer=i, bound=bound)
            if need_ctx:
                oc = _ctx_attention(q, k, vt, group=GQA_GROUP, dk=GQA_HEAD_DIM, n_lat=n_lat, layer=i)
            w_out = ga_w_out[j]
        elif kind == 1:
            kern = functools.partial(_na_proj_kernel, width=d, scale=NA_HEAD_DIM ** -0.5 * LOG2E)
            q, k, vt, gs = _project(kern, stream, mod3, i, norm_g[i], [na_w_in[j].astype(BF16)], (), [d, d, d, d], n_lat)
            o = _na_attention(q, k, vt, _na_bias(na_rpb[j], n_lat // GRID_W), n_lat=n_lat, layer=i)
            if need_ctx:
                oc = _na_ctx_attention(q, k, vt, n_lat=n_lat, layer=i)
            w_out = na_w_out[j]
        else:
            kern = functools.partial(_mla_proj_kernel, width=d, scale=(MLA_NOPE + MLA_ROPE) ** -0.5 * LOG2E)
            w_perm, uq, uk, uv = _mla_weights(mla_w_in[j], mla_w_uq[j], mla_w_ukv[j])
            consts = [w_perm, mla_q_g[j].reshape(1, -1), mla_kv_g[j].reshape(1, -1), uq, uk, uv]
            widths = [MLA_HEADS * MLA_QK_PAD, MLA_HEADS * MLA_QK_PAD, MLA_HEADS * MLA_V, d]
            q, k, vt, gs = _project(kern, stream, mod3, i, norm_g[i], consts, tab_m, widths, n_lat)
            o = _flash_attention(q, k, vt, group=1, dk=MLA_QK_PAD, n_lat=n_lat, layer=i)
            if need_ctx:
                oc = _ctx_attention(q, k, vt, group=1, dk=MLA_QK_PAD, n_lat=n_lat, layer=i)
            w_out = mla_w_out[j]
        if need_ctx:
            xs = _out_proj(o, oc, gs, w_out.astype(BF16), stream, mod3, i, n_lat)
            stream = (xs, xs, n_lat // ROW_TILE)
        else:
            out = _out_proj(o, None, gs, w_out.astype(BF16), stream, mod3, i, n_lat, final_g=final_g)
    return out
```

```python
import functools

import jax
import numpy as np
import jax.numpy as jnp
from jax import lax
from jax.experimental import pallas as pl
from jax.experimental.pallas import tpu as pltpu

F32 = jnp.float32
BF16 = jnp.bfloat16

NORM_EPS = 1e-6
ROPE_THETA = 10000.0
GRID_W = 64
N_MIXERS = 3
LANES = 128
MOD_ROWS = 8
MASK_VALUE = -1e30
LOG2E = 1.4426950408889634

GQA_HEAD_DIM = 128
GQA_GROUP = 4
NA_HEAD_DIM = 64
NA_KH = 8
NA_KW = 16
NA_BLOCK_ROWS = 4
NA_WIN_ROWS = 12
NA_STEP_BLOCKS = 4
MLA_HEADS = 8
MLA_Q_LORA = 512
MLA_KV_LORA = 256
MLA_NOPE = 128
MLA_ROPE = 64
MLA_V = 128
MLA_QK_PAD = 256

ROW_TILE = 256
Q_COLS = 1024
Q_TILE = 256
SUM_ROWS = 16
SKEW_LANES = 128
SAFE_SCORE_BOUND = 50.0
BOUNDED_UNROLL = 5
BOUND_SLACK = 1.02
KV_CHUNKS = (768, 512, 256)
VMEM_LIMIT = 48 * 1024 * 1024


def _params(*sem):
    return pltpu.CompilerParams(dimension_semantics=sem, vmem_limit_bytes=VMEM_LIMIT)


def _silu(v):
    return v * (1.0 / (1.0 + jnp.exp(-v)))


def _rms(v):
    return lax.rsqrt(jnp.mean(v * v, axis=-1, keepdims=True) + NORM_EPS)


def _rope(v, cos, sin_up, sin_dn, half):
    w = v.shape[-1]
    return v * cos + pltpu.roll(v, w - half, 1) * sin_up + pltpu.roll(v, half, 1) * sin_dn


def _dot(a, b):
    return jnp.dot(a, b, preferred_element_type=F32)


def _dot_nt(a, b):
    return lax.dot_general(a, b, (((1,), (1,)), ((), ())), preferred_element_type=F32)


def _mod_kernel(c_ref, w_ref, b_ref, o_ref):
    a = _silu(c_ref[...])
    w = w_ref[0]
    a_hi = a.astype(BF16)
    a_lo = (a - a_hi.astype(F32)).astype(BF16)
    w_hi = w.astype(BF16)
    w_lo = (w - w_hi.astype(F32)).astype(BF16)
    o_ref[0] = _dot(a_hi, w_hi) + _dot(a_lo, w_hi) + _dot(a_hi, w_lo) + b_ref[0]


def _modulation(cc, mod_w, mod_b):
    depth, d, n3 = mod_w.shape
    tn = 1024
    return pl.pallas_call(
        _mod_kernel,
        grid=(depth, n3 // tn),
        in_specs=[
            pl.BlockSpec((MOD_ROWS, d), lambda l, j: (0, 0)),
            pl.BlockSpec((1, d, tn), lambda l, j: (l, 0, j)),
            pl.BlockSpec((1, 1, tn), lambda l, j: (l, 0, j)),
        ],
        out_specs=pl.BlockSpec((1, MOD_ROWS, tn), lambda l, j: (l, 0, j)),
        out_shape=jax.ShapeDtypeStruct((depth, MOD_ROWS, n3), F32),
        compiler_params=_params("parallel", "parallel"),
        name="adaln_modulation",
    )(cc, mod_w, mod_b.reshape(depth, 1, n3))


def _modulated_norm(x_ref, xc_ref, ng_ref, mod_ref, n_lat_tiles):
    x = jnp.where(pl.program_id(1) >= n_lat_tiles, xc_ref[...], x_ref[...])
    shift = mod_ref[0, 0:1, :]
    scale = mod_ref[0, 1:2, :]
    return ((x * _rms(x)) * ng_ref[...] * (1.0 + scale) + shift).astype(BF16)


def _gqa_proj_kernel(x_ref, xc_ref, mod_ref, ng_ref, w_ref, qg_ref, kg_ref, cos_ref, su_ref, sd_ref,
                     q_ref, k_ref, vt_ref, g_ref, *, n_lat_tiles, n_q, n_kv, scale):
    h = _modulated_norm(x_ref, xc_ref, ng_ref, mod_ref, n_lat_tiles)
    cos, su, sd = cos_ref[...], su_ref[...], sd_ref[...]
    hd = GQA_HEAD_DIM
    qw, kw = n_q * hd, n_kv * hd
    q = _dot(h, w_ref[:, :qw])
    for i in range(n_q):
        qh = q[:, i * hd:(i + 1) * hd]
        qh = qh * _rms(qh) * qg_ref[...]
        q_ref[:, i * hd:(i + 1) * hd] = (_rope(qh, cos, su, sd, hd // 4) * scale).astype(BF16)
    k = _dot(h, w_ref[:, qw:qw + kw])
    for i in range(n_kv):
        kh = k[:, i * hd:(i + 1) * hd]
        kh = kh * _rms(kh) * kg_ref[...]
        k_ref[:, i * hd:(i + 1) * hd] = _rope(kh, cos, su, sd, hd // 4).astype(BF16)
    v = _dot(h, w_ref[:, qw + kw:qw + 2 * kw])
    for i in range(n_kv):
        vt_ref[i * hd:(i + 1) * hd, :] = v[:, i * hd:(i + 1) * hd].T.astype(BF16)
    g_ref[...] = _silu(_dot(h, w_ref[:, qw + 2 * kw:])).astype(BF16)


def _na_proj_kernel(x_ref, xc_ref, mod_ref, ng_ref, w_ref, q_ref, k_ref, vt_ref, g_ref,
                    *, n_lat_tiles, width, scale):
    h = _modulated_norm(x_ref, xc_ref, ng_ref, mod_ref, n_lat_tiles)
    q_ref[...] = (_dot(h, w_ref[:, :width]) * scale).astype(BF16)
    k_ref[...] = _dot(h, w_ref[:, width:2 * width]).astype(BF16)
    v = _dot(h, w_ref[:, 2 * width:3 * width])
    for i in range(width // LANES):
        vt_ref[i * LANES:(i + 1) * LANES, :] = v[:, i * LANES:(i + 1) * LANES].T.astype(BF16)
    g_ref[...] = _silu(_dot(h, w_ref[:, 3 * width:])).astype(BF16)


def _mla_proj_kernel(x_ref, xc_ref, mod_ref, ng_ref, w_ref, qg_ref, kvg_ref, wuq_ref, wuk_ref, wuv_ref,
                     cos_ref, su_ref, sd_ref, q_ref, k_ref, vt_ref, g_ref, norm_ref,
                     *, n_lat_tiles, width, scale):
    h = _modulated_norm(x_ref, xc_ref, ng_ref, mod_ref, n_lat_tiles)
    ones = jnp.ones((LANES, LANES), BF16)

    def row_sq(v):
        return _dot((v * v).astype(BF16), ones)

    cos, su, sd = cos_ref[...], su_ref[...], sd_ref[...]
    o_kv = MLA_Q_LORA
    o_g = o_kv + MLA_KV_LORA
    o_kr = o_g + width
    c_q = _dot(h, w_ref[:, :o_kv])
    c_q = (c_q * _rms(c_q) * qg_ref[...]).astype(BF16)
    q = _dot(c_q, wuq_ref[...])
    q_sq = None
    for i in range(MLA_HEADS):
        a = i * MLA_QK_PAD
        q_nope = q[:, a:a + MLA_NOPE] * scale
        q_rope = _rope(q[:, a + MLA_NOPE:a + MLA_QK_PAD], cos, su, sd, MLA_ROPE // 4) * scale
        q_ref[:, a:a + MLA_NOPE] = q_nope.astype(BF16)
        q_ref[:, a + MLA_NOPE:a + MLA_QK_PAD] = q_rope.astype(BF16)
        sq = row_sq(q_nope) + row_sq(q_rope)
        q_sq = sq if q_sq is None else jnp.maximum(q_sq, sq)
    c_kv = _dot(h, w_ref[:, o_kv:o_g])
    c_kv = (c_kv * _rms(c_kv) * kvg_ref[...]).astype(BF16)
    k_r = _rope(_dot(h, w_ref[:, o_kr:]), cos, su, sd, MLA_ROPE // 4)
    k_nope = _dot(c_kv, wuk_ref[...])
    k_sq = None
    for i in range(MLA_HEADS):
        a = i * MLA_QK_PAD
        k_h = k_nope[:, i * MLA_NOPE:(i + 1) * MLA_NOPE]
        k_ref[:, a:a + MLA_NOPE] = k_h.astype(BF16)
        k_ref[:, a + MLA_NOPE:a + MLA_QK_PAD] = k_r.astype(BF16)
        sq = row_sq(k_h)
        k_sq = sq if k_sq is None else jnp.maximum(k_sq, sq)
    k_sq = k_sq + row_sq(k_r)
    half = norm_ref.shape[0] // 2
    norm_ref[:half, :] = jnp.broadcast_to(jnp.max(q_sq, axis=0, keepdims=True), (half, LANES))
    norm_ref[half:, :] = jnp.broadcast_to(jnp.max(k_sq, axis=0, keepdims=True), (half, LANES))
    v = _dot(c_kv, wuv_ref[...])
    for i in range(MLA_HEADS):
        vt_ref[i * MLA_V:(i + 1) * MLA_V, :] = v[:, i * MLA_V:(i + 1) * MLA_V].T.astype(BF16)
    g_ref[...] = _silu(_dot(h, w_ref[:, o_g:o_kr])).astype(BF16)


def _project(kern, stream, mod3, layer, norm_g, consts, tables, out_widths, n_lat, with_norms=False):
    x_lat, x_ctx, ctx_tile = stream
    b, _, d = x_lat.shape
    tm = ROW_TILE
    n_lat_tiles = n_lat // tm
    n = n_lat + tm
    ctx_row = b

    def mod_idx(bi, t):
        return (layer * MOD_ROWS + jnp.where(t < n_lat_tiles, bi, ctx_row), 0, 0)

    in_specs = [
        pl.BlockSpec((None, tm, d), lambda bi, t: (bi, jnp.minimum(t, n_lat_tiles - 1), 0)),
        pl.BlockSpec((None, tm, d), lambda bi, t: (bi, ctx_tile, 0)),
        pl.BlockSpec((1, 3, d), mod_idx),
        pl.BlockSpec((1, d), lambda bi, t: (0, 0)),
    ]
    in_specs += [pl.BlockSpec(a.shape, lambda bi, t: (0, 0)) for a in consts]
    in_specs += [pl.BlockSpec((tm, LANES), lambda bi, t: (t, 0)) for _ in tables]
    out_specs = [pl.BlockSpec((None, tm, w), lambda bi, t: (bi, t, 0)) for w in out_widths]
    out_shape = [jax.ShapeDtypeStruct((b, n, w), BF16) for w in out_widths]
    out_specs[2] = pl.BlockSpec((None, None, out_widths[2], tm), lambda bi, t: (bi, t, 0, 0))
    out_shape[2] = jax.ShapeDtypeStruct((b, n // tm, out_widths[2], tm), BF16)
    if with_norms:
        out_specs.append(pl.BlockSpec((None, None, 2 * MOD_ROWS, LANES), lambda bi, t: (bi, t, 0, 0)))
        out_shape.append(jax.ShapeDtypeStruct((b, n // tm, 2 * MOD_ROWS, LANES), F32))
    return pl.pallas_call(
        functools.partial(kern, n_lat_tiles=n_lat_tiles),
        grid=(b, n // tm),
        in_specs=in_specs,
        out_specs=out_specs,
        out_shape=out_shape,
        compiler_params=_params("parallel", "parallel"),
        name="proj_" + str(layer),
    )(x_lat, x_ctx, mod3, norm_g.reshape(1, d), *consts, *tables)


def _stack_heads(q_ref, group, dk):
    return jnp.concatenate([q_ref[:, g * dk:(g + 1) * dk] for g in range(group)], axis=0)


def _unstack_heads(o_t, o_ref, group):
    tq = o_ref.shape[0]
    for g in range(group):
        o_ref[:, g * LANES:(g + 1) * LANES] = o_t[:, g * tq:(g + 1) * tq].T.astype(BF16)


def _pv_t(vt_ref, first_slab, p, sum_rows=0):
    slab = vt_ref.shape[-1]
    acc = None
    for i in range(p.shape[0] // slab):
        vt = vt_ref[first_slab + i]
        if sum_rows:
            vt = jnp.concatenate([vt, jnp.ones((sum_rows, slab), vt.dtype)], axis=0)
        t = _dot(vt, p[i * slab:(i + 1) * slab, :])
        acc = t if acc is None else acc + t
    return acc


def _flash_kernel(q_ref, k_ref, vt_ref, o_ref, sa_ref, sb_ref, acc_ref, m_ref, *, group, dk, bk):
    slab = vt_ref.shape[-1]
    n_chunks = k_ref.shape[0] // bk
    q = _stack_heads(q_ref, group, dk)
    tiles = [slice(j * Q_TILE, (j + 1) * Q_TILE) for j in range(q.shape[0] // Q_TILE)]

    def scores(c, s_ref):
        k = k_ref[pl.ds(pl.multiple_of(c * bk, bk), bk), :]
        out = []
        for t in tiles:
            s = _dot_nt(k, q[t, :])
            s_ref[:, t] = s
            out.append(jnp.max(s, axis=0, keepdims=True))
        return out

    def update(s_ref, s_max, c, t):
        m = m_ref[:, t]
        m_new = jnp.maximum(m, s_max)
        alpha = jnp.exp2(m - m_new)
        p = jnp.exp2(s_ref[:, t] - m_new).astype(BF16)
        m_ref[:, t] = m_new
        acc_ref[:, t] = alpha * acc_ref[:, t] + _pv_t(vt_ref, c * (bk // slab), p, SUM_ROWS)

    def step(c, cur_ref, cur_max, nxt_ref):
        k = k_ref[pl.ds(pl.multiple_of((c + 1) * bk, bk), bk), :]
        nxt_max = []
        for j, t in enumerate(tiles):
            s = _dot_nt(k, q[t, :])
            nxt_ref[:, t] = s
            nxt_max.append(jnp.max(s, axis=0, keepdims=True))
            update(cur_ref, cur_max[j], c, t)
        return nxt_max

    m_ref[...] = jnp.full(m_ref.shape, MASK_VALUE, F32)
    acc_ref[...] = jnp.zeros(acc_ref.shape, F32)
    max_a = scores(0, sa_ref)

    def body(i, max_a):
        max_b = step(2 * i, sa_ref, max_a, sb_ref)
        return tuple(step(2 * i + 1, sb_ref, max_b, sa_ref))

    pairs = (n_chunks - 1) // 2
    max_a = lax.fori_loop(0, pairs, body, tuple(max_a))
    last = 2 * pairs
    if n_chunks - last == 2:
        max_b = step(last, sa_ref, max_a, sb_ref)
        last, s_ref, s_max = last + 1, sb_ref, max_b
    else:
        s_ref, s_max = sa_ref, max_a
    for j, t in enumerate(tiles):
        update(s_ref, s_max[j], last, t)
    _unstack_heads(acc_ref[:LANES, :] * (1.0 / acc_ref[LANES:LANES + 1, :]), o_ref, group)


def _flash_bounded_kernel(q_ref, k_ref, vt_ref, bound_ref, o_ref, acc_ref, *, group, dk, bk):
    slab = vt_ref.shape[-1]
    n_chunks = k_ref.shape[0] // bk
    q = _stack_heads(q_ref, group, dk)
    tiles = [slice(0, q.shape[0])]
    bound = bound_ref[:, :1]
    acc_ref[...] = jnp.zeros(acc_ref.shape, F32)

    def chunk(c):
        k = k_ref[pl.ds(pl.multiple_of(c * bk, bk), bk), :]
        for t in tiles:
            p = jnp.exp2(_dot_nt(k, q[t, :]) - bound).astype(BF16)
            acc_ref[:, t] += _pv_t(vt_ref, c * (bk // slab), p, SUM_ROWS)

    def body(i, carry):
        for u in range(BOUNDED_UNROLL):
            chunk(BOUNDED_UNROLL * i + u)
        return carry

    lax.fori_loop(0, n_chunks // BOUNDED_UNROLL, body, 0)
    for c in range(n_chunks - n_chunks % BOUNDED_UNROLL, n_chunks):
        chunk(c)
    _unstack_heads(acc_ref[:LANES, :] * (1.0 / acc_ref[LANES:LANES + 1, :]), o_ref, group)


def _flash_attention(q, k, vt, *, group, dk, n_lat, layer, bound=None):
    b, n, _ = q.shape
    n_kv = k.shape[2] // dk
    n_slab, slab = vt.shape[1], vt.shape[3]
    tq = min(Q_COLS // group, n_lat)
    bk = max(c for c in KV_CHUNKS if n % c == 0)
    assert bk % slab == 0
    nq = group * tq
    in_specs = [
        pl.BlockSpec((None, tq, group * dk), lambda bi, h, t: (bi, t, h)),
        pl.BlockSpec((None, n, dk), lambda bi, h, t: (bi, 0, h)),
        pl.BlockSpec((None, n_slab, LANES, slab), lambda bi, h, t: (bi, 0, h, 0)),
    ]
    common = dict(
        grid=(b, n_kv, n_lat // tq),
        out_specs=pl.BlockSpec((None, tq, group * LANES), lambda bi, h, t: (bi, t, h)),
        out_shape=jax.ShapeDtypeStruct((b, n_lat, n_kv * group * LANES), BF16),
        compiler_params=_params("parallel", "parallel", "parallel"),
    )
    acc = pltpu.VMEM((LANES + SUM_ROWS, nq), F32)

    def general():
        s_buf = pltpu.VMEM((bk, nq + SKEW_LANES), F32)
        return pl.pallas_call(
            functools.partial(_flash_kernel, group=group, dk=dk, bk=bk),
            in_specs=in_specs,
            scratch_shapes=[s_buf, s_buf, acc, pltpu.VMEM((1, nq), F32)],
            name="flash_" + str(layer), **common,
        )(q, k, vt)

    if bound is None:
        return general()

    def bounded():
        return pl.pallas_call(
            functools.partial(_flash_bounded_kernel, group=group, dk=dk, bk=bk),
            in_specs=in_specs + [pl.BlockSpec((1, LANES), lambda bi, h, t: (0, 0))],
            scratch_shapes=[acc],
            name="flash_bounded_" + str(layer), **common,
        )(q, k, vt, jnp.full((1, LANES), bound, F32))

    return lax.cond(bound <= SAFE_SCORE_BOUND, bounded, general)


def _ctx_attn_kernel(q_ref, k_ref, vt_ref, o_ref, *, group, dk):
    q = _stack_heads(q_ref, group, dk)
    s = _dot_nt(k_ref[...], q)
    p = jnp.exp2(s - jnp.max(s, axis=0, keepdims=True))
    l = jnp.sum(p, axis=0, keepdims=True)
    _unstack_heads(_dot(vt_ref[...], p.astype(BF16)) * (1.0 / l), o_ref, group)


def _ctx_attention(q, k, vt, *, group, dk, n_lat, layer):
    b, n, _ = q.shape
    n_ctx = n - n_lat
    n_kv = k.shape[2] // dk
    slab = vt.shape[3]
    assert slab == n_ctx
    t = n_lat // n_ctx
    return pl.pallas_call(
        functools.partial(_ctx_attn_kernel, group=group, dk=dk),
        grid=(b, n_kv),
        in_specs=[
            pl.BlockSpec((None, n_ctx, group * dk), lambda bi, h: (bi, t, h)),
            pl.BlockSpec((None, n_ctx, dk), lambda bi, h: (bi, t, h)),
            pl.BlockSpec((None, None, LANES, slab), lambda bi, h: (bi, t, h, 0)),
        ],
        out_specs=pl.BlockSpec((None, n_ctx, group * LANES), lambda bi, h: (bi, 0, h)),
        out_shape=jax.ShapeDtypeStruct((b, n_ctx, n_kv * group * LANES), BF16),
        compiler_params=_params("parallel", "parallel"),
        name="ctx_attn_" + str(layer),
    )(q, k, vt)


def _split_heads(q):
    lane = lax.broadcasted_iota(jnp.int32, (1, LANES), 1)
    zero = jnp.zeros_like(q)
    return jnp.concatenate(
        [jnp.where((lane >= j * NA_HEAD_DIM) & (lane < (j + 1) * NA_HEAD_DIM), q, zero)
         for j in range(LANES // NA_HEAD_DIM)], axis=0)


def _merge_heads(o_t, nq):
    parts = [o_t[j * NA_HEAD_DIM:(j + 1) * NA_HEAD_DIM, j * nq:(j + 1) * nq]
             for j in range(LANES // NA_HEAD_DIM)]
    return jnp.concatenate(parts, axis=0).T


def _na_ctx_kernel(q_ref, k_ref, vt_ref, o_ref):
    q2 = _split_heads(q_ref[...])
    s = _dot_nt(k_ref[...], q2)
    p = jnp.exp2(s - jnp.max(s, axis=0, keepdims=True))
    l = jnp.sum(p, axis=0, keepdims=True)
    o_t = _dot(vt_ref[...], p.astype(BF16)) * (1.0 / l)
    o_ref[...] = _merge_heads(o_t, q_ref.shape[0]).astype(BF16)


def _na_ctx_attention(q, k, vt, *, n_lat, layer):
    b, n, width = q.shape
    n_ctx = n - n_lat
    assert vt.shape[3] == n_ctx
    t = n_lat // n_ctx
    blk = pl.BlockSpec((None, n_ctx, LANES), lambda bi, h: (bi, t, h))
    return pl.pallas_call(
        _na_ctx_kernel,
        grid=(b, width // LANES),
        in_specs=[blk, blk, pl.BlockSpec((None, None, LANES, n_ctx), lambda bi, h: (bi, t, h, 0))],
        out_specs=pl.BlockSpec((None, n_ctx, LANES), lambda bi, h: (bi, 0, h)),
        out_shape=jax.ShapeDtypeStruct((b, n_ctx, width), BF16),
        compiler_params=_params("parallel", "parallel"),
        name="ctx_attn_" + str(layer),
    )(q, k, vt)


def _na_window_start(blk, rows, lib):
    lo = lib.minimum(lib.maximum(NA_BLOCK_ROWS * blk - NA_KH // 2, 0), rows - NA_KH)
    return lib.minimum(lo, rows - NA_WIN_ROWS)


def _na_kernel(q_ref, k_ref, vt_ref, bias_ref, o_ref, sl_ref, sc_ref, *, n_lat, n_ctx, rows):
    slab = vt_ref.shape[-1]
    nq = NA_BLOCK_ROWS * GRID_W
    n_blocks = rows // NA_BLOCK_ROWS
    nk = NA_WIN_ROWS * GRID_W
    kc = k_ref[n_lat:n_lat + n_ctx, :]
    n_here = q_ref.shape[0] // nq

    def scores(i):
        blk = pl.program_id(2) * NA_STEP_BLOCKS + i
        kind = jnp.where(blk == 0, 1, jnp.where(blk == n_blocks - 1, 2, 0))
        ws = _na_window_start(blk, rows, jnp)
        q2 = _split_heads(q_ref[i * nq:(i + 1) * nq, :])
        s_loc = _dot_nt(k_ref[pl.ds(pl.multiple_of(ws * GRID_W, slab), nk), :], q2) + bias_ref[kind]
        s_ctx = _dot_nt(kc, q2)
        sl_ref[i % 2, :, :s_loc.shape[1]] = s_loc
        sc_ref[i % 2, :, :s_ctx.shape[1]] = s_ctx
        return ws, jnp.maximum(jnp.max(s_loc, axis=0, keepdims=True), jnp.max(s_ctx, axis=0, keepdims=True))

    nxt = scores(0)
    for i in range(n_here):
        ws, m = nxt
        if i + 1 < n_here:
            nxt = scores(i + 1)
        parts = []
        for j in range(LANES // NA_HEAD_DIM):
            t = slice(j * nq, (j + 1) * nq)
            p_loc = jnp.exp2(sl_ref[i % 2, :, t] - m[:, t]).astype(BF16)
            p_ctx = jnp.exp2(sc_ref[i % 2, :, t] - m[:, t]).astype(BF16)
            o_t = _pv_t(vt_ref, ws * GRID_W // slab, p_loc, SUM_ROWS)
            o_t = o_t + _pv_t(vt_ref, n_lat // slab, p_ctx, SUM_ROWS)
            parts.append(o_t[j * NA_HEAD_DIM:(j + 1) * NA_HEAD_DIM, :] * (1.0 / o_t[LANES:LANES + 1, :]))
        o_ref[i * nq:(i + 1) * nq, :] = jnp.concatenate(parts, axis=0).T.astype(BF16)


def _na_bias(rpb, rows):
    h, n_dr, n_dc = rpb.shape
    n_blocks = rows // NA_BLOCK_ROWS
    lead = GRID_W - NA_KW
    wide = 2 * GRID_W
    w = jnp.pad(rpb * LOG2E, ((0, 0), (0, 0), (lead, wide - lead - n_dc)), constant_values=MASK_VALUE)
    flat = jnp.broadcast_to(w[:, :, None, :], (h, n_dr, GRID_W, wide)).reshape(h, n_dr, GRID_W * wide)
    skew = flat[:, :, GRID_W - 1:GRID_W - 1 + GRID_W * (wide - 1)].reshape(h, n_dr, GRID_W, wide - 1)
    qc = jnp.arange(GRID_W)
    cs = jnp.clip(qc - NA_KW // 2, 0, GRID_W - NA_KW)
    col_ok = (qc[None, :] >= cs[:, None]) & (qc[None, :] < cs[:, None] + NA_KW)
    c_t = jnp.swapaxes(jnp.where(col_ok, skew[..., :GRID_W], MASK_VALUE), -1, -2)
    masked = jnp.full((h, GRID_W, GRID_W), MASK_VALUE, F32)
    kinds = []
    for blk in (1, 0, n_blocks - 1):
        ws = _na_window_start(blk, rows, np)
        strips = []
        for j in range(NA_WIN_ROWS):
            kr = ws + j
            blocks = []
            for i in range(NA_BLOCK_ROWS):
                r = NA_BLOCK_ROWS * blk + i
                rs = min(max(r - NA_KH // 2, 0), rows - NA_KH)
                blocks.append(c_t[:, kr - r + NA_KH - 1] if rs <= kr < rs + NA_KH else masked)
            strips.append(jnp.concatenate(blocks, axis=-1))
        kinds.append(jnp.concatenate(strips, axis=1))
    bias = jnp.stack(kinds, axis=1)
    sub = LANES // NA_HEAD_DIM
    bias = bias.reshape(h // sub, sub, len(kinds), NA_WIN_ROWS * GRID_W, NA_BLOCK_ROWS * GRID_W)
    return jnp.concatenate([bias[:, j] for j in range(sub)], axis=-1)


def _na_attention(q, k, vt, bias, *, n_lat, layer):
    b, n, width = q.shape
    n_ctx = n - n_lat
    rows = n_lat // GRID_W
    n_slab, slab = vt.shape[1], vt.shape[3]
    tq = NA_STEP_BLOCKS * NA_BLOCK_ROWS * GRID_W
    kern = functools.partial(_na_kernel, n_lat=n_lat, n_ctx=n_ctx, rows=rows)
    return pl.pallas_call(
        kern,
        grid=(width // LANES, b, n_lat // tq),
        in_specs=[
            pl.BlockSpec((None, tq, LANES), lambda hp, bi, rb: (bi, rb, hp)),
            pl.BlockSpec((None, n, LANES), lambda hp, bi, rb: (bi, 0, hp)),
            pl.BlockSpec((None, n_slab, LANES, slab), lambda hp, bi, rb: (bi, 0, hp, 0)),
            pl.BlockSpec((None,) + bias.shape[1:], lambda hp, bi, rb: (hp, 0, 0, 0)),
        ],
        out_specs=pl.BlockSpec((None, tq, LANES), lambda hp, bi, rb: (bi, rb, hp)),
        out_shape=jax.ShapeDtypeStruct((b, n_lat, width), BF16),
        scratch_shapes=[pltpu.VMEM((2, bias.shape[2], bias.shape[3] + SKEW_LANES), F32),
                        pltpu.VMEM((2, n_ctx, bias.shape[3] + SKEW_LANES), F32)],
        compiler_params=_params("parallel", "parallel", "arbitrary"),
        name="na_attn_" + str(layer),
    )(q, k, vt, bias)


def _out_proj_kernel(o_ref, oc_ref, g_ref, w_ref, x_ref, *refs, n_lat_tiles):
    mod_ref, xo_ref = refs[-2:]
    is_ctx = pl.program_id(1) >= n_lat_tiles
    o = jnp.where(is_ctx, oc_ref[...], o_ref[...])
    x = x_ref[...] if len(refs) == 2 else jnp.where(is_ctx, refs[0][...], x_ref[...])
    a = (o.astype(F32) * g_ref[...].astype(F32)).astype(BF16)
    xo_ref[...] = x + mod_ref[0, 2:3, :] * _dot(a, w_ref[...])


def _out_proj_final_kernel(o_ref, g_ref, w_ref, x_ref, mod_ref, fg_ref, y_ref):
    a = (o_ref[...].astype(F32) * g_ref[...].astype(F32)).astype(BF16)
    xn = x_ref[...] + mod_ref[0, 2:3, :] * _dot(a, w_ref[...])
    y_ref[...] = xn * _rms(xn) * fg_ref[...]


def _out_proj(o, o_ctx, gs, w_out, stream, mod3, layer, n_lat, final_g=None):
    x_lat, x_ctx, ctx_tile = stream
    b, _, d = x_lat.shape
    tm = ROW_TILE
    n_lat_tiles = n_lat // tm
    n = n_lat + tm
    ctx_row = b

    def mod_idx(bi, t):
        return (layer * MOD_ROWS + jnp.where(t < n_lat_tiles, bi, ctx_row), 0, 0)

    row = lambda bi, t: (bi, t, 0)
    lat_row = lambda bi, t: (bi, jnp.minimum(t, n_lat_tiles - 1), 0)
    w = o.shape[2]
    head = [pl.BlockSpec((None, tm, w), lat_row)]
    mid = [pl.BlockSpec((None, tm, w), row), pl.BlockSpec(w_out.shape, lambda bi, t: (0, 0)),
           pl.BlockSpec((None, tm, d), lat_row)]
    if final_g is None:
        combined = x_lat is x_ctx
        specs = head + [pl.BlockSpec((None, tm, w), lambda bi, t: (bi, 0, 0))] + mid[:2]
        if combined:
            specs, xs = specs + [pl.BlockSpec((None, tm, d), row)], [x_lat]
        else:
            specs += [mid[2], pl.BlockSpec((None, tm, d), lambda bi, t: (bi, ctx_tile, 0))]
            xs = [x_lat, x_ctx]
        return pl.pallas_call(
            functools.partial(_out_proj_kernel, n_lat_tiles=n_lat_tiles),
            grid=(b, n // tm),
            in_specs=specs + [pl.BlockSpec((1, 3, d), mod_idx)],
            out_specs=pl.BlockSpec((None, tm, d), row),
            out_shape=jax.ShapeDtypeStruct((b, n, d), F32),
            input_output_aliases={4: 0} if combined else {},
            compiler_params=_params("parallel", "parallel"),
            name="out_proj_" + str(layer),
        )(o, o_ctx, gs, w_out, *xs, mod3)
    return pl.pallas_call(
        _out_proj_final_kernel,
        grid=(b, n_lat_tiles),
        in_specs=head + mid + [pl.BlockSpec((1, 3, d), mod_idx), pl.BlockSpec((1, d), lambda bi, t: (0, 0))],
        out_specs=pl.BlockSpec((None, tm, d), row),
        out_shape=jax.ShapeDtypeStruct((b, n_lat, d), F32),
        compiler_params=_params("parallel", "parallel"),
        name="out_proj_final",
    )(o, gs, w_out, x_lat, mod3, final_g.reshape(1, d))


def _rope_tables(n_lat, n_ctx, rot_dim):
    n_freq = rot_dim // 4
    inv = ROPE_THETA ** (-jnp.arange(n_freq, dtype=F32) / n_freq)
    t = jnp.arange(n_lat)
    ang_r = (t // GRID_W).astype(F32)[:, None] * inv
    ang_c = (t % GRID_W).astype(F32)[:, None] * inv
    ang = jnp.concatenate([ang_r, ang_r, ang_c, ang_c], axis=-1)
    cos, sin = jnp.cos(ang), jnp.sin(ang)
    first = (jnp.arange(rot_dim) % (2 * n_freq)) < n_freq
    sin_up = jnp.where(first, -sin, 0.0)
    sin_dn = jnp.where(first, 0.0, sin)
    pad = LANES - rot_dim

    def finish(tab, fill):
        tab = jnp.pad(tab, ((0, 0), (0, pad)), constant_values=fill)
        return jnp.pad(tab, ((0, n_ctx), (0, 0)), constant_values=fill)

    return finish(cos, 1.0), finish(sin_up, 0.0), finish(sin_dn, 0.0)


def _mla_weights(w_in, w_uq, w_ukv):
    o_kv = MLA_Q_LORA
    o_kr = o_kv + MLA_KV_LORA
    o_g = o_kr + MLA_ROPE
    k_r = jnp.pad(w_in[:, o_kr:o_g], ((0, 0), (0, LANES - MLA_ROPE)))
    w_perm = jnp.concatenate([w_in[:, :o_kr], w_in[:, o_g:], k_r], axis=1).astype(BF16)
    uq = w_uq.reshape(MLA_Q_LORA, MLA_HEADS, MLA_NOPE + MLA_ROPE)
    uq = jnp.pad(uq, ((0, 0), (0, 0), (0, MLA_QK_PAD - MLA_NOPE - MLA_ROPE)))
    uq = uq.reshape(MLA_Q_LORA, MLA_HEADS * MLA_QK_PAD).astype(BF16)
    ukv = w_ukv.reshape(MLA_KV_LORA, MLA_HEADS, MLA_NOPE + MLA_V).astype(BF16)
    uk = ukv[:, :, :MLA_NOPE].reshape(MLA_KV_LORA, MLA_HEADS * MLA_NOPE)
    uv = ukv[:, :, MLA_NOPE:].reshape(MLA_KV_LORA, MLA_HEADS * MLA_V)
    return w_perm, uq, uk, uv


def kernel(x, c, ctx, c_ctx, mod_w, mod_b, norm_g, final_g, ga_w_in, ga_q_g, ga_k_g, ga_w_out, na_w_in, na_rpb, na_w_out, mla_w_in, mla_q_g, mla_kv_g, mla_w_uq, mla_w_ukv, mla_w_out):
    b, n_lat, d = x.shape
    n_ctx = ctx.shape[1]
    depth = mod_w.shape[0]
    assert n_lat % ROW_TILE == 0 and n_ctx == ROW_TILE and n_lat % n_ctx == 0
    assert b < MOD_ROWS and n_lat % (NA_STEP_BLOCKS * NA_BLOCK_ROWS * GRID_W) == 0
    assert n_lat // GRID_W >= NA_WIN_ROWS + NA_BLOCK_ROWS

    stream = (x, ctx, 0)
    cc = jnp.concatenate([c, c_ctx[None, :], jnp.zeros((MOD_ROWS - b - 1, d), F32)], axis=0)
    mod3 = _modulation(cc, mod_w, mod_b).reshape(depth * MOD_ROWS, 3, d)

    tab_a = _rope_tables(n_lat, n_ctx, GQA_HEAD_DIM)
    tab_m = _rope_tables(n_lat, n_ctx, MLA_ROPE)

    out = None
    for i in range(depth):
        kind, j = i % N_MIXERS, i // N_MIXERS
        need_ctx = i < depth - 1
        if kind == 0:
            n_q = d // GQA_HEAD_DIM
            n_kv = n_q // GQA_GROUP
            kern = functools.partial(_gqa_proj_kernel, n_q=n_q, n_kv=n_kv, scale=GQA_HEAD_DIM ** -0.5 * LOG2E)
            consts = [ga_w_in[j].astype(BF16), ga_q_g[j].reshape(1, -1), ga_k_g[j].reshape(1, -1)]
            kvw = n_kv * GQA_HEAD_DIM
            q, k, vt, gs = _project(kern, stream, mod3, i, norm_g[i], consts, tab_a, [d, kvw, kvw, d], n_lat)
            bound = (BOUND_SLACK * GQA_HEAD_DIM ** 0.5 * LOG2E
                     * jnp.max(jnp.abs(ga_q_g[j])) * jnp.max(jnp.abs(ga_k_g[j])))
            o = _flash_attention(q, k, vt, group=GQA_GROUP, dk=GQA_HEAD_DIM, n_lat=n_lat, layer=i, bound=bound)
            if need_ctx:
                oc = _ctx_attention(q, k, vt, group=GQA_GROUP, dk=GQA_HEAD_DIM, n_lat=n_lat, layer=i)
            w_out = ga_w_out[j]
        elif kind == 1:
            kern = functools.partial(_na_proj_kernel, width=d, scale=NA_HEAD_DIM ** -0.5 * LOG2E)
            q, k, vt, gs = _project(kern, stream, mod3, i, norm_g[i], [na_w_in[j].astype(BF16)], (), [d, d, d, d], n_lat)
            o = _na_attention(q, k, vt, _na_bias(na_rpb[j], n_lat // GRID_W), n_lat=n_lat, layer=i)
            if need_ctx:
                oc = _na_ctx_attention(q, k, vt, n_lat=n_lat, layer=i)
            w_out = na_w_out[j]
        else:
            kern = functools.partial(_mla_proj_kernel, width=d, scale=(MLA_NOPE + MLA_ROPE) ** -0.5 * LOG2E)
            w_perm, uq, uk, uv = _mla_weights(mla_w_in[j], mla_w_uq[j], mla_w_ukv[j])
            consts = [w_perm, mla_q_g[j].reshape(1, -1), mla_kv_g[j].reshape(1, -1), uq, uk, uv]
            widths = [MLA_HEADS * MLA_QK_PAD, MLA_HEADS * MLA_QK_PAD, MLA_HEADS * MLA_V, d]
            q, k, vt, gs, norms = _project(kern, stream, mod3, i, norm_g[i], consts, tab_m, widths, n_lat, True)
            bound = BOUND_SLACK * jnp.sqrt(jnp.max(norms[:, :, 0, 0]) * jnp.max(norms[:, :, MOD_ROWS, 0]))
            o = _flash_attention(q, k, vt, group=1, dk=MLA_QK_PAD, n_lat=n_lat, layer=i, bound=bound)
            if need_ctx:
                oc = _ctx_attention(q, k, vt, group=1, dk=MLA_QK_PAD, n_lat=n_lat, layer=i)
            w_out = mla_w_out[j]
        if need_ctx:
            xs = _out_proj(o, oc, gs, w_out.astype(BF16), stream, mod3, i, n_lat)
            stream = (xs, xs, n_lat // ROW_TILE)
        else:
            out = _out_proj(o, None, gs, w_out.astype(BF16), stream, mod3, i, n_lat, final_g=final_g)
    return out
```

```python
import functools

import jax
import numpy as np
import jax.numpy as jnp
from jax import lax
from jax.experimental import pallas as pl
from jax.experimental.pallas import tpu as pltpu

F32 = jnp.float32
BF16 = jnp.bfloat16

NORM_EPS = 1e-6
ROPE_THETA = 10000.0
GRID_W = 64
N_MIXERS = 3
LANES = 128
MOD_ROWS = 8
MASK_VALUE = -1e30
LOG2E = 1.4426950408889634

GQA_HEAD_DIM = 128
GQA_GROUP = 4
NA_HEAD_DIM = 64
NA_KH = 8
NA_KW = 16
NA_BLOCK_ROWS = 4
NA_WIN_ROWS = 12
NA_STEP_BLOCKS = 4
MLA_HEADS = 8
MLA_Q_LORA = 512
MLA_KV_LORA = 256
MLA_NOPE = 128
MLA_ROPE = 64
MLA_V = 128
MLA_QK_PAD = 256

ROW_TILE = 256
Q_COLS = 1024
Q_TILE = 256
SUM_ROWS = 16
SKEW_LANES = 128
SAFE_SCORE_BOUND = 50.0
BOUNDED_UNROLL = 5
BOUND_SLACK = 1.02
KV_CHUNKS = (768, 512, 256)
VMEM_LIMIT = 48 * 1024 * 1024


def _params(*sem):
    return pltpu.CompilerParams(dimension_semantics=sem, vmem_limit_bytes=VMEM_LIMIT)


def _silu(v):
    return v * (1.0 / (1.0 + jnp.exp(-v)))


def _rms(v):
    return lax.rsqrt(jnp.mean(v * v, axis=-1, keepdims=True) + NORM_EPS)


def _rope(v, cos, sin_up, sin_dn, half):
    w = v.shape[-1]
    return v * cos + pltpu.roll(v, w - half, 1) * sin_up + pltpu.roll(v, half, 1) * sin_dn


def _dot(a, b):
    return jnp.dot(a, b, preferred_element_type=F32)


def _dot_nt(a, b):
    return lax.dot_general(a, b, (((1,), (1,)), ((), ())), preferred_element_type=F32)


def _mod_kernel(c_ref, w_ref, b_ref, o_ref):
    a = _silu(c_ref[...])
    w = w_ref[0]
    a_hi = a.astype(BF16)
    a_lo = (a - a_hi.astype(F32)).astype(BF16)
    w_hi = w.astype(BF16)
    w_lo = (w - w_hi.astype(F32)).astype(BF16)
    o_ref[0] = _dot(a_hi, w_hi) + _dot(a_lo, w_hi) + _dot(a_hi, w_lo) + b_ref[0]


def _modulation(cc, mod_w, mod_b):
    depth, d, n3 = mod_w.shape
    tn = 1024
    return pl.pallas_call(
        _mod_kernel,
        grid=(depth, n3 // tn),
        in_specs=[
            pl.BlockSpec((MOD_ROWS, d), lambda l, j: (0, 0)),
            pl.BlockSpec((1, d, tn), lambda l, j: (l, 0, j)),
            pl.BlockSpec((1, 1, tn), lambda l, j: (l, 0, j)),
        ],
        out_specs=pl.BlockSpec((1, MOD_ROWS, tn), lambda l, j: (l, 0, j)),
        out_shape=jax.ShapeDtypeStruct((depth, MOD_ROWS, n3), F32),
        compiler_params=_params("parallel", "parallel"),
        name="adaln_modulation",
    )(cc, mod_w, mod_b.reshape(depth, 1, n3))


def _modulated_norm(x_ref, xc_ref, ng_ref, mod_ref, n_lat_tiles):
    x = jnp.where(pl.program_id(1) >= n_lat_tiles, xc_ref[...], x_ref[...])
    shift = mod_ref[0, 0:1, :]
    scale = mod_ref[0, 1:2, :]
    return ((x * _rms(x)) * ng_ref[...] * (1.0 + scale) + shift).astype(BF16)


def _gqa_proj_kernel(x_ref, xc_ref, mod_ref, ng_ref, w_ref, qg_ref, kg_ref, cos_ref, su_ref, sd_ref,
                     q_ref, k_ref, vt_ref, g_ref, *, n_lat_tiles, n_q, n_kv, scale):
    h = _modulated_norm(x_ref, xc_ref, ng_ref, mod_ref, n_lat_tiles)
    cos, su, sd = cos_ref[...], su_ref[...], sd_ref[...]
    hd = GQA_HEAD_DIM
    qw, kw = n_q * hd, n_kv * hd
    q = _dot(h, w_ref[:, :qw])
    for i in range(n_q):
        qh = q[:, i * hd:(i + 1) * hd]
        qh = qh * _rms(qh) * qg_ref[...]
        q_ref[:, i * hd:(i + 1) * hd] = (_rope(qh, cos, su, sd, hd // 4) * scale).astype(BF16)
    k = _dot(h, w_ref[:, qw:qw + kw])
    for i in range(n_kv):
        kh = k[:, i * hd:(i + 1) * hd]
        kh = kh * _rms(kh) * kg_ref[...]
        k_ref[:, i * hd:(i + 1) * hd] = _rope(kh, cos, su, sd, hd // 4).astype(BF16)
    v = _dot(h, w_ref[:, qw + kw:qw + 2 * kw])
    for i in range(n_kv):
        vt_ref[i * hd:(i + 1) * hd, :] = v[:, i * hd:(i + 1) * hd].T.astype(BF16)
    g_ref[...] = _silu(_dot(h, w_ref[:, qw + 2 * kw:])).astype(BF16)


def _na_proj_kernel(x_ref, xc_ref, mod_ref, ng_ref, w_ref, q_ref, k_ref, vt_ref, g_ref, norm_ref,
                    *, n_lat_tiles, width, scale):
    h = _modulated_norm(x_ref, xc_ref, ng_ref, mod_ref, n_lat_tiles)
    head_of_col = lax.broadcasted_iota(jnp.int32, (width, LANES), 0) // NA_HEAD_DIM
    lane = lax.broadcasted_iota(jnp.int32, (width, LANES), 1)
    pick = jnp.where(head_of_col == lane, 1.0, 0.0).astype(BF16)
    half = norm_ref.shape[0] // 2
    q = _dot(h, w_ref[:, :width]) * scale
    q_ref[...] = q.astype(BF16)
    q_sq = jnp.max(_dot((q * q).astype(BF16), pick), axis=0, keepdims=True)
    norm_ref[:half, :] = jnp.broadcast_to(q_sq, (half, LANES))
    k = _dot(h, w_ref[:, width:2 * width])
    k_ref[...] = k.astype(BF16)
    k_sq = jnp.max(_dot((k * k).astype(BF16), pick), axis=0, keepdims=True)
    norm_ref[half:, :] = jnp.broadcast_to(k_sq, (half, LANES))
    v = _dot(h, w_ref[:, 2 * width:3 * width])
    for i in range(width // LANES):
        vt_ref[i * LANES:(i + 1) * LANES, :] = v[:, i * LANES:(i + 1) * LANES].T.astype(BF16)
    g_ref[...] = _silu(_dot(h, w_ref[:, 3 * width:])).astype(BF16)


def _mla_proj_kernel(x_ref, xc_ref, mod_ref, ng_ref, w_ref, qg_ref, kvg_ref, wuq_ref, wuk_ref, wuv_ref,
                     cos_ref, su_ref, sd_ref, q_ref, k_ref, vt_ref, g_ref, norm_ref,
                     *, n_lat_tiles, width, scale):
    h = _modulated_norm(x_ref, xc_ref, ng_ref, mod_ref, n_lat_tiles)
    ones = jnp.ones((LANES, LANES), BF16)

    def row_sq(v):
        return _dot((v * v).astype(BF16), ones)

    cos, su, sd = cos_ref[...], su_ref[...], sd_ref[...]
    o_kv = MLA_Q_LORA
    o_g = o_kv + MLA_KV_LORA
    o_kr = o_g + width
    c_q = _dot(h, w_ref[:, :o_kv])
    c_q = (c_q * _rms(c_q) * qg_ref[...]).astype(BF16)
    q = _dot(c_q, wuq_ref[...])
    q_sq = None
    for i in range(MLA_HEADS):
        a = i * MLA_QK_PAD
        q_nope = q[:, a:a + MLA_NOPE] * scale
        q_rope = _rope(q[:, a + MLA_NOPE:a + MLA_QK_PAD], cos, su, sd, MLA_ROPE // 4) * scale
        q_ref[:, a:a + MLA_NOPE] = q_nope.astype(BF16)
        q_ref[:, a + MLA_NOPE:a + MLA_QK_PAD] = q_rope.astype(BF16)
        sq = row_sq(q_nope) + row_sq(q_rope)
        q_sq = sq if q_sq is None else jnp.maximum(q_sq, sq)
    c_kv = _dot(h, w_ref[:, o_kv:o_g])
    c_kv = (c_kv * _rms(c_kv) * kvg_ref[...]).astype(BF16)
    k_r = _rope(_dot(h, w_ref[:, o_kr:]), cos, su, sd, MLA_ROPE // 4)
    k_nope = _dot(c_kv, wuk_ref[...])
    k_sq = None
    for i in range(MLA_HEADS):
        a = i * MLA_QK_PAD
        k_h = k_nope[:, i * MLA_NOPE:(i + 1) * MLA_NOPE]
        k_ref[:, a:a + MLA_NOPE] = k_h.astype(BF16)
        k_ref[:, a + MLA_NOPE:a + MLA_QK_PAD] = k_r.astype(BF16)
        sq = row_sq(k_h)
        k_sq = sq if k_sq is None else jnp.maximum(k_sq, sq)
    k_sq = k_sq + row_sq(k_r)
    half = norm_ref.shape[0] // 2
    norm_ref[:half, :] = jnp.broadcast_to(jnp.max(q_sq, axis=0, keepdims=True), (half, LANES))
    norm_ref[half:, :] = jnp.broadcast_to(jnp.max(k_sq, axis=0, keepdims=True), (half, LANES))
    v = _dot(c_kv, wuv_ref[...])
    for i in range(MLA_HEADS):
        vt_ref[i * MLA_V:(i + 1) * MLA_V, :] = v[:, i * MLA_V:(i + 1) * MLA_V].T.astype(BF16)
    g_ref[...] = _silu(_dot(h, w_ref[:, o_g:o_kr])).astype(BF16)


def _project(kern, stream, mod3, layer, norm_g, consts, tables, out_widths, n_lat, with_norms=False):
    x_lat, x_ctx, ctx_tile = stream
    b, _, d = x_lat.shape
    tm = ROW_TILE
    n_lat_tiles = n_lat // tm
    n = n_lat + tm
    ctx_row = b

    def mod_idx(bi, t):
        return (layer * MOD_ROWS + jnp.where(t < n_lat_tiles, bi, ctx_row), 0, 0)

    in_specs = [
        pl.BlockSpec((None, tm, d), lambda bi, t: (bi, jnp.minimum(t, n_lat_tiles - 1), 0)),
        pl.BlockSpec((None, tm, d), lambda bi, t: (bi, ctx_tile, 0)),
        pl.BlockSpec((1, 3, d), mod_idx),
        pl.BlockSpec((1, d), lambda bi, t: (0, 0)),
    ]
    in_specs += [pl.BlockSpec(a.shape, lambda bi, t: (0, 0)) for a in consts]
    in_specs += [pl.BlockSpec((tm, LANES), lambda bi, t: (t, 0)) for _ in tables]
    out_specs = [pl.BlockSpec((None, tm, w), lambda bi, t: (bi, t, 0)) for w in out_widths]
    out_shape = [jax.ShapeDtypeStruct((b, n, w), BF16) for w in out_widths]
    out_specs[2] = pl.BlockSpec((None, None, out_widths[2], tm), lambda bi, t: (bi, t, 0, 0))
    out_shape[2] = jax.ShapeDtypeStruct((b, n // tm, out_widths[2], tm), BF16)
    if with_norms:
        out_specs.append(pl.BlockSpec((None, None, 2 * MOD_ROWS, LANES), lambda bi, t: (bi, t, 0, 0)))
        out_shape.append(jax.ShapeDtypeStruct((b, n // tm, 2 * MOD_ROWS, LANES), F32))
    return pl.pallas_call(
        functools.partial(kern, n_lat_tiles=n_lat_tiles),
        grid=(b, n // tm),
        in_specs=in_specs,
        out_specs=out_specs,
        out_shape=out_shape,
        compiler_params=_params("parallel", "parallel"),
        name="proj_" + str(layer),
    )(x_lat, x_ctx, mod3, norm_g.reshape(1, d), *consts, *tables)


def _stack_heads(q_ref, group, dk):
    return jnp.concatenate([q_ref[:, g * dk:(g + 1) * dk] for g in range(group)], axis=0)


def _unstack_heads(o_t, o_ref, group):
    tq = o_ref.shape[0]
    for g in range(group):
        o_ref[:, g * LANES:(g + 1) * LANES] = o_t[:, g * tq:(g + 1) * tq].T.astype(BF16)


def _pv_t(vt_ref, first_slab, p, sum_rows=0):
    slab = vt_ref.shape[-1]
    acc = None
    for i in range(p.shape[0] // slab):
        vt = vt_ref[first_slab + i]
        if sum_rows:
            vt = jnp.concatenate([vt, jnp.ones((sum_rows, slab), vt.dtype)], axis=0)
        t = _dot(vt, p[i * slab:(i + 1) * slab, :])
        acc = t if acc is None else acc + t
    return acc


def _flash_kernel(q_ref, k_ref, vt_ref, o_ref, sa_ref, sb_ref, acc_ref, m_ref, *, group, dk, bk):
    slab = vt_ref.shape[-1]
    n_chunks = k_ref.shape[0] // bk
    q = _stack_heads(q_ref, group, dk)
    tiles = [slice(j * Q_TILE, (j + 1) * Q_TILE) for j in range(q.shape[0] // Q_TILE)]

    def scores(c, s_ref):
        k = k_ref[pl.ds(pl.multiple_of(c * bk, bk), bk), :]
        out = []
        for t in tiles:
            s = _dot_nt(k, q[t, :])
            s_ref[:, t] = s
            out.append(jnp.max(s, axis=0, keepdims=True))
        return out

    def update(s_ref, s_max, c, t):
        m = m_ref[:, t]
        m_new = jnp.maximum(m, s_max)
        alpha = jnp.exp2(m - m_new)
        p = jnp.exp2(s_ref[:, t] - m_new).astype(BF16)
        m_ref[:, t] = m_new
        acc_ref[:, t] = alpha * acc_ref[:, t] + _pv_t(vt_ref, c * (bk // slab), p, SUM_ROWS)

    def step(c, cur_ref, cur_max, nxt_ref):
        k = k_ref[pl.ds(pl.multiple_of((c + 1) * bk, bk), bk), :]
        nxt_max = []
        for j, t in enumerate(tiles):
            s = _dot_nt(k, q[t, :])
            nxt_ref[:, t] = s
            nxt_max.append(jnp.max(s, axis=0, keepdims=True))
            update(cur_ref, cur_max[j], c, t)
        return nxt_max

    m_ref[...] = jnp.full(m_ref.shape, MASK_VALUE, F32)
    acc_ref[...] = jnp.zeros(acc_ref.shape, F32)
    max_a = scores(0, sa_ref)

    def body(i, max_a):
        max_b = step(2 * i, sa_ref, max_a, sb_ref)
        return tuple(step(2 * i + 1, sb_ref, max_b, sa_ref))

    pairs = (n_chunks - 1) // 2
    max_a = lax.fori_loop(0, pairs, body, tuple(max_a))
    last = 2 * pairs
    if n_chunks - last == 2:
        max_b = step(last, sa_ref, max_a, sb_ref)
        last, s_ref, s_max = last + 1, sb_ref, max_b
    else:
        s_ref, s_max = sa_ref, max_a
    for j, t in enumerate(tiles):
        update(s_ref, s_max[j], last, t)
    _unstack_heads(acc_ref[:LANES, :] * (1.0 / acc_ref[LANES:LANES + 1, :]), o_ref, group)


def _flash_bounded_kernel(q_ref, k_ref, vt_ref, bound_ref, o_ref, acc_ref, *, group, dk, bk):
    slab = vt_ref.shape[-1]
    n_chunks = k_ref.shape[0] // bk
    q = _stack_heads(q_ref, group, dk)
    tiles = [slice(0, q.shape[0])]
    bound = bound_ref[:, :1]
    acc_ref[...] = jnp.zeros(acc_ref.shape, F32)

    def chunk(c):
        k = k_ref[pl.ds(pl.multiple_of(c * bk, bk), bk), :]
        for t in tiles:
            p = jnp.exp2(_dot_nt(k, q[t, :]) - bound)
            acc_ref[:LANES, t] += _pv_t(vt_ref, c * (bk // slab), p.astype(BF16))
            acc_ref[LANES:, t] += jnp.sum(p.reshape(bk // MOD_ROWS, MOD_ROWS, p.shape[1]), axis=0)

    def body(i, carry):
        for u in range(BOUNDED_UNROLL):
            chunk(BOUNDED_UNROLL * i + u)
        return carry

    lax.fori_loop(0, n_chunks // BOUNDED_UNROLL, body, 0)
    for c in range(n_chunks - n_chunks % BOUNDED_UNROLL, n_chunks):
        chunk(c)
    l = jnp.sum(acc_ref[LANES:, :], axis=0, keepdims=True)
    _unstack_heads(acc_ref[:LANES, :] * (1.0 / l), o_ref, group)


def _flash_attention(q, k, vt, *, group, dk, n_lat, layer, bound=None):
    b, n, _ = q.shape
    n_kv = k.shape[2] // dk
    n_slab, slab = vt.shape[1], vt.shape[3]
    tq = min(Q_COLS // group, n_lat)
    bk = max(c for c in KV_CHUNKS if n % c == 0)
    assert bk % slab == 0
    nq = group * tq
    in_specs = [
        pl.BlockSpec((None, tq, group * dk), lambda bi, h, t: (bi, t, h)),
        pl.BlockSpec((None, n, dk), lambda bi, h, t: (bi, 0, h)),
        pl.BlockSpec((None, n_slab, LANES, slab), lambda bi, h, t: (bi, 0, h, 0)),
    ]
    common = dict(
        grid=(b, n_kv, n_lat // tq),
        out_specs=pl.BlockSpec((None, tq, group * LANES), lambda bi, h, t: (bi, t, h)),
        out_shape=jax.ShapeDtypeStruct((b, n_lat, n_kv * group * LANES), BF16),
        compiler_params=_params("parallel", "parallel", "parallel"),
    )
    acc = pltpu.VMEM((LANES + SUM_ROWS, nq), F32)

    def general():
        s_buf = pltpu.VMEM((bk, nq + SKEW_LANES), F32)
        return pl.pallas_call(
            functools.partial(_flash_kernel, group=group, dk=dk, bk=bk),
            in_specs=in_specs,
            scratch_shapes=[s_buf, s_buf, acc, pltpu.VMEM((1, nq), F32)],
            name="flash_" + str(layer), **common,
        )(q, k, vt)

    if bound is None:
        return general()

    def bounded():
        return pl.pallas_call(
            functools.partial(_flash_bounded_kernel, group=group, dk=dk, bk=bk),
            in_specs=in_specs + [pl.BlockSpec((1, LANES), lambda bi, h, t: (0, 0))],
            scratch_shapes=[pltpu.VMEM((LANES + MOD_ROWS, nq), F32)],
            name="flash_bounded_" + str(layer), **common,
        )(q, k, vt, jnp.full((1, LANES), bound, F32))

    return lax.cond(bound <= SAFE_SCORE_BOUND, bounded, general)


def _ctx_attn_kernel(q_ref, k_ref, vt_ref, o_ref, *, group, dk):
    q = _stack_heads(q_ref, group, dk)
    s = _dot_nt(k_ref[...], q)
    p = jnp.exp2(s - jnp.max(s, axis=0, keepdims=True))
    l = jnp.sum(p, axis=0, keepdims=True)
    _unstack_heads(_dot(vt_ref[...], p.astype(BF16)) * (1.0 / l), o_ref, group)


def _ctx_attention(q, k, vt, *, group, dk, n_lat, layer):
    b, n, _ = q.shape
    n_ctx = n - n_lat
    n_kv = k.shape[2] // dk
    slab = vt.shape[3]
    assert slab == n_ctx
    t = n_lat // n_ctx
    return pl.pallas_call(
        functools.partial(_ctx_attn_kernel, group=group, dk=dk),
        grid=(b, n_kv),
        in_specs=[
            pl.BlockSpec((None, n_ctx, group * dk), lambda bi, h: (bi, t, h)),
            pl.BlockSpec((None, n_ctx, dk), lambda bi, h: (bi, t, h)),
            pl.BlockSpec((None, None, LANES, slab), lambda bi, h: (bi, t, h, 0)),
        ],
        out_specs=pl.BlockSpec((None, n_ctx, group * LANES), lambda bi, h: (bi, 0, h)),
        out_shape=jax.ShapeDtypeStruct((b, n_ctx, n_kv * group * LANES), BF16),
        compiler_params=_params("parallel", "parallel"),
        name="ctx_attn_" + str(layer),
    )(q, k, vt)


def _split_heads(q):
    lane = lax.broadcasted_iota(jnp.int32, (1, LANES), 1)
    zero = jnp.zeros_like(q)
    return jnp.concatenate(
        [jnp.where((lane >= j * NA_HEAD_DIM) & (lane < (j + 1) * NA_HEAD_DIM), q, zero)
         for j in range(LANES // NA_HEAD_DIM)], axis=0)


def _merge_heads(o_t, nq):
    parts = [o_t[j * NA_HEAD_DIM:(j + 1) * NA_HEAD_DIM, j * nq:(j + 1) * nq]
             for j in range(LANES // NA_HEAD_DIM)]
    return jnp.concatenate(parts, axis=0).T


def _na_ctx_kernel(q_ref, k_ref, vt_ref, o_ref):
    q2 = _split_heads(q_ref[...])
    s = _dot_nt(k_ref[...], q2)
    p = jnp.exp2(s - jnp.max(s, axis=0, keepdims=True))
    l = jnp.sum(p, axis=0, keepdims=True)
    o_t = _dot(vt_ref[...], p.astype(BF16)) * (1.0 / l)
    o_ref[...] = _merge_heads(o_t, q_ref.shape[0]).astype(BF16)


def _na_ctx_attention(q, k, vt, *, n_lat, layer):
    b, n, width = q.shape
    n_ctx = n - n_lat
    assert vt.shape[3] == n_ctx
    t = n_lat // n_ctx
    blk = pl.BlockSpec((None, n_ctx, LANES), lambda bi, h: (bi, t, h))
    return pl.pallas_call(
        _na_ctx_kernel,
        grid=(b, width // LANES),
        in_specs=[blk, blk, pl.BlockSpec((None, None, LANES, n_ctx), lambda bi, h: (bi, t, h, 0))],
        out_specs=pl.BlockSpec((None, n_ctx, LANES), lambda bi, h: (bi, 0, h)),
        out_shape=jax.ShapeDtypeStruct((b, n_ctx, width), BF16),
        compiler_params=_params("parallel", "parallel"),
        name="ctx_attn_" + str(layer),
    )(q, k, vt)


def _na_window_start(blk, rows, lib):
    lo = lib.minimum(lib.maximum(NA_BLOCK_ROWS * blk - NA_KH // 2, 0), rows - NA_KH)
    return lib.minimum(lo, rows - NA_WIN_ROWS)


def _na_kernel(q_ref, k_ref, vt_ref, bias_ref, o_ref, sl_ref, sc_ref, *, n_lat, n_ctx, rows):
    slab = vt_ref.shape[-1]
    nq = NA_BLOCK_ROWS * GRID_W
    n_blocks = rows // NA_BLOCK_ROWS
    nk = NA_WIN_ROWS * GRID_W
    kc = k_ref[n_lat:n_lat + n_ctx, :]
    n_here = q_ref.shape[0] // nq

    def scores(i):
        blk = pl.program_id(2) * NA_STEP_BLOCKS + i
        kind = jnp.where(blk == 0, 1, jnp.where(blk == n_blocks - 1, 2, 0))
        ws = _na_window_start(blk, rows, jnp)
        q2 = _split_heads(q_ref[i * nq:(i + 1) * nq, :])
        s_loc = _dot_nt(k_ref[pl.ds(pl.multiple_of(ws * GRID_W, slab), nk), :], q2) + bias_ref[kind]
        s_ctx = _dot_nt(kc, q2)
        sl_ref[i % 2, :, :s_loc.shape[1]] = s_loc
        sc_ref[i % 2, :, :s_ctx.shape[1]] = s_ctx
        return ws, jnp.maximum(jnp.max(s_loc, axis=0, keepdims=True), jnp.max(s_ctx, axis=0, keepdims=True))

    nxt = scores(0)
    for i in range(n_here):
        ws, m = nxt
        if i + 1 < n_here:
            nxt = scores(i + 1)
        parts = []
        for j in range(LANES // NA_HEAD_DIM):
            t = slice(j * nq, (j + 1) * nq)
            p_loc = jnp.exp2(sl_ref[i % 2, :, t] - m[:, t]).astype(BF16)
            p_ctx = jnp.exp2(sc_ref[i % 2, :, t] - m[:, t]).astype(BF16)
            o_t = _pv_t(vt_ref, ws * GRID_W // slab, p_loc, SUM_ROWS)
            o_t = o_t + _pv_t(vt_ref, n_lat // slab, p_ctx, SUM_ROWS)
            parts.append(o_t[j * NA_HEAD_DIM:(j + 1) * NA_HEAD_DIM, :] * (1.0 / o_t[LANES:LANES + 1, :]))
        o_ref[i * nq:(i + 1) * nq, :] = jnp.concatenate(parts, axis=0).T.astype(BF16)


def _na_bounded_kernel(q_ref, k_ref, vt_ref, bias_ref, bound_ref, o_ref, *, n_lat, n_ctx, rows):
    slab = vt_ref.shape[-1]
    nq = NA_BLOCK_ROWS * GRID_W
    n_blocks = rows // NA_BLOCK_ROWS
    nk = NA_WIN_ROWS * GRID_W
    kc = k_ref[n_lat:n_lat + n_ctx, :]
    bound = bound_ref[:, :1]
    for i in range(q_ref.shape[0] // nq):
        blk = pl.program_id(2) * NA_STEP_BLOCKS + i
        kind = jnp.where(blk == 0, 1, jnp.where(blk == n_blocks - 1, 2, 0))
        ws = _na_window_start(blk, rows, jnp)
        q2 = _split_heads(q_ref[i * nq:(i + 1) * nq, :])
        kw = k_ref[pl.ds(pl.multiple_of(ws * GRID_W, slab), nk), :]
        p_loc = jnp.exp2(_dot_nt(kw, q2) + (bias_ref[kind] - bound)).astype(BF16)
        p_ctx = jnp.exp2(_dot_nt(kc, q2) - bound).astype(BF16)
        o_t = _pv_t(vt_ref, ws * GRID_W // slab, p_loc, SUM_ROWS)
        o_t = o_t + _pv_t(vt_ref, n_lat // slab, p_ctx, SUM_ROWS)
        o_t = o_t[:LANES, :] * (1.0 / o_t[LANES:LANES + 1, :])
        o_ref[i * nq:(i + 1) * nq, :] = _merge_heads(o_t, nq).astype(BF16)


def _na_bias(rpb, rows):
    h, n_dr, n_dc = rpb.shape
    n_blocks = rows // NA_BLOCK_ROWS
    lead = GRID_W - NA_KW
    wide = 2 * GRID_W
    w = jnp.pad(rpb * LOG2E, ((0, 0), (0, 0), (lead, wide - lead - n_dc)), constant_values=MASK_VALUE)
    flat = jnp.broadcast_to(w[:, :, None, :], (h, n_dr, GRID_W, wide)).reshape(h, n_dr, GRID_W * wide)
    skew = flat[:, :, GRID_W - 1:GRID_W - 1 + GRID_W * (wide - 1)].reshape(h, n_dr, GRID_W, wide - 1)
    qc = jnp.arange(GRID_W)
    cs = jnp.clip(qc - NA_KW // 2, 0, GRID_W - NA_KW)
    col_ok = (qc[None, :] >= cs[:, None]) & (qc[None, :] < cs[:, None] + NA_KW)
    c_t = jnp.swapaxes(jnp.where(col_ok, skew[..., :GRID_W], MASK_VALUE), -1, -2)
    masked = jnp.full((h, GRID_W, GRID_W), MASK_VALUE, F32)
    kinds = []
    for blk in (1, 0, n_blocks - 1):
        ws = _na_window_start(blk, rows, np)
        strips = []
        for j in range(NA_WIN_ROWS):
            kr = ws + j
            blocks = []
            for i in range(NA_BLOCK_ROWS):
                r = NA_BLOCK_ROWS * blk + i
                rs = min(max(r - NA_KH // 2, 0), rows - NA_KH)
                blocks.append(c_t[:, kr - r + NA_KH - 1] if rs <= kr < rs + NA_KH else masked)
            strips.append(jnp.concatenate(blocks, axis=-1))
        kinds.append(jnp.concatenate(strips, axis=1))
    bias = jnp.stack(kinds, axis=1)
    sub = LANES // NA_HEAD_DIM
    bias = bias.reshape(h // sub, sub, len(kinds), NA_WIN_ROWS * GRID_W, NA_BLOCK_ROWS * GRID_W)
    return jnp.concatenate([bias[:, j] for j in range(sub)], axis=-1)


def _na_attention(q, k, vt, bias, bound, *, n_lat, layer):
    b, n, width = q.shape
    n_ctx = n - n_lat
    rows = n_lat // GRID_W
    n_slab, slab = vt.shape[1], vt.shape[3]
    tq = NA_STEP_BLOCKS * NA_BLOCK_ROWS * GRID_W
    in_specs = [
        pl.BlockSpec((None, tq, LANES), lambda hp, bi, rb: (bi, rb, hp)),
        pl.BlockSpec((None, n, LANES), lambda hp, bi, rb: (bi, 0, hp)),
        pl.BlockSpec((None, n_slab, LANES, slab), lambda hp, bi, rb: (bi, 0, hp, 0)),
        pl.BlockSpec((None,) + bias.shape[1:], lambda hp, bi, rb: (hp, 0, 0, 0)),
    ]
    common = dict(
        grid=(width // LANES, b, n_lat // tq),
        out_specs=pl.BlockSpec((None, tq, LANES), lambda hp, bi, rb: (bi, rb, hp)),
        out_shape=jax.ShapeDtypeStruct((b, n_lat, width), BF16),
        compiler_params=_params("parallel", "parallel", "arbitrary"),
    )

    def general():
        return pl.pallas_call(
            functools.partial(_na_kernel, n_lat=n_lat, n_ctx=n_ctx, rows=rows),
            in_specs=in_specs,
            scratch_shapes=[pltpu.VMEM((2, bias.shape[2], bias.shape[3] + SKEW_LANES), F32),
                            pltpu.VMEM((2, n_ctx, bias.shape[3] + SKEW_LANES), F32)],
            name="na_attn_" + str(layer), **common,
        )(q, k, vt, bias)

    def bounded():
        return pl.pallas_call(
            functools.partial(_na_bounded_kernel, n_lat=n_lat, n_ctx=n_ctx, rows=rows),
            in_specs=in_specs + [pl.BlockSpec((1, LANES), lambda hp, bi, rb: (0, 0))],
            name="na_attn_bounded_" + str(layer), **common,
        )(q, k, vt, bias, jnp.full((1, LANES), bound, F32))

    return lax.cond(bound <= SAFE_SCORE_BOUND, bounded, general)


def _out_proj_kernel(o_ref, oc_ref, g_ref, w_ref, x_ref, *refs, n_lat_tiles):
    mod_ref, xo_ref = refs[-2:]
    is_ctx = pl.program_id(1) >= n_lat_tiles
    o = jnp.where(is_ctx, oc_ref[...], o_ref[...])
    x = x_ref[...] if len(refs) == 2 else jnp.where(is_ctx, refs[0][...], x_ref[...])
    a = (o.astype(F32) * g_ref[...].astype(F32)).astype(BF16)
    xo_ref[...] = x + mod_ref[0, 2:3, :] * _dot(a, w_ref[...])


def _out_proj_final_kernel(o_ref, g_ref, w_ref, x_ref, mod_ref, fg_ref, y_ref):
    a = (o_ref[...].astype(F32) * g_ref[...].astype(F32)).astype(BF16)
    xn = x_ref[...] + mod_ref[0, 2:3, :] * _dot(a, w_ref[...])
    y_ref[...] = xn * _rms(xn) * fg_ref[...]


def _out_proj(o, o_ctx, gs, w_out, stream, mod3, layer, n_lat, final_g=None):
    x_lat, x_ctx, ctx_tile = stream
    b, _, d = x_lat.shape
    tm = ROW_TILE
    n_lat_tiles = n_lat // tm
    n = n_lat + tm
    ctx_row = b

    def mod_idx(bi, t):
        return (layer * MOD_ROWS + jnp.where(t < n_lat_tiles, bi, ctx_row), 0, 0)

    row = lambda bi, t: (bi, t, 0)
    lat_row = lambda bi, t: (bi, jnp.minimum(t, n_lat_tiles - 1), 0)
    w = o.shape[2]
    head = [pl.BlockSpec((None, tm, w), lat_row)]
    mid = [pl.BlockSpec((None, tm, w), row), pl.BlockSpec(w_out.shape, lambda bi, t: (0, 0)),
           pl.BlockSpec((None, tm, d), lat_row)]
    if final_g is None:
        combined = x_lat is x_ctx
        specs = head + [pl.BlockSpec((None, tm, w), lambda bi, t: (bi, 0, 0))] + mid[:2]
        if combined:
            specs, xs = specs + [pl.BlockSpec((None, tm, d), row)], [x_lat]
        else:
            specs += [mid[2], pl.BlockSpec((None, tm, d), lambda bi, t: (bi, ctx_tile, 0))]
            xs = [x_lat, x_ctx]
        return pl.pallas_call(
            functools.partial(_out_proj_kernel, n_lat_tiles=n_lat_tiles),
            grid=(b, n // tm),
            in_specs=specs + [pl.BlockSpec((1, 3, d), mod_idx)],
            out_specs=pl.BlockSpec((None, tm, d), row),
            out_shape=jax.ShapeDtypeStruct((b, n, d), F32),
            input_output_aliases={4: 0} if combined else {},
            compiler_params=_params("parallel", "parallel"),
            name="out_proj_" + str(layer),
        )(o, o_ctx, gs, w_out, *xs, mod3)
    return pl.pallas_call(
        _out_proj_final_kernel,
        grid=(b, n_lat_tiles),
        in_specs=head + mid + [pl.BlockSpec((1, 3, d), mod_idx), pl.BlockSpec((1, d), lambda bi, t: (0, 0))],
        out_specs=pl.BlockSpec((None, tm, d), row),
        out_shape=jax.ShapeDtypeStruct((b, n_lat, d), F32),
        compiler_params=_params("parallel", "parallel"),
        name="out_proj_final",
    )(o, gs, w_out, x_lat, mod3, final_g.reshape(1, d))


def _rope_tables(n_lat, n_ctx, rot_dim):
    n_freq = rot_dim // 4
    inv = ROPE_THETA ** (-jnp.arange(n_freq, dtype=F32) / n_freq)
    t = jnp.arange(n_lat)
    ang_r = (t // GRID_W).astype(F32)[:, None] * inv
    ang_c = (t % GRID_W).astype(F32)[:, None] * inv
    ang = jnp.concatenate([ang_r, ang_r, ang_c, ang_c], axis=-1)
    cos, sin = jnp.cos(ang), jnp.sin(ang)
    first = (jnp.arange(rot_dim) % (2 * n_freq)) < n_freq
    sin_up = jnp.where(first, -sin, 0.0)
    sin_dn = jnp.where(first, 0.0, sin)
    pad = LANES - rot_dim

    def finish(tab, fill):
        tab = jnp.pad(tab, ((0, 0), (0, pad)), constant_values=fill)
        return jnp.pad(tab, ((0, n_ctx), (0, 0)), constant_values=fill)

    return finish(cos, 1.0), finish(sin_up, 0.0), finish(sin_dn, 0.0)


def _mla_weights(w_in, w_uq, w_ukv):
    o_kv = MLA_Q_LORA
    o_kr = o_kv + MLA_KV_LORA
    o_g = o_kr + MLA_ROPE
    k_r = jnp.pad(w_in[:, o_kr:o_g], ((0, 0), (0, LANES - MLA_ROPE)))
    w_perm = jnp.concatenate([w_in[:, :o_kr], w_in[:, o_g:], k_r], axis=1).astype(BF16)
    uq = w_uq.reshape(MLA_Q_LORA, MLA_HEADS, MLA_NOPE + MLA_ROPE)
    uq = jnp.pad(uq, ((0, 0), (0, 0), (0, MLA_QK_PAD - MLA_NOPE - MLA_ROPE)))
    uq = uq.reshape(MLA_Q_LORA, MLA_HEADS * MLA_QK_PAD).astype(BF16)
    ukv = w_ukv.reshape(MLA_KV_LORA, MLA_HEADS, MLA_NOPE + MLA_V).astype(BF16)
    uk = ukv[:, :, :MLA_NOPE].reshape(MLA_KV_LORA, MLA_HEADS * MLA_NOPE)
    uv = ukv[:, :, MLA_NOPE:].reshape(MLA_KV_LORA, MLA_HEADS * MLA_V)
    return w_perm, uq, uk, uv


def kernel(x, c, ctx, c_ctx, mod_w, mod_b, norm_g, final_g, ga_w_in, ga_q_g, ga_k_g, ga_w_out, na_w_in, na_rpb, na_w_out, mla_w_in, mla_q_g, mla_kv_g, mla_w_uq, mla_w_ukv, mla_w_out):
    b, n_lat, d = x.shape
    n_ctx = ctx.shape[1]
    depth = mod_w.shape[0]
    assert n_lat % ROW_TILE == 0 and n_ctx == ROW_TILE and n_lat % n_ctx == 0
    assert b < MOD_ROWS and n_lat % (NA_STEP_BLOCKS * NA_BLOCK_ROWS * GRID_W) == 0
    assert n_lat // GRID_W >= NA_WIN_ROWS + NA_BLOCK_ROWS

    stream = (x, ctx, 0)
    cc = jnp.concatenate([c, c_ctx[None, :], jnp.zeros((MOD_ROWS - b - 1, d), F32)], axis=0)
    mod3 = _modulation(cc, mod_w, mod_b).reshape(depth * MOD_ROWS, 3, d)

    tab_a = _rope_tables(n_lat, n_ctx, GQA_HEAD_DIM)
    tab_m = _rope_tables(n_lat, n_ctx, MLA_ROPE)

    out = None
    for i in range(depth):
        kind, j = i % N_MIXERS, i // N_MIXERS
        need_ctx = i < depth - 1
        if kind == 0:
            n_q = d // GQA_HEAD_DIM
            n_kv = n_q // GQA_GROUP
            kern = functools.partial(_gqa_proj_kernel, n_q=n_q, n_kv=n_kv, scale=GQA_HEAD_DIM ** -0.5 * LOG2E)
            consts = [ga_w_in[j].astype(BF16), ga_q_g[j].reshape(1, -1), ga_k_g[j].reshape(1, -1)]
            kvw = n_kv * GQA_HEAD_DIM
            q, k, vt, gs = _project(kern, stream, mod3, i, norm_g[i], consts, tab_a, [d, kvw, kvw, d], n_lat)
            bound = (BOUND_SLACK * GQA_HEAD_DIM ** 0.5 * LOG2E
                     * jnp.max(jnp.abs(ga_q_g[j])) * jnp.max(jnp.abs(ga_k_g[j])))
            o = _flash_attention(q, k, vt, group=GQA_GROUP, dk=GQA_HEAD_DIM, n_lat=n_lat, layer=i, bound=bound)
            if need_ctx:
                oc = _ctx_attention(q, k, vt, group=GQA_GROUP, dk=GQA_HEAD_DIM, n_lat=n_lat, layer=i)
            w_out = ga_w_out[j]
        elif kind == 1:
            kern = functools.partial(_na_proj_kernel, width=d, scale=NA_HEAD_DIM ** -0.5 * LOG2E)
            q, k, vt, gs, norms = _project(kern, stream, mod3, i, norm_g[i], [na_w_in[j].astype(BF16)], (),
                                           [d, d, d, d], n_lat, True)
            bound = (BOUND_SLACK * jnp.sqrt(jnp.max(norms[:, :, 0, :]) * jnp.max(norms[:, :, MOD_ROWS, :]))
                     + LOG2E * jnp.max(jnp.abs(na_rpb[j])))
            o = _na_attention(q, k, vt, _na_bias(na_rpb[j], n_lat // GRID_W), bound, n_lat=n_lat, layer=i)
            if need_ctx:
                oc = _na_ctx_attention(q, k, vt, n_lat=n_lat, layer=i)
            w_out = na_w_out[j]
        else:
            kern = functools.partial(_mla_proj_kernel, width=d, scale=(MLA_NOPE + MLA_ROPE) ** -0.5 * LOG2E)
            w_perm, uq, uk, uv = _mla_weights(mla_w_in[j], mla_w_uq[j], mla_w_ukv[j])
            consts = [w_perm, mla_q_g[j].reshape(1, -1), mla_kv_g[j].reshape(1, -1), uq, uk, uv]
            widths = [MLA_HEADS * MLA_QK_PAD, MLA_HEADS * MLA_QK_PAD, MLA_HEADS * MLA_V, d]
            q, k, vt, gs, norms = _project(kern, stream, mod3, i, norm_g[i], consts, tab_m, widths, n_lat, True)
            bound = BOUND_SLACK * jnp.sqrt(jnp.max(norms[:, :, 0, 0]) * jnp.max(norms[:, :, MOD_ROWS, 0]))
            o = _flash_attention(q, k, vt, group=1, dk=MLA_QK_PAD, n_lat=n_lat, layer=i, bound=bound)
            if need_ctx:
                oc = _ctx_attention(q, k, vt, group=1, dk=MLA_QK_PAD, n_lat=n_lat, layer=i)
            w_out = mla_w_out[j]
        if need_ctx:
            xs = _out_proj(o, oc, gs, w_out.astype(BF16), stream, mod3, i, n_lat)
            stream = (xs, xs, n_lat // ROW_TILE)
        else:
            out = _out_proj(o, None, gs, w_out.astype(BF16), stream, mod3, i, n_lat, final_g=final_g)
    return out
```

```python
import functools

import jax
import numpy as np
import jax.numpy as jnp
from jax import lax
from jax.experimental import pallas as pl
from jax.experimental.pallas import tpu as pltpu

F32 = jnp.float32
BF16 = jnp.bfloat16

NORM_EPS = 1e-6
ROPE_THETA = 10000.0
GRID_W = 64
N_MIXERS = 3
LANES = 128
MOD_ROWS = 8
MASK_VALUE = -1e30
LOG2E = 1.4426950408889634

GQA_HEAD_DIM = 128
GQA_GROUP = 4
NA_HEAD_DIM = 64
NA_KH = 8
NA_KW = 16
NA_BLOCK_ROWS = 4
NA_WIN_ROWS = 12
NA_STEP_BLOCKS = 4
MLA_HEADS = 8
MLA_Q_LORA = 512
MLA_KV_LORA = 256
MLA_NOPE = 128
MLA_ROPE = 64
MLA_V = 128
MLA_QK_PAD = 256

ROW_TILE = 256
Q_COLS = 2048
Q_TILE = 256
SUM_ROWS = 16
SKEW_LANES = 128
SAFE_SCORE_BOUND = 50.0
BOUNDED_UNROLL = 5
BOUND_SLACK = 1.02
KV_CHUNKS = (768, 512, 256)
VMEM_LIMIT = 48 * 1024 * 1024


def _params(*sem):
    return pltpu.CompilerParams(dimension_semantics=sem, vmem_limit_bytes=VMEM_LIMIT)


def _silu(v):
    return v * (1.0 / (1.0 + jnp.exp(-v)))


def _rms(v):
    return lax.rsqrt(jnp.mean(v * v, axis=-1, keepdims=True) + NORM_EPS)


def _rope(v, cos, sin_up, sin_dn, half):
    w = v.shape[-1]
    return v * cos + pltpu.roll(v, w - half, 1) * sin_up + pltpu.roll(v, half, 1) * sin_dn


def _dot(a, b):
    return jnp.dot(a, b, preferred_element_type=F32)


def _dot_nt(a, b):
    return lax.dot_general(a, b, (((1,), (1,)), ((), ())), preferred_element_type=F32)


def _mod_kernel(c_ref, w_ref, b_ref, o_ref):
    a = _silu(c_ref[...])
    w = w_ref[0]
    a_hi = a.astype(BF16)
    a_lo = (a - a_hi.astype(F32)).astype(BF16)
    w_hi = w.astype(BF16)
    w_lo = (w - w_hi.astype(F32)).astype(BF16)
    o_ref[0] = _dot(a_hi, w_hi) + _dot(a_lo, w_hi) + _dot(a_hi, w_lo) + b_ref[0]


def _modulation(cc, mod_w, mod_b):
    depth, d, n3 = mod_w.shape
    tn = 1024
    return pl.pallas_call(
        _mod_kernel,
        grid=(depth, n3 // tn),
        in_specs=[
            pl.BlockSpec((MOD_ROWS, d), lambda l, j: (0, 0)),
            pl.BlockSpec((1, d, tn), lambda l, j: (l, 0, j)),
            pl.BlockSpec((1, 1, tn), lambda l, j: (l, 0, j)),
        ],
        out_specs=pl.BlockSpec((1, MOD_ROWS, tn), lambda l, j: (l, 0, j)),
        out_shape=jax.ShapeDtypeStruct((depth, MOD_ROWS, n3), F32),
        compiler_params=_params("parallel", "parallel"),
        name="adaln_modulation",
    )(cc, mod_w, mod_b.reshape(depth, 1, n3))


def _modulated_norm(x_ref, xc_ref, ng_ref, mod_ref, n_lat_tiles):
    x = jnp.where(pl.program_id(1) >= n_lat_tiles, xc_ref[...], x_ref[...])
    shift = mod_ref[0, 0:1, :]
    scale = mod_ref[0, 1:2, :]
    return ((x * _rms(x)) * ng_ref[...] * (1.0 + scale) + shift).astype(BF16)


def _gqa_proj_kernel(x_ref, xc_ref, mod_ref, ng_ref, w_ref, qg_ref, kg_ref, cos_ref, su_ref, sd_ref,
                     q_ref, k_ref, vt_ref, g_ref, *, n_lat_tiles, n_q, n_kv, scale):
    h = _modulated_norm(x_ref, xc_ref, ng_ref, mod_ref, n_lat_tiles)
    cos, su, sd = cos_ref[...], su_ref[...], sd_ref[...]
    hd = GQA_HEAD_DIM
    qw, kw = n_q * hd, n_kv * hd
    q = _dot(h, w_ref[:, :qw])
    for i in range(n_q):
        qh = q[:, i * hd:(i + 1) * hd]
        qh = qh * _rms(qh) * qg_ref[...]
        q_ref[:, i * hd:(i + 1) * hd] = (_rope(qh, cos, su, sd, hd // 4) * scale).astype(BF16)
    k = _dot(h, w_ref[:, qw:qw + kw])
    for i in range(n_kv):
        kh = k[:, i * hd:(i + 1) * hd]
        kh = kh * _rms(kh) * kg_ref[...]
        k_ref[:, i * hd:(i + 1) * hd] = _rope(kh, cos, su, sd, hd // 4).astype(BF16)
    v = _dot(h, w_ref[:, qw + kw:qw + 2 * kw])
    for i in range(n_kv):
        vt_ref[i * hd:(i + 1) * hd, :] = v[:, i * hd:(i + 1) * hd].T.astype(BF16)
    g_ref[...] = _silu(_dot(h, w_ref[:, qw + 2 * kw:])).astype(BF16)


def _na_proj_kernel(x_ref, xc_ref, mod_ref, ng_ref, w_ref, q_ref, k_ref, vt_ref, g_ref, norm_ref,
                    *, n_lat_tiles, width, scale):
    h = _modulated_norm(x_ref, xc_ref, ng_ref, mod_ref, n_lat_tiles)
    head_of_col = lax.broadcasted_iota(jnp.int32, (width, LANES), 0) // NA_HEAD_DIM
    lane = lax.broadcasted_iota(jnp.int32, (width, LANES), 1)
    pick = jnp.where(head_of_col == lane, 1.0, 0.0).astype(BF16)
    half = norm_ref.shape[0] // 2
    q = _dot(h, w_ref[:, :width]) * scale
    q_ref[...] = q.astype(BF16)
    q_sq = jnp.max(_dot((q * q).astype(BF16), pick), axis=0, keepdims=True)
    norm_ref[:half, :] = jnp.broadcast_to(q_sq, (half, LANES))
    k = _dot(h, w_ref[:, width:2 * width])
    k_ref[...] = k.astype(BF16)
    k_sq = jnp.max(_dot((k * k).astype(BF16), pick), axis=0, keepdims=True)
    norm_ref[half:, :] = jnp.broadcast_to(k_sq, (half, LANES))
    v = _dot(h, w_ref[:, 2 * width:3 * width])
    for i in range(width // LANES):
        vt_ref[i * LANES:(i + 1) * LANES, :] = v[:, i * LANES:(i + 1) * LANES].T.astype(BF16)
    g_ref[...] = _silu(_dot(h, w_ref[:, 3 * width:])).astype(BF16)


def _mla_proj_kernel(x_ref, xc_ref, mod_ref, ng_ref, w_ref, qg_ref, kvg_ref, wuq_ref, wuk_ref, wuv_ref,
                     cos_ref, su_ref, sd_ref, q_ref, k_ref, vt_ref, g_ref, norm_ref,
                     *, n_lat_tiles, width, scale):
    h = _modulated_norm(x_ref, xc_ref, ng_ref, mod_ref, n_lat_tiles)
    ones = jnp.ones((LANES, LANES), BF16)

    def row_sq(v):
        return _dot((v * v).astype(BF16), ones)

    cos, su, sd = cos_ref[...], su_ref[...], sd_ref[...]
    o_kv = MLA_Q_LORA
    o_g = o_kv + MLA_KV_LORA
    o_kr = o_g + width
    c_q = _dot(h, w_ref[:, :o_kv])
    c_q = (c_q * _rms(c_q) * qg_ref[...]).astype(BF16)
    q = _dot(c_q, wuq_ref[...])
    q_sq = None
    for i in range(MLA_HEADS):
        a = i * MLA_QK_PAD
        q_nope = q[:, a:a + MLA_NOPE] * scale
        q_rope = _rope(q[:, a + MLA_NOPE:a + MLA_QK_PAD], cos, su, sd, MLA_ROPE // 4) * scale
        q_ref[:, a:a + MLA_NOPE] = q_nope.astype(BF16)
        q_ref[:, a + MLA_NOPE:a + MLA_QK_PAD] = q_rope.astype(BF16)
        sq = row_sq(q_nope) + row_sq(q_rope)
        q_sq = sq if q_sq is None else jnp.maximum(q_sq, sq)
    c_kv = _dot(h, w_ref[:, o_kv:o_g])
    c_kv = (c_kv * _rms(c_kv) * kvg_ref[...]).astype(BF16)
    k_r = _rope(_dot(h, w_ref[:, o_kr:]), cos, su, sd, MLA_ROPE // 4)
    k_nope = _dot(c_kv, wuk_ref[...])
    k_sq = None
    for i in range(MLA_HEADS):
        a = i * MLA_QK_PAD
        k_h = k_nope[:, i * MLA_NOPE:(i + 1) * MLA_NOPE]
        k_ref[:, a:a + MLA_NOPE] = k_h.astype(BF16)
        k_ref[:, a + MLA_NOPE:a + MLA_QK_PAD] = k_r.astype(BF16)
        sq = row_sq(k_h)
        k_sq = sq if k_sq is None else jnp.maximum(k_sq, sq)
    k_sq = k_sq + row_sq(k_r)
    half = norm_ref.shape[0] // 2
    norm_ref[:half, :] = jnp.broadcast_to(jnp.max(q_sq, axis=0, keepdims=True), (half, LANES))
    norm_ref[half:, :] = jnp.broadcast_to(jnp.max(k_sq, axis=0, keepdims=True), (half, LANES))
    v = _dot(c_kv, wuv_ref[...])
    for i in range(MLA_HEADS):
        vt_ref[i * MLA_V:(i + 1) * MLA_V, :] = v[:, i * MLA_V:(i + 1) * MLA_V].T.astype(BF16)
    g_ref[...] = _silu(_dot(h, w_ref[:, o_g:o_kr])).astype(BF16)


def _project(kern, stream, mod3, layer, norm_g, consts, tables, out_widths, n_lat, with_norms=False):
    x_lat, x_ctx, ctx_tile = stream
    b, _, d = x_lat.shape
    tm = ROW_TILE
    n_lat_tiles = n_lat // tm
    n = n_lat + tm
    ctx_row = b

    def mod_idx(bi, t):
        return (layer * MOD_ROWS + jnp.where(t < n_lat_tiles, bi, ctx_row), 0, 0)

    in_specs = [
        pl.BlockSpec((None, tm, d), lambda bi, t: (bi, jnp.minimum(t, n_lat_tiles - 1), 0)),
        pl.BlockSpec((None, tm, d), lambda bi, t: (bi, ctx_tile, 0)),
        pl.BlockSpec((1, 3, d), mod_idx),
        pl.BlockSpec((1, d), lambda bi, t: (0, 0)),
    ]
    in_specs += [pl.BlockSpec(a.shape, lambda bi, t: (0, 0)) for a in consts]
    in_specs += [pl.BlockSpec((tm, LANES), lambda bi, t: (t, 0)) for _ in tables]
    out_specs = [pl.BlockSpec((None, tm, w), lambda bi, t: (bi, t, 0)) for w in out_widths]
    out_shape = [jax.ShapeDtypeStruct((b, n, w), BF16) for w in out_widths]
    out_specs[2] = pl.BlockSpec((None, None, out_widths[2], tm), lambda bi, t: (bi, t, 0, 0))
    out_shape[2] = jax.ShapeDtypeStruct((b, n // tm, out_widths[2], tm), BF16)
    if with_norms:
        out_specs.append(pl.BlockSpec((None, None, 2 * MOD_ROWS, LANES), lambda bi, t: (bi, t, 0, 0)))
        out_shape.append(jax.ShapeDtypeStruct((b, n // tm, 2 * MOD_ROWS, LANES), F32))
    return pl.pallas_call(
        functools.partial(kern, n_lat_tiles=n_lat_tiles),
        grid=(b, n // tm),
        in_specs=in_specs,
        out_specs=out_specs,
        out_shape=out_shape,
        compiler_params=_params("parallel", "parallel"),
        name="proj_" + str(layer),
    )(x_lat, x_ctx, mod3, norm_g.reshape(1, d), *consts, *tables)


def _stack_heads(q_ref, group, dk):
    return jnp.concatenate([q_ref[:, g * dk:(g + 1) * dk] for g in range(group)], axis=0)


def _unstack_heads(o_t, o_ref, group):
    tq = o_ref.shape[0]
    for g in range(group):
        o_ref[:, g * LANES:(g + 1) * LANES] = o_t[:, g * tq:(g + 1) * tq].T.astype(BF16)


def _pv_t(vt_ref, first_slab, p, sum_rows=0):
    slab = vt_ref.shape[-1]
    acc = None
    for i in range(p.shape[0] // slab):
        vt = vt_ref[first_slab + i]
        if sum_rows:
            vt = jnp.concatenate([vt, jnp.ones((sum_rows, slab), vt.dtype)], axis=0)
        t = _dot(vt, p[i * slab:(i + 1) * slab, :])
        acc = t if acc is None else acc + t
    return acc


def _flash_kernel(q_ref, k_ref, vt_ref, o_ref, sa_ref, sb_ref, acc_ref, m_ref, *, group, dk, bk):
    slab = vt_ref.shape[-1]
    n_chunks = k_ref.shape[0] // bk
    q = _stack_heads(q_ref, group, dk)
    tiles = [slice(j * Q_TILE, (j + 1) * Q_TILE) for j in range(q.shape[0] // Q_TILE)]

    def scores(c, s_ref):
        k = k_ref[pl.ds(pl.multiple_of(c * bk, bk), bk), :]
        out = []
        for t in tiles:
            s = _dot_nt(k, q[t, :])
            s_ref[:, t] = s
            out.append(jnp.max(s, axis=0, keepdims=True))
        return out

    def update(s_ref, s_max, c, t):
        m = m_ref[:, t]
        m_new = jnp.maximum(m, s_max)
        alpha = jnp.exp2(m - m_new)
        p = jnp.exp2(s_ref[:, t] - m_new).astype(BF16)
        m_ref[:, t] = m_new
        acc_ref[:, t] = alpha * acc_ref[:, t] + _pv_t(vt_ref, c * (bk // slab), p, SUM_ROWS)

    def step(c, cur_ref, cur_max, nxt_ref):
        k = k_ref[pl.ds(pl.multiple_of((c + 1) * bk, bk), bk), :]
        nxt_max = []
        for j, t in enumerate(tiles):
            s = _dot_nt(k, q[t, :])
            nxt_ref[:, t] = s
            nxt_max.append(jnp.max(s, axis=0, keepdims=True))
            update(cur_ref, cur_max[j], c, t)
        return nxt_max

    m_ref[...] = jnp.full(m_ref.shape, MASK_VALUE, F32)
    acc_ref[...] = jnp.zeros(acc_ref.shape, F32)
    max_a = scores(0, sa_ref)

    def body(i, max_a):
        max_b = step(2 * i, sa_ref, max_a, sb_ref)
        return tuple(step(2 * i + 1, sb_ref, max_b, sa_ref))

    pairs = (n_chunks - 1) // 2
    max_a = lax.fori_loop(0, pairs, body, tuple(max_a))
    last = 2 * pairs
    if n_chunks - last == 2:
        max_b = step(last, sa_ref, max_a, sb_ref)
        last, s_ref, s_max = last + 1, sb_ref, max_b
    else:
        s_ref, s_max = sa_ref, max_a
    for j, t in enumerate(tiles):
        update(s_ref, s_max[j], last, t)
    _unstack_heads(acc_ref[:LANES, :] * (1.0 / acc_ref[LANES:LANES + 1, :]), o_ref, group)


def _flash_bounded_kernel(q_ref, k_ref, vt_ref, bound_ref, o_ref, acc_ref, *, group, dk, bk):
    slab = vt_ref.shape[-1]
    n_chunks = k_ref.shape[0] // bk
    q = _stack_heads(q_ref, group, dk)
    tiles = [slice(0, q.shape[0])]
    bound = bound_ref[:, :1]
    acc_ref[...] = jnp.zeros(acc_ref.shape, F32)

    def chunk(c):
        k = k_ref[pl.ds(pl.multiple_of(c * bk, bk), bk), :]
        for t in tiles:
            p = jnp.exp2(_dot_nt(k, q[t, :]) - bound)
            acc_ref[:LANES, t] += _pv_t(vt_ref, c * (bk // slab), p.astype(BF16))
            acc_ref[LANES:, t] += jnp.sum(p.reshape(bk // MOD_ROWS, MOD_ROWS, p.shape[1]), axis=0)

    def body(i, carry):
        for u in range(BOUNDED_UNROLL):
            chunk(BOUNDED_UNROLL * i + u)
        return carry

    lax.fori_loop(0, n_chunks // BOUNDED_UNROLL, body, 0)
    for c in range(n_chunks - n_chunks % BOUNDED_UNROLL, n_chunks):
        chunk(c)
    l = jnp.sum(acc_ref[LANES:, :], axis=0, keepdims=True)
    _unstack_heads(acc_ref[:LANES, :] * (1.0 / l), o_ref, group)


def _flash_attention(q, k, vt, *, group, dk, n_lat, layer, bound=None):
    b, n, _ = q.shape
    n_kv = k.shape[2] // dk
    n_slab, slab = vt.shape[1], vt.shape[3]
    tq = min(Q_COLS // group, n_lat)
    bk = max(c for c in KV_CHUNKS if n % c == 0)
    assert bk % slab == 0
    nq = group * tq
    in_specs = [
        pl.BlockSpec((None, tq, group * dk), lambda bi, h, t: (bi, t, h)),
        pl.BlockSpec((None, n, dk), lambda bi, h, t: (bi, 0, h)),
        pl.BlockSpec((None, n_slab, LANES, slab), lambda bi, h, t: (bi, 0, h, 0)),
    ]
    common = dict(
        grid=(b, n_kv, n_lat // tq),
        out_specs=pl.BlockSpec((None, tq, group * LANES), lambda bi, h, t: (bi, t, h)),
        out_shape=jax.ShapeDtypeStruct((b, n_lat, n_kv * group * LANES), BF16),
        compiler_params=_params("parallel", "parallel", "parallel"),
    )
    acc = pltpu.VMEM((LANES + SUM_ROWS, nq), F32)

    def general():
        s_buf = pltpu.VMEM((bk, nq + SKEW_LANES), F32)
        return pl.pallas_call(
            functools.partial(_flash_kernel, group=group, dk=dk, bk=bk),
            in_specs=in_specs,
            scratch_shapes=[s_buf, s_buf, acc, pltpu.VMEM((1, nq), F32)],
            name="flash_" + str(layer), **common,
        )(q, k, vt)

    if bound is None:
        return general()

    def bounded():
        return pl.pallas_call(
            functools.partial(_flash_bounded_kernel, group=group, dk=dk, bk=bk),
            in_specs=in_specs + [pl.BlockSpec((1, LANES), lambda bi, h, t: (0, 0))],
            scratch_shapes=[pltpu.VMEM((LANES + MOD_ROWS, nq), F32)],
            name="flash_bounded_" + str(layer), **common,
        )(q, k, vt, jnp.full((1, LANES), bound, F32))

    return lax.cond(bound <= SAFE_SCORE_BOUND, bounded, general)


def _ctx_attn_kernel(q_ref, k_ref, vt_ref, o_ref, *, group, dk):
    q = _stack_heads(q_ref, group, dk)
    s = _dot_nt(k_ref[...], q)
    p = jnp.exp2(s - jnp.max(s, axis=0, keepdims=True))
    l = jnp.sum(p, axis=0, keepdims=True)
    _unstack_heads(_dot(vt_ref[...], p.astype(BF16)) * (1.0 / l), o_ref, group)


def _ctx_attention(q, k, vt, *, group, dk, n_lat, layer):
    b, n, _ = q.shape
    n_ctx = n - n_lat
    n_kv = k.shape[2] // dk
    slab = vt.shape[3]
    assert slab == n_ctx
    t = n_lat // n_ctx
    return pl.pallas_call(
        functools.partial(_ctx_attn_kernel, group=group, dk=dk),
        grid=(b, n_kv),
        in_specs=[
            pl.BlockSpec((None, n_ctx, group * dk), lambda bi, h: (bi, t, h)),
            pl.BlockSpec((None, n_ctx, dk), lambda bi, h: (bi, t, h)),
            pl.BlockSpec((None, None, LANES, slab), lambda bi, h: (bi, t, h, 0)),
        ],
        out_specs=pl.BlockSpec((None, n_ctx, group * LANES), lambda bi, h: (bi, 0, h)),
        out_shape=jax.ShapeDtypeStruct((b, n_ctx, n_kv * group * LANES), BF16),
        compiler_params=_params("parallel", "parallel"),
        name="ctx_attn_" + str(layer),
    )(q, k, vt)


def _split_heads(q):
    lane = lax.broadcasted_iota(jnp.int32, (1, LANES), 1)
    zero = jnp.zeros_like(q)
    return jnp.concatenate(
        [jnp.where((lane >= j * NA_HEAD_DIM) & (lane < (j + 1) * NA_HEAD_DIM), q, zero)
         for j in range(LANES // NA_HEAD_DIM)], axis=0)


def _merge_heads(o_t, nq):
    parts = [o_t[j * NA_HEAD_DIM:(j + 1) * NA_HEAD_DIM, j * nq:(j + 1) * nq]
             for j in range(LANES // NA_HEAD_DIM)]
    return jnp.concatenate(parts, axis=0).T


def _na_ctx_kernel(q_ref, k_ref, vt_ref, o_ref):
    q2 = _split_heads(q_ref[...])
    s = _dot_nt(k_ref[...], q2)
    p = jnp.exp2(s - jnp.max(s, axis=0, keepdims=True))
    l = jnp.sum(p, axis=0, keepdims=True)
    o_t = _dot(vt_ref[...], p.astype(BF16)) * (1.0 / l)
    o_ref[...] = _merge_heads(o_t, q_ref.shape[0]).astype(BF16)


def _na_ctx_attention(q, k, vt, *, n_lat, layer):
    b, n, width = q.shape
    n_ctx = n - n_lat
    assert vt.shape[3] == n_ctx
    t = n_lat // n_ctx
    blk = pl.BlockSpec((None, n_ctx, LANES), lambda bi, h: (bi, t, h))
    return pl.pallas_call(
        _na_ctx_kernel,
        grid=(b, width // LANES),
        in_specs=[blk, blk, pl.BlockSpec((None, None, LANES, n_ctx), lambda bi, h: (bi, t, h, 0))],
        out_specs=pl.BlockSpec((None, n_ctx, LANES), lambda bi, h: (bi, 0, h)),
        out_shape=jax.ShapeDtypeStruct((b, n_ctx, width), BF16),
        compiler_params=_params("parallel", "parallel"),
        name="ctx_attn_" + str(layer),
    )(q, k, vt)


def _na_window_start(blk, rows, lib):
    lo = lib.minimum(lib.maximum(NA_BLOCK_ROWS * blk - NA_KH // 2, 0), rows - NA_KH)
    return lib.minimum(lo, rows - NA_WIN_ROWS)


def _na_kernel(q_ref, k_ref, vt_ref, bias_ref, o_ref, sl_ref, sc_ref, *, n_lat, n_ctx, rows):
    slab = vt_ref.shape[-1]
    nq = NA_BLOCK_ROWS * GRID_W
    n_blocks = rows // NA_BLOCK_ROWS
    nk = NA_WIN_ROWS * GRID_W
    kc = k_ref[n_lat:n_lat + n_ctx, :]
    n_here = q_ref.shape[0] // nq

    def scores(i):
        blk = pl.program_id(2) * NA_STEP_BLOCKS + i
        kind = jnp.where(blk == 0, 1, jnp.where(blk == n_blocks - 1, 2, 0))
        ws = _na_window_start(blk, rows, jnp)
        q2 = _split_heads(q_ref[i * nq:(i + 1) * nq, :])
        s_loc = _dot_nt(k_ref[pl.ds(pl.multiple_of(ws * GRID_W, slab), nk), :], q2) + bias_ref[kind]
        s_ctx = _dot_nt(kc, q2)
        sl_ref[i % 2, :, :s_loc.shape[1]] = s_loc
        sc_ref[i % 2, :, :s_ctx.shape[1]] = s_ctx
        return ws, jnp.maximum(jnp.max(s_loc, axis=0, keepdims=True), jnp.max(s_ctx, axis=0, keepdims=True))

    nxt = scores(0)
    for i in range(n_here):
        ws, m = nxt
        if i + 1 < n_here:
            nxt = scores(i + 1)
        parts = []
        for j in range(LANES // NA_HEAD_DIM):
            t = slice(j * nq, (j + 1) * nq)
            p_loc = jnp.exp2(sl_ref[i % 2, :, t] - m[:, t]).astype(BF16)
            p_ctx = jnp.exp2(sc_ref[i % 2, :, t] - m[:, t]).astype(BF16)
            o_t = _pv_t(vt_ref, ws * GRID_W // slab, p_loc, SUM_ROWS)
            o_t = o_t + _pv_t(vt_ref, n_lat // slab, p_ctx, SUM_ROWS)
            parts.append(o_t[j * NA_HEAD_DIM:(j + 1) * NA_HEAD_DIM, :] * (1.0 / o_t[LANES:LANES + 1, :]))
        o_ref[i * nq:(i + 1) * nq, :] = jnp.concatenate(parts, axis=0).T.astype(BF16)


def _na_bounded_kernel(q_ref, k_ref, vt_ref, bias_ref, bound_ref, o_ref, *, n_lat, n_ctx, rows):
    slab = vt_ref.shape[-1]
    nq = NA_BLOCK_ROWS * GRID_W
    n_blocks = rows // NA_BLOCK_ROWS
    nk = NA_WIN_ROWS * GRID_W
    kc = k_ref[n_lat:n_lat + n_ctx, :]
    bound = bound_ref[:, :1]
    for i in range(q_ref.shape[0] // nq):
        blk = pl.program_id(2) * NA_STEP_BLOCKS + i
        kind = jnp.where(blk == 0, 1, jnp.where(blk == n_blocks - 1, 2, 0))
        ws = _na_window_start(blk, rows, jnp)
        q2 = _split_heads(q_ref[i * nq:(i + 1) * nq, :])
        kw = k_ref[pl.ds(pl.multiple_of(ws * GRID_W, slab), nk), :]
        p_loc = jnp.exp2(_dot_nt(kw, q2) + (bias_ref[kind] - bound)).astype(BF16)
        p_ctx = jnp.exp2(_dot_nt(kc, q2) - bound).astype(BF16)
        o_t = _pv_t(vt_ref, ws * GRID_W // slab, p_loc, SUM_ROWS)
        o_t = o_t + _pv_t(vt_ref, n_lat // slab, p_ctx, SUM_ROWS)
        o_t = o_t[:LANES, :] * (1.0 / o_t[LANES:LANES + 1, :])
        o_ref[i * nq:(i + 1) * nq, :] = _merge_heads(o_t, nq).astype(BF16)


def _na_bias(rpb, rows):
    h, n_dr, n_dc = rpb.shape
    n_blocks = rows // NA_BLOCK_ROWS
    lead = GRID_W - NA_KW
    wide = 2 * GRID_W
    w = jnp.pad(rpb * LOG2E, ((0, 0), (0, 0), (lead, wide - lead - n_dc)), constant_values=MASK_VALUE)
    flat = jnp.broadcast_to(w[:, :, None, :], (h, n_dr, GRID_W, wide)).reshape(h, n_dr, GRID_W * wide)
    skew = flat[:, :, GRID_W - 1:GRID_W - 1 + GRID_W * (wide - 1)].reshape(h, n_dr, GRID_W, wide - 1)
    qc = jnp.arange(GRID_W)
    cs = jnp.clip(qc - NA_KW // 2, 0, GRID_W - NA_KW)
    col_ok = (qc[None, :] >= cs[:, None]) & (qc[None, :] < cs[:, None] + NA_KW)
    c_t = jnp.swapaxes(jnp.where(col_ok, skew[..., :GRID_W], MASK_VALUE), -1, -2)
    masked = jnp.full((h, GRID_W, GRID_W), MASK_VALUE, F32)
    kinds = []
    for blk in (1, 0, n_blocks - 1):
        ws = _na_window_start(blk, rows, np)
        strips = []
        for j in range(NA_WIN_ROWS):
            kr = ws + j
            blocks = []
            for i in range(NA_BLOCK_ROWS):
                r = NA_BLOCK_ROWS * blk + i
                rs = min(max(r - NA_KH // 2, 0), rows - NA_KH)
                blocks.append(c_t[:, kr - r + NA_KH - 1] if rs <= kr < rs + NA_KH else masked)
            strips.append(jnp.concatenate(blocks, axis=-1))
        kinds.append(jnp.concatenate(strips, axis=1))
    bias = jnp.stack(kinds, axis=1)
    sub = LANES // NA_HEAD_DIM
    bias = bias.reshape(h // sub, sub, len(kinds), NA_WIN_ROWS * GRID_W, NA_BLOCK_ROWS * GRID_W)
    return jnp.concatenate([bias[:, j] for j in range(sub)], axis=-1)


def _na_attention(q, k, vt, bias, bound, *, n_lat, layer):
    b, n, width = q.shape
    n_ctx = n - n_lat
    rows = n_lat // GRID_W
    n_slab, slab = vt.shape[1], vt.shape[3]
    tq = NA_STEP_BLOCKS * NA_BLOCK_ROWS * GRID_W
    in_specs = [
        pl.BlockSpec((None, tq, LANES), lambda hp, bi, rb: (bi, rb, hp)),
        pl.BlockSpec((None, n, LANES), lambda hp, bi, rb: (bi, 0, hp)),
        pl.BlockSpec((None, n_slab, LANES, slab), lambda hp, bi, rb: (bi, 0, hp, 0)),
        pl.BlockSpec((None,) + bias.shape[1:], lambda hp, bi, rb: (hp, 0, 0, 0)),
    ]
    common = dict(
        grid=(width // LANES, b, n_lat // tq),
        out_specs=pl.BlockSpec((None, tq, LANES), lambda hp, bi, rb: (bi, rb, hp)),
        out_shape=jax.ShapeDtypeStruct((b, n_lat, width), BF16),
        compiler_params=_params("parallel", "parallel", "arbitrary"),
    )

    def general():
        return pl.pallas_call(
            functools.partial(_na_kernel, n_lat=n_lat, n_ctx=n_ctx, rows=rows),
            in_specs=in_specs,
            scratch_shapes=[pltpu.VMEM((2, bias.shape[2], bias.shape[3] + SKEW_LANES), F32),
                            pltpu.VMEM((2, n_ctx, bias.shape[3] + SKEW_LANES), F32)],
            name="na_attn_" + str(layer), **common,
        )(q, k, vt, bias)

    def bounded():
        return pl.pallas_call(
            functools.partial(_na_bounded_kernel, n_lat=n_lat, n_ctx=n_ctx, rows=rows),
            in_specs=in_specs + [pl.BlockSpec((1, LANES), lambda hp, bi, rb: (0, 0))],
            name="na_attn_bounded_" + str(layer), **common,
        )(q, k, vt, bias, jnp.full((1, LANES), bound, F32))

    return lax.cond(bound <= SAFE_SCORE_BOUND, bounded, general)


def _out_proj_kernel(o_ref, oc_ref, g_ref, w_ref, x_ref, *refs, n_lat_tiles):
    mod_ref, xo_ref = refs[-2:]
    is_ctx = pl.program_id(1) >= n_lat_tiles
    o = jnp.where(is_ctx, oc_ref[...], o_ref[...])
    x = x_ref[...] if len(refs) == 2 else jnp.where(is_ctx, refs[0][...], x_ref[...])
    a = (o.astype(F32) * g_ref[...].astype(F32)).astype(BF16)
    xo_ref[...] = x + mod_ref[0, 2:3, :] * _dot(a, w_ref[...])


def _out_proj_final_kernel(o_ref, g_ref, w_ref, x_ref, mod_ref, fg_ref, y_ref):
    a = (o_ref[...].astype(F32) * g_ref[...].astype(F32)).astype(BF16)
    xn = x_ref[...] + mod_ref[0, 2:3, :] * _dot(a, w_ref[...])
    y_ref[...] = xn * _rms(xn) * fg_ref[...]


def _out_proj(o, o_ctx, gs, w_out, stream, mod3, layer, n_lat, final_g=None):
    x_lat, x_ctx, ctx_tile = stream
    b, _, d = x_lat.shape
    tm = ROW_TILE
    n_lat_tiles = n_lat // tm
    n = n_lat + tm
    ctx_row = b

    def mod_idx(bi, t):
        return (layer * MOD_ROWS + jnp.where(t < n_lat_tiles, bi, ctx_row), 0, 0)

    row = lambda bi, t: (bi, t, 0)
    lat_row = lambda bi, t: (bi, jnp.minimum(t, n_lat_tiles - 1), 0)
    w = o.shape[2]
    head = [pl.BlockSpec((None, tm, w), lat_row)]
    mid = [pl.BlockSpec((None, tm, w), row), pl.BlockSpec(w_out.shape, lambda bi, t: (0, 0)),
           pl.BlockSpec((None, tm, d), lat_row)]
    if final_g is None:
        combined = x_lat is x_ctx
        specs = head + [pl.BlockSpec((None, tm, w), lambda bi, t: (bi, 0, 0))] + mid[:2]
        if combined:
            specs, xs = specs + [pl.BlockSpec((None, tm, d), row)], [x_lat]
        else:
            specs += [mid[2], pl.BlockSpec((None, tm, d), lambda bi, t: (bi, ctx_tile, 0))]
            xs = [x_lat, x_ctx]
        return pl.pallas_call(
            functools.partial(_out_proj_kernel, n_lat_tiles=n_lat_tiles),
            grid=(b, n // tm),
            in_specs=specs + [pl.BlockSpec((1, 3, d), mod_idx)],
            out_specs=pl.BlockSpec((None, tm, d), row),
            out_shape=jax.ShapeDtypeStruct((b, n, d), F32),
            input_output_aliases={4: 0} if combined else {},
            compiler_params=_params("parallel", "parallel"),
            name="out_proj_" + str(layer),
        )(o, o_ctx, gs, w_out, *xs, mod3)
    return pl.pallas_call(
        _out_proj_final_kernel,
        grid=(b, n_lat_tiles),
        in_specs=head + mid + [pl.BlockSpec((1, 3, d), mod_idx), pl.BlockSpec((1, d), lambda bi, t: (0, 0))],
        out_specs=pl.BlockSpec((None, tm, d), row),
        out_shape=jax.ShapeDtypeStruct((b, n_lat, d), F32),
        compiler_params=_params("parallel", "parallel"),
        name="out_proj_final",
    )(o, gs, w_out, x_lat, mod3, final_g.reshape(1, d))


def _rope_tables(n_lat, n_ctx, rot_dim):
    n_freq = rot_dim // 4
    inv = ROPE_THETA ** (-jnp.arange(n_freq, dtype=F32) / n_freq)
    t = jnp.arange(n_lat)
    ang_r = (t // GRID_W).astype(F32)[:, None] * inv
    ang_c = (t % GRID_W).astype(F32)[:, None] * inv
    ang = jnp.concatenate([ang_r, ang_r, ang_c, ang_c], axis=-1)
    cos, sin = jnp.cos(ang), jnp.sin(ang)
    first = (jnp.arange(rot_dim) % (2 * n_freq)) < n_freq
    sin_up = jnp.where(first, -sin, 0.0)
    sin_dn = jnp.where(first, 0.0, sin)
    pad = LANES - rot_dim

    def finish(tab, fill):
        tab = jnp.pad(tab, ((0, 0), (0, pad)), constant_values=fill)
        return jnp.pad(tab, ((0, n_ctx), (0, 0)), constant_values=fill)

    return finish(cos, 1.0), finish(sin_up, 0.0), finish(sin_dn, 0.0)


def _mla_weights(w_in, w_uq, w_ukv):
    o_kv = MLA_Q_LORA
    o_kr = o_kv + MLA_KV_LORA
    o_g = o_kr + MLA_ROPE
    k_r = jnp.pad(w_in[:, o_kr:o_g], ((0, 0), (0, LANES - MLA_ROPE)))
    w_perm = jnp.concatenate([w_in[:, :o_kr], w_in[:, o_g:], k_r], axis=1).astype(BF16)
    uq = w_uq.reshape(MLA_Q_LORA, MLA_HEADS, MLA_NOPE + MLA_ROPE)
    uq = jnp.pad(uq, ((0, 0), (0, 0), (0, MLA_QK_PAD - MLA_NOPE - MLA_ROPE)))
    uq = uq.reshape(MLA_Q_LORA, MLA_HEADS * MLA_QK_PAD).astype(BF16)
    ukv = w_ukv.reshape(MLA_KV_LORA, MLA_HEADS, MLA_NOPE + MLA_V).astype(BF16)
    uk = ukv[:, :, :MLA_NOPE].reshape(MLA_KV_LORA, MLA_HEADS * MLA_NOPE)
    uv = ukv[:, :, MLA_NOPE:].reshape(MLA_KV_LORA, MLA_HEADS * MLA_V)
    return w_perm, uq, uk, uv


def kernel(x, c, ctx, c_ctx, mod_w, mod_b, norm_g, final_g, ga_w_in, ga_q_g, ga_k_g, ga_w_out, na_w_in, na_rpb, na_w_out, mla_w_in, mla_q_g, mla_kv_g, mla_w_uq, mla_w_ukv, mla_w_out):
    b, n_lat, d = x.shape
    n_ctx = ctx.shape[1]
    depth = mod_w.shape[0]
    assert n_lat % ROW_TILE == 0 and n_ctx == ROW_TILE and n_lat % n_ctx == 0
    assert b < MOD_ROWS and n_lat % (NA_STEP_BLOCKS * NA_BLOCK_ROWS * GRID_W) == 0
    assert n_lat // GRID_W >= NA_WIN_ROWS + NA_BLOCK_ROWS

    stream = (x, ctx, 0)
    cc = jnp.concatenate([c, c_ctx[None, :], jnp.zeros((MOD_ROWS - b - 1, d), F32)], axis=0)
    mod3 = _modulation(cc, mod_w, mod_b).reshape(depth * MOD_ROWS, 3, d)

    tab_a = _rope_tables(n_lat, n_ctx, GQA_HEAD_DIM)
    tab_m = _rope_tables(n_lat, n_ctx, MLA_ROPE)

    out = None
    for i in range(depth):
        kind, j = i % N_MIXERS, i // N_MIXERS
        need_ctx = i < depth - 1
        if kind == 0:
            n_q = d // GQA_HEAD_DIM
            n_kv = n_q // GQA_GROUP
            kern = functools.partial(_gqa_proj_kernel, n_q=n_q, n_kv=n_kv, scale=GQA_HEAD_DIM ** -0.5 * LOG2E)
            consts = [ga_w_in[j].astype(BF16), ga_q_g[j].reshape(1, -1), ga_k_g[j].reshape(1, -1)]
            kvw = n_kv * GQA_HEAD_DIM
            q, k, vt, gs = _project(kern, stream, mod3, i, norm_g[i], consts, tab_a, [d, kvw, kvw, d], n_lat)
            bound = (BOUND_SLACK * GQA_HEAD_DIM ** 0.5 * LOG2E
                     * jnp.max(jnp.abs(ga_q_g[j])) * jnp.max(jnp.abs(ga_k_g[j])))
            o = _flash_attention(q, k, vt, group=GQA_GROUP, dk=GQA_HEAD_DIM, n_lat=n_lat, layer=i, bound=bound)
            if need_ctx:
                oc = _ctx_attention(q, k, vt, group=GQA_GROUP, dk=GQA_HEAD_DIM, n_lat=n_lat, layer=i)
            w_out = ga_w_out[j]
        elif kind == 1:
            kern = functools.partial(_na_proj_kernel, width=d, scale=NA_HEAD_DIM ** -0.5 * LOG2E)
            q, k, vt, gs, norms = _project(kern, stream, mod3, i, norm_g[i], [na_w_in[j].astype(BF16)], (),
                                           [d, d, d, d], n_lat, True)
            bound = (BOUND_SLACK * jnp.sqrt(jnp.max(norms[:, :, 0, :]) * jnp.max(norms[:, :, MOD_ROWS, :]))
                     + LOG2E * jnp.max(jnp.abs(na_rpb[j])))
            o = _na_attention(q, k, vt, _na_bias(na_rpb[j], n_lat // GRID_W), bound, n_lat=n_lat, layer=i)
            if need_ctx:
                oc = _na_ctx_attention(q, k, vt, n_lat=n_lat, layer=i)
            w_out = na_w_out[j]
        else:
            kern = functools.partial(_mla_proj_kernel, width=d, scale=(MLA_NOPE + MLA_ROPE) ** -0.5 * LOG2E)
            w_perm, uq, uk, uv = _mla_weights(mla_w_in[j], mla_w_uq[j], mla_w_ukv[j])
            consts = [w_perm, mla_q_g[j].reshape(1, -1), mla_kv_g[j].reshape(1, -1), uq, uk, uv]
            widths = [MLA_HEADS * MLA_QK_PAD, MLA_HEADS * MLA_QK_PAD, MLA_HEADS * MLA_V, d]
            q, k, vt, gs, norms = _project(kern, stream, mod3, i, norm_g[i], consts, tab_m, widths, n_lat, True)
            bound = BOUND_SLACK * jnp.sqrt(jnp.max(norms[:, :, 0, 0]) * jnp.max(norms[:, :, MOD_ROWS, 0]))
            o = _flash_attention(q, k, vt, group=1, dk=MLA_QK_PAD, n_lat=n_lat, layer=i, bound=bound)
            if need_ctx:
                oc = _ctx_attention(q, k, vt, group=1, dk=MLA_QK_PAD, n_lat=n_lat, layer=i)
            w_out = mla_w_out[j]
        if need_ctx:
            xs = _out_proj(o, oc, gs, w_out.astype(BF16), stream, mod3, i, n_lat)
            stream = (xs, xs, n_lat // ROW_TILE)
        else:
            out = _out_proj(o, None, gs, w_out.astype(BF16), stream, mod3, i, n_lat, final_g=final_g)
    return out
```

```python
import functools

import jax
import numpy as np
import jax.numpy as jnp
from jax import lax
from jax.experimental import pallas as pl
from jax.experimental.pallas import tpu as pltpu

F32 = jnp.float32
BF16 = jnp.bfloat16

NORM_EPS = 1e-6
ROPE_THETA = 10000.0
GRID_W = 64
N_MIXERS = 3
LANES = 128
MOD_ROWS = 8
MASK_VALUE = -1e30
LOG2E = 1.4426950408889634

GQA_HEAD_DIM = 128
GQA_GROUP = 4
NA_HEAD_DIM = 64
NA_KH = 8
NA_KW = 16
NA_BLOCK_ROWS = 4
NA_WIN_ROWS = 12
NA_STEP_BLOCKS = 4
MLA_HEADS = 8
MLA_Q_LORA = 512
MLA_KV_LORA = 256
MLA_NOPE = 128
MLA_ROPE = 64
MLA_V = 128
MLA_QK_PAD = 256

ROW_TILE = 256
Q_COLS = 4096
Q_TILE = 256
SUM_ROWS = 16
SKEW_LANES = 128
SAFE_SCORE_BOUND = 50.0
BOUNDED_UNROLL = 5
BOUND_SLACK = 1.02
KV_CHUNKS = (768, 512, 256)
VMEM_LIMIT = 48 * 1024 * 1024


def _params(*sem):
    return pltpu.CompilerParams(dimension_semantics=sem, vmem_limit_bytes=VMEM_LIMIT)


def _silu(v):
    return v * (1.0 / (1.0 + jnp.exp(-v)))


def _rms(v):
    return lax.rsqrt(jnp.mean(v * v, axis=-1, keepdims=True) + NORM_EPS)


def _rope(v, cos, sin_up, sin_dn, half):
    w = v.shape[-1]
    return v * cos + pltpu.roll(v, w - half, 1) * sin_up + pltpu.roll(v, half, 1) * sin_dn


def _dot(a, b):
    return jnp.dot(a, b, preferred_element_type=F32)


def _dot_nt(a, b):
    return lax.dot_general(a, b, (((1,), (1,)), ((), ())), preferred_element_type=F32)


def _mod_kernel(c_ref, w_ref, b_ref, o_ref):
    a = _silu(c_ref[...])
    w = w_ref[0]
    a_hi = a.astype(BF16)
    a_lo = (a - a_hi.astype(F32)).astype(BF16)
    w_hi = w.astype(BF16)
    w_lo = (w - w_hi.astype(F32)).astype(BF16)
    o_ref[0] = _dot(a_hi, w_hi) + _dot(a_lo, w_hi) + _dot(a_hi, w_lo) + b_ref[0]


def _modulation(cc, mod_w, mod_b):
    depth, d, n3 = mod_w.shape
    tn = 1024
    return pl.pallas_call(
        _mod_kernel,
        grid=(depth, n3 // tn),
        in_specs=[
            pl.BlockSpec((MOD_ROWS, d), lambda l, j: (0, 0)),
            pl.BlockSpec((1, d, tn), lambda l, j: (l, 0, j)),
            pl.BlockSpec((1, 1, tn), lambda l, j: (l, 0, j)),
        ],
        out_specs=pl.BlockSpec((1, MOD_ROWS, tn), lambda l, j: (l, 0, j)),
        out_shape=jax.ShapeDtypeStruct((depth, MOD_ROWS, n3), F32),
        compiler_params=_params("parallel", "parallel"),
        name="adaln_modulation",
    )(cc, mod_w, mod_b.reshape(depth, 1, n3))


def _modulated_norm(x_ref, xc_ref, ng_ref, mod_ref, n_lat_tiles):
    x = jnp.where(pl.program_id(1) >= n_lat_tiles, xc_ref[...], x_ref[...])
    shift = mod_ref[0, 0:1, :]
    scale = mod_ref[0, 1:2, :]
    return ((x * _rms(x)) * ng_ref[...] * (1.0 + scale) + shift).astype(BF16)


def _gqa_proj_kernel(x_ref, xc_ref, mod_ref, ng_ref, w_ref, qg_ref, kg_ref, cos_ref, su_ref, sd_ref,
                     q_ref, k_ref, vt_ref, g_ref, *, n_lat_tiles, n_q, n_kv, scale):
    h = _modulated_norm(x_ref, xc_ref, ng_ref, mod_ref, n_lat_tiles)
    cos, su, sd = cos_ref[...], su_ref[...], sd_ref[...]
    hd = GQA_HEAD_DIM
    qw, kw = n_q * hd, n_kv * hd
    q = _dot(h, w_ref[:, :qw])
    for i in range(n_q):
        qh = q[:, i * hd:(i + 1) * hd]
        qh = qh * _rms(qh) * qg_ref[...]
        q_ref[:, i * hd:(i + 1) * hd] = (_rope(qh, cos, su, sd, hd // 4) * scale).astype(BF16)
    k = _dot(h, w_ref[:, qw:qw + kw])
    for i in range(n_kv):
        kh = k[:, i * hd:(i + 1) * hd]
        kh = kh * _rms(kh) * kg_ref[...]
        k_ref[:, i * hd:(i + 1) * hd] = _rope(kh, cos, su, sd, hd // 4).astype(BF16)
    v = _dot(h, w_ref[:, qw + kw:qw + 2 * kw])
    for i in range(n_kv):
        vt_ref[i * hd:(i + 1) * hd, :] = v[:, i * hd:(i + 1) * hd].T.astype(BF16)
    g_ref[...] = _silu(_dot(h, w_ref[:, qw + 2 * kw:])).astype(BF16)


def _na_proj_kernel(x_ref, xc_ref, mod_ref, ng_ref, w_ref, q_ref, k_ref, vt_ref, g_ref, norm_ref,
                    *, n_lat_tiles, width, scale):
    h = _modulated_norm(x_ref, xc_ref, ng_ref, mod_ref, n_lat_tiles)
    head_of_col = lax.broadcasted_iota(jnp.int32, (width, LANES), 0) // NA_HEAD_DIM
    lane = lax.broadcasted_iota(jnp.int32, (width, LANES), 1)
    pick = jnp.where(head_of_col == lane, 1.0, 0.0).astype(BF16)
    half = norm_ref.shape[0] // 2
    q = _dot(h, w_ref[:, :width]) * scale
    q_ref[...] = q.astype(BF16)
    q_sq = jnp.max(_dot((q * q).astype(BF16), pick), axis=0, keepdims=True)
    norm_ref[:half, :] = jnp.broadcast_to(q_sq, (half, LANES))
    k = _dot(h, w_ref[:, width:2 * width])
    k_ref[...] = k.astype(BF16)
    k_sq = jnp.max(_dot((k * k).astype(BF16), pick), axis=0, keepdims=True)
    norm_ref[half:, :] = jnp.broadcast_to(k_sq, (half, LANES))
    v = _dot(h, w_ref[:, 2 * width:3 * width])
    for i in range(width // LANES):
        vt_ref[i * LANES:(i + 1) * LANES, :] = v[:, i * LANES:(i + 1) * LANES].T.astype(BF16)
    g_ref[...] = _silu(_dot(h, w_ref[:, 3 * width:])).astype(BF16)


def _mla_proj_kernel(x_ref, xc_ref, mod_ref, ng_ref, w_ref, qg_ref, kvg_ref, wuq_ref, wuk_ref, wuv_ref,
                     cos_ref, su_ref, sd_ref, q_ref, k_ref, vt_ref, g_ref, norm_ref,
                     *, n_lat_tiles, width, scale):
    h = _modulated_norm(x_ref, xc_ref, ng_ref, mod_ref, n_lat_tiles)
    ones = jnp.ones((LANES, LANES), BF16)

    def row_sq(v):
        return _dot((v * v).astype(BF16), ones)

    cos, su, sd = cos_ref[...], su_ref[...], sd_ref[...]
    o_kv = MLA_Q_LORA
    o_g = o_kv + MLA_KV_LORA
    o_kr = o_g + width
    c_q = _dot(h, w_ref[:, :o_kv])
    c_q = (c_q * _rms(c_q) * qg_ref[...]).astype(BF16)
    q = _dot(c_q, wuq_ref[...])
    q_sq = None
    for i in range(MLA_HEADS):
        a = i * MLA_QK_PAD
        q_nope = q[:, a:a + MLA_NOPE] * scale
        q_rope = _rope(q[:, a + MLA_NOPE:a + MLA_QK_PAD], cos, su, sd, MLA_ROPE // 4) * scale
        q_ref[:, a:a + MLA_NOPE] = q_nope.astype(BF16)
        q_ref[:, a + MLA_NOPE:a + MLA_QK_PAD] = q_rope.astype(BF16)
        sq = row_sq(q_nope) + row_sq(q_rope)
        q_sq = sq if q_sq is None else jnp.maximum(q_sq, sq)
    c_kv = _dot(h, w_ref[:, o_kv:o_g])
    c_kv = (c_kv * _rms(c_kv) * kvg_ref[...]).astype(BF16)
    k_r = _rope(_dot(h, w_ref[:, o_kr:]), cos, su, sd, MLA_ROPE // 4)
    k_nope = _dot(c_kv, wuk_ref[...])
    k_sq = None
    for i in range(MLA_HEADS):
        a = i * MLA_QK_PAD
        k_h = k_nope[:, i * MLA_NOPE:(i + 1) * MLA_NOPE]
        k_ref[:, a:a + MLA_NOPE] = k_h.astype(BF16)
        k_ref[:, a + MLA_NOPE:a + MLA_QK_PAD] = k_r.astype(BF16)
        sq = row_sq(k_h)
        k_sq = sq if k_sq is None else jnp.maximum(k_sq, sq)
    k_sq = k_sq + row_sq(k_r)
    half = norm_ref.shape[0] // 2
    norm_ref[:half, :] = jnp.broadcast_to(jnp.max(q_sq, axis=0, keepdims=True), (half, LANES))
    norm_ref[half:, :] = jnp.broadcast_to(jnp.max(k_sq, axis=0, keepdims=True), (half, LANES))
    v = _dot(c_kv, wuv_ref[...])
    for i in range(MLA_HEADS):
        vt_ref[i * MLA_V:(i + 1) * MLA_V, :] = v[:, i * MLA_V:(i + 1) * MLA_V].T.astype(BF16)
    g_ref[...] = _silu(_dot(h, w_ref[:, o_g:o_kr])).astype(BF16)


def _project(kern, stream, mod3, layer, norm_g, consts, tables, out_widths, n_lat, with_norms=False):
    x_lat, x_ctx, ctx_tile = stream
    b, _, d = x_lat.shape
    tm = ROW_TILE
    n_lat_tiles = n_lat // tm
    n = n_lat + tm
    ctx_row = b

    def mod_idx(bi, t):
        return (layer * MOD_ROWS + jnp.where(t < n_lat_tiles, bi, ctx_row), 0, 0)

    in_specs = [
        pl.BlockSpec((None, tm, d), lambda bi, t: (bi, jnp.minimum(t, n_lat_tiles - 1), 0)),
        pl.BlockSpec((None, tm, d), lambda bi, t: (bi, ctx_tile, 0)),
        pl.BlockSpec((1, 3, d), mod_idx),
        pl.BlockSpec((1, d), lambda bi, t: (0, 0)),
    ]
    in_specs += [pl.BlockSpec(a.shape, lambda bi, t: (0, 0)) for a in consts]
    in_specs += [pl.BlockSpec((tm, LANES), lambda bi, t: (t, 0)) for _ in tables]
    out_specs = [pl.BlockSpec((None, tm, w), lambda bi, t: (bi, t, 0)) for w in out_widths]
    out_shape = [jax.ShapeDtypeStruct((b, n, w), BF16) for w in out_widths]
    out_specs[2] = pl.BlockSpec((None, None, out_widths[2], tm), lambda bi, t: (bi, t, 0, 0))
    out_shape[2] = jax.ShapeDtypeStruct((b, n // tm, out_widths[2], tm), BF16)
    if with_norms:
        out_specs.append(pl.BlockSpec((None, None, 2 * MOD_ROWS, LANES), lambda bi, t: (bi, t, 0, 0)))
        out_shape.append(jax.ShapeDtypeStruct((b, n // tm, 2 * MOD_ROWS, LANES), F32))
    return pl.pallas_call(
        functools.partial(kern, n_lat_tiles=n_lat_tiles),
        grid=(b, n // tm),
        in_specs=in_specs,
        out_specs=out_specs,
        out_shape=out_shape,
        compiler_params=_params("parallel", "parallel"),
        name="proj_" + str(layer),
    )(x_lat, x_ctx, mod3, norm_g.reshape(1, d), *consts, *tables)


def _stack_heads(q_ref, group, dk):
    return jnp.concatenate([q_ref[:, g * dk:(g + 1) * dk] for g in range(group)], axis=0)


def _unstack_heads(o_t, o_ref, group):
    tq = o_ref.shape[0]
    for g in range(group):
        o_ref[:, g * LANES:(g + 1) * LANES] = o_t[:, g * tq:(g + 1) * tq].T.astype(BF16)


def _pv_t(vt_ref, first_slab, p, sum_rows=0):
    slab = vt_ref.shape[-1]
    acc = None
    for i in range(p.shape[0] // slab):
        vt = vt_ref[first_slab + i]
        if sum_rows:
            vt = jnp.concatenate([vt, jnp.ones((sum_rows, slab), vt.dtype)], axis=0)
        t = _dot(vt, p[i * slab:(i + 1) * slab, :])
        acc = t if acc is None else acc + t
    return acc


def _flash_kernel(q_ref, k_ref, vt_ref, o_ref, sa_ref, sb_ref, acc_ref, m_ref, *, group, dk, bk):
    slab = vt_ref.shape[-1]
    n_chunks = k_ref.shape[0] // bk
    q = _stack_heads(q_ref, group, dk)
    tiles = [slice(j * Q_TILE, (j + 1) * Q_TILE) for j in range(q.shape[0] // Q_TILE)]

    def scores(c, s_ref):
        k = k_ref[pl.ds(pl.multiple_of(c * bk, bk), bk), :]
        out = []
        for t in tiles:
            s = _dot_nt(k, q[t, :])
            s_ref[:, t] = s
            out.append(jnp.max(s, axis=0, keepdims=True))
        return out

    def update(s_ref, s_max, c, t):
        m = m_ref[:, t]
        m_new = jnp.maximum(m, s_max)
        alpha = jnp.exp2(m - m_new)
        p = jnp.exp2(s_ref[:, t] - m_new).astype(BF16)
        m_ref[:, t] = m_new
        acc_ref[:, t] = alpha * acc_ref[:, t] + _pv_t(vt_ref, c * (bk // slab), p, SUM_ROWS)

    def step(c, cur_ref, cur_max, nxt_ref):
        k = k_ref[pl.ds(pl.multiple_of((c + 1) * bk, bk), bk), :]
        nxt_max = []
        for j, t in enumerate(tiles):
            s = _dot_nt(k, q[t, :])
            nxt_ref[:, t] = s
            nxt_max.append(jnp.max(s, axis=0, keepdims=True))
            update(cur_ref, cur_max[j], c, t)
        return nxt_max

    m_ref[...] = jnp.full(m_ref.shape, MASK_VALUE, F32)
    acc_ref[...] = jnp.zeros(acc_ref.shape, F32)
    max_a = scores(0, sa_ref)

    def body(i, max_a):
        max_b = step(2 * i, sa_ref, max_a, sb_ref)
        return tuple(step(2 * i + 1, sb_ref, max_b, sa_ref))

    pairs = (n_chunks - 1) // 2
    max_a = lax.fori_loop(0, pairs, body, tuple(max_a))
    last = 2 * pairs
    if n_chunks - last == 2:
        max_b = step(last, sa_ref, max_a, sb_ref)
        last, s_ref, s_max = last + 1, sb_ref, max_b
    else:
        s_ref, s_max = sa_ref, max_a
    for j, t in enumerate(tiles):
        update(s_ref, s_max[j], last, t)
    _unstack_heads(acc_ref[:LANES, :] * (1.0 / acc_ref[LANES:LANES + 1, :]), o_ref, group)


def _flash_bounded_kernel(q_ref, k_ref, vt_ref, bound_ref, o_ref, acc_ref, *, group, dk, bk):
    slab = vt_ref.shape[-1]
    n_chunks = k_ref.shape[0] // bk
    q = _stack_heads(q_ref, group, dk)
    tiles = [slice(0, q.shape[0])]
    bound = bound_ref[:, :1]
    acc_ref[...] = jnp.zeros(acc_ref.shape, F32)

    def chunk(c):
        k = k_ref[pl.ds(pl.multiple_of(c * bk, bk), bk), :]
        for t in tiles:
            p = jnp.exp2(_dot_nt(k, q[t, :]) - bound)
            acc_ref[:LANES, t] += _pv_t(vt_ref, c * (bk // slab), p.astype(BF16))
            acc_ref[LANES:, t] += jnp.sum(p.reshape(bk // MOD_ROWS, MOD_ROWS, p.shape[1]), axis=0)

    def body(i, carry):
        for u in range(BOUNDED_UNROLL):
            chunk(BOUNDED_UNROLL * i + u)
        return carry

    lax.fori_loop(0, n_chunks // BOUNDED_UNROLL, body, 0)
    for c in range(n_chunks - n_chunks % BOUNDED_UNROLL, n_chunks):
        chunk(c)
    l = jnp.sum(acc_ref[LANES:, :], axis=0, keepdims=True)
    _unstack_heads(acc_ref[:LANES, :] * (1.0 / l), o_ref, group)


def _flash_attention(q, k, vt, *, group, dk, n_lat, layer, bound=None):
    b, n, _ = q.shape
    n_kv = k.shape[2] // dk
    n_slab, slab = vt.shape[1], vt.shape[3]
    tq = min(Q_COLS // group, n_lat)
    bk = max(c for c in KV_CHUNKS if n % c == 0)
    assert bk % slab == 0
    nq = group * tq
    in_specs = [
        pl.BlockSpec((None, tq, group * dk), lambda bi, h, t: (bi, t, h)),
        pl.BlockSpec((None, n, dk), lambda bi, h, t: (bi, 0, h)),
        pl.BlockSpec((None, n_slab, LANES, slab), lambda bi, h, t: (bi, 0, h, 0)),
    ]
    common = dict(
        grid=(b, n_kv, n_lat // tq),
        out_specs=pl.BlockSpec((None, tq, group * LANES), lambda bi, h, t: (bi, t, h)),
        out_shape=jax.ShapeDtypeStruct((b, n_lat, n_kv * group * LANES), BF16),
        compiler_params=_params("parallel", "parallel", "parallel"),
    )
    acc = pltpu.VMEM((LANES + SUM_ROWS, nq), F32)

    def general():
        s_buf = pltpu.VMEM((bk, nq + SKEW_LANES), F32)
        return pl.pallas_call(
            functools.partial(_flash_kernel, group=group, dk=dk, bk=bk),
            in_specs=in_specs,
            scratch_shapes=[s_buf, s_buf, acc, pltpu.VMEM((1, nq), F32)],
            name="flash_" + str(layer), **common,
        )(q, k, vt)

    if bound is None:
        return general()

    def bounded():
        return pl.pallas_call(
            functools.partial(_flash_bounded_kernel, group=group, dk=dk, bk=bk),
            in_specs=in_specs + [pl.BlockSpec((1, LANES), lambda bi, h, t: (0, 0))],
            scratch_shapes=[pltpu.VMEM((LANES + MOD_ROWS, nq), F32)],
            name="flash_bounded_" + str(layer), **common,
        )(q, k, vt, jnp.full((1, LANES), bound, F32))

    return lax.cond(bound <= SAFE_SCORE_BOUND, bounded, general)


def _ctx_attn_kernel(q_ref, k_ref, vt_ref, o_ref, *, group, dk):
    q = _stack_heads(q_ref, group, dk)
    s = _dot_nt(k_ref[...], q)
    p = jnp.exp2(s - jnp.max(s, axis=0, keepdims=True))
    l = jnp.sum(p, axis=0, keepdims=True)
    _unstack_heads(_dot(vt_ref[...], p.astype(BF16)) * (1.0 / l), o_ref, group)


def _ctx_attention(q, k, vt, *, group, dk, n_lat, layer):
    b, n, _ = q.shape
    n_ctx = n - n_lat
    n_kv = k.shape[2] // dk
    slab = vt.shape[3]
    assert slab == n_ctx
    t = n_lat // n_ctx
    return pl.pallas_call(
        functools.partial(_ctx_attn_kernel, group=group, dk=dk),
        grid=(b, n_kv),
        in_specs=[
            pl.BlockSpec((None, n_ctx, group * dk), lambda bi, h: (bi, t, h)),
            pl.BlockSpec((None, n_ctx, dk), lambda bi, h: (bi, t, h)),
            pl.BlockSpec((None, None, LANES, slab), lambda bi, h: (bi, t, h, 0)),
        ],
        out_specs=pl.BlockSpec((None, n_ctx, group * LANES), lambda bi, h: (bi, 0, h)),
        out_shape=jax.ShapeDtypeStruct((b, n_ctx, n_kv * group * LANES), BF16),
        compiler_params=_params("parallel", "parallel"),
        name="ctx_attn_" + str(layer),
    )(q, k, vt)


def _split_heads(q):
    lane = lax.broadcasted_iota(jnp.int32, (1, LANES), 1)
    zero = jnp.zeros_like(q)
    return jnp.concatenate(
        [jnp.where((lane >= j * NA_HEAD_DIM) & (lane < (j + 1) * NA_HEAD_DIM), q, zero)
         for j in range(LANES // NA_HEAD_DIM)], axis=0)


def _merge_heads(o_t, nq):
    parts = [o_t[j * NA_HEAD_DIM:(j + 1) * NA_HEAD_DIM, j * nq:(j + 1) * nq]
             for j in range(LANES // NA_HEAD_DIM)]
    return jnp.concatenate(parts, axis=0).T


def _na_ctx_kernel(q_ref, k_ref, vt_ref, o_ref):
    q2 = _split_heads(q_ref[...])
    s = _dot_nt(k_ref[...], q2)
    p = jnp.exp2(s - jnp.max(s, axis=0, keepdims=True))
    l = jnp.sum(p, axis=0, keepdims=True)
    o_t = _dot(vt_ref[...], p.astype(BF16)) * (1.0 / l)
    o_ref[...] = _merge_heads(o_t, q_ref.shape[0]).astype(BF16)


def _na_ctx_attention(q, k, vt, *, n_lat, layer):
    b, n, width = q.shape
    n_ctx = n - n_lat
    assert vt.shape[3] == n_ctx
    t = n_lat // n_ctx
    blk = pl.BlockSpec((None, n_ctx, LANES), lambda bi, h: (bi, t, h))
    return pl.pallas_call(
        _na_ctx_kernel,
        grid=(b, width // LANES),
        in_specs=[blk, blk, pl.BlockSpec((None, None, LANES, n_ctx), lambda bi, h: (bi, t, h, 0))],
        out_specs=pl.BlockSpec((None, n_ctx, LANES), lambda bi, h: (bi, 0, h)),
        out_shape=jax.ShapeDtypeStruct((b, n_ctx, width), BF16),
        compiler_params=_params("parallel", "parallel"),
        name="ctx_attn_" + str(layer),
    )(q, k, vt)


def _na_window_start(blk, rows, lib):
    lo = lib.minimum(lib.maximum(NA_BLOCK_ROWS * blk - NA_KH // 2, 0), rows - NA_KH)
    return lib.minimum(lo, rows - NA_WIN_ROWS)


def _na_kernel(q_ref, k_ref, vt_ref, bias_ref, o_ref, sl_ref, sc_ref, *, n_lat, n_ctx, rows):
    slab = vt_ref.shape[-1]
    nq = NA_BLOCK_ROWS * GRID_W
    n_blocks = rows // NA_BLOCK_ROWS
    nk = NA_WIN_ROWS * GRID_W
    kc = k_ref[n_lat:n_lat + n_ctx, :]
    n_here = q_ref.shape[0] // nq

    def scores(i):
        blk = pl.program_id(2) * NA_STEP_BLOCKS + i
        kind = jnp.where(blk == 0, 1, jnp.where(blk == n_blocks - 1, 2, 0))
        ws = _na_window_start(blk, rows, jnp)
        q2 = _split_heads(q_ref[i * nq:(i + 1) * nq, :])
        s_loc = _dot_nt(k_ref[pl.ds(pl.multiple_of(ws * GRID_W, slab), nk), :], q2) + bias_ref[kind]
        s_ctx = _dot_nt(kc, q2)
        sl_ref[i % 2, :, :s_loc.shape[1]] = s_loc
        sc_ref[i % 2, :, :s_ctx.shape[1]] = s_ctx
        return ws, jnp.maximum(jnp.max(s_loc, axis=0, keepdims=True), jnp.max(s_ctx, axis=0, keepdims=True))

    nxt = scores(0)
    for i in range(n_here):
        ws, m = nxt
        if i + 1 < n_here:
            nxt = scores(i + 1)
        parts = []
        for j in range(LANES // NA_HEAD_DIM):
            t = slice(j * nq, (j + 1) * nq)
            p_loc = jnp.exp2(sl_ref[i % 2, :, t] - m[:, t]).astype(BF16)
            p_ctx = jnp.exp2(sc_ref[i % 2, :, t] - m[:, t]).astype(BF16)
            o_t = _pv_t(vt_ref, ws * GRID_W // slab, p_loc, SUM_ROWS)
            o_t = o_t + _pv_t(vt_ref, n_lat // slab, p_ctx, SUM_ROWS)
            parts.append(o_t[j * NA_HEAD_DIM:(j + 1) * NA_HEAD_DIM, :] * (1.0 / o_t[LANES:LANES + 1, :]))
        o_ref[i * nq:(i + 1) * nq, :] = jnp.concatenate(parts, axis=0).T.astype(BF16)


def _na_bounded_kernel(q_ref, k_ref, vt_ref, bias_ref, bound_ref, o_ref, *, n_lat, n_ctx, rows):
    slab = vt_ref.shape[-1]
    nq = NA_BLOCK_ROWS * GRID_W
    n_blocks = rows // NA_BLOCK_ROWS
    nk = NA_WIN_ROWS * GRID_W
    kc = k_ref[n_lat:n_lat + n_ctx, :]
    bound = bound_ref[:, :1]
    for i in range(q_ref.shape[0] // nq):
        blk = pl.program_id(2) * NA_STEP_BLOCKS + i
        kind = jnp.where(blk == 0, 1, jnp.where(blk == n_blocks - 1, 2, 0))
        ws = _na_window_start(blk, rows, jnp)
        q2 = _split_heads(q_ref[i * nq:(i + 1) * nq, :])
        kw = k_ref[pl.ds(pl.multiple_of(ws * GRID_W, slab), nk), :]
        p_loc = jnp.exp2(_dot_nt(kw, q2) + (bias_ref[kind] - bound)).astype(BF16)
        p_ctx = jnp.exp2(_dot_nt(kc, q2) - bound).astype(BF16)
        o_t = _pv_t(vt_ref, ws * GRID_W // slab, p_loc, SUM_ROWS)
        o_t = o_t + _pv_t(vt_ref, n_lat // slab, p_ctx, SUM_ROWS)
        o_t = o_t[:LANES, :] * (1.0 / o_t[LANES:LANES + 1, :])
        o_ref[i * nq:(i + 1) * nq, :] = _merge_heads(o_t, nq).astype(BF16)


def _na_bias(rpb, rows):
    h, n_dr, n_dc = rpb.shape
    n_blocks = rows // NA_BLOCK_ROWS
    lead = GRID_W - NA_KW
    wide = 2 * GRID_W
    w = jnp.pad(rpb * LOG2E, ((0, 0), (0, 0), (lead, wide - lead - n_dc)), constant_values=MASK_VALUE)
    flat = jnp.broadcast_to(w[:, :, None, :], (h, n_dr, GRID_W, wide)).reshape(h, n_dr, GRID_W * wide)
    skew = flat[:, :, GRID_W - 1:GRID_W - 1 + GRID_W * (wide - 1)].reshape(h, n_dr, GRID_W, wide - 1)
    qc = jnp.arange(GRID_W)
    cs = jnp.clip(qc - NA_KW // 2, 0, GRID_W - NA_KW)
    col_ok = (qc[None, :] >= cs[:, None]) & (qc[None, :] < cs[:, None] + NA_KW)
    c_t = jnp.swapaxes(jnp.where(col_ok, skew[..., :GRID_W], MASK_VALUE), -1, -2)
    masked = jnp.full((h, GRID_W, GRID_W), MASK_VALUE, F32)
    kinds = []
    for blk in (1, 0, n_blocks - 1):
        ws = _na_window_start(blk, rows, np)
        strips = []
        for j in range(NA_WIN_ROWS):
            kr = ws + j
            blocks = []
            for i in range(NA_BLOCK_ROWS):
                r = NA_BLOCK_ROWS * blk + i
                rs = min(max(r - NA_KH // 2, 0), rows - NA_KH)
                blocks.append(c_t[:, kr - r + NA_KH - 1] if rs <= kr < rs + NA_KH else masked)
            strips.append(jnp.concatenate(blocks, axis=-1))
        kinds.append(jnp.concatenate(strips, axis=1))
    bias = jnp.stack(kinds, axis=1)
    sub = LANES // NA_HEAD_DIM
    bias = bias.reshape(h // sub, sub, len(kinds), NA_WIN_ROWS * GRID_W, NA_BLOCK_ROWS * GRID_W)
    return jnp.concatenate([bias[:, j] for j in range(sub)], axis=-1)


def _na_attention(q, k, vt, bias, bound, *, n_lat, layer):
    b, n, width = q.shape
    n_ctx = n - n_lat
    rows = n_lat // GRID_W
    n_slab, slab = vt.shape[1], vt.shape[3]
    tq = NA_STEP_BLOCKS * NA_BLOCK_ROWS * GRID_W
    in_specs = [
        pl.BlockSpec((None, tq, LANES), lambda hp, bi, rb: (bi, rb, hp)),
        pl.BlockSpec((None, n, LANES), lambda hp, bi, rb: (bi, 0, hp)),
        pl.BlockSpec((None, n_slab, LANES, slab), lambda hp, bi, rb: (bi, 0, hp, 0)),
        pl.BlockSpec((None,) + bias.shape[1:], lambda hp, bi, rb: (hp, 0, 0, 0)),
    ]
    common = dict(
        grid=(width // LANES, b, n_lat // tq),
        out_specs=pl.BlockSpec((None, tq, LANES), lambda hp, bi, rb: (bi, rb, hp)),
        out_shape=jax.ShapeDtypeStruct((b, n_lat, width), BF16),
        compiler_params=_params("parallel", "parallel", "arbitrary"),
    )

    def general():
        return pl.pallas_call(
            functools.partial(_na_kernel, n_lat=n_lat, n_ctx=n_ctx, rows=rows),
            in_specs=in_specs,
            scratch_shapes=[pltpu.VMEM((2, bias.shape[2], bias.shape[3] + SKEW_LANES), F32),
                            pltpu.VMEM((2, n_ctx, bias.shape[3] + SKEW_LANES), F32)],
            name="na_attn_" + str(layer), **common,
        )(q, k, vt, bias)

    def bounded():
        return pl.pallas_call(
            functools.partial(_na_bounded_kernel, n_lat=n_lat, n_ctx=n_ctx, rows=rows),
            in_specs=in_specs + [pl.BlockSpec((1, LANES), lambda hp, bi, rb: (0, 0))],
            name="na_attn_bounded_" + str(layer), **common,
        )(q, k, vt, bias, jnp.full((1, LANES), bound, F32))

    return lax.cond(bound <= SAFE_SCORE_BOUND, bounded, general)


def _out_proj_kernel(o_ref, oc_ref, g_ref, w_ref, x_ref, *refs, n_lat_tiles):
    mod_ref, xo_ref = refs[-2:]
    is_ctx = pl.program_id(1) >= n_lat_tiles
    o = jnp.where(is_ctx, oc_ref[...], o_ref[...])
    x = x_ref[...] if len(refs) == 2 else jnp.where(is_ctx, refs[0][...], x_ref[...])
    a = (o.astype(F32) * g_ref[...].astype(F32)).astype(BF16)
    xo_ref[...] = x + mod_ref[0, 2:3, :] * _dot(a, w_ref[...])


def _out_proj_final_kernel(o_ref, g_ref, w_ref, x_ref, mod_ref, fg_ref, y_ref):
    a = (o_ref[...].astype(F32) * g_ref[...].astype(F32)).astype(BF16)
    xn = x_ref[...] + mod_ref[0, 2:3, :] * _dot(a, w_ref[...])
    y_ref[...] = xn * _rms(xn) * fg_ref[...]


def _out_proj(o, o_ctx, gs, w_out, stream, mod3, layer, n_lat, final_g=None):
    x_lat, x_ctx, ctx_tile = stream
    b, _, d = x_lat.shape
    tm = ROW_TILE
    n_lat_tiles = n_lat // tm
    n = n_lat + tm
    ctx_row = b

    def mod_idx(bi, t):
        return (layer * MOD_ROWS + jnp.where(t < n_lat_tiles, bi, ctx_row), 0, 0)

    row = lambda bi, t: (bi, t, 0)
    lat_row = lambda bi, t: (bi, jnp.minimum(t, n_lat_tiles - 1), 0)
    w = o.shape[2]
    head = [pl.BlockSpec((None, tm, w), lat_row)]
    mid = [pl.BlockSpec((None, tm, w), row), pl.BlockSpec(w_out.shape, lambda bi, t: (0, 0)),
           pl.BlockSpec((None, tm, d), lat_row)]
    if final_g is None:
        combined = x_lat is x_ctx
        specs = head + [pl.BlockSpec((None, tm, w), lambda bi, t: (bi, 0, 0))] + mid[:2]
        if combined:
            specs, xs = specs + [pl.BlockSpec((None, tm, d), row)], [x_lat]
        else:
            specs += [mid[2], pl.BlockSpec((None, tm, d), lambda bi, t: (bi, ctx_tile, 0))]
            xs = [x_lat, x_ctx]
        return pl.pallas_call(
            functools.partial(_out_proj_kernel, n_lat_tiles=n_lat_tiles),
            grid=(b, n // tm),
            in_specs=specs + [pl.BlockSpec((1, 3, d), mod_idx)],
            out_specs=pl.BlockSpec((None, tm, d), row),
            out_shape=jax.ShapeDtypeStruct((b, n, d), F32),
            input_output_aliases={4: 0} if combined else {},
            compiler_params=_params("parallel", "parallel"),
            name="out_proj_" + str(layer),
        )(o, o_ctx, gs, w_out, *xs, mod3)
    return pl.pallas_call(
        _out_proj_final_kernel,
        grid=(b, n_lat_tiles),
        in_specs=head + mid + [pl.BlockSpec((1, 3, d), mod_idx), pl.BlockSpec((1, d), lambda bi, t: (0, 0))],
        out_specs=pl.BlockSpec((None, tm, d), row),
        out_shape=jax.ShapeDtypeStruct((b, n_lat, d), F32),
        compiler_params=_params("parallel", "parallel"),
        name="out_proj_final",
    )(o, gs, w_out, x_lat, mod3, final_g.reshape(1, d))


def _rope_tables(n_lat, n_ctx, rot_dim):
    n_freq = rot_dim // 4
    inv = ROPE_THETA ** (-jnp.arange(n_freq, dtype=F32) / n_freq)
    t = jnp.arange(n_lat)
    ang_r = (t // GRID_W).astype(F32)[:, None] * inv
    ang_c = (t % GRID_W).astype(F32)[:, None] * inv
    ang = jnp.concatenate([ang_r, ang_r, ang_c, ang_c], axis=-1)
    cos, sin = jnp.cos(ang), jnp.sin(ang)
    first = (jnp.arange(rot_dim) % (2 * n_freq)) < n_freq
    sin_up = jnp.where(first, -sin, 0.0)
    sin_dn = jnp.where(first, 0.0, sin)
    pad = LANES - rot_dim

    def finish(tab, fill):
        tab = jnp.pad(tab, ((0, 0), (0, pad)), constant_values=fill)
        return jnp.pad(tab, ((0, n_ctx), (0, 0)), constant_values=fill)

    return finish(cos, 1.0), finish(sin_up, 0.0), finish(sin_dn, 0.0)


def _mla_weights(w_in, w_uq, w_ukv):
    o_kv = MLA_Q_LORA
    o_kr = o_kv + MLA_KV_LORA
    o_g = o_kr + MLA_ROPE
    k_r = jnp.pad(w_in[:, o_kr:o_g], ((0, 0), (0, LANES - MLA_ROPE)))
    w_perm = jnp.concatenate([w_in[:, :o_kr], w_in[:, o_g:], k_r], axis=1).astype(BF16)
    uq = w_uq.reshape(MLA_Q_LORA, MLA_HEADS, MLA_NOPE + MLA_ROPE)
    uq = jnp.pad(uq, ((0, 0), (0, 0), (0, MLA_QK_PAD - MLA_NOPE - MLA_ROPE)))
    uq = uq.reshape(MLA_Q_LORA, MLA_HEADS * MLA_QK_PAD).astype(BF16)
    ukv = w_ukv.reshape(MLA_KV_LORA, MLA_HEADS, MLA_NOPE + MLA_V).astype(BF16)
    uk = ukv[:, :, :MLA_NOPE].reshape(MLA_KV_LORA, MLA_HEADS * MLA_NOPE)
    uv = ukv[:, :, MLA_NOPE:].reshape(MLA_KV_LORA, MLA_HEADS * MLA_V)
    return w_perm, uq, uk, uv


def kernel(x, c, ctx, c_ctx, mod_w, mod_b, norm_g, final_g, ga_w_in, ga_q_g, ga_k_g, ga_w_out, na_w_in, na_rpb, na_w_out, mla_w_in, mla_q_g, mla_kv_g, mla_w_uq, mla_w_ukv, mla_w_out):
    b, n_lat, d = x.shape
    n_ctx = ctx.shape[1]
    depth = mod_w.shape[0]
    assert n_lat % ROW_TILE == 0 and n_ctx == ROW_TILE and n_lat % n_ctx == 0
    assert b < MOD_ROWS and n_lat % (NA_STEP_BLOCKS * NA_BLOCK_ROWS * GRID_W) == 0
    assert n_lat // GRID_W >= NA_WIN_ROWS + NA_BLOCK_ROWS

    stream = (x, ctx, 0)
    cc = jnp.concatenate([c, c_ctx[None, :], jnp.zeros((MOD_ROWS - b - 1, d), F32)], axis=0)
    mod3 = _modulation(cc, mod_w, mod_b).reshape(depth * MOD_ROWS, 3, d)

    tab_a = _rope_tables(n_lat, n_ctx, GQA_HEAD_DIM)
    tab_m = _rope_tables(n_lat, n_ctx, MLA_ROPE)

    out = None
    for i in range(depth):
        kind, j = i % N_MIXERS, i // N_MIXERS
        need_ctx = i < depth - 1
        if kind == 0:
            n_q = d // GQA_HEAD_DIM
            n_kv = n_q // GQA_GROUP
            kern = functools.partial(_gqa_proj_kernel, n_q=n_q, n_kv=n_kv, scale=GQA_HEAD_DIM ** -0.5 * LOG2E)
            consts = [ga_w_in[j].astype(BF16), ga_q_g[j].reshape(1, -1), ga_k_g[j].reshape(1, -1)]
            kvw = n_kv * GQA_HEAD_DIM
            q, k, vt, gs = _project(kern, stream, mod3, i, norm_g[i], consts, tab_a, [d, kvw, kvw, d], n_lat)
            bound = (BOUND_SLACK * GQA_HEAD_DIM ** 0.5 * LOG2E
                     * jnp.max(jnp.abs(ga_q_g[j])) * jnp.max(jnp.abs(ga_k_g[j])))
            o = _flash_attention(q, k, vt, group=GQA_GROUP, dk=GQA_HEAD_DIM, n_lat=n_lat, layer=i, bound=bound)
            if need_ctx:
                oc = _ctx_attention(q, k, vt, group=GQA_GROUP, dk=GQA_HEAD_DIM, n_lat=n_lat, layer=i)
            w_out = ga_w_out[j]
        elif kind == 1:
            kern = functools.partial(_na_proj_kernel, width=d, scale=NA_HEAD_DIM ** -0.5 * LOG2E)
            q, k, vt, gs, norms = _project(kern, stream, mod3, i, norm_g[i], [na_w_in[j].astype(BF16)], (),
                                           [d, d, d, d], n_lat, True)
            bound = (BOUND_SLACK * jnp.sqrt(jnp.max(norms[:, :, 0, :]) * jnp.max(norms[:, :, MOD_ROWS, :]))
                     + LOG2E * jnp.max(jnp.abs(na_rpb[j])))
            o = _na_attention(q, k, vt, _na_bias(na_rpb[j], n_lat // GRID_W), bound, n_lat=n_lat, layer=i)
            if need_ctx:
                oc = _na_ctx_attention(q, k, vt, n_lat=n_lat, layer=i)
            w_out = na_w_out[j]
        else:
            kern = functools.partial(_mla_proj_kernel, width=d, scale=(MLA_NOPE + MLA_ROPE) ** -0.5 * LOG2E)
            w_perm, uq, uk, uv = _mla_weights(mla_w_in[j], mla_w_uq[j], mla_w_ukv[j])
            consts = [w_perm, mla_q_g[j].reshape(1, -1), mla_kv_g[j].reshape(1, -1), uq, uk, uv]
            widths = [MLA_HEADS * MLA_QK_PAD, MLA_HEADS * MLA_QK_PAD, MLA_HEADS * MLA_V, d]
            q, k, vt, gs, norms = _project(kern, stream, mod3, i, norm_g[i], consts, tab_m, widths, n_lat, True)
            bound = BOUND_SLACK * jnp.sqrt(jnp.max(norms[:, :, 0, 0]) * jnp.max(norms[:, :, MOD_ROWS, 0]))
            o = _flash_attention(q, k, vt, group=1, dk=MLA_QK_PAD, n_lat=n_lat, layer=i, bound=bound)
            if need_ctx:
                oc = _ctx_attention(q, k, vt, group=1, dk=MLA_QK_PAD, n_lat=n_lat, layer=i)
            w_out = mla_w_out[j]
        if need_ctx:
            xs = _out_proj(o, oc, gs, w_out.astype(BF16), stream, mod3, i, n_lat)
            stream = (xs, xs, n_lat // ROW_TILE)
        else:
            out = _out_proj(o, None, gs, w_out.astype(BF16), stream, mod3, i, n_lat, final_g=final_g)
    return out
```

```python
import functools

import jax
import numpy as np
import jax.numpy as jnp
from jax import lax
from jax.experimental import pallas as pl
from jax.experimental.pallas import tpu as pltpu

F32 = jnp.float32
BF16 = jnp.bfloat16

NORM_EPS = 1e-6
ROPE_THETA = 10000.0
GRID_W = 64
N_MIXERS = 3
LANES = 128
MOD_ROWS = 8
MASK_VALUE = -1e30
LOG2E = 1.4426950408889634

GQA_HEAD_DIM = 128
GQA_GROUP = 4
NA_HEAD_DIM = 64
NA_KH = 8
NA_KW = 16
NA_BLOCK_ROWS = 4
NA_WIN_ROWS = 12
NA_STEP_BLOCKS = 4
MLA_HEADS = 8
MLA_Q_LORA = 512
MLA_KV_LORA = 256
MLA_NOPE = 128
MLA_ROPE = 64
MLA_V = 128
MLA_QK_PAD = 256

ROW_TILE = 256
Q_COLS = 4096
Q_TILE = 256
SUM_ROWS = 16
SKEW_LANES = 128
SAFE_SCORE_BOUND = 50.0
BOUNDED_UNROLL = 5
BOUND_SLACK = 1.02
KV_CHUNKS = (768, 512, 256)
VMEM_LIMIT = 48 * 1024 * 1024


def _params(*sem):
    return pltpu.CompilerParams(dimension_semantics=sem, vmem_limit_bytes=VMEM_LIMIT)


def _silu(v):
    return v * (1.0 / (1.0 + jnp.exp(-v)))


def _rms(v):
    return lax.rsqrt(jnp.mean(v * v, axis=-1, keepdims=True) + NORM_EPS)


def _rope(v, cos, sin_up, sin_dn, half):
    w = v.shape[-1]
    return v * cos + pltpu.roll(v, w - half, 1) * sin_up + pltpu.roll(v, half, 1) * sin_dn


def _dot(a, b):
    return jnp.dot(a, b, preferred_element_type=F32)


def _dot_nt(a, b):
    return lax.dot_general(a, b, (((1,), (1,)), ((), ())), preferred_element_type=F32)


def _mod_kernel(c_ref, w_ref, b_ref, o_ref):
    a = _silu(c_ref[...])
    w = w_ref[0]
    a_hi = a.astype(BF16)
    a_lo = (a - a_hi.astype(F32)).astype(BF16)
    w_hi = w.astype(BF16)
    w_lo = (w - w_hi.astype(F32)).astype(BF16)
    o_ref[0] = _dot(a_hi, w_hi) + _dot(a_lo, w_hi) + _dot(a_hi, w_lo) + b_ref[0]


def _modulation(cc, mod_w, mod_b):
    depth, d, n3 = mod_w.shape
    tn = 1024
    return pl.pallas_call(
        _mod_kernel,
        grid=(depth, n3 // tn),
        in_specs=[
            pl.BlockSpec((MOD_ROWS, d), lambda l, j: (0, 0)),
            pl.BlockSpec((1, d, tn), lambda l, j: (l, 0, j)),
            pl.BlockSpec((1, 1, tn), lambda l, j: (l, 0, j)),
        ],
        out_specs=pl.BlockSpec((1, MOD_ROWS, tn), lambda l, j: (l, 0, j)),
        out_shape=jax.ShapeDtypeStruct((depth, MOD_ROWS, n3), F32),
        compiler_params=_params("parallel", "parallel"),
        name="adaln_modulation",
    )(cc, mod_w, mod_b.reshape(depth, 1, n3))


def _modulated_norm(x_ref, xc_ref, ng_ref, mod_ref, n_lat_tiles):
    x = jnp.where(pl.program_id(1) >= n_lat_tiles, xc_ref[...], x_ref[...])
    shift = mod_ref[0, 0:1, :]
    scale = mod_ref[0, 1:2, :]
    return ((x * _rms(x)) * ng_ref[...] * (1.0 + scale) + shift).astype(BF16)


def _gqa_proj_kernel(x_ref, xc_ref, mod_ref, ng_ref, w_ref, qg_ref, kg_ref, cos_ref, su_ref, sd_ref,
                     q_ref, k_ref, vt_ref, g_ref, *, n_lat_tiles, n_q, n_kv, scale):
    h = _modulated_norm(x_ref, xc_ref, ng_ref, mod_ref, n_lat_tiles)
    cos, su, sd = cos_ref[...], su_ref[...], sd_ref[...]
    hd = GQA_HEAD_DIM
    qw, kw = n_q * hd, n_kv * hd
    q = _dot(h, w_ref[:, :qw])
    for i in range(n_q):
        qh = q[:, i * hd:(i + 1) * hd]
        qh = qh * _rms(qh) * qg_ref[...]
        q_ref[:, i * hd:(i + 1) * hd] = (_rope(qh, cos, su, sd, hd // 4) * scale).astype(BF16)
    k = _dot(h, w_ref[:, qw:qw + kw])
    for i in range(n_kv):
        kh = k[:, i * hd:(i + 1) * hd]
        kh = kh * _rms(kh) * kg_ref[...]
        k_ref[:, i * hd:(i + 1) * hd] = _rope(kh, cos, su, sd, hd // 4).astype(BF16)
    v = _dot(h, w_ref[:, qw + kw:qw + 2 * kw])
    for i in range(n_kv):
        vt_ref[i * hd:(i + 1) * hd, :] = v[:, i * hd:(i + 1) * hd].T.astype(BF16)
    g_ref[...] = _silu(_dot(h, w_ref[:, qw + 2 * kw:])).astype(BF16)


def _na_proj_kernel(x_ref, xc_ref, mod_ref, ng_ref, w_ref, q_ref, k_ref, vt_ref, g_ref, norm_ref,
                    *, n_lat_tiles, width, scale):
    h = _modulated_norm(x_ref, xc_ref, ng_ref, mod_ref, n_lat_tiles)
    head_of_col = lax.broadcasted_iota(jnp.int32, (width, LANES), 0) // NA_HEAD_DIM
    lane = lax.broadcasted_iota(jnp.int32, (width, LANES), 1)
    pick = jnp.where(head_of_col == lane, 1.0, 0.0).astype(BF16)
    half = norm_ref.shape[0] // 2
    q = _dot(h, w_ref[:, :width]) * scale
    q_ref[...] = q.astype(BF16)
    q_sq = jnp.max(_dot((q * q).astype(BF16), pick), axis=0, keepdims=True)
    norm_ref[:half, :] = jnp.broadcast_to(q_sq, (half, LANES))
    k = _dot(h, w_ref[:, width:2 * width])
    k_ref[...] = k.astype(BF16)
    k_sq = jnp.max(_dot((k * k).astype(BF16), pick), axis=0, keepdims=True)
    norm_ref[half:, :] = jnp.broadcast_to(k_sq, (half, LANES))
    v = _dot(h, w_ref[:, 2 * width:3 * width])
    for i in range(width // LANES):
        vt_ref[i * LANES:(i + 1) * LANES, :] = v[:, i * LANES:(i + 1) * LANES].T.astype(BF16)
    g_ref[...] = _silu(_dot(h, w_ref[:, 3 * width:])).astype(BF16)


def _mla_proj_kernel(x_ref, xc_ref, mod_ref, ng_ref, w_ref, qg_ref, kvg_ref, wuq_ref, wuk_ref, wuv_ref,
                     cos_ref, su_ref, sd_ref, q_ref, k_ref, vt_ref, g_ref, norm_ref,
                     *, n_lat_tiles, width, scale):
    h = _modulated_norm(x_ref, xc_ref, ng_ref, mod_ref, n_lat_tiles)
    ones = jnp.ones((LANES, LANES), BF16)

    def row_sq(v):
        return _dot((v * v).astype(BF16), ones)

    cos, su, sd = cos_ref[...], su_ref[...], sd_ref[...]
    o_kv = MLA_Q_LORA
    o_g = o_kv + MLA_KV_LORA
    o_kr = o_g + width
    c_q = _dot(h, w_ref[:, :o_kv])
    c_q = (c_q * _rms(c_q) * qg_ref[...]).astype(BF16)
    q = _dot(c_q, wuq_ref[...])
    q_sq = None
    for i in range(MLA_HEADS):
        a = i * MLA_QK_PAD
        q_nope = q[:, a:a + MLA_NOPE] * scale
        q_rope = _rope(q[:, a + MLA_NOPE:a + MLA_QK_PAD], cos, su, sd, MLA_ROPE // 4) * scale
        q_ref[:, a:a + MLA_NOPE] = q_nope.astype(BF16)
        q_ref[:, a + MLA_NOPE:a + MLA_QK_PAD] = q_rope.astype(BF16)
        sq = row_sq(q_nope) + row_sq(q_rope)
        q_sq = sq if q_sq is None else jnp.maximum(q_sq, sq)
    c_kv = _dot(h, w_ref[:, o_kv:o_g])
    c_kv = (c_kv * _rms(c_kv) * kvg_ref[...]).astype(BF16)
    k_r = _rope(_dot(h, w_ref[:, o_kr:]), cos, su, sd, MLA_ROPE // 4)
    k_nope = _dot(c_kv, wuk_ref[...])
    k_sq = None
    for i in range(MLA_HEADS):
        a = i * MLA_QK_PAD
        k_h = k_nope[:, i * MLA_NOPE:(i + 1) * MLA_NOPE]
        k_ref[:, a:a + MLA_NOPE] = k_h.astype(BF16)
        k_ref[:, a + MLA_NOPE:a + MLA_QK_PAD] = k_r.astype(BF16)
        sq = row_sq(k_h)
        k_sq = sq if k_sq is None else jnp.maximum(k_sq, sq)
    k_sq = k_sq + row_sq(k_r)
    half = norm_ref.shape[0] // 2
    norm_ref[:half, :] = jnp.broadcast_to(jnp.max(q_sq, axis=0, keepdims=True), (half, LANES))
    norm_ref[half:, :] = jnp.broadcast_to(jnp.max(k_sq, axis=0, keepdims=True), (half, LANES))
    v = _dot(c_kv, wuv_ref[...])
    for i in range(MLA_HEADS):
        vt_ref[i * MLA_V:(i + 1) * MLA_V, :] = v[:, i * MLA_V:(i + 1) * MLA_V].T.astype(BF16)
    g_ref[...] = _silu(_dot(h, w_ref[:, o_g:o_kr])).astype(BF16)


class _Loaded:
    def __init__(self, value):
        self.value = value

    def __getitem__(self, _):
        return self.value


def _fused_kernel(*refs, proj, split, n_in, n_lat_tiles):
    ins, outs = refs[:n_in], refs[n_in:]
    o_ref, oc_ref, g_ref, w_ref, x_ref = ins[:5]
    rest = ins[5:]
    is_ctx = pl.program_id(1) >= n_lat_tiles
    x = x_ref[...]
    if split:
        x = jnp.where(is_ctx, rest[0][...], x)
        rest = rest[1:]
    o = jnp.where(is_ctx, oc_ref[...], o_ref[...])
    a = (o.astype(F32) * g_ref[...].astype(F32)).astype(BF16)
    x = x + rest[0][0, 2:3, :] * _dot(a, w_ref[...])
    outs[0][...] = x
    x = _Loaded(x)
    proj(x, x, *rest[1:], *outs[1:], n_lat_tiles=n_lat_tiles)


def _project(kern, stream, mod3, layer, norm_g, consts, tables, out_widths, n_lat, with_norms=False, prev=None):
    x_lat, x_ctx, ctx_tile = stream
    b, _, d = x_lat.shape
    tm = ROW_TILE
    n_lat_tiles = n_lat // tm
    n = n_lat + tm
    ctx_row = b
    row = lambda bi, t: (bi, t, 0)
    lat_row = lambda bi, t: (bi, jnp.minimum(t, n_lat_tiles - 1), 0)
    const = lambda bi, t: (0, 0)

    def mod_spec(which):
        return pl.BlockSpec((1, 3, d), lambda bi, t: (which * MOD_ROWS + jnp.where(t < n_lat_tiles, bi, ctx_row), 0, 0))

    x_specs = [pl.BlockSpec((None, tm, d), lat_row), pl.BlockSpec((None, tm, d), lambda bi, t: (bi, ctx_tile, 0))]
    in_specs = [mod_spec(layer), pl.BlockSpec((1, d), const)]
    in_specs += [pl.BlockSpec(a.shape, const) for a in consts]
    in_specs += [pl.BlockSpec((tm, LANES), lambda bi, t: (t, 0)) for _ in tables]
    args = [mod3, norm_g.reshape(1, d), *consts, *tables]
    out_specs = [pl.BlockSpec((None, tm, w), row) for w in out_widths]
    out_shape = [jax.ShapeDtypeStruct((b, n, w), BF16) for w in out_widths]
    out_specs[2] = pl.BlockSpec((None, None, out_widths[2], tm), lambda bi, t: (bi, t, 0, 0))
    out_shape[2] = jax.ShapeDtypeStruct((b, n // tm, out_widths[2], tm), BF16)
    if with_norms:
        out_specs.append(pl.BlockSpec((None, None, 2 * MOD_ROWS, LANES), lambda bi, t: (bi, t, 0, 0)))
        out_shape.append(jax.ShapeDtypeStruct((b, n // tm, 2 * MOD_ROWS, LANES), F32))
    call = dict(grid=(b, n // tm), compiler_params=_params("parallel", "parallel"), name="proj_" + str(layer))
    if prev is None:
        return pl.pallas_call(
            functools.partial(kern, n_lat_tiles=n_lat_tiles),
            in_specs=x_specs + in_specs, out_specs=out_specs, out_shape=out_shape, **call,
        )(x_lat, x_ctx, *args)
    o, o_ctx, gs, w_out, prev_layer = prev
    combined = x_lat is x_ctx
    wo = o.shape[2]
    pre_specs = [pl.BlockSpec((None, tm, wo), lat_row), pl.BlockSpec((None, tm, wo), lambda bi, t: (bi, 0, 0)),
                 pl.BlockSpec((None, tm, wo), row), pl.BlockSpec(w_out.shape, const)]
    if combined:
        x_specs, xs = [pl.BlockSpec((None, tm, d), row)], [x_lat]
    else:
        xs = [x_lat, x_ctx]
    in_specs = pre_specs + x_specs + [mod_spec(prev_layer)] + in_specs
    return pl.pallas_call(
        functools.partial(_fused_kernel, proj=kern, split=not combined, n_in=len(in_specs), n_lat_tiles=n_lat_tiles),
        in_specs=in_specs,
        out_specs=[pl.BlockSpec((None, tm, d), row)] + out_specs,
        out_shape=[jax.ShapeDtypeStruct((b, n, d), F32)] + out_shape,
        input_output_aliases={4: 0} if combined else {},
        **call,
    )(o, o_ctx, gs, w_out, *xs, mod3, *args)


def _stack_heads(q_ref, group, dk):
    return jnp.concatenate([q_ref[:, g * dk:(g + 1) * dk] for g in range(group)], axis=0)


def _unstack_heads(o_t, o_ref, group):
    tq = o_ref.shape[0]
    for g in range(group):
        o_ref[:, g * LANES:(g + 1) * LANES] = o_t[:, g * tq:(g + 1) * tq].T.astype(BF16)


def _pv_t(vt_ref, first_slab, p, sum_rows=0):
    slab = vt_ref.shape[-1]
    acc = None
    for i in range(p.shape[0] // slab):
        vt = vt_ref[first_slab + i]
        if sum_rows:
            vt = jnp.concatenate([vt, jnp.ones((sum_rows, slab), vt.dtype)], axis=0)
        t = _dot(vt, p[i * slab:(i + 1) * slab, :])
        acc = t if acc is None else acc + t
    return acc


def _flash_kernel(q_ref, k_ref, vt_ref, o_ref, sa_ref, sb_ref, acc_ref, m_ref, *, group, dk, bk):
    slab = vt_ref.shape[-1]
    n_chunks = k_ref.shape[0] // bk
    q = _stack_heads(q_ref, group, dk)
    tiles = [slice(j * Q_TILE, (j + 1) * Q_TILE) for j in range(q.shape[0] // Q_TILE)]

    def scores(c, s_ref):
        k = k_ref[pl.ds(pl.multiple_of(c * bk, bk), bk), :]
        out = []
        for t in tiles:
            s = _dot_nt(k, q[t, :])
            s_ref[:, t] = s
            out.append(jnp.max(s, axis=0, keepdims=True))
        return out

    def update(s_ref, s_max, c, t):
        m = m_ref[:, t]
        m_new = jnp.maximum(m, s_max)
        alpha = jnp.exp2(m - m_new)
        p = jnp.exp2(s_ref[:, t] - m_new).astype(BF16)
        m_ref[:, t] = m_new
        acc_ref[:, t] = alpha * acc_ref[:, t] + _pv_t(vt_ref, c * (bk // slab), p, SUM_ROWS)

    def step(c, cur_ref, cur_max, nxt_ref):
        k = k_ref[pl.ds(pl.multiple_of((c + 1) * bk, bk), bk), :]
        nxt_max = []
        for j, t in enumerate(tiles):
            s = _dot_nt(k, q[t, :])
            nxt_ref[:, t] = s
            nxt_max.append(jnp.max(s, axis=0, keepdims=True))
            update(cur_ref, cur_max[j], c, t)
        return nxt_max

    m_ref[...] = jnp.full(m_ref.shape, MASK_VALUE, F32)
    acc_ref[...] = jnp.zeros(acc_ref.shape, F32)
    max_a = scores(0, sa_ref)

    def body(i, max_a):
        max_b = step(2 * i, sa_ref, max_a, sb_ref)
        return tuple(step(2 * i + 1, sb_ref, max_b, sa_ref))

    pairs = (n_chunks - 1) // 2
    max_a = lax.fori_loop(0, pairs, body, tuple(max_a))
    last = 2 * pairs
    if n_chunks - last == 2:
        max_b = step(last, sa_ref, max_a, sb_ref)
        last, s_ref, s_max = last + 1, sb_ref, max_b
    else:
        s_ref, s_max = sa_ref, max_a
    for j, t in enumerate(tiles):
        update(s_ref, s_max[j], last, t)
    _unstack_heads(acc_ref[:LANES, :] * (1.0 / acc_ref[LANES:LANES + 1, :]), o_ref, group)


def _flash_bounded_kernel(q_ref, k_ref, vt_ref, bound_ref, o_ref, acc_ref, *, group, dk, bk):
    slab = vt_ref.shape[-1]
    n_chunks = k_ref.shape[0] // bk
    q = _stack_heads(q_ref, group, dk)
    tiles = [slice(0, q.shape[0])]
    bound = bound_ref[:, :1]
    acc_ref[...] = jnp.zeros(acc_ref.shape, F32)

    def chunk(c):
        k = k_ref[pl.ds(pl.multiple_of(c * bk, bk), bk), :]
        for t in tiles:
            p = jnp.exp2(_dot_nt(k, q[t, :]) - bound)
            acc_ref[:LANES, t] += _pv_t(vt_ref, c * (bk // slab), p.astype(BF16))
            acc_ref[LANES:, t] += jnp.sum(p.reshape(bk // MOD_ROWS, MOD_ROWS, p.shape[1]), axis=0)

    def body(i, carry):
        for u in range(BOUNDED_UNROLL):
            chunk(BOUNDED_UNROLL * i + u)
        return carry

    lax.fori_loop(0, n_chunks // BOUNDED_UNROLL, body, 0)
    for c in range(n_chunks - n_chunks % BOUNDED_UNROLL, n_chunks):
        chunk(c)
    l = jnp.sum(acc_ref[LANES:, :], axis=0, keepdims=True)
    _unstack_heads(acc_ref[:LANES, :] * (1.0 / l), o_ref, group)


def _flash_attention(q, k, vt, *, group, dk, n_lat, layer, bound=None):
    b, n, _ = q.shape
    n_kv = k.shape[2] // dk
    n_slab, slab = vt.shape[1], vt.shape[3]
    tq = min(Q_COLS // group, n_lat)
    bk = max(c for c in KV_CHUNKS if n % c == 0)
    assert bk % slab == 0
    nq = group * tq
    in_specs = [
        pl.BlockSpec((None, tq, group * dk), lambda bi, h, t: (bi, t, h)),
        pl.BlockSpec((None, n, dk), lambda bi, h, t: (bi, 0, h)),
        pl.BlockSpec((None, n_slab, LANES, slab), lambda bi, h, t: (bi, 0, h, 0)),
    ]
    common = dict(
        grid=(b, n_kv, n_lat // tq),
        out_specs=pl.BlockSpec((None, tq, group * LANES), lambda bi, h, t: (bi, t, h)),
        out_shape=jax.ShapeDtypeStruct((b, n_lat, n_kv * group * LANES), BF16),
        compiler_params=_params("parallel", "parallel", "parallel"),
    )
    acc = pltpu.VMEM((LANES + SUM_ROWS, nq), F32)

    def general():
        s_buf = pltpu.VMEM((bk, nq + SKEW_LANES), F32)
        return pl.pallas_call(
            functools.partial(_flash_kernel, group=group, dk=dk, bk=bk),
            in_specs=in_specs,
            scratch_shapes=[s_buf, s_buf, acc, pltpu.VMEM((1, nq), F32)],
            name="flash_" + str(layer), **common,
        )(q, k, vt)

    if bound is None:
        return general()

    def bounded():
        return pl.pallas_call(
            functools.partial(_flash_bounded_kernel, group=group, dk=dk, bk=bk),
            in_specs=in_specs + [pl.BlockSpec((1, LANES), lambda bi, h, t: (0, 0))],
            scratch_shapes=[pltpu.VMEM((LANES + MOD_ROWS, nq), F32)],
            name="flash_bounded_" + str(layer), **common,
        )(q, k, vt, jnp.full((1, LANES), bound, F32))

    return lax.cond(bound <= SAFE_SCORE_BOUND, bounded, general)


def _ctx_attn_kernel(q_ref, k_ref, vt_ref, o_ref, *, group, dk):
    q = _stack_heads(q_ref, group, dk)
    s = _dot_nt(k_ref[...], q)
    p = jnp.exp2(s - jnp.max(s, axis=0, keepdims=True))
    l = jnp.sum(p, axis=0, keepdims=True)
    _unstack_heads(_dot(vt_ref[...], p.astype(BF16)) * (1.0 / l), o_ref, group)


def _ctx_attention(q, k, vt, *, group, dk, n_lat, layer):
    b, n, _ = q.shape
    n_ctx = n - n_lat
    n_kv = k.shape[2] // dk
    slab = vt.shape[3]
    assert slab == n_ctx
    t = n_lat // n_ctx
    return pl.pallas_call(
        functools.partial(_ctx_attn_kernel, group=group, dk=dk),
        grid=(b, n_kv),
        in_specs=[
            pl.BlockSpec((None, n_ctx, group * dk), lambda bi, h: (bi, t, h)),
            pl.BlockSpec((None, n_ctx, dk), lambda bi, h: (bi, t, h)),
            pl.BlockSpec((None, None, LANES, slab), lambda bi, h: (bi, t, h, 0)),
        ],
        out_specs=pl.BlockSpec((None, n_ctx, group * LANES), lambda bi, h: (bi, 0, h)),
        out_shape=jax.ShapeDtypeStruct((b, n_ctx, n_kv * group * LANES), BF16),
        compiler_params=_params("parallel", "parallel"),
        name="ctx_attn_" + str(layer),
    )(q, k, vt)


def _split_heads(q):
    lane = lax.broadcasted_iota(jnp.int32, (1, LANES), 1)
    zero = jnp.zeros_like(q)
    return jnp.concatenate(
        [jnp.where((lane >= j * NA_HEAD_DIM) & (lane < (j + 1) * NA_HEAD_DIM), q, zero)
         for j in range(LANES // NA_HEAD_DIM)], axis=0)


def _merge_heads(o_t, nq):
    parts = [o_t[j * NA_HEAD_DIM:(j + 1) * NA_HEAD_DIM, j * nq:(j + 1) * nq]
             for j in range(LANES // NA_HEAD_DIM)]
    return jnp.concatenate(parts, axis=0).T


def _na_ctx_kernel(q_ref, k_ref, vt_ref, o_ref):
    q2 = _split_heads(q_ref[...])
    s = _dot_nt(k_ref[...], q2)
    p = jnp.exp2(s - jnp.max(s, axis=0, keepdims=True))
    l = jnp.sum(p, axis=0, keepdims=True)
    o_t = _dot(vt_ref[...], p.astype(BF16)) * (1.0 / l)
    o_ref[...] = _merge_heads(o_t, q_ref.shape[0]).astype(BF16)


def _na_ctx_attention(q, k, vt, *, n_lat, layer):
    b, n, width = q.shape
    n_ctx = n - n_lat
    assert vt.shape[3] == n_ctx
    t = n_lat // n_ctx
    blk = pl.BlockSpec((None, n_ctx, LANES), lambda bi, h: (bi, t, h))
    return pl.pallas_call(
        _na_ctx_kernel,
        grid=(b, width // LANES),
        in_specs=[blk, blk, pl.BlockSpec((None, None, LANES, n_ctx), lambda bi, h: (bi, t, h, 0))],
        out_specs=pl.BlockSpec((None, n_ctx, LANES), lambda bi, h: (bi, 0, h)),
        out_shape=jax.ShapeDtypeStruct((b, n_ctx, width), BF16),
        compiler_params=_params("parallel", "parallel"),
        name="ctx_attn_" + str(layer),
    )(q, k, vt)


def _na_window_start(blk, rows, lib):
    lo = lib.minimum(lib.maximum(NA_BLOCK_ROWS * blk - NA_KH // 2, 0), rows - NA_KH)
    return lib.minimum(lo, rows - NA_WIN_ROWS)


def _na_kernel(q_ref, k_ref, vt_ref, bias_ref, o_ref, sl_ref, sc_ref, *, n_lat, n_ctx, rows):
    slab = vt_ref.shape[-1]
    nq = NA_BLOCK_ROWS * GRID_W
    n_blocks = rows // NA_BLOCK_ROWS
    nk = NA_WIN_ROWS * GRID_W
    kc = k_ref[n_lat:n_lat + n_ctx, :]
    n_here = q_ref.shape[0] // nq

    def scores(i):
        blk = pl.program_id(2) * NA_STEP_BLOCKS + i
        kind = jnp.where(blk == 0, 1, jnp.where(blk == n_blocks - 1, 2, 0))
        ws = _na_window_start(blk, rows, jnp)
        q2 = _split_heads(q_ref[i * nq:(i + 1) * nq, :])
        s_loc = _dot_nt(k_ref[pl.ds(pl.multiple_of(ws * GRID_W, slab), nk), :], q2) + bias_ref[kind]
        s_ctx = _dot_nt(kc, q2)
        sl_ref[i % 2, :, :s_loc.shape[1]] = s_loc
        sc_ref[i % 2, :, :s_ctx.shape[1]] = s_ctx
        return ws, jnp.maximum(jnp.max(s_loc, axis=0, keepdims=True), jnp.max(s_ctx, axis=0, keepdims=True))

    nxt = scores(0)
    for i in range(n_here):
        ws, m = nxt
        if i + 1 < n_here:
            nxt = scores(i + 1)
        parts = []
        for j in range(LANES // NA_HEAD_DIM):
            t = slice(j * nq, (j + 1) * nq)
            p_loc = jnp.exp2(sl_ref[i % 2, :, t] - m[:, t]).astype(BF16)
            p_ctx = jnp.exp2(sc_ref[i % 2, :, t] - m[:, t]).astype(BF16)
            o_t = _pv_t(vt_ref, ws * GRID_W // slab, p_loc, SUM_ROWS)
            o_t = o_t + _pv_t(vt_ref, n_lat // slab, p_ctx, SUM_ROWS)
            parts.append(o_t[j * NA_HEAD_DIM:(j + 1) * NA_HEAD_DIM, :] * (1.0 / o_t[LANES:LANES + 1, :]))
        o_ref[i * nq:(i + 1) * nq, :] = jnp.concatenate(parts, axis=0).T.astype(BF16)


def _na_bounded_kernel(q_ref, k_ref, vt_ref, bias_ref, bound_ref, o_ref, *, n_lat, n_ctx, rows):
    slab = vt_ref.shape[-1]
    nq = NA_BLOCK_ROWS * GRID_W
    n_blocks = rows // NA_BLOCK_ROWS
    nk = NA_WIN_ROWS * GRID_W
    kc = k_ref[n_lat:n_lat + n_ctx, :]
    bound = bound_ref[:, :1]
    for i in range(q_ref.shape[0] // nq):
        blk = pl.program_id(2) * NA_STEP_BLOCKS + i
        kind = jnp.where(blk == 0, 1, jnp.where(blk == n_blocks - 1, 2, 0))
        ws = _na_window_start(blk, rows, jnp)
        q2 = _split_heads(q_ref[i * nq:(i + 1) * nq, :])
        kw = k_ref[pl.ds(pl.multiple_of(ws * GRID_W, slab), nk), :]
        p_loc = jnp.exp2(_dot_nt(kw, q2) + (bias_ref[kind] - bound)).astype(BF16)
        p_ctx = jnp.exp2(_dot_nt(kc, q2) - bound).astype(BF16)
        o_t = _pv_t(vt_ref, ws * GRID_W // slab, p_loc, SUM_ROWS)
        o_t = o_t + _pv_t(vt_ref, n_lat // slab, p_ctx, SUM_ROWS)
        o_t = o_t[:LANES, :] * (1.0 / o_t[LANES:LANES + 1, :])
        o_ref[i * nq:(i + 1) * nq, :] = _merge_heads(o_t, nq).astype(BF16)


def _na_bias(rpb, rows):
    h, n_dr, n_dc = rpb.shape
    n_blocks = rows // NA_BLOCK_ROWS
    lead = GRID_W - NA_KW
    wide = 2 * GRID_W
    w = jnp.pad(rpb * LOG2E, ((0, 0), (0, 0), (lead, wide - lead - n_dc)), constant_values=MASK_VALUE)
    flat = jnp.broadcast_to(w[:, :, None, :], (h, n_dr, GRID_W, wide)).reshape(h, n_dr, GRID_W * wide)
    skew = flat[:, :, GRID_W - 1:GRID_W - 1 + GRID_W * (wide - 1)].reshape(h, n_dr, GRID_W, wide - 1)
    qc = jnp.arange(GRID_W)
    cs = jnp.clip(qc - NA_KW // 2, 0, GRID_W - NA_KW)
    col_ok = (qc[None, :] >= cs[:, None]) & (qc[None, :] < cs[:, None] + NA_KW)
    c_t = jnp.swapaxes(jnp.where(col_ok, skew[..., :GRID_W], MASK_VALUE), -1, -2)
    masked = jnp.full((h, GRID_W, GRID_W), MASK_VALUE, F32)
    kinds = []
    for blk in (1, 0, n_blocks - 1):
        ws = _na_window_start(blk, rows, np)
        strips = []
        for j in range(NA_WIN_ROWS):
            kr = ws + j
            blocks = []
            for i in range(NA_BLOCK_ROWS):
                r = NA_BLOCK_ROWS * blk + i
                rs = min(max(r - NA_KH // 2, 0), rows - NA_KH)
                blocks.append(c_t[:, kr - r + NA_KH - 1] if rs <= kr < rs + NA_KH else masked)
            strips.append(jnp.concatenate(blocks, axis=-1))
        kinds.append(jnp.concatenate(strips, axis=1))
    bias = jnp.stack(kinds, axis=1)
    sub = LANES // NA_HEAD_DIM
    bias = bias.reshape(h // sub, sub, len(kinds), NA_WIN_ROWS * GRID_W, NA_BLOCK_ROWS * GRID_W)
    return jnp.concatenate([bias[:, j] for j in range(sub)], axis=-1)


def _na_attention(q, k, vt, bias, bound, *, n_lat, layer):
    b, n, width = q.shape
    n_ctx = n - n_lat
    rows = n_lat // GRID_W
    n_slab, slab = vt.shape[1], vt.shape[3]
    tq = NA_STEP_BLOCKS * NA_BLOCK_ROWS * GRID_W
    in_specs = [
        pl.BlockSpec((None, tq, LANES), lambda hp, bi, rb: (bi, rb, hp)),
        pl.BlockSpec((None, n, LANES), lambda hp, bi, rb: (bi, 0, hp)),
        pl.BlockSpec((None, n_slab, LANES, slab), lambda hp, bi, rb: (bi, 0, hp, 0)),
        pl.BlockSpec((None,) + bias.shape[1:], lambda hp, bi, rb: (hp, 0, 0, 0)),
    ]
    common = dict(
        grid=(width // LANES, b, n_lat // tq),
        out_specs=pl.BlockSpec((None, tq, LANES), lambda hp, bi, rb: (bi, rb, hp)),
        out_shape=jax.ShapeDtypeStruct((b, n_lat, width), BF16),
        compiler_params=_params("parallel", "parallel", "arbitrary"),
    )

    def general():
        return pl.pallas_call(
            functools.partial(_na_kernel, n_lat=n_lat, n_ctx=n_ctx, rows=rows),
            in_specs=in_specs,
            scratch_shapes=[pltpu.VMEM((2, bias.shape[2], bias.shape[3] + SKEW_LANES), F32),
                            pltpu.VMEM((2, n_ctx, bias.shape[3] + SKEW_LANES), F32)],
            name="na_attn_" + str(layer), **common,
        )(q, k, vt, bias)

    def bounded():
        return pl.pallas_call(
            functools.partial(_na_bounded_kernel, n_lat=n_lat, n_ctx=n_ctx, rows=rows),
            in_specs=in_specs + [pl.BlockSpec((1, LANES), lambda hp, bi, rb: (0, 0))],
            name="na_attn_bounded_" + str(layer), **common,
        )(q, k, vt, bias, jnp.full((1, LANES), bound, F32))

    return lax.cond(bound <= SAFE_SCORE_BOUND, bounded, general)


def _out_proj_final_kernel(o_ref, g_ref, w_ref, x_ref, mod_ref, fg_ref, y_ref):
    a = (o_ref[...].astype(F32) * g_ref[...].astype(F32)).astype(BF16)
    xn = x_ref[...] + mod_ref[0, 2:3, :] * _dot(a, w_ref[...])
    y_ref[...] = xn * _rms(xn) * fg_ref[...]


def _out_proj_final(o, gs, w_out, x_lat, mod3, layer, n_lat, final_g):
    b, _, d = x_lat.shape
    tm = ROW_TILE
    row = lambda bi, t: (bi, t, 0)
    w = o.shape[2]
    return pl.pallas_call(
        _out_proj_final_kernel,
        grid=(b, n_lat // tm),
        in_specs=[pl.BlockSpec((None, tm, w), row), pl.BlockSpec((None, tm, w), row),
                  pl.BlockSpec(w_out.shape, lambda bi, t: (0, 0)), pl.BlockSpec((None, tm, d), row),
                  pl.BlockSpec((1, 3, d), lambda bi, t: (layer * MOD_ROWS + bi, 0, 0)),
                  pl.BlockSpec((1, d), lambda bi, t: (0, 0))],
        out_specs=pl.BlockSpec((None, tm, d), row),
        out_shape=jax.ShapeDtypeStruct((b, n_lat, d), F32),
        compiler_params=_params("parallel", "parallel"),
        name="out_proj_final",
    )(o, gs, w_out, x_lat, mod3, final_g.reshape(1, d))


def _rope_tables(n_lat, n_ctx, rot_dim):
    n_freq = rot_dim // 4
    inv = ROPE_THETA ** (-jnp.arange(n_freq, dtype=F32) / n_freq)
    t = jnp.arange(n_lat)
    ang_r = (t // GRID_W).astype(F32)[:, None] * inv
    ang_c = (t % GRID_W).astype(F32)[:, None] * inv
    ang = jnp.concatenate([ang_r, ang_r, ang_c, ang_c], axis=-1)
    cos, sin = jnp.cos(ang), jnp.sin(ang)
    first = (jnp.arange(rot_dim) % (2 * n_freq)) < n_freq
    sin_up = jnp.where(first, -sin, 0.0)
    sin_dn = jnp.where(first, 0.0, sin)
    pad = LANES - rot_dim

    def finish(tab, fill):
        tab = jnp.pad(tab, ((0, 0), (0, pad)), constant_values=fill)
        return jnp.pad(tab, ((0, n_ctx), (0, 0)), constant_values=fill)

    return finish(cos, 1.0), finish(sin_up, 0.0), finish(sin_dn, 0.0)


def _mla_weights(w_in, w_uq, w_ukv):
    o_kv = MLA_Q_LORA
    o_kr = o_kv + MLA_KV_LORA
    o_g = o_kr + MLA_ROPE
    k_r = jnp.pad(w_in[:, o_kr:o_g], ((0, 0), (0, LANES - MLA_ROPE)))
    w_perm = jnp.concatenate([w_in[:, :o_kr], w_in[:, o_g:], k_r], axis=1).astype(BF16)
    uq = w_uq.reshape(MLA_Q_LORA, MLA_HEADS, MLA_NOPE + MLA_ROPE)
    uq = jnp.pad(uq, ((0, 0), (0, 0), (0, MLA_QK_PAD - MLA_NOPE - MLA_ROPE)))
    uq = uq.reshape(MLA_Q_LORA, MLA_HEADS * MLA_QK_PAD).astype(BF16)
    ukv = w_ukv.reshape(MLA_KV_LORA, MLA_HEADS, MLA_NOPE + MLA_V).astype(BF16)
    uk = ukv[:, :, :MLA_NOPE].reshape(MLA_KV_LORA, MLA_HEADS * MLA_NOPE)
    uv = ukv[:, :, MLA_NOPE:].reshape(MLA_KV_LORA, MLA_HEADS * MLA_V)
    return w_perm, uq, uk, uv


def kernel(x, c, ctx, c_ctx, mod_w, mod_b, norm_g, final_g, ga_w_in, ga_q_g, ga_k_g, ga_w_out, na_w_in, na_rpb, na_w_out, mla_w_in, mla_q_g, mla_kv_g, mla_w_uq, mla_w_ukv, mla_w_out):
    b, n_lat, d = x.shape
    n_ctx = ctx.shape[1]
    depth = mod_w.shape[0]
    assert n_lat % ROW_TILE == 0 and n_ctx == ROW_TILE and n_lat % n_ctx == 0
    assert b < MOD_ROWS and n_lat % (NA_STEP_BLOCKS * NA_BLOCK_ROWS * GRID_W) == 0
    assert n_lat // GRID_W >= NA_WIN_ROWS + NA_BLOCK_ROWS

    stream = (x, ctx, 0)
    cc = jnp.concatenate([c, c_ctx[None, :], jnp.zeros((MOD_ROWS - b - 1, d), F32)], axis=0)
    mod3 = _modulation(cc, mod_w, mod_b).reshape(depth * MOD_ROWS, 3, d)

    tab_a = _rope_tables(n_lat, n_ctx, GQA_HEAD_DIM)
    tab_m = _rope_tables(n_lat, n_ctx, MLA_ROPE)

    prev = None
    for i in range(depth):
        kind, j = i % N_MIXERS, i // N_MIXERS
        need_ctx = i < depth - 1

        def project(kern, consts, tables, widths, with_norms=False):
            outs = _project(kern, stream, mod3, i, norm_g[i], consts, tables, widths, n_lat, with_norms, prev)
            if prev is None:
                return stream, outs
            return (outs[0], outs[0], n_lat // ROW_TILE), outs[1:]

        if kind == 0:
            n_q = d // GQA_HEAD_DIM
            n_kv = n_q // GQA_GROUP
            kern = functools.partial(_gqa_proj_kernel, n_q=n_q, n_kv=n_kv, scale=GQA_HEAD_DIM ** -0.5 * LOG2E)
            consts = [ga_w_in[j].astype(BF16), ga_q_g[j].reshape(1, -1), ga_k_g[j].reshape(1, -1)]
            kvw = n_kv * GQA_HEAD_DIM
            stream, (q, k, vt, gs) = project(kern, consts, tab_a, [d, kvw, kvw, d])
            bound = (BOUND_SLACK * GQA_HEAD_DIM ** 0.5 * LOG2E
                     * jnp.max(jnp.abs(ga_q_g[j])) * jnp.max(jnp.abs(ga_k_g[j])))
            o = _flash_attention(q, k, vt, group=GQA_GROUP, dk=GQA_HEAD_DIM, n_lat=n_lat, layer=i, bound=bound)
            if need_ctx:
                oc = _ctx_attention(q, k, vt, group=GQA_GROUP, dk=GQA_HEAD_DIM, n_lat=n_lat, layer=i)
            w_out = ga_w_out[j]
        elif kind == 1:
            kern = functools.partial(_na_proj_kernel, width=d, scale=NA_HEAD_DIM ** -0.5 * LOG2E)
            stream, (q, k, vt, gs, norms) = project(kern, [na_w_in[j].astype(BF16)], (), [d, d, d, d], True)
            bound = (BOUND_SLACK * jnp.sqrt(jnp.max(norms[:, :, 0, :]) * jnp.max(norms[:, :, MOD_ROWS, :]))
                     + LOG2E * jnp.max(jnp.abs(na_rpb[j])))
            o = _na_attention(q, k, vt, _na_bias(na_rpb[j], n_lat // GRID_W), bound, n_lat=n_lat, layer=i)
            if need_ctx:
                oc = _na_ctx_attention(q, k, vt, n_lat=n_lat, layer=i)
            w_out = na_w_out[j]
        else:
            kern = functools.partial(_mla_proj_kernel, width=d, scale=(MLA_NOPE + MLA_ROPE) ** -0.5 * LOG2E)
            w_perm, uq, uk, uv = _mla_weights(mla_w_in[j], mla_w_uq[j], mla_w_ukv[j])
            consts = [w_perm, mla_q_g[j].reshape(1, -1), mla_kv_g[j].reshape(1, -1), uq, uk, uv]
            widths = [MLA_HEADS * MLA_QK_PAD, MLA_HEADS * MLA_QK_PAD, MLA_HEADS * MLA_V, d]
            stream, (q, k, vt, gs, norms) = project(kern, consts, tab_m, widths, True)
            bound = BOUND_SLACK * jnp.sqrt(jnp.max(norms[:, :, 0, 0]) * jnp.max(norms[:, :, MOD_ROWS, 0]))
            o = _flash_attention(q, k, vt, group=1, dk=MLA_QK_PAD, n_lat=n_lat, layer=i, bound=bound)
            if need_ctx:
                oc = _ctx_attention(q, k, vt, group=1, dk=MLA_QK_PAD, n_lat=n_lat, layer=i)
            w_out = mla_w_out[j]
        if need_ctx:
            prev = (o, oc, gs, w_out.astype(BF16), i)
    return _out_proj_final(o, gs, w_out.astype(BF16), stream[0], mod3, depth - 1, n_lat, final_g)
```

```python
import functools

import jax
import numpy as np
import jax.numpy as jnp
from jax import lax
from jax.experimental import pallas as pl
from jax.experimental.pallas import tpu as pltpu

F32 = jnp.float32
BF16 = jnp.bfloat16

NORM_EPS = 1e-6
ROPE_THETA = 10000.0
GRID_W = 64
N_MIXERS = 3
LANES = 128
MOD_ROWS = 8
MASK_VALUE = -1e30
LOG2E = 1.4426950408889634

GQA_HEAD_DIM = 128
GQA_GROUP = 4
NA_HEAD_DIM = 64
NA_KH = 8
NA_KW = 16
NA_BLOCK_ROWS = 4
NA_WIN_ROWS = 12
NA_STEP_BLOCKS = 8
MLA_HEADS = 8
MLA_Q_LORA = 512
MLA_KV_LORA = 256
MLA_NOPE = 128
MLA_ROPE = 64
MLA_V = 128
MLA_QK_PAD = 256

ROW_TILE = 256
FINAL_TILE = 512
Q_COLS = 4096
Q_TILE = 256
SUM_ROWS = 16
SKEW_LANES = 128
SAFE_SCORE_BOUND = 50.0
BOUNDED_UNROLL = 5
BOUND_SLACK = 1.02
KV_CHUNKS = (768, 512, 256)
VMEM_LIMIT = 48 * 1024 * 1024


def _params(*sem):
    return pltpu.CompilerParams(dimension_semantics=sem, vmem_limit_bytes=VMEM_LIMIT)


def _silu(v):
    return v * (1.0 / (1.0 + jnp.exp(-v)))


def _rms(v):
    return lax.rsqrt(jnp.mean(v * v, axis=-1, keepdims=True) + NORM_EPS)


def _rope(v, cos, sin_up, sin_dn, half):
    w = v.shape[-1]
    return v * cos + pltpu.roll(v, w - half, 1) * sin_up + pltpu.roll(v, half, 1) * sin_dn


def _dot(a, b):
    return jnp.dot(a, b, preferred_element_type=F32)


def _dot_nt(a, b):
    return lax.dot_general(a, b, (((1,), (1,)), ((), ())), preferred_element_type=F32)


def _mod_kernel(c_ref, w_ref, b_ref, o_ref):
    a = _silu(c_ref[...])
    w = w_ref[0]
    a_hi = a.astype(BF16)
    a_lo = (a - a_hi.astype(F32)).astype(BF16)
    w_hi = w.astype(BF16)
    w_lo = (w - w_hi.astype(F32)).astype(BF16)
    o_ref[0] = _dot(a_hi, w_hi) + _dot(a_lo, w_hi) + _dot(a_hi, w_lo) + b_ref[0]


def _modulation(cc, mod_w, mod_b):
    depth, d, n3 = mod_w.shape
    tn = 1024
    return pl.pallas_call(
        _mod_kernel,
        grid=(depth, n3 // tn),
        in_specs=[
            pl.BlockSpec((MOD_ROWS, d), lambda l, j: (0, 0)),
            pl.BlockSpec((1, d, tn), lambda l, j: (l, 0, j)),
            pl.BlockSpec((1, 1, tn), lambda l, j: (l, 0, j)),
        ],
        out_specs=pl.BlockSpec((1, MOD_ROWS, tn), lambda l, j: (l, 0, j)),
        out_shape=jax.ShapeDtypeStruct((depth, MOD_ROWS, n3), F32),
        compiler_params=_params("parallel", "parallel"),
        name="adaln_modulation",
    )(cc, mod_w, mod_b.reshape(depth, 1, n3))


def _modulated_norm(x_ref, xc_ref, ng_ref, mod_ref, n_lat_tiles):
    x = jnp.where(pl.program_id(1) >= n_lat_tiles, xc_ref[...], x_ref[...])
    shift = mod_ref[0, 0:1, :]
    scale = mod_ref[0, 1:2, :]
    return ((x * _rms(x)) * ng_ref[...] * (1.0 + scale) + shift).astype(BF16)


def _gqa_proj_kernel(x_ref, xc_ref, mod_ref, ng_ref, w_ref, qg_ref, kg_ref, cos_ref, su_ref, sd_ref,
                     q_ref, k_ref, vt_ref, g_ref, *, n_lat_tiles, n_q, n_kv, scale):
    h = _modulated_norm(x_ref, xc_ref, ng_ref, mod_ref, n_lat_tiles)
    cos, su, sd = cos_ref[...], su_ref[...], sd_ref[...]
    hd = GQA_HEAD_DIM
    qw, kw = n_q * hd, n_kv * hd
    q = _dot(h, w_ref[:, :qw])
    for i in range(n_q):
        qh = q[:, i * hd:(i + 1) * hd]
        qh = qh * _rms(qh) * qg_ref[...]
        q_ref[:, i * hd:(i + 1) * hd] = (_rope(qh, cos, su, sd, hd // 4) * scale).astype(BF16)
    k = _dot(h, w_ref[:, qw:qw + kw])
    for i in range(n_kv):
        kh = k[:, i * hd:(i + 1) * hd]
        kh = kh * _rms(kh) * kg_ref[...]
        k_ref[:, i * hd:(i + 1) * hd] = _rope(kh, cos, su, sd, hd // 4).astype(BF16)
    v = _dot(h, w_ref[:, qw + kw:qw + 2 * kw])
    for i in range(n_kv):
        vt_ref[i * hd:(i + 1) * hd, :] = v[:, i * hd:(i + 1) * hd].T.astype(BF16)
    g_ref[...] = _silu(_dot(h, w_ref[:, qw + 2 * kw:])).astype(BF16)


def _na_proj_kernel(x_ref, xc_ref, mod_ref, ng_ref, w_ref, q_ref, k_ref, vt_ref, g_ref, norm_ref,
                    *, n_lat_tiles, width, scale):
    h = _modulated_norm(x_ref, xc_ref, ng_ref, mod_ref, n_lat_tiles)
    head_of_col = lax.broadcasted_iota(jnp.int32, (width, LANES), 0) // NA_HEAD_DIM
    lane = lax.broadcasted_iota(jnp.int32, (width, LANES), 1)
    pick = jnp.where(head_of_col == lane, 1.0, 0.0).astype(BF16)
    half = norm_ref.shape[0] // 2
    q = _dot(h, w_ref[:, :width]) * scale
    q_ref[...] = q.astype(BF16)
    q_sq = jnp.max(_dot((q * q).astype(BF16), pick), axis=0, keepdims=True)
    norm_ref[:half, :] = jnp.broadcast_to(q_sq, (half, LANES))
    k = _dot(h, w_ref[:, width:2 * width])
    k_ref[...] = k.astype(BF16)
    k_sq = jnp.max(_dot((k * k).astype(BF16), pick), axis=0, keepdims=True)
    norm_ref[half:, :] = jnp.broadcast_to(k_sq, (half, LANES))
    v = _dot(h, w_ref[:, 2 * width:3 * width])
    for i in range(width // LANES):
        vt_ref[i * LANES:(i + 1) * LANES, :] = v[:, i * LANES:(i + 1) * LANES].T.astype(BF16)
    g_ref[...] = _silu(_dot(h, w_ref[:, 3 * width:])).astype(BF16)


def _mla_proj_kernel(x_ref, xc_ref, mod_ref, ng_ref, w_ref, qg_ref, kvg_ref, wuq_ref, wuk_ref, wuv_ref,
                     cos_ref, su_ref, sd_ref, q_ref, k_ref, vt_ref, g_ref, norm_ref,
                     *, n_lat_tiles, width, scale):
    h = _modulated_norm(x_ref, xc_ref, ng_ref, mod_ref, n_lat_tiles)
    ones = jnp.ones((LANES, LANES), BF16)

    def row_sq(v):
        return _dot((v * v).astype(BF16), ones)

    cos, su, sd = cos_ref[...], su_ref[...], sd_ref[...]
    o_kv = MLA_Q_LORA
    o_g = o_kv + MLA_KV_LORA
    o_kr = o_g + width
    c_q = _dot(h, w_ref[:, :o_kv])
    c_q = (c_q * _rms(c_q) * qg_ref[...]).astype(BF16)
    q = _dot(c_q, wuq_ref[...])
    q_sq = None
    for i in range(MLA_HEADS):
        a = i * MLA_QK_PAD
        q_nope = q[:, a:a + MLA_NOPE] * scale
        q_rope = _rope(q[:, a + MLA_NOPE:a + MLA_QK_PAD], cos, su, sd, MLA_ROPE // 4) * scale
        q_ref[:, a:a + MLA_NOPE] = q_nope.astype(BF16)
        q_ref[:, a + MLA_NOPE:a + MLA_QK_PAD] = q_rope.astype(BF16)
        sq = row_sq(q_nope) + row_sq(q_rope)
        q_sq = sq if q_sq is None else jnp.maximum(q_sq, sq)
    c_kv = _dot(h, w_ref[:, o_kv:o_g])
    c_kv = (c_kv * _rms(c_kv) * kvg_ref[...]).astype(BF16)
    k_r = _rope(_dot(h, w_ref[:, o_kr:]), cos, su, sd, MLA_ROPE // 4)
    k_nope = _dot(c_kv, wuk_ref[...])
    k_sq = None
    for i in range(MLA_HEADS):
        a = i * MLA_QK_PAD
        k_h = k_nope[:, i * MLA_NOPE:(i + 1) * MLA_NOPE]
        k_ref[:, a:a + MLA_NOPE] = k_h.astype(BF16)
        k_ref[:, a + MLA_NOPE:a + MLA_QK_PAD] = k_r.astype(BF16)
        sq = row_sq(k_h)
        k_sq = sq if k_sq is None else jnp.maximum(k_sq, sq)
    k_sq = k_sq + row_sq(k_r)
    half = norm_ref.shape[0] // 2
    norm_ref[:half, :] = jnp.broadcast_to(jnp.max(q_sq, axis=0, keepdims=True), (half, LANES))
    norm_ref[half:, :] = jnp.broadcast_to(jnp.max(k_sq, axis=0, keepdims=True), (half, LANES))
    v = _dot(c_kv, wuv_ref[...])
    for i in range(MLA_HEADS):
        vt_ref[i * MLA_V:(i + 1) * MLA_V, :] = v[:, i * MLA_V:(i + 1) * MLA_V].T.astype(BF16)
    g_ref[...] = _silu(_dot(h, w_ref[:, o_g:o_kr])).astype(BF16)


class _Loaded:
    def __init__(self, value):
        self.value = value

    def __getitem__(self, _):
        return self.value


def _fused_kernel(*refs, proj, split, n_in, n_lat_tiles):
    ins, outs = refs[:n_in], refs[n_in:]
    o_ref, oc_ref, g_ref, w_ref, x_ref = ins[:5]
    rest = ins[5:]
    is_ctx = pl.program_id(1) >= n_lat_tiles
    x = x_ref[...]
    if split:
        x = jnp.where(is_ctx, rest[0][...], x)
        rest = rest[1:]
    o = jnp.where(is_ctx, oc_ref[...], o_ref[...])
    a = (o.astype(F32) * g_ref[...].astype(F32)).astype(BF16)
    x = x + rest[0][0, 2:3, :] * _dot(a, w_ref[...])
    outs[0][...] = x
    x = _Loaded(x)
    proj(x, x, *rest[1:], *outs[1:], n_lat_tiles=n_lat_tiles)


def _project(kern, stream, mod3, layer, norm_g, consts, tables, out_widths, n_lat, with_norms=False, prev=None):
    x_lat, x_ctx, ctx_tile = stream
    b, _, d = x_lat.shape
    tm = ROW_TILE
    n_lat_tiles = n_lat // tm
    n = n_lat + tm
    ctx_row = b
    row = lambda bi, t: (bi, t, 0)
    lat_row = lambda bi, t: (bi, jnp.minimum(t, n_lat_tiles - 1), 0)
    const = lambda bi, t: (0, 0)

    def mod_spec(which):
        return pl.BlockSpec((1, 3, d), lambda bi, t: (which * MOD_ROWS + jnp.where(t < n_lat_tiles, bi, ctx_row), 0, 0))

    x_specs = [pl.BlockSpec((None, tm, d), lat_row), pl.BlockSpec((None, tm, d), lambda bi, t: (bi, ctx_tile, 0))]
    in_specs = [mod_spec(layer), pl.BlockSpec((1, d), const)]
    in_specs += [pl.BlockSpec(a.shape, const) for a in consts]
    in_specs += [pl.BlockSpec((tm, LANES), lambda bi, t: (t, 0)) for _ in tables]
    args = [mod3, norm_g.reshape(1, d), *consts, *tables]
    out_specs = [pl.BlockSpec((None, tm, w), row) for w in out_widths]
    out_shape = [jax.ShapeDtypeStruct((b, n, w), BF16) for w in out_widths]
    out_specs[2] = pl.BlockSpec((None, None, out_widths[2], tm), lambda bi, t: (bi, t, 0, 0))
    out_shape[2] = jax.ShapeDtypeStruct((b, n // tm, out_widths[2], tm), BF16)
    if with_norms:
        out_specs.append(pl.BlockSpec((None, None, 2 * MOD_ROWS, LANES), lambda bi, t: (bi, t, 0, 0)))
        out_shape.append(jax.ShapeDtypeStruct((b, n // tm, 2 * MOD_ROWS, LANES), F32))
    call = dict(grid=(b, n // tm), compiler_params=_params("parallel", "parallel"), name="proj_" + str(layer))
    if prev is None:
        return pl.pallas_call(
            functools.partial(kern, n_lat_tiles=n_lat_tiles),
            in_specs=x_specs + in_specs, out_specs=out_specs, out_shape=out_shape, **call,
        )(x_lat, x_ctx, *args)
    o, o_ctx, gs, w_out, prev_layer = prev
    combined = x_lat is x_ctx
    wo = o.shape[2]
    pre_specs = [pl.BlockSpec((None, tm, wo), lat_row), pl.BlockSpec((None, tm, wo), lambda bi, t: (bi, 0, 0)),
                 pl.BlockSpec((None, tm, wo), row), pl.BlockSpec(w_out.shape, const)]
    if combined:
        x_specs, xs = [pl.BlockSpec((None, tm, d), row)], [x_lat]
    else:
        xs = [x_lat, x_ctx]
    in_specs = pre_specs + x_specs + [mod_spec(prev_layer)] + in_specs
    return pl.pallas_call(
        functools.partial(_fused_kernel, proj=kern, split=not combined, n_in=len(in_specs), n_lat_tiles=n_lat_tiles),
        in_specs=in_specs,
        out_specs=[pl.BlockSpec((None, tm, d), row)] + out_specs,
        out_shape=[jax.ShapeDtypeStruct((b, n, d), F32)] + out_shape,
        input_output_aliases={4: 0} if combined else {},
        **call,
    )(o, o_ctx, gs, w_out, *xs, mod3, *args)


def _stack_heads(q_ref, group, dk):
    return jnp.concatenate([q_ref[:, g * dk:(g + 1) * dk] for g in range(group)], axis=0)


def _unstack_heads(o_t, o_ref, group):
    tq = o_ref.shape[0]
    for g in range(group):
        o_ref[:, g * LANES:(g + 1) * LANES] = o_t[:, g * tq:(g + 1) * tq].T.astype(BF16)


def _pv_t(vt_ref, first_slab, p, sum_rows=0):
    slab = vt_ref.shape[-1]
    acc = None
    for i in range(p.shape[0] // slab):
        vt = vt_ref[first_slab + i]
        if sum_rows:
            vt = jnp.concatenate([vt, jnp.ones((sum_rows, slab), vt.dtype)], axis=0)
        t = _dot(vt, p[i * slab:(i + 1) * slab, :])
        acc = t if acc is None else acc + t
    return acc


def _flash_kernel(q_ref, k_ref, vt_ref, o_ref, sa_ref, sb_ref, acc_ref, m_ref, *, group, dk, bk):
    slab = vt_ref.shape[-1]
    n_chunks = k_ref.shape[0] // bk
    q = _stack_heads(q_ref, group, dk)
    tiles = [slice(j * Q_TILE, (j + 1) * Q_TILE) for j in range(q.shape[0] // Q_TILE)]

    def scores(c, s_ref):
        k = k_ref[pl.ds(pl.multiple_of(c * bk, bk), bk), :]
        out = []
        for t in tiles:
            s = _dot_nt(k, q[t, :])
            s_ref[:, t] = s
            out.append(jnp.max(s, axis=0, keepdims=True))
        return out

    def update(s_ref, s_max, c, t):
        m = m_ref[:, t]
        m_new = jnp.maximum(m, s_max)
        alpha = jnp.exp2(m - m_new)
        p = jnp.exp2(s_ref[:, t] - m_new).astype(BF16)
        m_ref[:, t] = m_new
        acc_ref[:, t] = alpha * acc_ref[:, t] + _pv_t(vt_ref, c * (bk // slab), p, SUM_ROWS)

    def step(c, cur_ref, cur_max, nxt_ref):
        k = k_ref[pl.ds(pl.multiple_of((c + 1) * bk, bk), bk), :]
        nxt_max = []
        for j, t in enumerate(tiles):
            s = _dot_nt(k, q[t, :])
            nxt_ref[:, t] = s
            nxt_max.append(jnp.max(s, axis=0, keepdims=True))
            update(cur_ref, cur_max[j], c, t)
        return nxt_max

    m_ref[...] = jnp.full(m_ref.shape, MASK_VALUE, F32)
    acc_ref[...] = jnp.zeros(acc_ref.shape, F32)
    max_a = scores(0, sa_ref)

    def body(i, max_a):
        max_b = step(2 * i, sa_ref, max_a, sb_ref)
        return tuple(step(2 * i + 1, sb_ref, max_b, sa_ref))

    pairs = (n_chunks - 1) // 2
    max_a = lax.fori_loop(0, pairs, body, tuple(max_a))
    last = 2 * pairs
    if n_chunks - last == 2:
        max_b = step(last, sa_ref, max_a, sb_ref)
        last, s_ref, s_max = last + 1, sb_ref, max_b
    else:
        s_ref, s_max = sa_ref, max_a
    for j, t in enumerate(tiles):
        update(s_ref, s_max[j], last, t)
    _unstack_heads(acc_ref[:LANES, :] * (1.0 / acc_ref[LANES:LANES + 1, :]), o_ref, group)


def _flash_bounded_kernel(q_ref, k_ref, vt_ref, bound_ref, o_ref, acc_ref, *, group, dk, bk):
    slab = vt_ref.shape[-1]
    n_chunks = k_ref.shape[0] // bk
    q = _stack_heads(q_ref, group, dk)
    tiles = [slice(0, q.shape[0])]
    bound = bound_ref[:, :1]
    acc_ref[...] = jnp.zeros(acc_ref.shape, F32)

    def chunk(c):
        k = k_ref[pl.ds(pl.multiple_of(c * bk, bk), bk), :]
        for t in tiles:
            p = jnp.exp2(_dot_nt(k, q[t, :]) - bound)
            acc_ref[:LANES, t] += _pv_t(vt_ref, c * (bk // slab), p.astype(BF16))
            acc_ref[LANES:, t] += jnp.sum(p.reshape(bk // MOD_ROWS, MOD_ROWS, p.shape[1]), axis=0)

    def body(i, carry):
        for u in range(BOUNDED_UNROLL):
            chunk(BOUNDED_UNROLL * i + u)
        return carry

    lax.fori_loop(0, n_chunks // BOUNDED_UNROLL, body, 0)
    for c in range(n_chunks - n_chunks % BOUNDED_UNROLL, n_chunks):
        chunk(c)
    l = jnp.sum(acc_ref[LANES:, :], axis=0, keepdims=True)
    _unstack_heads(acc_ref[:LANES, :] * (1.0 / l), o_ref, group)


def _flash_attention(q, k, vt, *, group, dk, n_lat, layer, bound=None):
    b, n, _ = q.shape
    n_kv = k.shape[2] // dk
    n_slab, slab = vt.shape[1], vt.shape[3]
    tq = min(Q_COLS // group, n_lat)
    bk = max(c for c in KV_CHUNKS if n % c == 0)
    assert bk % slab == 0
    nq = group * tq
    in_specs = [
        pl.BlockSpec((None, tq, group * dk), lambda bi, h, t: (bi, t, h)),
        pl.BlockSpec((None, n, dk), lambda bi, h, t: (bi, 0, h)),
        pl.BlockSpec((None, n_slab, LANES, slab), lambda bi, h, t: (bi, 0, h, 0)),
    ]
    common = dict(
        grid=(b, n_kv, n_lat // tq),
        out_specs=pl.BlockSpec((None, tq, group * LANES), lambda bi, h, t: (bi, t, h)),
        out_shape=jax.ShapeDtypeStruct((b, n_lat, n_kv * group * LANES), BF16),
        compiler_params=_params("parallel", "parallel", "parallel"),
    )
    acc = pltpu.VMEM((LANES + SUM_ROWS, nq), F32)

    def general():
        s_buf = pltpu.VMEM((bk, nq + SKEW_LANES), F32)
        return pl.pallas_call(
            functools.partial(_flash_kernel, group=group, dk=dk, bk=bk),
            in_specs=in_specs,
            scratch_shapes=[s_buf, s_buf, acc, pltpu.VMEM((1, nq), F32)],
            name="flash_" + str(layer), **common,
        )(q, k, vt)

    if bound is None:
        return general()

    def bounded():
        return pl.pallas_call(
            functools.partial(_flash_bounded_kernel, group=group, dk=dk, bk=bk),
            in_specs=in_specs + [pl.BlockSpec((1, LANES), lambda bi, h, t: (0, 0))],
            scratch_shapes=[pltpu.VMEM((LANES + MOD_ROWS, nq), F32)],
            name="flash_bounded_" + str(layer), **common,
        )(q, k, vt, jnp.full((1, LANES), bound, F32))

    return lax.cond(bound <= SAFE_SCORE_BOUND, bounded, general)


def _ctx_attn_kernel(q_ref, k_ref, vt_ref, o_ref, *, group, dk):
    q = _stack_heads(q_ref, group, dk)
    s = _dot_nt(k_ref[...], q)
    p = jnp.exp2(s - jnp.max(s, axis=0, keepdims=True))
    l = jnp.sum(p, axis=0, keepdims=True)
    _unstack_heads(_dot(vt_ref[...], p.astype(BF16)) * (1.0 / l), o_ref, group)


def _ctx_attention(q, k, vt, *, group, dk, n_lat, layer):
    b, n, _ = q.shape
    n_ctx = n - n_lat
    n_kv = k.shape[2] // dk
    slab = vt.shape[3]
    assert slab == n_ctx
    t = n_lat // n_ctx
    return pl.pallas_call(
        functools.partial(_ctx_attn_kernel, group=group, dk=dk),
        grid=(b, n_kv),
        in_specs=[
            pl.BlockSpec((None, n_ctx, group * dk), lambda bi, h: (bi, t, h)),
            pl.BlockSpec((None, n_ctx, dk), lambda bi, h: (bi, t, h)),
            pl.BlockSpec((None, None, LANES, slab), lambda bi, h: (bi, t, h, 0)),
        ],
        out_specs=pl.BlockSpec((None, n_ctx, group * LANES), lambda bi, h: (bi, 0, h)),
        out_shape=jax.ShapeDtypeStruct((b, n_ctx, n_kv * group * LANES), BF16),
        compiler_params=_params("parallel", "parallel"),
        name="ctx_attn_" + str(layer),
    )(q, k, vt)


def _split_heads(q):
    lane = lax.broadcasted_iota(jnp.int32, (1, LANES), 1)
    zero = jnp.zeros_like(q)
    return jnp.concatenate(
        [jnp.where((lane >= j * NA_HEAD_DIM) & (lane < (j + 1) * NA_HEAD_DIM), q, zero)
         for j in range(LANES // NA_HEAD_DIM)], axis=0)


def _merge_heads(o_t, nq):
    parts = [o_t[j * NA_HEAD_DIM:(j + 1) * NA_HEAD_DIM, j * nq:(j + 1) * nq]
             for j in range(LANES // NA_HEAD_DIM)]
    return jnp.concatenate(parts, axis=0).T


def _na_ctx_kernel(q_ref, k_ref, vt_ref, o_ref):
    q2 = _split_heads(q_ref[...])
    s = _dot_nt(k_ref[...], q2)
    p = jnp.exp2(s - jnp.max(s, axis=0, keepdims=True))
    l = jnp.sum(p, axis=0, keepdims=True)
    o_t = _dot(vt_ref[...], p.astype(BF16)) * (1.0 / l)
    o_ref[...] = _merge_heads(o_t, q_ref.shape[0]).astype(BF16)


def _na_ctx_attention(q, k, vt, *, n_lat, layer):
    b, n, width = q.shape
    n_ctx = n - n_lat
    assert vt.shape[3] == n_ctx
    t = n_lat // n_ctx
    blk = pl.BlockSpec((None, n_ctx, LANES), lambda bi, h: (bi, t, h))
    return pl.pallas_call(
        _na_ctx_kernel,
        grid=(b, width // LANES),
        in_specs=[blk, blk, pl.BlockSpec((None, None, LANES, n_ctx), lambda bi, h: (bi, t, h, 0))],
        out_specs=pl.BlockSpec((None, n_ctx, LANES), lambda bi, h: (bi, 0, h)),
        out_shape=jax.ShapeDtypeStruct((b, n_ctx, width), BF16),
        compiler_params=_params("parallel", "parallel"),
        name="ctx_attn_" + str(layer),
    )(q, k, vt)


def _na_window_start(blk, rows, lib):
    lo = lib.minimum(lib.maximum(NA_BLOCK_ROWS * blk - NA_KH // 2, 0), rows - NA_KH)
    return lib.minimum(lo, rows - NA_WIN_ROWS)


def _na_kernel(q_ref, k_ref, vt_ref, bias_ref, o_ref, sl_ref, sc_ref, *, n_lat, n_ctx, rows):
    slab = vt_ref.shape[-1]
    nq = NA_BLOCK_ROWS * GRID_W
    n_blocks = rows // NA_BLOCK_ROWS
    nk = NA_WIN_ROWS * GRID_W
    kc = k_ref[n_lat:n_lat + n_ctx, :]
    n_here = q_ref.shape[0] // nq

    def scores(i):
        blk = pl.program_id(2) * NA_STEP_BLOCKS + i
        kind = jnp.where(blk == 0, 1, jnp.where(blk == n_blocks - 1, 2, 0))
        ws = _na_window_start(blk, rows, jnp)
        q2 = _split_heads(q_ref[i * nq:(i + 1) * nq, :])
        s_loc = _dot_nt(k_ref[pl.ds(pl.multiple_of(ws * GRID_W, slab), nk), :], q2) + bias_ref[kind]
        s_ctx = _dot_nt(kc, q2)
        sl_ref[i % 2, :, :s_loc.shape[1]] = s_loc
        sc_ref[i % 2, :, :s_ctx.shape[1]] = s_ctx
        return ws, jnp.maximum(jnp.max(s_loc, axis=0, keepdims=True), jnp.max(s_ctx, axis=0, keepdims=True))

    nxt = scores(0)
    for i in range(n_here):
        ws, m = nxt
        if i + 1 < n_here:
            nxt = scores(i + 1)
        parts = []
        for j in range(LANES // NA_HEAD_DIM):
            t = slice(j * nq, (j + 1) * nq)
            p_loc = jnp.exp2(sl_ref[i % 2, :, t] - m[:, t]).astype(BF16)
            p_ctx = jnp.exp2(sc_ref[i % 2, :, t] - m[:, t]).astype(BF16)
            o_t = _pv_t(vt_ref, ws * GRID_W // slab, p_loc, SUM_ROWS)
            o_t = o_t + _pv_t(vt_ref, n_lat // slab, p_ctx, SUM_ROWS)
            parts.append(o_t[j * NA_HEAD_DIM:(j + 1) * NA_HEAD_DIM, :] * (1.0 / o_t[LANES:LANES + 1, :]))
        o_ref[i * nq:(i + 1) * nq, :] = jnp.concatenate(parts, axis=0).T.astype(BF16)


def _na_bounded_kernel(q_ref, k_ref, vt_ref, bias_ref, bound_ref, o_ref, *, n_lat, n_ctx, rows):
    slab = vt_ref.shape[-1]
    nq = NA_BLOCK_ROWS * GRID_W
    n_blocks = rows // NA_BLOCK_ROWS
    nk = NA_WIN_ROWS * GRID_W
    kc = k_ref[n_lat:n_lat + n_ctx, :]
    bound = bound_ref[:, :1]
    for i in range(q_ref.shape[0] // nq):
        blk = pl.program_id(2) * NA_STEP_BLOCKS + i
        kind = jnp.where(blk == 0, 1, jnp.where(blk == n_blocks - 1, 2, 0))
        ws = _na_window_start(blk, rows, jnp)
        q2 = _split_heads(q_ref[i * nq:(i + 1) * nq, :])
        kw = k_ref[pl.ds(pl.multiple_of(ws * GRID_W, slab), nk), :]
        p_loc = jnp.exp2(_dot_nt(kw, q2) + (bias_ref[kind] - bound)).astype(BF16)
        p_ctx = jnp.exp2(_dot_nt(kc, q2) - bound).astype(BF16)
        o_t = _pv_t(vt_ref, ws * GRID_W // slab, p_loc, SUM_ROWS)
        o_t = o_t + _pv_t(vt_ref, n_lat // slab, p_ctx, SUM_ROWS)
        o_t = o_t[:LANES, :] * (1.0 / o_t[LANES:LANES + 1, :])
        o_ref[i * nq:(i + 1) * nq, :] = _merge_heads(o_t, nq).astype(BF16)


def _na_bias(rpb, rows):
    h, n_dr, n_dc = rpb.shape
    n_blocks = rows // NA_BLOCK_ROWS
    lead = GRID_W - NA_KW
    wide = 2 * GRID_W
    w = jnp.pad(rpb * LOG2E, ((0, 0), (0, 0), (lead, wide - lead - n_dc)), constant_values=MASK_VALUE)
    flat = jnp.broadcast_to(w[:, :, None, :], (h, n_dr, GRID_W, wide)).reshape(h, n_dr, GRID_W * wide)
    skew = flat[:, :, GRID_W - 1:GRID_W - 1 + GRID_W * (wide - 1)].reshape(h, n_dr, GRID_W, wide - 1)
    qc = jnp.arange(GRID_W)
    cs = jnp.clip(qc - NA_KW // 2, 0, GRID_W - NA_KW)
    col_ok = (qc[None, :] >= cs[:, None]) & (qc[None, :] < cs[:, None] + NA_KW)
    c_t = jnp.swapaxes(jnp.where(col_ok, skew[..., :GRID_W], MASK_VALUE), -1, -2)
    c_ext = jnp.concatenate([c_t, jnp.full((h, 1, GRID_W, GRID_W), MASK_VALUE, F32)], axis=1)
    idx = np.full((3, NA_WIN_ROWS, NA_BLOCK_ROWS), n_dr, np.int32)
    for kind, blk in enumerate((1, 0, n_blocks - 1)):
        ws = _na_window_start(blk, rows, np)
        for j in range(NA_WIN_ROWS):
            for i in range(NA_BLOCK_ROWS):
                r = NA_BLOCK_ROWS * blk + i
                rs = min(max(r - NA_KH // 2, 0), rows - NA_KH)
                if rs <= ws + j < rs + NA_KH:
                    idx[kind, j, i] = ws + j - r + NA_KH - 1
    sub = LANES // NA_HEAD_DIM
    blocks = c_ext[:, idx].reshape(h // sub, sub, 3, NA_WIN_ROWS, NA_BLOCK_ROWS, GRID_W, GRID_W)
    bias = blocks.transpose(0, 2, 3, 5, 1, 4, 6)
    return bias.reshape(h // sub, 3, NA_WIN_ROWS * GRID_W, sub * NA_BLOCK_ROWS * GRID_W)


def _na_attention(q, k, vt, bias, bound, *, n_lat, layer):
    b, n, width = q.shape
    n_ctx = n - n_lat
    rows = n_lat // GRID_W
    n_slab, slab = vt.shape[1], vt.shape[3]
    tq = NA_STEP_BLOCKS * NA_BLOCK_ROWS * GRID_W
    in_specs = [
        pl.BlockSpec((None, tq, LANES), lambda hp, bi, rb: (bi, rb, hp)),
        pl.BlockSpec((None, n, LANES), lambda hp, bi, rb: (bi, 0, hp)),
        pl.BlockSpec((None, n_slab, LANES, slab), lambda hp, bi, rb: (bi, 0, hp, 0)),
        pl.BlockSpec((None,) + bias.shape[1:], lambda hp, bi, rb: (hp, 0, 0, 0)),
    ]
    common = dict(
        grid=(width // LANES, b, n_lat // tq),
        out_specs=pl.BlockSpec((None, tq, LANES), lambda hp, bi, rb: (bi, rb, hp)),
        out_shape=jax.ShapeDtypeStruct((b, n_lat, width), BF16),
        compiler_params=_params("parallel", "parallel", "arbitrary"),
    )

    def general():
        return pl.pallas_call(
            functools.partial(_na_kernel, n_lat=n_lat, n_ctx=n_ctx, rows=rows),
            in_specs=in_specs,
            scratch_shapes=[pltpu.VMEM((2, bias.shape[2], bias.shape[3] + SKEW_LANES), F32),
                            pltpu.VMEM((2, n_ctx, bias.shape[3] + SKEW_LANES), F32)],
            name="na_attn_" + str(layer), **common,
        )(q, k, vt, bias)

    def bounded():
        return pl.pallas_call(
            functools.partial(_na_bounded_kernel, n_lat=n_lat, n_ctx=n_ctx, rows=rows),
            in_specs=in_specs + [pl.BlockSpec((1, LANES), lambda hp, bi, rb: (0, 0))],
            name="na_attn_bounded_" + str(layer), **common,
        )(q, k, vt, bias, jnp.full((1, LANES), bound, F32))

    return lax.cond(bound <= SAFE_SCORE_BOUND, bounded, general)


def _out_proj_final_kernel(o_ref, g_ref, w_ref, x_ref, mod_ref, fg_ref, y_ref):
    a = (o_ref[...].astype(F32) * g_ref[...].astype(F32)).astype(BF16)
    xn = x_ref[...] + mod_ref[0, 2:3, :] * _dot(a, w_ref[...])
    y_ref[...] = xn * _rms(xn) * fg_ref[...]


def _out_proj_final(o, gs, w_out, x_lat, mod3, layer, n_lat, final_g):
    b, _, d = x_lat.shape
    tm = min(FINAL_TILE, n_lat)
    row = lambda bi, t: (bi, t, 0)
    w = o.shape[2]
    return pl.pallas_call(
        _out_proj_final_kernel,
        grid=(b, n_lat // tm),
        in_specs=[pl.BlockSpec((None, tm, w), row), pl.BlockSpec((None, tm, w), row),
                  pl.BlockSpec(w_out.shape, lambda bi, t: (0, 0)), pl.BlockSpec((None, tm, d), row),
                  pl.BlockSpec((1, 3, d), lambda bi, t: (layer * MOD_ROWS + bi, 0, 0)),
                  pl.BlockSpec((1, d), lambda bi, t: (0, 0))],
        out_specs=pl.BlockSpec((None, tm, d), row),
        out_shape=jax.ShapeDtypeStruct((b, n_lat, d), F32),
        compiler_params=_params("parallel", "parallel"),
        name="out_proj_final",
    )(o, gs, w_out, x_lat, mod3, final_g.reshape(1, d))


def _rope_tables(n_lat, n_ctx, rot_dim):
    n_freq = rot_dim // 4
    inv = ROPE_THETA ** (-jnp.arange(n_freq, dtype=F32) / n_freq)
    t = jnp.arange(n_lat)
    ang_r = (t // GRID_W).astype(F32)[:, None] * inv
    ang_c = (t % GRID_W).astype(F32)[:, None] * inv
    ang = jnp.concatenate([ang_r, ang_r, ang_c, ang_c], axis=-1)
    cos, sin = jnp.cos(ang), jnp.sin(ang)
    first = (jnp.arange(rot_dim) % (2 * n_freq)) < n_freq
    sin_up = jnp.where(first, -sin, 0.0)
    sin_dn = jnp.where(first, 0.0, sin)
    pad = LANES - rot_dim

    def finish(tab, fill):
        tab = jnp.pad(tab, ((0, 0), (0, pad)), constant_values=fill)
        return jnp.pad(tab, ((0, n_ctx), (0, 0)), constant_values=fill)

    return finish(cos, 1.0), finish(sin_up, 0.0), finish(sin_dn, 0.0)


def _mla_weights(w_in, w_uq, w_ukv):
    o_kv = MLA_Q_LORA
    o_kr = o_kv + MLA_KV_LORA
    o_g = o_kr + MLA_ROPE
    k_r = jnp.pad(w_in[:, o_kr:o_g], ((0, 0), (0, LANES - MLA_ROPE)))
    w_perm = jnp.concatenate([w_in[:, :o_kr], w_in[:, o_g:], k_r], axis=1).astype(BF16)
    uq = w_uq.reshape(MLA_Q_LORA, MLA_HEADS, MLA_NOPE + MLA_ROPE)
    uq = jnp.pad(uq, ((0, 0), (0, 0), (0, MLA_QK_PAD - MLA_NOPE - MLA_ROPE)))
    uq = uq.reshape(MLA_Q_LORA, MLA_HEADS * MLA_QK_PAD).astype(BF16)
    ukv = w_ukv.reshape(MLA_KV_LORA, MLA_HEADS, MLA_NOPE + MLA_V).astype(BF16)
    uk = ukv[:, :, :MLA_NOPE].reshape(MLA_KV_LORA, MLA_HEADS * MLA_NOPE)
    uv = ukv[:, :, MLA_NOPE:].reshape(MLA_KV_LORA, MLA_HEADS * MLA_V)
    return w_perm, uq, uk, uv


def kernel(x, c, ctx, c_ctx, mod_w, mod_b, norm_g, final_g, ga_w_in, ga_q_g, ga_k_g, ga_w_out, na_w_in, na_rpb, na_w_out, mla_w_in, mla_q_g, mla_kv_g, mla_w_uq, mla_w_ukv, mla_w_out):
    b, n_lat, d = x.shape
    n_ctx = ctx.shape[1]
    depth = mod_w.shape[0]
    assert n_lat % ROW_TILE == 0 and n_ctx == ROW_TILE and n_lat % n_ctx == 0
    assert b < MOD_ROWS and n_lat % (NA_STEP_BLOCKS * NA_BLOCK_ROWS * GRID_W) == 0
    assert n_lat // GRID_W >= NA_WIN_ROWS + NA_BLOCK_ROWS

    stream = (x, ctx, 0)
    cc = jnp.concatenate([c, c_ctx[None, :], jnp.zeros((MOD_ROWS - b - 1, d), F32)], axis=0)
    mod3 = _modulation(cc, mod_w, mod_b).reshape(depth * MOD_ROWS, 3, d)

    tab_a = _rope_tables(n_lat, n_ctx, GQA_HEAD_DIM)
    tab_m = _rope_tables(n_lat, n_ctx, MLA_ROPE)

    prev = None
    for i in range(depth):
        kind, j = i % N_MIXERS, i // N_MIXERS
        need_ctx = i < depth - 1

        def project(kern, consts, tables, widths, with_norms=False):
            outs = _project(kern, stream, mod3, i, norm_g[i], consts, tables, widths, n_lat, with_norms, prev)
            if prev is None:
                return stream, outs
            return (outs[0], outs[0], n_lat // ROW_TILE), outs[1:]

        if kind == 0:
            n_q = d // GQA_HEAD_DIM
            n_kv = n_q // GQA_GROUP
            kern = functools.partial(_gqa_proj_kernel, n_q=n_q, n_kv=n_kv, scale=GQA_HEAD_DIM ** -0.5 * LOG2E)
            consts = [ga_w_in[j].astype(BF16), ga_q_g[j].reshape(1, -1), ga_k_g[j].reshape(1, -1)]
            kvw = n_kv * GQA_HEAD_DIM
            stream, (q, k, vt, gs) = project(kern, consts, tab_a, [d, kvw, kvw, d])
            bound = (BOUND_SLACK * GQA_HEAD_DIM ** 0.5 * LOG2E
                     * jnp.max(jnp.abs(ga_q_g[j])) * jnp.max(jnp.abs(ga_k_g[j])))
            o = _flash_attention(q, k, vt, group=GQA_GROUP, dk=GQA_HEAD_DIM, n_lat=n_lat, layer=i, bound=bound)
            if need_ctx:
                oc = _ctx_attention(q, k, vt, group=GQA_GROUP, dk=GQA_HEAD_DIM, n_lat=n_lat, layer=i)
            w_out = ga_w_out[j]
        elif kind == 1:
            kern = functools.partial(_na_proj_kernel, width=d, scale=NA_HEAD_DIM ** -0.5 * LOG2E)
            stream, (q, k, vt, gs, norms) = project(kern, [na_w_in[j].astype(BF16)], (), [d, d, d, d], True)
            bound = (BOUND_SLACK * jnp.sqrt(jnp.max(norms[:, :, 0, :]) * jnp.max(norms[:, :, MOD_ROWS, :]))
                     + LOG2E * jnp.max(jnp.abs(na_rpb[j])))
            o = _na_attention(q, k, vt, _na_bias(na_rpb[j], n_lat // GRID_W), bound, n_lat=n_lat, layer=i)
            if need_ctx:
                oc = _na_ctx_attention(q, k, vt, n_lat=n_lat, layer=i)
            w_out = na_w_out[j]
        else:
            kern = functools.partial(_mla_proj_kernel, width=d, scale=(MLA_NOPE + MLA_ROPE) ** -0.5 * LOG2E)
            w_perm, uq, uk, uv = _mla_weights(mla_w_in[j], mla_w_uq[j], mla_w_ukv[j])
            consts = [w_perm, mla_q_g[j].reshape(1, -1), mla_kv_g[j].reshape(1, -1), uq, uk, uv]
            widths = [MLA_HEADS * MLA_QK_PAD, MLA_HEADS * MLA_QK_PAD, MLA_HEADS * MLA_V, d]
            stream, (q, k, vt, gs, norms) = project(kern, consts, tab_m, widths, True)
            bound = BOUND_SLACK * jnp.sqrt(jnp.max(norms[:, :, 0, 0]) * jnp.max(norms[:, :, MOD_ROWS, 0]))
            o = _flash_attention(q, k, vt, group=1, dk=MLA_QK_PAD, n_lat=n_lat, layer=i, bound=bound)
            if need_ctx:
                oc = _ctx_attention(q, k, vt, group=1, dk=MLA_QK_PAD, n_lat=n_lat, layer=i)
            w_out = mla_w_out[j]
        if need_ctx:
            prev = (o, oc, gs, w_out.astype(BF16), i)
    return _out_proj_final(o, gs, w_out.astype(BF16), stream[0], mod3, depth - 1, n_lat, final_g)
```

```python
import functools

import jax
import numpy as np
import jax.numpy as jnp
from jax import lax
from jax.experimental import pallas as pl
from jax.experimental.pallas import tpu as pltpu

F32 = jnp.float32
BF16 = jnp.bfloat16

NORM_EPS = 1e-6
ROPE_THETA = 10000.0
GRID_W = 64
N_MIXERS = 3
LANES = 128
MOD_ROWS = 8
MASK_VALUE = -1e30
LOG2E = 1.4426950408889634

GQA_HEAD_DIM = 128
GQA_GROUP = 4
NA_HEAD_DIM = 64
NA_KH = 8
NA_KW = 16
NA_BLOCK_ROWS = 4
NA_WIN_ROWS = 12
NA_STEP_BLOCKS = 8
MLA_HEADS = 8
MLA_Q_LORA = 512
MLA_KV_LORA = 256
MLA_NOPE = 128
MLA_ROPE = 64
MLA_V = 128
MLA_QK_PAD = 256

ROW_TILE = 256
FINAL_TILE = 512
Q_COLS = 4096
Q_COLS_GENERAL = 1024
Q_TILE = 256
SUM_ROWS = 16
SKEW_LANES = 128
SAFE_SCORE_BOUND = 50.0
BOUNDED_UNROLL = 5
BOUND_SLACK = 1.02
KV_CHUNKS = (768, 512, 256)
VMEM_LIMIT = 48 * 1024 * 1024


def _params(*sem):
    return pltpu.CompilerParams(dimension_semantics=sem, vmem_limit_bytes=VMEM_LIMIT)


def _silu(v):
    return v * (1.0 / (1.0 + jnp.exp(-v)))


def _rms(v):
    return lax.rsqrt(jnp.mean(v * v, axis=-1, keepdims=True) + NORM_EPS)


def _rope(v, cos, sin_up, sin_dn, half):
    w = v.shape[-1]
    return v * cos + pltpu.roll(v, w - half, 1) * sin_up + pltpu.roll(v, half, 1) * sin_dn


def _dot(a, b):
    return jnp.dot(a, b, preferred_element_type=F32)


def _dot_nt(a, b):
    return lax.dot_general(a, b, (((1,), (1,)), ((), ())), preferred_element_type=F32)


def _mod_kernel(c_ref, w_ref, b_ref, o_ref):
    a = _silu(c_ref[...])
    w = w_ref[0]
    a_hi = a.astype(BF16)
    a_lo = (a - a_hi.astype(F32)).astype(BF16)
    w_hi = w.astype(BF16)
    w_lo = (w - w_hi.astype(F32)).astype(BF16)
    o_ref[0] = _dot(a_hi, w_hi) + _dot(a_lo, w_hi) + _dot(a_hi, w_lo) + b_ref[0]


def _modulation(cc, mod_w, mod_b):
    depth, d, n3 = mod_w.shape
    tn = 1024
    return pl.pallas_call(
        _mod_kernel,
        grid=(depth, n3 // tn),
        in_specs=[
            pl.BlockSpec((MOD_ROWS, d), lambda l, j: (0, 0)),
            pl.BlockSpec((1, d, tn), lambda l, j: (l, 0, j)),
            pl.BlockSpec((1, 1, tn), lambda l, j: (l, 0, j)),
        ],
        out_specs=pl.BlockSpec((1, MOD_ROWS, tn), lambda l, j: (l, 0, j)),
        out_shape=jax.ShapeDtypeStruct((depth, MOD_ROWS, n3), F32),
        compiler_params=_params("parallel", "parallel"),
        name="adaln_modulation",
    )(cc, mod_w, mod_b.reshape(depth, 1, n3))


def _modulated_norm(x_ref, xc_ref, ng_ref, mod_ref, n_lat_tiles):
    x = jnp.where(pl.program_id(1) >= n_lat_tiles, xc_ref[...], x_ref[...])
    shift = mod_ref[0, 0:1, :]
    scale = mod_ref[0, 1:2, :]
    return ((x * _rms(x)) * ng_ref[...] * (1.0 + scale) + shift).astype(BF16)


def _gqa_proj_kernel(x_ref, xc_ref, mod_ref, ng_ref, w_ref, qg_ref, kg_ref, cos_ref, su_ref, sd_ref,
                     q_ref, k_ref, vt_ref, g_ref, *, n_lat_tiles, n_q, n_kv, scale):
    h = _modulated_norm(x_ref, xc_ref, ng_ref, mod_ref, n_lat_tiles)
    cos, su, sd = cos_ref[...], su_ref[...], sd_ref[...]
    hd = GQA_HEAD_DIM
    qw, kw = n_q * hd, n_kv * hd
    q = _dot(h, w_ref[:, :qw])
    for i in range(n_q):
        qh = q[:, i * hd:(i + 1) * hd]
        qh = qh * _rms(qh) * qg_ref[...]
        q_ref[:, i * hd:(i + 1) * hd] = (_rope(qh, cos, su, sd, hd // 4) * scale).astype(BF16)
    k = _dot(h, w_ref[:, qw:qw + kw])
    for i in range(n_kv):
        kh = k[:, i * hd:(i + 1) * hd]
        kh = kh * _rms(kh) * kg_ref[...]
        k_ref[:, i * hd:(i + 1) * hd] = _rope(kh, cos, su, sd, hd // 4).astype(BF16)
    v = _dot(h, w_ref[:, qw + kw:qw + 2 * kw])
    for i in range(n_kv):
        vt_ref[i * hd:(i + 1) * hd, :] = v[:, i * hd:(i + 1) * hd].T.astype(BF16)
    g_ref[...] = _silu(_dot(h, w_ref[:, qw + 2 * kw:])).astype(BF16)


def _na_proj_kernel(x_ref, xc_ref, mod_ref, ng_ref, w_ref, q_ref, k_ref, vt_ref, g_ref, norm_ref,
                    *, n_lat_tiles, width, scale):
    h = _modulated_norm(x_ref, xc_ref, ng_ref, mod_ref, n_lat_tiles)
    head_of_col = lax.broadcasted_iota(jnp.int32, (width, LANES), 0) // NA_HEAD_DIM
    lane = lax.broadcasted_iota(jnp.int32, (width, LANES), 1)
    pick = jnp.where(head_of_col == lane, 1.0, 0.0).astype(BF16)
    half = norm_ref.shape[0] // 2
    q = _dot(h, w_ref[:, :width]) * scale
    q_ref[...] = q.astype(BF16)
    q_sq = jnp.max(_dot((q * q).astype(BF16), pick), axis=0, keepdims=True)
    norm_ref[:half, :] = jnp.broadcast_to(q_sq, (half, LANES))
    k = _dot(h, w_ref[:, width:2 * width])
    k_ref[...] = k.astype(BF16)
    k_sq = jnp.max(_dot((k * k).astype(BF16), pick), axis=0, keepdims=True)
    norm_ref[half:, :] = jnp.broadcast_to(k_sq, (half, LANES))
    v = _dot(h, w_ref[:, 2 * width:3 * width])
    for i in range(width // LANES):
        vt_ref[i * LANES:(i + 1) * LANES, :] = v[:, i * LANES:(i + 1) * LANES].T.astype(BF16)
    g_ref[...] = _silu(_dot(h, w_ref[:, 3 * width:])).astype(BF16)


def _mla_proj_kernel(x_ref, xc_ref, mod_ref, ng_ref, w_ref, qg_ref, kvg_ref, wuq_ref, wuk_ref, wuv_ref,
                     cos_ref, su_ref, sd_ref, q_ref, k_ref, vt_ref, g_ref, norm_ref,
                     *, n_lat_tiles, width, scale):
    h = _modulated_norm(x_ref, xc_ref, ng_ref, mod_ref, n_lat_tiles)
    ones = jnp.ones((LANES, LANES), BF16)

    def row_sq(v):
        return _dot((v * v).astype(BF16), ones)

    cos, su, sd = cos_ref[...], su_ref[...], sd_ref[...]
    o_kv = MLA_Q_LORA
    o_g = o_kv + MLA_KV_LORA
    o_kr = o_g + width
    c_q = _dot(h, w_ref[:, :o_kv])
    c_q = (c_q * _rms(c_q) * qg_ref[...]).astype(BF16)
    q = _dot(c_q, wuq_ref[...])
    q_sq = None
    for i in range(MLA_HEADS):
        a = i * MLA_QK_PAD
        q_nope = q[:, a:a + MLA_NOPE] * scale
        q_rope = _rope(q[:, a + MLA_NOPE:a + MLA_QK_PAD], cos, su, sd, MLA_ROPE // 4) * scale
        q_ref[:, a:a + MLA_NOPE] = q_nope.astype(BF16)
        q_ref[:, a + MLA_NOPE:a + MLA_QK_PAD] = q_rope.astype(BF16)
        sq = row_sq(q_nope) + row_sq(q_rope)
        q_sq = sq if q_sq is None else jnp.maximum(q_sq, sq)
    c_kv = _dot(h, w_ref[:, o_kv:o_g])
    c_kv = (c_kv * _rms(c_kv) * kvg_ref[...]).astype(BF16)
    k_r = _rope(_dot(h, w_ref[:, o_kr:]), cos, su, sd, MLA_ROPE // 4)
    k_nope = _dot(c_kv, wuk_ref[...])
    k_sq = None
    for i in range(MLA_HEADS):
        a = i * MLA_QK_PAD
        k_h = k_nope[:, i * MLA_NOPE:(i + 1) * MLA_NOPE]
        k_ref[:, a:a + MLA_NOPE] = k_h.astype(BF16)
        k_ref[:, a + MLA_NOPE:a + MLA_QK_PAD] = k_r.astype(BF16)
        sq = row_sq(k_h)
        k_sq = sq if k_sq is None else jnp.maximum(k_sq, sq)
    k_sq = k_sq + row_sq(k_r)
    half = norm_ref.shape[0] // 2
    norm_ref[:half, :] = jnp.broadcast_to(jnp.max(q_sq, axis=0, keepdims=True), (half, LANES))
    norm_ref[half:, :] = jnp.broadcast_to(jnp.max(k_sq, axis=0, keepdims=True), (half, LANES))
    v = _dot(c_kv, wuv_ref[...])
    for i in range(MLA_HEADS):
        vt_ref[i * MLA_V:(i + 1) * MLA_V, :] = v[:, i * MLA_V:(i + 1) * MLA_V].T.astype(BF16)
    g_ref[...] = _silu(_dot(h, w_ref[:, o_g:o_kr])).astype(BF16)


class _Loaded:
    def __init__(self, value):
        self.value = value

    def __getitem__(self, _):
        return self.value


def _fused_kernel(*refs, proj, split, n_in, n_lat_tiles):
    ins, outs = refs[:n_in], refs[n_in:]
    o_ref, oc_ref, g_ref, w_ref, x_ref = ins[:5]
    rest = ins[5:]
    is_ctx = pl.program_id(1) >= n_lat_tiles
    x = x_ref[...]
    if split:
        x = jnp.where(is_ctx, rest[0][...], x)
        rest = rest[1:]
    o = jnp.where(is_ctx, oc_ref[...], o_ref[...])
    a = (o.astype(F32) * g_ref[...].astype(F32)).astype(BF16)
    x = x + rest[0][0, 2:3, :] * _dot(a, w_ref[...])
    outs[0][...] = x
    x = _Loaded(x)
    proj(x, x, *rest[1:], *outs[1:], n_lat_tiles=n_lat_tiles)


def _project(kern, stream, mod3, layer, norm_g, consts, tables, out_widths, n_lat, with_norms=False, prev=None):
    x_lat, x_ctx, ctx_tile = stream
    b, _, d = x_lat.shape
    tm = ROW_TILE
    n_lat_tiles = n_lat // tm
    n = n_lat + tm
    ctx_row = b
    row = lambda bi, t: (bi, t, 0)
    lat_row = lambda bi, t: (bi, jnp.minimum(t, n_lat_tiles - 1), 0)
    const = lambda bi, t: (0, 0)

    def mod_spec(which):
        return pl.BlockSpec((1, 3, d), lambda bi, t: (which * MOD_ROWS + jnp.where(t < n_lat_tiles, bi, ctx_row), 0, 0))

    x_specs = [pl.BlockSpec((None, tm, d), lat_row), pl.BlockSpec((None, tm, d), lambda bi, t: (bi, ctx_tile, 0))]
    in_specs = [mod_spec(layer), pl.BlockSpec((1, d), const)]
    in_specs += [pl.BlockSpec(a.shape, const) for a in consts]
    in_specs += [pl.BlockSpec((tm, LANES), lambda bi, t: (t, 0)) for _ in tables]
    args = [mod3, norm_g.reshape(1, d), *consts, *tables]
    out_specs = [pl.BlockSpec((None, tm, w), row) for w in out_widths]
    out_shape = [jax.ShapeDtypeStruct((b, n, w), BF16) for w in out_widths]
    out_specs[2] = pl.BlockSpec((None, None, out_widths[2], tm), lambda bi, t: (bi, t, 0, 0))
    out_shape[2] = jax.ShapeDtypeStruct((b, n // tm, out_widths[2], tm), BF16)
    if with_norms:
        out_specs.append(pl.BlockSpec((None, None, 2 * MOD_ROWS, LANES), lambda bi, t: (bi, t, 0, 0)))
        out_shape.append(jax.ShapeDtypeStruct((b, n // tm, 2 * MOD_ROWS, LANES), F32))
    call = dict(grid=(b, n // tm), compiler_params=_params("parallel", "parallel"), name="proj_" + str(layer))
    if prev is None:
        return pl.pallas_call(
            functools.partial(kern, n_lat_tiles=n_lat_tiles),
            in_specs=x_specs + in_specs, out_specs=out_specs, out_shape=out_shape, **call,
        )(x_lat, x_ctx, *args)
    o, o_ctx, gs, w_out, prev_layer = prev
    combined = x_lat is x_ctx
    wo = o.shape[2]
    pre_specs = [pl.BlockSpec((None, tm, wo), lat_row), pl.BlockSpec((None, tm, wo), lambda bi, t: (bi, 0, 0)),
                 pl.BlockSpec((None, tm, wo), row), pl.BlockSpec(w_out.shape, const)]
    if combined:
        x_specs, xs = [pl.BlockSpec((None, tm, d), row)], [x_lat]
    else:
        xs = [x_lat, x_ctx]
    in_specs = pre_specs + x_specs + [mod_spec(prev_layer)] + in_specs
    return pl.pallas_call(
        functools.partial(_fused_kernel, proj=kern, split=not combined, n_in=len(in_specs), n_lat_tiles=n_lat_tiles),
        in_specs=in_specs,
        out_specs=[pl.BlockSpec((None, tm, d), row)] + out_specs,
        out_shape=[jax.ShapeDtypeStruct((b, n, d), F32)] + out_shape,
        input_output_aliases={4: 0} if combined else {},
        **call,
    )(o, o_ctx, gs, w_out, *xs, mod3, *args)


def _stack_heads(q_ref, group, dk):
    return jnp.concatenate([q_ref[:, g * dk:(g + 1) * dk] for g in range(group)], axis=0)


def _unstack_heads(o_t, o_ref, group):
    tq = o_ref.shape[0]
    for g in range(group):
        o_ref[:, g * LANES:(g + 1) * LANES] = o_t[:, g * tq:(g + 1) * tq].T.astype(BF16)


def _pv_t(vt_ref, first_slab, p, sum_rows=0):
    slab = vt_ref.shape[-1]
    acc = None
    for i in range(p.shape[0] // slab):
        vt = vt_ref[first_slab + i]
        if sum_rows:
            vt = jnp.concatenate([vt, jnp.ones((sum_rows, slab), vt.dtype)], axis=0)
        t = _dot(vt, p[i * slab:(i + 1) * slab, :])
        acc = t if acc is None else acc + t
    return acc


def _flash_kernel(q_ref, k_ref, vt_ref, o_ref, sa_ref, sb_ref, acc_ref, m_ref, *, group, dk, bk):
    slab = vt_ref.shape[-1]
    n_chunks = k_ref.shape[0] // bk
    q = _stack_heads(q_ref, group, dk)
    tiles = [slice(j * Q_TILE, (j + 1) * Q_TILE) for j in range(q.shape[0] // Q_TILE)]

    def scores(c, s_ref):
        k = k_ref[pl.ds(pl.multiple_of(c * bk, bk), bk), :]
        out = []
        for t in tiles:
            s = _dot_nt(k, q[t, :])
            s_ref[:, t] = s
            out.append(jnp.max(s, axis=0, keepdims=True))
        return out

    def update(s_ref, s_max, c, t):
        m = m_ref[:, t]
        m_new = jnp.maximum(m, s_max)
        alpha = jnp.exp2(m - m_new)
        p = jnp.exp2(s_ref[:, t] - m_new).astype(BF16)
        m_ref[:, t] = m_new
        acc_ref[:, t] = alpha * acc_ref[:, t] + _pv_t(vt_ref, c * (bk // slab), p, SUM_ROWS)

    def step(c, cur_ref, cur_max, nxt_ref):
        k = k_ref[pl.ds(pl.multiple_of((c + 1) * bk, bk), bk), :]
        nxt_max = []
        for j, t in enumerate(tiles):
            s = _dot_nt(k, q[t, :])
            nxt_ref[:, t] = s
            nxt_max.append(jnp.max(s, axis=0, keepdims=True))
            update(cur_ref, cur_max[j], c, t)
        return nxt_max

    m_ref[...] = jnp.full(m_ref.shape, MASK_VALUE, F32)
    acc_ref[...] = jnp.zeros(acc_ref.shape, F32)
    max_a = scores(0, sa_ref)

    def body(i, max_a):
        max_b = step(2 * i, sa_ref, max_a, sb_ref)
        return tuple(step(2 * i + 1, sb_ref, max_b, sa_ref))

    pairs = (n_chunks - 1) // 2
    max_a = lax.fori_loop(0, pairs, body, tuple(max_a))
    last = 2 * pairs
    if n_chunks - last == 2:
        max_b = step(last, sa_ref, max_a, sb_ref)
        last, s_ref, s_max = last + 1, sb_ref, max_b
    else:
        s_ref, s_max = sa_ref, max_a
    for j, t in enumerate(tiles):
        update(s_ref, s_max[j], last, t)
    _unstack_heads(acc_ref[:LANES, :] * (1.0 / acc_ref[LANES:LANES + 1, :]), o_ref, group)


def _flash_bounded_kernel(q_ref, k_ref, vt_ref, bound_ref, o_ref, acc_ref, *, group, dk, bk):
    slab = vt_ref.shape[-1]
    n_chunks = k_ref.shape[0] // bk
    q = _stack_heads(q_ref, group, dk)
    tiles = [slice(0, q.shape[0])]
    bound = bound_ref[:, :1]
    acc_ref[...] = jnp.zeros(acc_ref.shape, F32)

    def chunk(c):
        k = k_ref[pl.ds(pl.multiple_of(c * bk, bk), bk), :]
        for t in tiles:
            p = jnp.exp2(_dot_nt(k, q[t, :]) - bound)
            acc_ref[:LANES, t] += _pv_t(vt_ref, c * (bk // slab), p.astype(BF16))
            acc_ref[LANES:, t] += jnp.sum(p.reshape(bk // MOD_ROWS, MOD_ROWS, p.shape[1]), axis=0)

    def body(i, carry):
        for u in range(BOUNDED_UNROLL):
            chunk(BOUNDED_UNROLL * i + u)
        return carry

    lax.fori_loop(0, n_chunks // BOUNDED_UNROLL, body, 0)
    for c in range(n_chunks - n_chunks % BOUNDED_UNROLL, n_chunks):
        chunk(c)
    l = jnp.sum(acc_ref[LANES:, :], axis=0, keepdims=True)
    _unstack_heads(acc_ref[:LANES, :] * (1.0 / l), o_ref, group)


def _flash_attention(q, k, vt, *, group, dk, n_lat, layer, bound=None):
    b, n, _ = q.shape
    n_kv = k.shape[2] // dk
    n_slab, slab = vt.shape[1], vt.shape[3]
    bk = max(c for c in KV_CHUNKS if n % c == 0)
    assert bk % slab == 0

    def call(kern, q_cols, name, scratch, extra_specs=(), extra_args=()):
        tq = min(q_cols // group, n_lat)
        return pl.pallas_call(
            functools.partial(kern, group=group, dk=dk, bk=bk),
            grid=(b, n_kv, n_lat // tq),
            in_specs=[
                pl.BlockSpec((None, tq, group * dk), lambda bi, h, t: (bi, t, h)),
                pl.BlockSpec((None, n, dk), lambda bi, h, t: (bi, 0, h)),
                pl.BlockSpec((None, n_slab, LANES, slab), lambda bi, h, t: (bi, 0, h, 0)),
                *extra_specs,
            ],
            out_specs=pl.BlockSpec((None, tq, group * LANES), lambda bi, h, t: (bi, t, h)),
            out_shape=jax.ShapeDtypeStruct((b, n_lat, n_kv * group * LANES), BF16),
            scratch_shapes=scratch(group * tq),
            compiler_params=_params("parallel", "parallel", "parallel"),
            name=name + str(layer),
        )(q, k, vt, *extra_args)

    def general():
        return call(_flash_kernel, Q_COLS_GENERAL, "flash_", lambda nq: [
            pltpu.VMEM((bk, nq + SKEW_LANES), F32), pltpu.VMEM((bk, nq + SKEW_LANES), F32),
            pltpu.VMEM((LANES + SUM_ROWS, nq), F32), pltpu.VMEM((1, nq), F32)])

    if bound is None:
        return general()

    def bounded():
        return call(_flash_bounded_kernel, Q_COLS, "flash_bounded_",
                    lambda nq: [pltpu.VMEM((LANES + MOD_ROWS, nq), F32)],
                    [pl.BlockSpec((1, LANES), lambda bi, h, t: (0, 0))], [jnp.full((1, LANES), bound, F32)])

    return lax.cond(bound <= SAFE_SCORE_BOUND, bounded, general)


def _ctx_attn_kernel(q_ref, k_ref, vt_ref, o_ref, *, group, dk):
    q = _stack_heads(q_ref, group, dk)
    s = _dot_nt(k_ref[...], q)
    p = jnp.exp2(s - jnp.max(s, axis=0, keepdims=True))
    l = jnp.sum(p, axis=0, keepdims=True)
    _unstack_heads(_dot(vt_ref[...], p.astype(BF16)) * (1.0 / l), o_ref, group)


def _ctx_attention(q, k, vt, *, group, dk, n_lat, layer):
    b, n, _ = q.shape
    n_ctx = n - n_lat
    n_kv = k.shape[2] // dk
    slab = vt.shape[3]
    assert slab == n_ctx
    t = n_lat // n_ctx
    return pl.pallas_call(
        functools.partial(_ctx_attn_kernel, group=group, dk=dk),
        grid=(b, n_kv),
        in_specs=[
            pl.BlockSpec((None, n_ctx, group * dk), lambda bi, h: (bi, t, h)),
            pl.BlockSpec((None, n_ctx, dk), lambda bi, h: (bi, t, h)),
            pl.BlockSpec((None, None, LANES, slab), lambda bi, h: (bi, t, h, 0)),
        ],
        out_specs=pl.BlockSpec((None, n_ctx, group * LANES), lambda bi, h: (bi, 0, h)),
        out_shape=jax.ShapeDtypeStruct((b, n_ctx, n_kv * group * LANES), BF16),
        compiler_params=_params("parallel", "parallel"),
        name="ctx_attn_" + str(layer),
    )(q, k, vt)


def _split_heads(q):
    lane = lax.broadcasted_iota(jnp.int32, (1, LANES), 1)
    zero = jnp.zeros_like(q)
    return jnp.concatenate(
        [jnp.where((lane >= j * NA_HEAD_DIM) & (lane < (j + 1) * NA_HEAD_DIM), q, zero)
         for j in range(LANES // NA_HEAD_DIM)], axis=0)


def _merge_heads(o_t, nq):
    parts = [o_t[j * NA_HEAD_DIM:(j + 1) * NA_HEAD_DIM, j * nq:(j + 1) * nq]
             for j in range(LANES // NA_HEAD_DIM)]
    return jnp.concatenate(parts, axis=0).T


def _na_ctx_kernel(q_ref, k_ref, vt_ref, o_ref):
    q2 = _split_heads(q_ref[...])
    s = _dot_nt(k_ref[...], q2)
    p = jnp.exp2(s - jnp.max(s, axis=0, keepdims=True))
    l = jnp.sum(p, axis=0, keepdims=True)
    o_t = _dot(vt_ref[...], p.astype(BF16)) * (1.0 / l)
    o_ref[...] = _merge_heads(o_t, q_ref.shape[0]).astype(BF16)


def _na_ctx_attention(q, k, vt, *, n_lat, layer):
    b, n, width = q.shape
    n_ctx = n - n_lat
    assert vt.shape[3] == n_ctx
    t = n_lat // n_ctx
    blk = pl.BlockSpec((None, n_ctx, LANES), lambda bi, h: (bi, t, h))
    return pl.pallas_call(
        _na_ctx_kernel,
        grid=(b, width // LANES),
        in_specs=[blk, blk, pl.BlockSpec((None, None, LANES, n_ctx), lambda bi, h: (bi, t, h, 0))],
        out_specs=pl.BlockSpec((None, n_ctx, LANES), lambda bi, h: (bi, 0, h)),
        out_shape=jax.ShapeDtypeStruct((b, n_ctx, width), BF16),
        compiler_params=_params("parallel", "parallel"),
        name="ctx_attn_" + str(layer),
    )(q, k, vt)


def _na_window_start(blk, rows, lib):
    lo = lib.minimum(lib.maximum(NA_BLOCK_ROWS * blk - NA_KH // 2, 0), rows - NA_KH)
    return lib.minimum(lo, rows - NA_WIN_ROWS)


def _na_kernel(q_ref, k_ref, vt_ref, bias_ref, o_ref, sl_ref, sc_ref, *, n_lat, n_ctx, rows):
    slab = vt_ref.shape[-1]
    nq = NA_BLOCK_ROWS * GRID_W
    n_blocks = rows // NA_BLOCK_ROWS
    nk = NA_WIN_ROWS * GRID_W
    kc = k_ref[n_lat:n_lat + n_ctx, :]
    n_here = q_ref.shape[0] // nq

    def scores(i):
        blk = pl.program_id(2) * NA_STEP_BLOCKS + i
        kind = jnp.where(blk == 0, 1, jnp.where(blk == n_blocks - 1, 2, 0))
        ws = _na_window_start(blk, rows, jnp)
        q2 = _split_heads(q_ref[i * nq:(i + 1) * nq, :])
        s_loc = _dot_nt(k_ref[pl.ds(pl.multiple_of(ws * GRID_W, slab), nk), :], q2) + bias_ref[kind]
        s_ctx = _dot_nt(kc, q2)
        sl_ref[i % 2, :, :s_loc.shape[1]] = s_loc
        sc_ref[i % 2, :, :s_ctx.shape[1]] = s_ctx
        return ws, jnp.maximum(jnp.max(s_loc, axis=0, keepdims=True), jnp.max(s_ctx, axis=0, keepdims=True))

    nxt = scores(0)
    for i in range(n_here):
        ws, m = nxt
        if i + 1 < n_here:
            nxt = scores(i + 1)
        parts = []
        for j in range(LANES // NA_HEAD_DIM):
            t = slice(j * nq, (j + 1) * nq)
            p_loc = jnp.exp2(sl_ref[i % 2, :, t] - m[:, t]).astype(BF16)
            p_ctx = jnp.exp2(sc_ref[i % 2, :, t] - m[:, t]).astype(BF16)
            o_t = _pv_t(vt_ref, ws * GRID_W // slab, p_loc, SUM_ROWS)
            o_t = o_t + _pv_t(vt_ref, n_lat // slab, p_ctx, SUM_ROWS)
            parts.append(o_t[j * NA_HEAD_DIM:(j + 1) * NA_HEAD_DIM, :] * (1.0 / o_t[LANES:LANES + 1, :]))
        o_ref[i * nq:(i + 1) * nq, :] = jnp.concatenate(parts, axis=0).T.astype(BF16)


def _na_bounded_kernel(q_ref, k_ref, vt_ref, bias_ref, bound_ref, o_ref, *, n_lat, n_ctx, rows):
    slab = vt_ref.shape[-1]
    nq = NA_BLOCK_ROWS * GRID_W
    n_blocks = rows // NA_BLOCK_ROWS
    nk = NA_WIN_ROWS * GRID_W
    kc = k_ref[n_lat:n_lat + n_ctx, :]
    bound = bound_ref[:, :1]
    for i in range(q_ref.shape[0] // nq):
        blk = pl.program_id(2) * NA_STEP_BLOCKS + i
        kind = jnp.where(blk == 0, 1, jnp.where(blk == n_blocks - 1, 2, 0))
        ws = _na_window_start(blk, rows, jnp)
        q2 = _split_heads(q_ref[i * nq:(i + 1) * nq, :])
        kw = k_ref[pl.ds(pl.multiple_of(ws * GRID_W, slab), nk), :]
        p_loc = jnp.exp2(_dot_nt(kw, q2) + (bias_ref[kind] - bound)).astype(BF16)
        p_ctx = jnp.exp2(_dot_nt(kc, q2) - bound).astype(BF16)
        o_t = _pv_t(vt_ref, ws * GRID_W // slab, p_loc, SUM_ROWS)
        o_t = o_t + _pv_t(vt_ref, n_lat // slab, p_ctx, SUM_ROWS)
        o_t = o_t[:LANES, :] * (1.0 / o_t[LANES:LANES + 1, :])
        o_ref[i * nq:(i + 1) * nq, :] = _merge_heads(o_t, nq).astype(BF16)


def _na_bias(rpb, rows):
    h, n_dr, n_dc = rpb.shape
    n_blocks = rows // NA_BLOCK_ROWS
    lead = GRID_W - NA_KW
    wide = 2 * GRID_W
    w = jnp.pad(rpb * LOG2E, ((0, 0), (0, 0), (lead, wide - lead - n_dc)), constant_values=MASK_VALUE)
    flat = jnp.broadcast_to(w[:, :, None, :], (h, n_dr, GRID_W, wide)).reshape(h, n_dr, GRID_W * wide)
    skew = flat[:, :, GRID_W - 1:GRID_W - 1 + GRID_W * (wide - 1)].reshape(h, n_dr, GRID_W, wide - 1)
    qc = jnp.arange(GRID_W)
    cs = jnp.clip(qc - NA_KW // 2, 0, GRID_W - NA_KW)
    col_ok = (qc[None, :] >= cs[:, None]) & (qc[None, :] < cs[:, None] + NA_KW)
    c_t = jnp.swapaxes(jnp.where(col_ok, skew[..., :GRID_W], MASK_VALUE), -1, -2)
    c_ext = jnp.concatenate([c_t, jnp.full((h, 1, GRID_W, GRID_W), MASK_VALUE, F32)], axis=1)
    idx = np.full((3, NA_WIN_ROWS, NA_BLOCK_ROWS), n_dr, np.int32)
    for kind, blk in enumerate((1, 0, n_blocks - 1)):
        ws = _na_window_start(blk, rows, np)
        for j in range(NA_WIN_ROWS):
            for i in range(NA_BLOCK_ROWS):
                r = NA_BLOCK_ROWS * blk + i
                rs = min(max(r - NA_KH // 2, 0), rows - NA_KH)
                if rs <= ws + j < rs + NA_KH:
                    idx[kind, j, i] = ws + j - r + NA_KH - 1
    sub = LANES // NA_HEAD_DIM
    blocks = c_ext[:, idx].reshape(h // sub, sub, 3, NA_WIN_ROWS, NA_BLOCK_ROWS, GRID_W, GRID_W)
    bias = blocks.transpose(0, 2, 3, 5, 1, 4, 6)
    return bias.reshape(h // sub, 3, NA_WIN_ROWS * GRID_W, sub * NA_BLOCK_ROWS * GRID_W)


def _na_attention(q, k, vt, bias, bound, *, n_lat, layer):
    b, n, width = q.shape
    n_ctx = n - n_lat
    rows = n_lat // GRID_W
    n_slab, slab = vt.shape[1], vt.shape[3]
    tq = NA_STEP_BLOCKS * NA_BLOCK_ROWS * GRID_W
    in_specs = [
        pl.BlockSpec((None, tq, LANES), lambda hp, bi, rb: (bi, rb, hp)),
        pl.BlockSpec((None, n, LANES), lambda hp, bi, rb: (bi, 0, hp)),
        pl.BlockSpec((None, n_slab, LANES, slab), lambda hp, bi, rb: (bi, 0, hp, 0)),
        pl.BlockSpec((None,) + bias.shape[1:], lambda hp, bi, rb: (hp, 0, 0, 0)),
    ]
    common = dict(
        grid=(width // LANES, b, n_lat // tq),
        out_specs=pl.BlockSpec((None, tq, LANES), lambda hp, bi, rb: (bi, rb, hp)),
        out_shape=jax.ShapeDtypeStruct((b, n_lat, width), BF16),
        compiler_params=_params("parallel", "parallel", "arbitrary"),
    )

    def general():
        return pl.pallas_call(
            functools.partial(_na_kernel, n_lat=n_lat, n_ctx=n_ctx, rows=rows),
            in_specs=in_specs,
            scratch_shapes=[pltpu.VMEM((2, bias.shape[2], bias.shape[3] + SKEW_LANES), F32),
                            pltpu.VMEM((2, n_ctx, bias.shape[3] + SKEW_LANES), F32)],
            name="na_attn_" + str(layer), **common,
        )(q, k, vt, bias)

    def bounded():
        return pl.pallas_call(
            functools.partial(_na_bounded_kernel, n_lat=n_lat, n_ctx=n_ctx, rows=rows),
            in_specs=in_specs + [pl.BlockSpec((1, LANES), lambda hp, bi, rb: (0, 0))],
            name="na_attn_bounded_" + str(layer), **common,
        )(q, k, vt, bias, jnp.full((1, LANES), bound, F32))

    return lax.cond(bound <= SAFE_SCORE_BOUND, bounded, general)


def _out_proj_final_kernel(o_ref, g_ref, w_ref, x_ref, mod_ref, fg_ref, y_ref):
    a = (o_ref[...].astype(F32) * g_ref[...].astype(F32)).astype(BF16)
    xn = x_ref[...] + mod_ref[0, 2:3, :] * _dot(a, w_ref[...])
    y_ref[...] = xn * _rms(xn) * fg_ref[...]


def _out_proj_final(o, gs, w_out, x_lat, mod3, layer, n_lat, final_g):
    b, _, d = x_lat.shape
    tm = min(FINAL_TILE, n_lat)
    row = lambda bi, t: (bi, t, 0)
    w = o.shape[2]
    return pl.pallas_call(
        _out_proj_final_kernel,
        grid=(b, n_lat // tm),
        in_specs=[pl.BlockSpec((None, tm, w), row), pl.BlockSpec((None, tm, w), row),
                  pl.BlockSpec(w_out.shape, lambda bi, t: (0, 0)), pl.BlockSpec((None, tm, d), row),
                  pl.BlockSpec((1, 3, d), lambda bi, t: (layer * MOD_ROWS + bi, 0, 0)),
                  pl.BlockSpec((1, d), lambda bi, t: (0, 0))],
        out_specs=pl.BlockSpec((None, tm, d), row),
        out_shape=jax.ShapeDtypeStruct((b, n_lat, d), F32),
        compiler_params=_params("parallel", "parallel"),
        name="out_proj_final",
    )(o, gs, w_out, x_lat, mod3, final_g.reshape(1, d))


def _rope_tables(n_lat, n_ctx, rot_dim):
    n_freq = rot_dim // 4
    inv = ROPE_THETA ** (-jnp.arange(n_freq, dtype=F32) / n_freq)
    t = jnp.arange(n_lat)
    ang_r = (t // GRID_W).astype(F32)[:, None] * inv
    ang_c = (t % GRID_W).astype(F32)[:, None] * inv
    ang = jnp.concatenate([ang_r, ang_r, ang_c, ang_c], axis=-1)
    cos, sin = jnp.cos(ang), jnp.sin(ang)
    first = (jnp.arange(rot_dim) % (2 * n_freq)) < n_freq
    sin_up = jnp.where(first, -sin, 0.0)
    sin_dn = jnp.where(first, 0.0, sin)
    pad = LANES - rot_dim

    def finish(tab, fill):
        tab = jnp.pad(tab, ((0, 0), (0, pad)), constant_values=fill)
        return jnp.pad(tab, ((0, n_ctx), (0, 0)), constant_values=fill)

    return finish(cos, 1.0), finish(sin_up, 0.0), finish(sin_dn, 0.0)


def _mla_weights(w_in, w_uq, w_ukv):
    o_kv = MLA_Q_LORA
    o_kr = o_kv + MLA_KV_LORA
    o_g = o_kr + MLA_ROPE
    k_r = jnp.pad(w_in[:, o_kr:o_g], ((0, 0), (0, LANES - MLA_ROPE)))
    w_perm = jnp.concatenate([w_in[:, :o_kr], w_in[:, o_g:], k_r], axis=1).astype(BF16)
    uq = w_uq.reshape(MLA_Q_LORA, MLA_HEADS, MLA_NOPE + MLA_ROPE)
    uq = jnp.pad(uq, ((0, 0), (0, 0), (0, MLA_QK_PAD - MLA_NOPE - MLA_ROPE)))
    uq = uq.reshape(MLA_Q_LORA, MLA_HEADS * MLA_QK_PAD).astype(BF16)
    ukv = w_ukv.reshape(MLA_KV_LORA, MLA_HEADS, MLA_NOPE + MLA_V).astype(BF16)
    uk = ukv[:, :, :MLA_NOPE].reshape(MLA_KV_LORA, MLA_HEADS * MLA_NOPE)
    uv = ukv[:, :, MLA_NOPE:].reshape(MLA_KV_LORA, MLA_HEADS * MLA_V)
    return w_perm, uq, uk, uv


def kernel(x, c, ctx, c_ctx, mod_w, mod_b, norm_g, final_g, ga_w_in, ga_q_g, ga_k_g, ga_w_out, na_w_in, na_rpb, na_w_out, mla_w_in, mla_q_g, mla_kv_g, mla_w_uq, mla_w_ukv, mla_w_out):
    b, n_lat, d = x.shape
    n_ctx = ctx.shape[1]
    depth = mod_w.shape[0]
    assert n_lat % ROW_TILE == 0 and n_ctx == ROW_TILE and n_lat % n_ctx == 0
    assert b < MOD_ROWS and n_lat % (NA_STEP_BLOCKS * NA_BLOCK_ROWS * GRID_W) == 0
    assert n_lat // GRID_W >= NA_WIN_ROWS + NA_BLOCK_ROWS

    stream = (x, ctx, 0)
    cc = jnp.concatenate([c, c_ctx[None, :], jnp.zeros((MOD_ROWS - b - 1, d), F32)], axis=0)
    mod3 = _modulation(cc, mod_w, mod_b).reshape(depth * MOD_ROWS, 3, d)

    tab_a = _rope_tables(n_lat, n_ctx, GQA_HEAD_DIM)
    tab_m = _rope_tables(n_lat, n_ctx, MLA_ROPE)

    prev = None
    for i in range(depth):
        kind, j = i % N_MIXERS, i // N_MIXERS
        need_ctx = i < depth - 1

        def project(kern, consts, tables, widths, with_norms=False):
            outs = _project(kern, stream, mod3, i, norm_g[i], consts, tables, widths, n_lat, with_norms, prev)
            if prev is None:
                return stream, outs
            return (outs[0], outs[0], n_lat // ROW_TILE), outs[1:]

        if kind == 0:
            n_q = d // GQA_HEAD_DIM
            n_kv = n_q // GQA_GROUP
            kern = functools.partial(_gqa_proj_kernel, n_q=n_q, n_kv=n_kv, scale=GQA_HEAD_DIM ** -0.5 * LOG2E)
            consts = [ga_w_in[j].astype(BF16), ga_q_g[j].reshape(1, -1), ga_k_g[j].reshape(1, -1)]
            kvw = n_kv * GQA_HEAD_DIM
            stream, (q, k, vt, gs) = project(kern, consts, tab_a, [d, kvw, kvw, d])
            bound = (BOUND_SLACK * GQA_HEAD_DIM ** 0.5 * LOG2E
                     * jnp.max(jnp.abs(ga_q_g[j])) * jnp.max(jnp.abs(ga_k_g[j])))
            o = _flash_attention(q, k, vt, group=GQA_GROUP, dk=GQA_HEAD_DIM, n_lat=n_lat, layer=i, bound=bound)
            if need_ctx:
                oc = _ctx_attention(q, k, vt, group=GQA_GROUP, dk=GQA_HEAD_DIM, n_lat=n_lat, layer=i)
            w_out = ga_w_out[j]
        elif kind == 1:
            kern = functools.partial(_na_proj_kernel, width=d, scale=NA_HEAD_DIM ** -0.5 * LOG2E)
            stream, (q, k, vt, gs, norms) = project(kern, [na_w_in[j].astype(BF16)], (), [d, d, d, d], True)
            bound = (BOUND_SLACK * jnp.sqrt(jnp.max(norms[:, :, 0, :]) * jnp.max(norms[:, :, MOD_ROWS, :]))
                     + LOG2E * jnp.max(jnp.abs(na_rpb[j])))
            o = _na_attention(q, k, vt, _na_bias(na_rpb[j], n_lat // GRID_W), bound, n_lat=n_lat, layer=i)
            if need_ctx:
                oc = _na_ctx_attention(q, k, vt, n_lat=n_lat, layer=i)
            w_out = na_w_out[j]
        else:
            kern = functools.partial(_mla_proj_kernel, width=d, scale=(MLA_NOPE + MLA_ROPE) ** -0.5 * LOG2E)
            w_perm, uq, uk, uv = _mla_weights(mla_w_in[j], mla_w_uq[j], mla_w_ukv[j])
            consts = [w_perm, mla_q_g[j].reshape(1, -1), mla_kv_g[j].reshape(1, -1), uq, uk, uv]
            widths = [MLA_HEADS * MLA_QK_PAD, MLA_HEADS * MLA_QK_PAD, MLA_HEADS * MLA_V, d]
            stream, (q, k, vt, gs, norms) = project(kern, consts, tab_m, widths, True)
            bound = BOUND_SLACK * jnp.sqrt(jnp.max(norms[:, :, 0, 0]) * jnp.max(norms[:, :, MOD_ROWS, 0]))
            o = _flash_attention(q, k, vt, group=1, dk=MLA_QK_PAD, n_lat=n_lat, layer=i, bound=bound)
            if need_ctx:
                oc = _ctx_attention(q, k, vt, group=1, dk=MLA_QK_PAD, n_lat=n_lat, layer=i)
            w_out = mla_w_out[j]
        if need_ctx:
            prev = (o, oc, gs, w_out.astype(BF16), i)
    return _out_proj_final(o, gs, w_out.astype(BF16), stream[0], mod3, depth - 1, n_lat, final_g)
```

```python
import functools

import jax
import numpy as np
import jax.numpy as jnp
from jax import lax
from jax.experimental import pallas as pl
from jax.experimental.pallas import tpu as pltpu

F32 = jnp.float32
BF16 = jnp.bfloat16

NORM_EPS = 1e-6
ROPE_THETA = 10000.0
GRID_W = 64
N_MIXERS = 3
LANES = 128
MOD_ROWS = 8
MASK_VALUE = -1e30
LOG2E = 1.4426950408889634

GQA_HEAD_DIM = 128
GQA_GROUP = 4
NA_HEAD_DIM = 64
NA_KH = 8
NA_KW = 16
NA_BLOCK_ROWS = 4
NA_WIN_ROWS = 12
NA_STEP_BLOCKS = 16
MLA_HEADS = 8
MLA_Q_LORA = 512
MLA_KV_LORA = 256
MLA_NOPE = 128
MLA_ROPE = 64
MLA_V = 128
MLA_QK_PAD = 256

ROW_TILE = 256
FINAL_TILE = 512
Q_COLS = 4096
Q_COLS_GENERAL = 1024
Q_TILE = 256
SUM_ROWS = 16
SKEW_LANES = 128
SAFE_SCORE_BOUND = 50.0
BOUNDED_UNROLL = 5
BOUND_SLACK = 1.02
KV_CHUNKS = (768, 512, 256)
VMEM_LIMIT = 48 * 1024 * 1024


def _params(*sem):
    return pltpu.CompilerParams(dimension_semantics=sem, vmem_limit_bytes=VMEM_LIMIT)


def _silu(v):
    return v * (1.0 / (1.0 + jnp.exp(-v)))


def _rms(v):
    return lax.rsqrt(jnp.mean(v * v, axis=-1, keepdims=True) + NORM_EPS)


def _rope(v, cos, sin_up, sin_dn, half):
    w = v.shape[-1]
    return v * cos + pltpu.roll(v, w - half, 1) * sin_up + pltpu.roll(v, half, 1) * sin_dn


def _dot(a, b):
    return jnp.dot(a, b, preferred_element_type=F32)


def _dot_nt(a, b):
    return lax.dot_general(a, b, (((1,), (1,)), ((), ())), preferred_element_type=F32)


def _mod_kernel(c_ref, w_ref, b_ref, o_ref):
    a = _silu(c_ref[...])
    w = w_ref[0]
    a_hi = a.astype(BF16)
    a_lo = (a - a_hi.astype(F32)).astype(BF16)
    w_hi = w.astype(BF16)
    w_lo = (w - w_hi.astype(F32)).astype(BF16)
    o_ref[0] = _dot(a_hi, w_hi) + _dot(a_lo, w_hi) + _dot(a_hi, w_lo) + b_ref[0]


def _modulation(cc, mod_w, mod_b):
    depth, d, n3 = mod_w.shape
    tn = 1024
    return pl.pallas_call(
        _mod_kernel,
        grid=(depth, n3 // tn),
        in_specs=[
            pl.BlockSpec((MOD_ROWS, d), lambda l, j: (0, 0)),
            pl.BlockSpec((1, d, tn), lambda l, j: (l, 0, j)),
            pl.BlockSpec((1, 1, tn), lambda l, j: (l, 0, j)),
        ],
        out_specs=pl.BlockSpec((1, MOD_ROWS, tn), lambda l, j: (l, 0, j)),
        out_shape=jax.ShapeDtypeStruct((depth, MOD_ROWS, n3), F32),
        compiler_params=_params("parallel", "parallel"),
        name="adaln_modulation",
    )(cc, mod_w, mod_b.reshape(depth, 1, n3))


def _modulated_norm(x_ref, xc_ref, ng_ref, mod_ref, n_lat_tiles):
    x = jnp.where(pl.program_id(1) >= n_lat_tiles, xc_ref[...], x_ref[...])
    shift = mod_ref[0, 0:1, :]
    scale = mod_ref[0, 1:2, :]
    return ((x * _rms(x)) * ng_ref[...] * (1.0 + scale) + shift).astype(BF16)


def _gqa_proj_kernel(x_ref, xc_ref, mod_ref, ng_ref, w_ref, qg_ref, kg_ref, cos_ref, su_ref, sd_ref,
                     q_ref, k_ref, vt_ref, g_ref, *, n_lat_tiles, n_q, n_kv, scale):
    h = _modulated_norm(x_ref, xc_ref, ng_ref, mod_ref, n_lat_tiles)
    cos, su, sd = cos_ref[...], su_ref[...], sd_ref[...]
    hd = GQA_HEAD_DIM
    qw, kw = n_q * hd, n_kv * hd
    q = _dot(h, w_ref[:, :qw])
    for i in range(n_q):
        qh = q[:, i * hd:(i + 1) * hd]
        qh = qh * _rms(qh) * qg_ref[...]
        q_ref[:, i * hd:(i + 1) * hd] = (_rope(qh, cos, su, sd, hd // 4) * scale).astype(BF16)
    k = _dot(h, w_ref[:, qw:qw + kw])
    for i in range(n_kv):
        kh = k[:, i * hd:(i + 1) * hd]
        kh = kh * _rms(kh) * kg_ref[...]
        k_ref[:, i * hd:(i + 1) * hd] = _rope(kh, cos, su, sd, hd // 4).astype(BF16)
    v = _dot(h, w_ref[:, qw + kw:qw + 2 * kw])
    for i in range(n_kv):
        vt_ref[i * hd:(i + 1) * hd, :] = v[:, i * hd:(i + 1) * hd].T.astype(BF16)
    g_ref[...] = _silu(_dot(h, w_ref[:, qw + 2 * kw:])).astype(BF16)


def _na_proj_kernel(x_ref, xc_ref, mod_ref, ng_ref, w_ref, q_ref, k_ref, vt_ref, g_ref, norm_ref,
                    *, n_lat_tiles, width, scale):
    h = _modulated_norm(x_ref, xc_ref, ng_ref, mod_ref, n_lat_tiles)
    head_of_col = lax.broadcasted_iota(jnp.int32, (width, LANES), 0) // NA_HEAD_DIM
    lane = lax.broadcasted_iota(jnp.int32, (width, LANES), 1)
    pick = jnp.where(head_of_col == lane, 1.0, 0.0).astype(BF16)
    half = norm_ref.shape[0] // 2
    q = _dot(h, w_ref[:, :width]) * scale
    q_ref[...] = q.astype(BF16)
    q_sq = jnp.max(_dot((q * q).astype(BF16), pick), axis=0, keepdims=True)
    norm_ref[:half, :] = jnp.broadcast_to(q_sq, (half, LANES))
    k = _dot(h, w_ref[:, width:2 * width])
    k_ref[...] = k.astype(BF16)
    k_sq = jnp.max(_dot((k * k).astype(BF16), pick), axis=0, keepdims=True)
    norm_ref[half:, :] = jnp.broadcast_to(k_sq, (half, LANES))
    v = _dot(h, w_ref[:, 2 * width:3 * width])
    for i in range(width // LANES):
        vt_ref[i * LANES:(i + 1) * LANES, :] = v[:, i * LANES:(i + 1) * LANES].T.astype(BF16)
    g_ref[...] = _silu(_dot(h, w_ref[:, 3 * width:])).astype(BF16)


def _mla_proj_kernel(x_ref, xc_ref, mod_ref, ng_ref, w_ref, qg_ref, kvg_ref, wuq_ref, wuk_ref, wuv_ref,
                     cos_ref, su_ref, sd_ref, q_ref, k_ref, vt_ref, g_ref, norm_ref,
                     *, n_lat_tiles, width, scale):
    h = _modulated_norm(x_ref, xc_ref, ng_ref, mod_ref, n_lat_tiles)
    ones = jnp.ones((LANES, LANES), BF16)

    def row_sq(v):
        return _dot((v * v).astype(BF16), ones)

    cos, su, sd = cos_ref[...], su_ref[...], sd_ref[...]
    o_kv = MLA_Q_LORA
    o_g = o_kv + MLA_KV_LORA
    o_kr = o_g + width
    c_q = _dot(h, w_ref[:, :o_kv])
    c_q = (c_q * _rms(c_q) * qg_ref[...]).astype(BF16)
    q = _dot(c_q, wuq_ref[...])
    q_sq = None
    for i in range(MLA_HEADS):
        a = i * MLA_QK_PAD
        q_nope = q[:, a:a + MLA_NOPE] * scale
        q_rope = _rope(q[:, a + MLA_NOPE:a + MLA_QK_PAD], cos, su, sd, MLA_ROPE // 4) * scale
        q_ref[:, a:a + MLA_NOPE] = q_nope.astype(BF16)
        q_ref[:, a + MLA_NOPE:a + MLA_QK_PAD] = q_rope.astype(BF16)
        sq = row_sq(q_nope) + row_sq(q_rope)
        q_sq = sq if q_sq is None else jnp.maximum(q_sq, sq)
    c_kv = _dot(h, w_ref[:, o_kv:o_g])
    c_kv = (c_kv * _rms(c_kv) * kvg_ref[...]).astype(BF16)
    k_r = _rope(_dot(h, w_ref[:, o_kr:]), cos, su, sd, MLA_ROPE // 4)
    k_nope = _dot(c_kv, wuk_ref[...])
    k_sq = None
    for i in range(MLA_HEADS):
        a = i * MLA_QK_PAD
        k_h = k_nope[:, i * MLA_NOPE:(i + 1) * MLA_NOPE]
        k_ref[:, a:a + MLA_NOPE] = k_h.astype(BF16)
        k_ref[:, a + MLA_NOPE:a + MLA_QK_PAD] = k_r.astype(BF16)
        sq = row_sq(k_h)
        k_sq = sq if k_sq is None else jnp.maximum(k_sq, sq)
    k_sq = k_sq + row_sq(k_r)
    half = norm_ref.shape[0] // 2
    norm_ref[:half, :] = jnp.broadcast_to(jnp.max(q_sq, axis=0, keepdims=True), (half, LANES))
    norm_ref[half:, :] = jnp.broadcast_to(jnp.max(k_sq, axis=0, keepdims=True), (half, LANES))
    v = _dot(c_kv, wuv_ref[...])
    for i in range(MLA_HEADS):
        vt_ref[i * MLA_V:(i + 1) * MLA_V, :] = v[:, i * MLA_V:(i + 1) * MLA_V].T.astype(BF16)
    g_ref[...] = _silu(_dot(h, w_ref[:, o_g:o_kr])).astype(BF16)


class _Loaded:
    def __init__(self, value):
        self.value = value

    def __getitem__(self, _):
        return self.value


def _fused_kernel(*refs, proj, split, n_in, n_lat_tiles):
    ins, outs = refs[:n_in], refs[n_in:]
    o_ref, oc_ref, g_ref, w_ref, x_ref = ins[:5]
    rest = ins[5:]
    is_ctx = pl.program_id(1) >= n_lat_tiles
    x = x_ref[...]
    if split:
        x = jnp.where(is_ctx, rest[0][...], x)
        rest = rest[1:]
    o = jnp.where(is_ctx, oc_ref[...], o_ref[...])
    a = (o.astype(F32) * g_ref[...].astype(F32)).astype(BF16)
    x = x + rest[0][0, 2:3, :] * _dot(a, w_ref[...])
    outs[0][...] = x
    x = _Loaded(x)
    proj(x, x, *rest[1:], *outs[1:], n_lat_tiles=n_lat_tiles)


def _project(kern, stream, mod3, layer, norm_g, consts, tables, out_widths, n_lat, with_norms=False, prev=None):
    x_lat, x_ctx, ctx_tile = stream
    b, _, d = x_lat.shape
    tm = ROW_TILE
    n_lat_tiles = n_lat // tm
    n = n_lat + tm
    ctx_row = b
    row = lambda bi, t: (bi, t, 0)
    lat_row = lambda bi, t: (bi, jnp.minimum(t, n_lat_tiles - 1), 0)
    const = lambda bi, t: (0, 0)

    def mod_spec(which):
        return pl.BlockSpec((1, 3, d), lambda bi, t: (which * MOD_ROWS + jnp.where(t < n_lat_tiles, bi, ctx_row), 0, 0))

    x_specs = [pl.BlockSpec((None, tm, d), lat_row), pl.BlockSpec((None, tm, d), lambda bi, t: (bi, ctx_tile, 0))]
    in_specs = [mod_spec(layer), pl.BlockSpec((1, d), const)]
    in_specs += [pl.BlockSpec(a.shape, const) for a in consts]
    in_specs += [pl.BlockSpec((tm, LANES), lambda bi, t: (t, 0)) for _ in tables]
    args = [mod3, norm_g.reshape(1, d), *consts, *tables]
    out_specs = [pl.BlockSpec((None, tm, w), row) for w in out_widths]
    out_shape = [jax.ShapeDtypeStruct((b, n, w), BF16) for w in out_widths]
    out_specs[2] = pl.BlockSpec((None, None, out_widths[2], tm), lambda bi, t: (bi, t, 0, 0))
    out_shape[2] = jax.ShapeDtypeStruct((b, n // tm, out_widths[2], tm), BF16)
    if with_norms:
        out_specs.append(pl.BlockSpec((None, None, 2 * MOD_ROWS, LANES), lambda bi, t: (bi, t, 0, 0)))
        out_shape.append(jax.ShapeDtypeStruct((b, n // tm, 2 * MOD_ROWS, LANES), F32))
    call = dict(grid=(b, n // tm), compiler_params=_params("parallel", "parallel"), name="proj_" + str(layer))
    if prev is None:
        return pl.pallas_call(
            functools.partial(kern, n_lat_tiles=n_lat_tiles),
            in_specs=x_specs + in_specs, out_specs=out_specs, out_shape=out_shape, **call,
        )(x_lat, x_ctx, *args)
    o, o_ctx, gs, w_out, prev_layer = prev
    combined = x_lat is x_ctx
    wo = o.shape[2]
    pre_specs = [pl.BlockSpec((None, tm, wo), lat_row), pl.BlockSpec((None, tm, wo), lambda bi, t: (bi, 0, 0)),
                 pl.BlockSpec((None, tm, wo), row), pl.BlockSpec(w_out.shape, const)]
    if combined:
        x_specs, xs = [pl.BlockSpec((None, tm, d), row)], [x_lat]
    else:
        xs = [x_lat, x_ctx]
    in_specs = pre_specs + x_specs + [mod_spec(prev_layer)] + in_specs
    return pl.pallas_call(
        functools.partial(_fused_kernel, proj=kern, split=not combined, n_in=len(in_specs), n_lat_tiles=n_lat_tiles),
        in_specs=in_specs,
        out_specs=[pl.BlockSpec((None, tm, d), row)] + out_specs,
        out_shape=[jax.ShapeDtypeStruct((b, n, d), F32)] + out_shape,
        input_output_aliases={4: 0} if combined else {},
        **call,
    )(o, o_ctx, gs, w_out, *xs, mod3, *args)


def _stack_heads(q_ref, group, dk):
    return jnp.concatenate([q_ref[:, g * dk:(g + 1) * dk] for g in range(group)], axis=0)


def _unstack_heads(o_t, o_ref, group):
    tq = o_ref.shape[0]
    for g in range(group):
        o_ref[:, g * LANES:(g + 1) * LANES] = o_t[:, g * tq:(g + 1) * tq].T.astype(BF16)


def _pv_t(vt_ref, first_slab, p, sum_rows=0):
    slab = vt_ref.shape[-1]
    acc = None
    for i in range(p.shape[0] // slab):
        vt = vt_ref[first_slab + i]
        if sum_rows:
            vt = jnp.concatenate([vt, jnp.ones((sum_rows, slab), vt.dtype)], axis=0)
        t = _dot(vt, p[i * slab:(i + 1) * slab, :])
        acc = t if acc is None else acc + t
    return acc


def _flash_kernel(q_ref, k_ref, vt_ref, o_ref, sa_ref, sb_ref, acc_ref, m_ref, *, group, dk, bk):
    slab = vt_ref.shape[-1]
    n_chunks = k_ref.shape[0] // bk
    q = _stack_heads(q_ref, group, dk)
    tiles = [slice(j * Q_TILE, (j + 1) * Q_TILE) for j in range(q.shape[0] // Q_TILE)]

    def scores(c, s_ref):
        k = k_ref[pl.ds(pl.multiple_of(c * bk, bk), bk), :]
        out = []
        for t in tiles:
            s = _dot_nt(k, q[t, :])
            s_ref[:, t] = s
            out.append(jnp.max(s, axis=0, keepdims=True))
        return out

    def update(s_ref, s_max, c, t):
        m = m_ref[:, t]
        m_new = jnp.maximum(m, s_max)
        alpha = jnp.exp2(m - m_new)
        p = jnp.exp2(s_ref[:, t] - m_new).astype(BF16)
        m_ref[:, t] = m_new
        acc_ref[:, t] = alpha * acc_ref[:, t] + _pv_t(vt_ref, c * (bk // slab), p, SUM_ROWS)

    def step(c, cur_ref, cur_max, nxt_ref):
        k = k_ref[pl.ds(pl.multiple_of((c + 1) * bk, bk), bk), :]
        nxt_max = []
        for j, t in enumerate(tiles):
            s = _dot_nt(k, q[t, :])
            nxt_ref[:, t] = s
            nxt_max.append(jnp.max(s, axis=0, keepdims=True))
            update(cur_ref, cur_max[j], c, t)
        return nxt_max

    m_ref[...] = jnp.full(m_ref.shape, MASK_VALUE, F32)
    acc_ref[...] = jnp.zeros(acc_ref.shape, F32)
    max_a = scores(0, sa_ref)

    def body(i, max_a):
        max_b = step(2 * i, sa_ref, max_a, sb_ref)
        return tuple(step(2 * i + 1, sb_ref, max_b, sa_ref))

    pairs = (n_chunks - 1) // 2
    max_a = lax.fori_loop(0, pairs, body, tuple(max_a))
    last = 2 * pairs
    if n_chunks - last == 2:
        max_b = step(last, sa_ref, max_a, sb_ref)
        last, s_ref, s_max = last + 1, sb_ref, max_b
    else:
        s_ref, s_max = sa_ref, max_a
    for j, t in enumerate(tiles):
        update(s_ref, s_max[j], last, t)
    _unstack_heads(acc_ref[:LANES, :] * (1.0 / acc_ref[LANES:LANES + 1, :]), o_ref, group)


def _flash_bounded_kernel(q_ref, k_ref, vt_ref, bound_ref, o_ref, acc_ref, *, group, dk, bk):
    slab = vt_ref.shape[-1]
    n_chunks = k_ref.shape[0] // bk
    q = _stack_heads(q_ref, group, dk)
    tiles = [slice(0, q.shape[0])]
    bound = bound_ref[:, :1]
    acc_ref[...] = jnp.zeros(acc_ref.shape, F32)

    def chunk(c):
        k = k_ref[pl.ds(pl.multiple_of(c * bk, bk), bk), :]
        for t in tiles:
            p = jnp.exp2(_dot_nt(k, q[t, :]) - bound)
            acc_ref[:LANES, t] += _pv_t(vt_ref, c * (bk // slab), p.astype(BF16))
            acc_ref[LANES:, t] += jnp.sum(p.reshape(bk // MOD_ROWS, MOD_ROWS, p.shape[1]), axis=0)

    def body(i, carry):
        for u in range(BOUNDED_UNROLL):
            chunk(BOUNDED_UNROLL * i + u)
        return carry

    lax.fori_loop(0, n_chunks // BOUNDED_UNROLL, body, 0)
    for c in range(n_chunks - n_chunks % BOUNDED_UNROLL, n_chunks):
        chunk(c)
    l = jnp.sum(acc_ref[LANES:, :], axis=0, keepdims=True)
    _unstack_heads(acc_ref[:LANES, :] * (1.0 / l), o_ref, group)


def _flash_attention(q, k, vt, *, group, dk, n_lat, layer, bound=None):
    b, n, _ = q.shape
    n_kv = k.shape[2] // dk
    n_slab, slab = vt.shape[1], vt.shape[3]
    bk = max(c for c in KV_CHUNKS if n % c == 0)
    assert bk % slab == 0

    def call(kern, q_cols, name, scratch, extra_specs=(), extra_args=()):
        tq = min(q_cols // group, n_lat)
        return pl.pallas_call(
            functools.partial(kern, group=group, dk=dk, bk=bk),
            grid=(b, n_kv, n_lat // tq),
            in_specs=[
                pl.BlockSpec((None, tq, group * dk), lambda bi, h, t: (bi, t, h)),
                pl.BlockSpec((None, n, dk), lambda bi, h, t: (bi, 0, h)),
                pl.BlockSpec((None, n_slab, LANES, slab), lambda bi, h, t: (bi, 0, h, 0)),
                *extra_specs,
            ],
            out_specs=pl.BlockSpec((None, tq, group * LANES), lambda bi, h, t: (bi, t, h)),
            out_shape=jax.ShapeDtypeStruct((b, n_lat, n_kv * group * LANES), BF16),
            scratch_shapes=scratch(group * tq),
            compiler_params=_params("parallel", "parallel", "parallel"),
            name=name + str(layer),
        )(q, k, vt, *extra_args)

    def general():
        return call(_flash_kernel, Q_COLS_GENERAL, "flash_", lambda nq: [
            pltpu.VMEM((bk, nq + SKEW_LANES), F32), pltpu.VMEM((bk, nq + SKEW_LANES), F32),
            pltpu.VMEM((LANES + SUM_ROWS, nq), F32), pltpu.VMEM((1, nq), F32)])

    if bound is None:
        return general()

    def bounded():
        return call(_flash_bounded_kernel, Q_COLS, "flash_bounded_",
                    lambda nq: [pltpu.VMEM((LANES + MOD_ROWS, nq), F32)],
                    [pl.BlockSpec((1, LANES), lambda bi, h, t: (0, 0))], [jnp.full((1, LANES), bound, F32)])

    return lax.cond(bound <= SAFE_SCORE_BOUND, bounded, general)


def _ctx_attn_kernel(q_ref, k_ref, vt_ref, o_ref, *, group, dk):
    q = _stack_heads(q_ref, group, dk)
    s = _dot_nt(k_ref[...], q)
    p = jnp.exp2(s - jnp.max(s, axis=0, keepdims=True))
    l = jnp.sum(p, axis=0, keepdims=True)
    _unstack_heads(_dot(vt_ref[...], p.astype(BF16)) * (1.0 / l), o_ref, group)


def _ctx_attention(q, k, vt, *, group, dk, n_lat, layer):
    b, n, _ = q.shape
    n_ctx = n - n_lat
    n_kv = k.shape[2] // dk
    slab = vt.shape[3]
    assert slab == n_ctx
    t = n_lat // n_ctx
    return pl.pallas_call(
        functools.partial(_ctx_attn_kernel, group=group, dk=dk),
        grid=(b, n_kv),
        in_specs=[
            pl.BlockSpec((None, n_ctx, group * dk), lambda bi, h: (bi, t, h)),
            pl.BlockSpec((None, n_ctx, dk), lambda bi, h: (bi, t, h)),
            pl.BlockSpec((None, None, LANES, slab), lambda bi, h: (bi, t, h, 0)),
        ],
        out_specs=pl.BlockSpec((None, n_ctx, group * LANES), lambda bi, h: (bi, 0, h)),
        out_shape=jax.ShapeDtypeStruct((b, n_ctx, n_kv * group * LANES), BF16),
        compiler_params=_params("parallel", "parallel"),
        name="ctx_attn_" + str(layer),
    )(q, k, vt)


def _split_heads(q):
    lane = lax.broadcasted_iota(jnp.int32, (1, LANES), 1)
    zero = jnp.zeros_like(q)
    return jnp.concatenate(
        [jnp.where((lane >= j * NA_HEAD_DIM) & (lane < (j + 1) * NA_HEAD_DIM), q, zero)
         for j in range(LANES // NA_HEAD_DIM)], axis=0)


def _merge_heads(o_t, nq):
    parts = [o_t[j * NA_HEAD_DIM:(j + 1) * NA_HEAD_DIM, j * nq:(j + 1) * nq]
             for j in range(LANES // NA_HEAD_DIM)]
    return jnp.concatenate(parts, axis=0).T


def _na_ctx_kernel(q_ref, k_ref, vt_ref, o_ref):
    q2 = _split_heads(q_ref[...])
    s = _dot_nt(k_ref[...], q2)
    p = jnp.exp2(s - jnp.max(s, axis=0, keepdims=True))
    l = jnp.sum(p, axis=0, keepdims=True)
    o_t = _dot(vt_ref[...], p.astype(BF16)) * (1.0 / l)
    o_ref[...] = _merge_heads(o_t, q_ref.shape[0]).astype(BF16)


def _na_ctx_attention(q, k, vt, *, n_lat, layer):
    b, n, width = q.shape
    n_ctx = n - n_lat
    assert vt.shape[3] == n_ctx
    t = n_lat // n_ctx
    blk = pl.BlockSpec((None, n_ctx, LANES), lambda bi, h: (bi, t, h))
    return pl.pallas_call(
        _na_ctx_kernel,
        grid=(b, width // LANES),
        in_specs=[blk, blk, pl.BlockSpec((None, None, LANES, n_ctx), lambda bi, h: (bi, t, h, 0))],
        out_specs=pl.BlockSpec((None, n_ctx, LANES), lambda bi, h: (bi, 0, h)),
        out_shape=jax.ShapeDtypeStruct((b, n_ctx, width), BF16),
        compiler_params=_params("parallel", "parallel"),
        name="ctx_attn_" + str(layer),
    )(q, k, vt)


def _na_window_start(blk, rows, lib):
    lo = lib.minimum(lib.maximum(NA_BLOCK_ROWS * blk - NA_KH // 2, 0), rows - NA_KH)
    return lib.minimum(lo, rows - NA_WIN_ROWS)


def _na_kernel(q_ref, k_ref, vt_ref, bias_ref, o_ref, sl_ref, sc_ref, *, n_lat, n_ctx, rows):
    slab = vt_ref.shape[-1]
    nq = NA_BLOCK_ROWS * GRID_W
    n_blocks = rows // NA_BLOCK_ROWS
    nk = NA_WIN_ROWS * GRID_W
    kc = k_ref[n_lat:n_lat + n_ctx, :]
    n_here = q_ref.shape[0] // nq

    def scores(i):
        blk = pl.program_id(2) * NA_STEP_BLOCKS + i
        kind = jnp.where(blk == 0, 1, jnp.where(blk == n_blocks - 1, 2, 0))
        ws = _na_window_start(blk, rows, jnp)
        q2 = _split_heads(q_ref[i * nq:(i + 1) * nq, :])
        s_loc = _dot_nt(k_ref[pl.ds(pl.multiple_of(ws * GRID_W, slab), nk), :], q2) + bias_ref[kind]
        s_ctx = _dot_nt(kc, q2)
        sl_ref[i % 2, :, :s_loc.shape[1]] = s_loc
        sc_ref[i % 2, :, :s_ctx.shape[1]] = s_ctx
        return ws, jnp.maximum(jnp.max(s_loc, axis=0, keepdims=True), jnp.max(s_ctx, axis=0, keepdims=True))

    nxt = scores(0)
    for i in range(n_here):
        ws, m = nxt
        if i + 1 < n_here:
            nxt = scores(i + 1)
        parts = []
        for j in range(LANES // NA_HEAD_DIM):
            t = slice(j * nq, (j + 1) * nq)
            p_loc = jnp.exp2(sl_ref[i % 2, :, t] - m[:, t]).astype(BF16)
            p_ctx = jnp.exp2(sc_ref[i % 2, :, t] - m[:, t]).astype(BF16)
            o_t = _pv_t(vt_ref, ws * GRID_W // slab, p_loc, SUM_ROWS)
            o_t = o_t + _pv_t(vt_ref, n_lat // slab, p_ctx, SUM_ROWS)
            parts.append(o_t[j * NA_HEAD_DIM:(j + 1) * NA_HEAD_DIM, :] * (1.0 / o_t[LANES:LANES + 1, :]))
        o_ref[i * nq:(i + 1) * nq, :] = jnp.concatenate(parts, axis=0).T.astype(BF16)


def _na_bounded_kernel(q_ref, k_ref, vt_ref, bias_ref, bound_ref, o_ref, *, n_lat, n_ctx, rows):
    slab = vt_ref.shape[-1]
    nq = NA_BLOCK_ROWS * GRID_W
    n_blocks = rows // NA_BLOCK_ROWS
    nk = NA_WIN_ROWS * GRID_W
    kc = k_ref[n_lat:n_lat + n_ctx, :]
    bound = bound_ref[:, :1]
    for i in range(q_ref.shape[0] // nq):
        blk = pl.program_id(2) * NA_STEP_BLOCKS + i
        kind = jnp.where(blk == 0, 1, jnp.where(blk == n_blocks - 1, 2, 0))
        ws = _na_window_start(blk, rows, jnp)
        q2 = _split_heads(q_ref[i * nq:(i + 1) * nq, :])
        kw = k_ref[pl.ds(pl.multiple_of(ws * GRID_W, slab), nk), :]
        p_loc = jnp.exp2(_dot_nt(kw, q2) + (bias_ref[kind] - bound)).astype(BF16)
        p_ctx = jnp.exp2(_dot_nt(kc, q2) - bound).astype(BF16)
        o_t = _pv_t(vt_ref, ws * GRID_W // slab, p_loc, SUM_ROWS)
        o_t = o_t + _pv_t(vt_ref, n_lat // slab, p_ctx, SUM_ROWS)
        o_t = o_t[:LANES, :] * (1.0 / o_t[LANES:LANES + 1, :])
        o_ref[i * nq:(i + 1) * nq, :] = _merge_heads(o_t, nq).astype(BF16)


def _na_bias(rpb, rows):
    h, n_dr, n_dc = rpb.shape
    n_blocks = rows // NA_BLOCK_ROWS
    lead = GRID_W - NA_KW
    wide = 2 * GRID_W
    w = jnp.pad(rpb * LOG2E, ((0, 0), (0, 0), (lead, wide - lead - n_dc)), constant_values=MASK_VALUE)
    flat = jnp.broadcast_to(w[:, :, None, :], (h, n_dr, GRID_W, wide)).reshape(h, n_dr, GRID_W * wide)
    skew = flat[:, :, GRID_W - 1:GRID_W - 1 + GRID_W * (wide - 1)].reshape(h, n_dr, GRID_W, wide - 1)
    qc = jnp.arange(GRID_W)
    cs = jnp.clip(qc - NA_KW // 2, 0, GRID_W - NA_KW)
    col_ok = (qc[None, :] >= cs[:, None]) & (qc[None, :] < cs[:, None] + NA_KW)
    c_t = jnp.swapaxes(jnp.where(col_ok, skew[..., :GRID_W], MASK_VALUE), -1, -2)
    c_ext = jnp.concatenate([c_t, jnp.full((h, 1, GRID_W, GRID_W), MASK_VALUE, F32)], axis=1)
    idx = np.full((3, NA_WIN_ROWS, NA_BLOCK_ROWS), n_dr, np.int32)
    for kind, blk in enumerate((1, 0, n_blocks - 1)):
        ws = _na_window_start(blk, rows, np)
        for j in range(NA_WIN_ROWS):
            for i in range(NA_BLOCK_ROWS):
                r = NA_BLOCK_ROWS * blk + i
                rs = min(max(r - NA_KH // 2, 0), rows - NA_KH)
                if rs <= ws + j < rs + NA_KH:
                    idx[kind, j, i] = ws + j - r + NA_KH - 1
    sub = LANES // NA_HEAD_DIM
    blocks = c_ext[:, idx].reshape(h // sub, sub, 3, NA_WIN_ROWS, NA_BLOCK_ROWS, GRID_W, GRID_W)
    bias = blocks.transpose(0, 2, 3, 5, 1, 4, 6)
    return bias.reshape(h // sub, 3, NA_WIN_ROWS * GRID_W, sub * NA_BLOCK_ROWS * GRID_W)


def _na_attention(q, k, vt, bias, bound, *, n_lat, layer):
    b, n, width = q.shape
    n_ctx = n - n_lat
    rows = n_lat // GRID_W
    n_slab, slab = vt.shape[1], vt.shape[3]
    tq = NA_STEP_BLOCKS * NA_BLOCK_ROWS * GRID_W
    in_specs = [
        pl.BlockSpec((None, tq, LANES), lambda hp, bi, rb: (bi, rb, hp)),
        pl.BlockSpec((None, n, LANES), lambda hp, bi, rb: (bi, 0, hp)),
        pl.BlockSpec((None, n_slab, LANES, slab), lambda hp, bi, rb: (bi, 0, hp, 0)),
        pl.BlockSpec((None,) + bias.shape[1:], lambda hp, bi, rb: (hp, 0, 0, 0)),
    ]
    common = dict(
        grid=(width // LANES, b, n_lat // tq),
        out_specs=pl.BlockSpec((None, tq, LANES), lambda hp, bi, rb: (bi, rb, hp)),
        out_shape=jax.ShapeDtypeStruct((b, n_lat, width), BF16),
        compiler_params=_params("parallel", "parallel", "arbitrary"),
    )

    def general():
        return pl.pallas_call(
            functools.partial(_na_kernel, n_lat=n_lat, n_ctx=n_ctx, rows=rows),
            in_specs=in_specs,
            scratch_shapes=[pltpu.VMEM((2, bias.shape[2], bias.shape[3] + SKEW_LANES), F32),
                            pltpu.VMEM((2, n_ctx, bias.shape[3] + SKEW_LANES), F32)],
            name="na_attn_" + str(layer), **common,
        )(q, k, vt, bias)

    def bounded():
        return pl.pallas_call(
            functools.partial(_na_bounded_kernel, n_lat=n_lat, n_ctx=n_ctx, rows=rows),
            in_specs=in_specs + [pl.BlockSpec((1, LANES), lambda hp, bi, rb: (0, 0))],
            name="na_attn_bounded_" + str(layer), **common,
        )(q, k, vt, bias, jnp.full((1, LANES), bound, F32))

    return lax.cond(bound <= SAFE_SCORE_BOUND, bounded, general)


def _out_proj_final_kernel(o_ref, g_ref, w_ref, x_ref, mod_ref, fg_ref, y_ref):
    a = (o_ref[...].astype(F32) * g_ref[...].astype(F32)).astype(BF16)
    xn = x_ref[...] + mod_ref[0, 2:3, :] * _dot(a, w_ref[...])
    y_ref[...] = xn * _rms(xn) * fg_ref[...]


def _out_proj_final(o, gs, w_out, x_lat, mod3, layer, n_lat, final_g):
    b, _, d = x_lat.shape
    tm = min(FINAL_TILE, n_lat)
    row = lambda bi, t: (bi, t, 0)
    w = o.shape[2]
    return pl.pallas_call(
        _out_proj_final_kernel,
        grid=(b, n_lat // tm),
        in_specs=[pl.BlockSpec((None, tm, w), row), pl.BlockSpec((None, tm, w), row),
                  pl.BlockSpec(w_out.shape, lambda bi, t: (0, 0)), pl.BlockSpec((None, tm, d), row),
                  pl.BlockSpec((1, 3, d), lambda bi, t: (layer * MOD_ROWS + bi, 0, 0)),
                  pl.BlockSpec((1, d), lambda bi, t: (0, 0))],
        out_specs=pl.BlockSpec((None, tm, d), row),
        out_shape=jax.ShapeDtypeStruct((b, n_lat, d), F32),
        compiler_params=_params("parallel", "parallel"),
        name="out_proj_final",
    )(o, gs, w_out, x_lat, mod3, final_g.reshape(1, d))


def _rope_tables(n_lat, n_ctx, rot_dim):
    n_freq = rot_dim // 4
    inv = ROPE_THETA ** (-jnp.arange(n_freq, dtype=F32) / n_freq)
    t = jnp.arange(n_lat)
    ang_r = (t // GRID_W).astype(F32)[:, None] * inv
    ang_c = (t % GRID_W).astype(F32)[:, None] * inv
    ang = jnp.concatenate([ang_r, ang_r, ang_c, ang_c], axis=-1)
    cos, sin = jnp.cos(ang), jnp.sin(ang)
    first = (jnp.arange(rot_dim) % (2 * n_freq)) < n_freq
    sin_up = jnp.where(first, -sin, 0.0)
    sin_dn = jnp.where(first, 0.0, sin)
    pad = LANES - rot_dim

    def finish(tab, fill):
        tab = jnp.pad(tab, ((0, 0), (0, pad)), constant_values=fill)
        return jnp.pad(tab, ((0, n_ctx), (0, 0)), constant_values=fill)

    return finish(cos, 1.0), finish(sin_up, 0.0), finish(sin_dn, 0.0)


def _mla_weights(w_in, w_uq, w_ukv):
    o_kv = MLA_Q_LORA
    o_kr = o_kv + MLA_KV_LORA
    o_g = o_kr + MLA_ROPE
    k_r = jnp.pad(w_in[:, o_kr:o_g], ((0, 0), (0, LANES - MLA_ROPE)))
    w_perm = jnp.concatenate([w_in[:, :o_kr], w_in[:, o_g:], k_r], axis=1).astype(BF16)
    uq = w_uq.reshape(MLA_Q_LORA, MLA_HEADS, MLA_NOPE + MLA_ROPE)
    uq = jnp.pad(uq, ((0, 0), (0, 0), (0, MLA_QK_PAD - MLA_NOPE - MLA_ROPE)))
    uq = uq.reshape(MLA_Q_LORA, MLA_HEADS * MLA_QK_PAD).astype(BF16)
    ukv = w_ukv.reshape(MLA_KV_LORA, MLA_HEADS, MLA_NOPE + MLA_V).astype(BF16)
    uk = ukv[:, :, :MLA_NOPE].reshape(MLA_KV_LORA, MLA_HEADS * MLA_NOPE)
    uv = ukv[:, :, MLA_NOPE:].reshape(MLA_KV_LORA, MLA_HEADS * MLA_V)
    return w_perm, uq, uk, uv


def kernel(x, c, ctx, c_ctx, mod_w, mod_b, norm_g, final_g, ga_w_in, ga_q_g, ga_k_g, ga_w_out, na_w_in, na_rpb, na_w_out, mla_w_in, mla_q_g, mla_kv_g, mla_w_uq, mla_w_ukv, mla_w_out):
    b, n_lat, d = x.shape
    n_ctx = ctx.shape[1]
    depth = mod_w.shape[0]
    assert n_lat % ROW_TILE == 0 and n_ctx == ROW_TILE and n_lat % n_ctx == 0
    assert b < MOD_ROWS and n_lat % (NA_STEP_BLOCKS * NA_BLOCK_ROWS * GRID_W) == 0
    assert n_lat // GRID_W >= NA_WIN_ROWS + NA_BLOCK_ROWS

    stream = (x, ctx, 0)
    cc = jnp.concatenate([c, c_ctx[None, :], jnp.zeros((MOD_ROWS - b - 1, d), F32)], axis=0)
    mod3 = _modulation(cc, mod_w, mod_b).reshape(depth * MOD_ROWS, 3, d)

    tab_a = _rope_tables(n_lat, n_ctx, GQA_HEAD_DIM)
    tab_m = _rope_tables(n_lat, n_ctx, MLA_ROPE)

    prev = None
    for i in range(depth):
        kind, j = i % N_MIXERS, i // N_MIXERS
        need_ctx = i < depth - 1

        def project(kern, consts, tables, widths, with_norms=False):
            outs = _project(kern, stream, mod3, i, norm_g[i], consts, tables, widths, n_lat, with_norms, prev)
            if prev is None:
                return stream, outs
            return (outs[0], outs[0], n_lat // ROW_TILE), outs[1:]

        if kind == 0:
            n_q = d // GQA_HEAD_DIM
            n_kv = n_q // GQA_GROUP
            kern = functools.partial(_gqa_proj_kernel, n_q=n_q, n_kv=n_kv, scale=GQA_HEAD_DIM ** -0.5 * LOG2E)
            consts = [ga_w_in[j].astype(BF16), ga_q_g[j].reshape(1, -1), ga_k_g[j].reshape(1, -1)]
            kvw = n_kv * GQA_HEAD_DIM
            stream, (q, k, vt, gs) = project(kern, consts, tab_a, [d, kvw, kvw, d])
            bound = (BOUND_SLACK * GQA_HEAD_DIM ** 0.5 * LOG2E
                     * jnp.max(jnp.abs(ga_q_g[j])) * jnp.max(jnp.abs(ga_k_g[j])))
            o = _flash_attention(q, k, vt, group=GQA_GROUP, dk=GQA_HEAD_DIM, n_lat=n_lat, layer=i, bound=bound)
            if need_ctx:
                oc = _ctx_attention(q, k, vt, group=GQA_GROUP, dk=GQA_HEAD_DIM, n_lat=n_lat, layer=i)
            w_out = ga_w_out[j]
        elif kind == 1:
            kern = functools.partial(_na_proj_kernel, width=d, scale=NA_HEAD_DIM ** -0.5 * LOG2E)
            stream, (q, k, vt, gs, norms) = project(kern, [na_w_in[j].astype(BF16)], (), [d, d, d, d], True)
            bound = (BOUND_SLACK * jnp.sqrt(jnp.max(norms[:, :, 0, :]) * jnp.max(norms[:, :, MOD_ROWS, :]))
                     + LOG2E * jnp.max(jnp.abs(na_rpb[j])))
            o = _na_attention(q, k, vt, _na_bias(na_rpb[j], n_lat // GRID_W), bound, n_lat=n_lat, layer=i)
            if need_ctx:
                oc = _na_ctx_attention(q, k, vt, n_lat=n_lat, layer=i)
            w_out = na_w_out[j]
        else:
            kern = functools.partial(_mla_proj_kernel, width=d, scale=(MLA_NOPE + MLA_ROPE) ** -0.5 * LOG2E)
            w_perm, uq, uk, uv = _mla_weights(mla_w_in[j], mla_w_uq[j], mla_w_ukv[j])
            consts = [w_perm, mla_q_g[j].reshape(1, -1), mla_kv_g[j].reshape(1, -1), uq, uk, uv]
            widths = [MLA_HEADS * MLA_QK_PAD, MLA_HEADS * MLA_QK_PAD, MLA_HEADS * MLA_V, d]
            stream, (q, k, vt, gs, norms) = project(kern, consts, tab_m, widths, True)
            bound = BOUND_SLACK * jnp.sqrt(jnp.max(norms[:, :, 0, 0]) * jnp.max(norms[:, :, MOD_ROWS, 0]))
            o = _flash_attention(q, k, vt, group=1, dk=MLA_QK_PAD, n_lat=n_lat, layer=i, bound=bound)
            if need_ctx:
                oc = _ctx_attention(q, k, vt, group=1, dk=MLA_QK_PAD, n_lat=n_lat, layer=i)
            w_out = mla_w_out[j]
        if need_ctx:
            prev = (o, oc, gs, w_out.astype(BF16), i)
    return _out_proj_final(o, gs, w_out.astype(BF16), stream[0], mod3, depth - 1, n_lat, final_g)
```

```python
import functools

import jax
import numpy as np
import jax.numpy as jnp
from jax import lax
from jax.experimental import pallas as pl
from jax.experimental.pallas import tpu as pltpu

F32 = jnp.float32
BF16 = jnp.bfloat16

NORM_EPS = 1e-6
ROPE_THETA = 10000.0
GRID_W = 64
N_MIXERS = 3
LANES = 128
MOD_ROWS = 8
MASK_VALUE = -1e30
LOG2E = 1.4426950408889634

GQA_HEAD_DIM = 128
GQA_GROUP = 4
NA_HEAD_DIM = 64
NA_KH = 8
NA_KW = 16
NA_BLOCK_ROWS = 4
NA_WIN_ROWS = 12
NA_STEP_BLOCKS = 8
MLA_HEADS = 8
MLA_Q_LORA = 512
MLA_KV_LORA = 256
MLA_NOPE = 128
MLA_ROPE = 64
MLA_V = 128
MLA_QK_PAD = 256

ROW_TILE = 256
FINAL_TILE = 512
Q_COLS = 4096
Q_COLS_GENERAL = 1024
Q_TILE = 256
SUM_ROWS = 16
SKEW_LANES = 128
SAFE_SCORE_BOUND = 50.0
BOUNDED_UNROLL = 5
BOUND_SLACK = 1.02
KV_CHUNKS = (768, 512, 256)
VMEM_LIMIT = 48 * 1024 * 1024


def _params(*sem):
    return pltpu.CompilerParams(dimension_semantics=sem, vmem_limit_bytes=VMEM_LIMIT)


def _silu(v):
    return v * (1.0 / (1.0 + jnp.exp(-v)))


def _rms(v):
    return lax.rsqrt(jnp.mean(v * v, axis=-1, keepdims=True) + NORM_EPS)


def _rope(v, cos, sin_up, sin_dn, half):
    w = v.shape[-1]
    return v * cos + pltpu.roll(v, w - half, 1) * sin_up + pltpu.roll(v, half, 1) * sin_dn


def _dot(a, b):
    return jnp.dot(a, b, preferred_element_type=F32)


def _dot_nt(a, b):
    return lax.dot_general(a, b, (((1,), (1,)), ((), ())), preferred_element_type=F32)


def _mod_kernel(c_ref, w_ref, b_ref, o_ref):
    a = _silu(c_ref[...])
    w = w_ref[0]
    a_hi = a.astype(BF16)
    a_lo = (a - a_hi.astype(F32)).astype(BF16)
    w_hi = w.astype(BF16)
    w_lo = (w - w_hi.astype(F32)).astype(BF16)
    o_ref[0] = _dot(a_hi, w_hi) + _dot(a_lo, w_hi) + _dot(a_hi, w_lo) + b_ref[0]


def _modulation(cc, mod_w, mod_b):
    depth, d, n3 = mod_w.shape
    tn = 1024
    return pl.pallas_call(
        _mod_kernel,
        grid=(depth, n3 // tn),
        in_specs=[
            pl.BlockSpec((MOD_ROWS, d), lambda l, j: (0, 0)),
            pl.BlockSpec((1, d, tn), lambda l, j: (l, 0, j)),
            pl.BlockSpec((1, 1, tn), lambda l, j: (l, 0, j)),
        ],
        out_specs=pl.BlockSpec((1, MOD_ROWS, tn), lambda l, j: (l, 0, j)),
        out_shape=jax.ShapeDtypeStruct((depth, MOD_ROWS, n3), F32),
        compiler_params=_params("parallel", "parallel"),
        name="adaln_modulation",
    )(cc, mod_w, mod_b.reshape(depth, 1, n3))


def _modulated_norm(x_ref, xc_ref, ng_ref, mod_ref, n_lat_tiles):
    x = jnp.where(pl.program_id(1) >= n_lat_tiles, xc_ref[...], x_ref[...])
    shift = mod_ref[0, 0:1, :]
    scale = mod_ref[0, 1:2, :]
    return ((x * _rms(x)) * ng_ref[...] * (1.0 + scale) + shift).astype(BF16)


def _gqa_proj_kernel(x_ref, xc_ref, mod_ref, ng_ref, w_ref, qg_ref, kg_ref, cos_ref, su_ref, sd_ref,
                     q_ref, k_ref, vt_ref, g_ref, *, n_lat_tiles, n_q, n_kv, scale):
    h = _modulated_norm(x_ref, xc_ref, ng_ref, mod_ref, n_lat_tiles)
    cos, su, sd = cos_ref[...], su_ref[...], sd_ref[...]
    hd = GQA_HEAD_DIM
    qw, kw = n_q * hd, n_kv * hd
    q = _dot(h, w_ref[:, :qw])
    for i in range(n_q):
        qh = q[:, i * hd:(i + 1) * hd]
        qh = qh * _rms(qh) * qg_ref[...]
        q_ref[:, i * hd:(i + 1) * hd] = (_rope(qh, cos, su, sd, hd // 4) * scale).astype(BF16)
    k = _dot(h, w_ref[:, qw:qw + kw])
    for i in range(n_kv):
        kh = k[:, i * hd:(i + 1) * hd]
        kh = kh * _rms(kh) * kg_ref[...]
        k_ref[:, i * hd:(i + 1) * hd] = _rope(kh, cos, su, sd, hd // 4).astype(BF16)
    v = _dot(h, w_ref[:, qw + kw:qw + 2 * kw])
    for i in range(n_kv):
        vt_ref[i * hd:(i + 1) * hd, :] = v[:, i * hd:(i + 1) * hd].T.astype(BF16)
    g_ref[...] = _silu(_dot(h, w_ref[:, qw + 2 * kw:])).astype(BF16)


def _na_proj_kernel(x_ref, xc_ref, mod_ref, ng_ref, w_ref, q_ref, k_ref, vt_ref, g_ref, norm_ref,
                    *, n_lat_tiles, width, scale):
    h = _modulated_norm(x_ref, xc_ref, ng_ref, mod_ref, n_lat_tiles)
    head_of_col = lax.broadcasted_iota(jnp.int32, (width, LANES), 0) // NA_HEAD_DIM
    lane = lax.broadcasted_iota(jnp.int32, (width, LANES), 1)
    pick = jnp.where(head_of_col == lane, 1.0, 0.0).astype(BF16)
    half = norm_ref.shape[0] // 2
    q = _dot(h, w_ref[:, :width]) * scale
    q_ref[...] = q.astype(BF16)
    q_sq = jnp.max(_dot((q * q).astype(BF16), pick), axis=0, keepdims=True)
    norm_ref[:half, :] = jnp.broadcast_to(q_sq, (half, LANES))
    k = _dot(h, w_ref[:, width:2 * width])
    k_ref[...] = k.astype(BF16)
    k_sq = jnp.max(_dot((k * k).astype(BF16), pick), axis=0, keepdims=True)
    norm_ref[half:, :] = jnp.broadcast_to(k_sq, (half, LANES))
    v = _dot(h, w_ref[:, 2 * width:3 * width])
    for i in range(width // LANES):
        vt_ref[i * LANES:(i + 1) * LANES, :] = v[:, i * LANES:(i + 1) * LANES].T.astype(BF16)
    g_ref[...] = _silu(_dot(h, w_ref[:, 3 * width:])).astype(BF16)


def _mla_proj_kernel(x_ref, xc_ref, mod_ref, ng_ref, w_ref, qg_ref, kvg_ref, wuq_ref, wuk_ref, wuv_ref,
                     cos_ref, su_ref, sd_ref, q_ref, k_ref, vt_ref, g_ref, norm_ref,
                     *, n_lat_tiles, width, scale):
    h = _modulated_norm(x_ref, xc_ref, ng_ref, mod_ref, n_lat_tiles)
    ones = jnp.ones((LANES, LANES), BF16)

    def row_sq(v):
        return _dot((v * v).astype(BF16), ones)

    cos, su, sd = cos_ref[...], su_ref[...], sd_ref[...]
    o_kv = MLA_Q_LORA
    o_g = o_kv + MLA_KV_LORA
    o_kr = o_g + width
    c_q = _dot(h, w_ref[:, :o_kv])
    c_q = (c_q * _rms(c_q) * qg_ref[...]).astype(BF16)
    q = _dot(c_q, wuq_ref[...])
    q_sq = None
    for i in range(MLA_HEADS):
        a = i * MLA_QK_PAD
        q_nope = q[:, a:a + MLA_NOPE] * scale
        q_rope = _rope(q[:, a + MLA_NOPE:a + MLA_QK_PAD], cos, su, sd, MLA_ROPE // 4) * scale
        q_ref[:, a:a + MLA_NOPE] = q_nope.astype(BF16)
        q_ref[:, a + MLA_NOPE:a + MLA_QK_PAD] = q_rope.astype(BF16)
        sq = row_sq(q_nope) + row_sq(q_rope)
        q_sq = sq if q_sq is None else jnp.maximum(q_sq, sq)
    c_kv = _dot(h, w_ref[:, o_kv:o_g])
    c_kv = (c_kv * _rms(c_kv) * kvg_ref[...]).astype(BF16)
    k_r = _rope(_dot(h, w_ref[:, o_kr:]), cos, su, sd, MLA_ROPE // 4)
    k_nope = _dot(c_kv, wuk_ref[...])
    k_sq = None
    for i in range(MLA_HEADS):
        a = i * MLA_QK_PAD
        k_h = k_nope[:, i * MLA_NOPE:(i + 1) * MLA_NOPE]
        k_ref[:, a:a + MLA_NOPE] = k_h.astype(BF16)
        k_ref[:, a + MLA_NOPE:a + MLA_QK_PAD] = k_r.astype(BF16)
        sq = row_sq(k_h)
        k_sq = sq if k_sq is None else jnp.maximum(k_sq, sq)
    k_sq = k_sq + row_sq(k_r)
    half = norm_ref.shape[0] // 2
    norm_ref[:half, :] = jnp.broadcast_to(jnp.max(q_sq, axis=0, keepdims=True), (half, LANES))
    norm_ref[half:, :] = jnp.broadcast_to(jnp.max(k_sq, axis=0, keepdims=True), (half, LANES))
    v = _dot(c_kv, wuv_ref[...])
    for i in range(MLA_HEADS):
        vt_ref[i * MLA_V:(i + 1) * MLA_V, :] = v[:, i * MLA_V:(i + 1) * MLA_V].T.astype(BF16)
    g_ref[...] = _silu(_dot(h, w_ref[:, o_g:o_kr])).astype(BF16)


class _Loaded:
    def __init__(self, value):
        self.value = value

    def __getitem__(self, _):
        return self.value


def _fused_kernel(*refs, proj, split, n_in, n_lat_tiles):
    ins, outs = refs[:n_in], refs[n_in:]
    o_ref, oc_ref, g_ref, w_ref, x_ref = ins[:5]
    rest = ins[5:]
    is_ctx = pl.program_id(1) >= n_lat_tiles
    x = x_ref[...]
    if split:
        x = jnp.where(is_ctx, rest[0][...], x)
        rest = rest[1:]
    o = jnp.where(is_ctx, oc_ref[...], o_ref[...])
    a = (o.astype(F32) * g_ref[...].astype(F32)).astype(BF16)
    x = x + rest[0][0, 2:3, :] * _dot(a, w_ref[...])
    outs[0][...] = x
    x = _Loaded(x)
    proj(x, x, *rest[1:], *outs[1:], n_lat_tiles=n_lat_tiles)


def _project(kern, stream, mod3, layer, norm_g, consts, tables, out_widths, n_lat, with_norms=False, prev=None):
    x_lat, x_ctx, ctx_tile = stream
    b, _, d = x_lat.shape
    tm = ROW_TILE
    n_lat_tiles = n_lat // tm
    n = n_lat + tm
    ctx_row = b
    row = lambda bi, t: (bi, t, 0)
    lat_row = lambda bi, t: (bi, jnp.minimum(t, n_lat_tiles - 1), 0)
    const = lambda bi, t: (0, 0)

    def mod_spec(which):
        return pl.BlockSpec((1, 3, d), lambda bi, t: (which * MOD_ROWS + jnp.where(t < n_lat_tiles, bi, ctx_row), 0, 0))

    x_specs = [pl.BlockSpec((None, tm, d), lat_row), pl.BlockSpec((None, tm, d), lambda bi, t: (bi, ctx_tile, 0))]
    in_specs = [mod_spec(layer), pl.BlockSpec((1, d), const)]
    in_specs += [pl.BlockSpec(a.shape, const) for a in consts]
    in_specs += [pl.BlockSpec((tm, LANES), lambda bi, t: (t, 0)) for _ in tables]
    args = [mod3, norm_g.reshape(1, d), *consts, *tables]
    out_specs = [pl.BlockSpec((None, tm, w), row) for w in out_widths]
    out_shape = [jax.ShapeDtypeStruct((b, n, w), BF16) for w in out_widths]
    out_specs[2] = pl.BlockSpec((None, None, out_widths[2], tm), lambda bi, t: (bi, t, 0, 0))
    out_shape[2] = jax.ShapeDtypeStruct((b, n // tm, out_widths[2], tm), BF16)
    if with_norms:
        out_specs.append(pl.BlockSpec((None, None, 2 * MOD_ROWS, LANES), lambda bi, t: (bi, t, 0, 0)))
        out_shape.append(jax.ShapeDtypeStruct((b, n // tm, 2 * MOD_ROWS, LANES), F32))
    call = dict(grid=(b, n // tm), compiler_params=_params("parallel", "parallel"), name="proj_" + str(layer))
    if prev is None:
        return pl.pallas_call(
            functools.partial(kern, n_lat_tiles=n_lat_tiles),
            in_specs=x_specs + in_specs, out_specs=out_specs, out_shape=out_shape, **call,
        )(x_lat, x_ctx, *args)
    o, o_ctx, gs, w_out, prev_layer = prev
    combined = x_lat is x_ctx
    wo = o.shape[2]
    pre_specs = [pl.BlockSpec((None, tm, wo), lat_row), pl.BlockSpec((None, tm, wo), lambda bi, t: (bi, 0, 0)),
                 pl.BlockSpec((None, tm, wo), row), pl.BlockSpec(w_out.shape, const)]
    if combined:
        x_specs, xs = [pl.BlockSpec((None, tm, d), row)], [x_lat]
    else:
        xs = [x_lat, x_ctx]
    in_specs = pre_specs + x_specs + [mod_spec(prev_layer)] + in_specs
    return pl.pallas_call(
        functools.partial(_fused_kernel, proj=kern, split=not combined, n_in=len(in_specs), n_lat_tiles=n_lat_tiles),
        in_specs=in_specs,
        out_specs=[pl.BlockSpec((None, tm, d), row)] + out_specs,
        out_shape=[jax.ShapeDtypeStruct((b, n, d), F32)] + out_shape,
        input_output_aliases={4: 0} if combined else {},
        **call,
    )(o, o_ctx, gs, w_out, *xs, mod3, *args)


def _stack_heads(q_ref, group, dk):
    return jnp.concatenate([q_ref[:, g * dk:(g + 1) * dk] for g in range(group)], axis=0)


def _unstack_heads(o_t, o_ref, group):
    tq = o_ref.shape[0]
    for g in range(group):
        o_ref[:, g * LANES:(g + 1) * LANES] = o_t[:, g * tq:(g + 1) * tq].T.astype(BF16)


def _pv_t(vt_ref, first_slab, p, sum_rows=0):
    slab = vt_ref.shape[-1]
    acc = None
    for i in range(p.shape[0] // slab):
        vt = vt_ref[first_slab + i]
        if sum_rows:
            vt = jnp.concatenate([vt, jnp.ones((sum_rows, slab), vt.dtype)], axis=0)
        t = _dot(vt, p[i * slab:(i + 1) * slab, :])
        acc = t if acc is None else acc + t
    return acc


def _flash_kernel(q_ref, k_ref, vt_ref, o_ref, sa_ref, sb_ref, acc_ref, m_ref, *, group, dk, bk):
    slab = vt_ref.shape[-1]
    n_chunks = k_ref.shape[0] // bk
    q = _stack_heads(q_ref, group, dk)
    tiles = [slice(j * Q_TILE, (j + 1) * Q_TILE) for j in range(q.shape[0] // Q_TILE)]

    def scores(c, s_ref):
        k = k_ref[pl.ds(pl.multiple_of(c * bk, bk), bk), :]
        out = []
        for t in tiles:
            s = _dot_nt(k, q[t, :])
            s_ref[:, t] = s
            out.append(jnp.max(s, axis=0, keepdims=True))
        return out

    def update(s_ref, s_max, c, t):
        m = m_ref[:, t]
        m_new = jnp.maximum(m, s_max)
        alpha = jnp.exp2(m - m_new)
        p = jnp.exp2(s_ref[:, t] - m_new).astype(BF16)
        m_ref[:, t] = m_new
        acc_ref[:, t] = alpha * acc_ref[:, t] + _pv_t(vt_ref, c * (bk // slab), p, SUM_ROWS)

    def step(c, cur_ref, cur_max, nxt_ref):
        k = k_ref[pl.ds(pl.multiple_of((c + 1) * bk, bk), bk), :]
        nxt_max = []
        for j, t in enumerate(tiles):
            s = _dot_nt(k, q[t, :])
            nxt_ref[:, t] = s
            nxt_max.append(jnp.max(s, axis=0, keepdims=True))
            update(cur_ref, cur_max[j], c, t)
        return nxt_max

    m_ref[...] = jnp.full(m_ref.shape, MASK_VALUE, F32)
    acc_ref[...] = jnp.zeros(acc_ref.shape, F32)
    max_a = scores(0, sa_ref)

    def body(i, max_a):
        max_b = step(2 * i, sa_ref, max_a, sb_ref)
        return tuple(step(2 * i + 1, sb_ref, max_b, sa_ref))

    pairs = (n_chunks - 1) // 2
    max_a = lax.fori_loop(0, pairs, body, tuple(max_a))
    last = 2 * pairs
    if n_chunks - last == 2:
        max_b = step(last, sa_ref, max_a, sb_ref)
        last, s_ref, s_max = last + 1, sb_ref, max_b
    else:
        s_ref, s_max = sa_ref, max_a
    for j, t in enumerate(tiles):
        update(s_ref, s_max[j], last, t)
    _unstack_heads(acc_ref[:LANES, :] * (1.0 / acc_ref[LANES:LANES + 1, :]), o_ref, group)


def _flash_bounded_kernel(q_ref, k_ref, vt_ref, bound_ref, o_ref, acc_ref, *, group, dk, bk):
    slab = vt_ref.shape[-1]
    n_chunks = k_ref.shape[0] // bk
    q = _stack_heads(q_ref, group, dk)
    tiles = [slice(0, q.shape[0])]
    bound = bound_ref[:, :1]
    acc_ref[...] = jnp.zeros(acc_ref.shape, F32)

    def chunk(c):
        k = k_ref[pl.ds(pl.multiple_of(c * bk, bk), bk), :]
        for t in tiles:
            p = jnp.exp2(_dot_nt(k, q[t, :]) - bound)
            acc_ref[:LANES, t] += _pv_t(vt_ref, c * (bk // slab), p.astype(BF16))
            acc_ref[LANES:, t] += jnp.sum(p.reshape(bk // MOD_ROWS, MOD_ROWS, p.shape[1]), axis=0)

    def body(i, carry):
        for u in range(BOUNDED_UNROLL):
            chunk(BOUNDED_UNROLL * i + u)
        return carry

    lax.fori_loop(0, n_chunks // BOUNDED_UNROLL, body, 0)
    for c in range(n_chunks - n_chunks % BOUNDED_UNROLL, n_chunks):
        chunk(c)
    l = jnp.sum(acc_ref[LANES:, :], axis=0, keepdims=True)
    _unstack_heads(acc_ref[:LANES, :] * (1.0 / l), o_ref, group)


def _flash_attention(q, k, vt, *, group, dk, n_lat, layer, bound=None):
    b, n, _ = q.shape
    n_kv = k.shape[2] // dk
    n_slab, slab = vt.shape[1], vt.shape[3]
    bk = max(c for c in KV_CHUNKS if n % c == 0)
    assert bk % slab == 0

    def call(kern, q_cols, name, scratch, extra_specs=(), extra_args=()):
        tq = min(q_cols // group, n_lat)
        return pl.pallas_call(
            functools.partial(kern, group=group, dk=dk, bk=bk),
            grid=(b, n_kv, n_lat // tq),
            in_specs=[
                pl.BlockSpec((None, tq, group * dk), lambda bi, h, t: (bi, t, h)),
                pl.BlockSpec((None, n, dk), lambda bi, h, t: (bi, 0, h)),
                pl.BlockSpec((None, n_slab, LANES, slab), lambda bi, h, t: (bi, 0, h, 0)),
                *extra_specs,
            ],
            out_specs=pl.BlockSpec((None, tq, group * LANES), lambda bi, h, t: (bi, t, h)),
            out_shape=jax.ShapeDtypeStruct((b, n_lat, n_kv * group * LANES), BF16),
            scratch_shapes=scratch(group * tq),
            compiler_params=_params("parallel", "parallel", "parallel"),
            name=name + str(layer),
        )(q, k, vt, *extra_args)

    def general():
        return call(_flash_kernel, Q_COLS_GENERAL, "flash_", lambda nq: [
            pltpu.VMEM((bk, nq + SKEW_LANES), F32), pltpu.VMEM((bk, nq + SKEW_LANES), F32),
            pltpu.VMEM((LANES + SUM_ROWS, nq), F32), pltpu.VMEM((1, nq), F32)])

    if bound is None:
        return general()

    def bounded():
        return call(_flash_bounded_kernel, Q_COLS, "flash_bounded_",
                    lambda nq: [pltpu.VMEM((LANES + MOD_ROWS, nq), F32)],
                    [pl.BlockSpec((1, LANES), lambda bi, h, t: (0, 0))], [jnp.full((1, LANES), bound, F32)])

    return lax.cond(bound <= SAFE_SCORE_BOUND, bounded, general)


def _ctx_attn_kernel(q_ref, k_ref, vt_ref, o_ref, *, group, dk):
    q = _stack_heads(q_ref, group, dk)
    s = _dot_nt(k_ref[...], q)
    p = jnp.exp2(s - jnp.max(s, axis=0, keepdims=True))
    l = jnp.sum(p, axis=0, keepdims=True)
    _unstack_heads(_dot(vt_ref[...], p.astype(BF16)) * (1.0 / l), o_ref, group)


def _ctx_attention(q, k, vt, *, group, dk, n_lat, layer):
    b, n, _ = q.shape
    n_ctx = n - n_lat
    n_kv = k.shape[2] // dk
    slab = vt.shape[3]
    assert slab == n_ctx
    t = n_lat // n_ctx
    return pl.pallas_call(
        functools.partial(_ctx_attn_kernel, group=group, dk=dk),
        grid=(b, n_kv),
        in_specs=[
            pl.BlockSpec((None, n_ctx, group * dk), lambda bi, h: (bi, t, h)),
            pl.BlockSpec((None, n_ctx, dk), lambda bi, h: (bi, t, h)),
            pl.BlockSpec((None, None, LANES, slab), lambda bi, h: (bi, t, h, 0)),
        ],
        out_specs=pl.BlockSpec((None, n_ctx, group * LANES), lambda bi, h: (bi, 0, h)),
        out_shape=jax.ShapeDtypeStruct((b, n_ctx, n_kv * group * LANES), BF16),
        compiler_params=_params("parallel", "parallel"),
        name="ctx_attn_" + str(layer),
    )(q, k, vt)


def _split_heads(q):
    lane = lax.broadcasted_iota(jnp.int32, (1, LANES), 1)
    zero = jnp.zeros_like(q)
    return jnp.concatenate(
        [jnp.where((lane >= j * NA_HEAD_DIM) & (lane < (j + 1) * NA_HEAD_DIM), q, zero)
         for j in range(LANES // NA_HEAD_DIM)], axis=0)


def _merge_heads(o_t, nq):
    parts = [o_t[j * NA_HEAD_DIM:(j + 1) * NA_HEAD_DIM, j * nq:(j + 1) * nq]
             for j in range(LANES // NA_HEAD_DIM)]
    return jnp.concatenate(parts, axis=0).T


def _na_ctx_kernel(q_ref, k_ref, vt_ref, o_ref):
    q2 = _split_heads(q_ref[...])
    s = _dot_nt(k_ref[...], q2)
    p = jnp.exp2(s - jnp.max(s, axis=0, keepdims=True))
    l = jnp.sum(p, axis=0, keepdims=True)
    o_t = _dot(vt_ref[...], p.astype(BF16)) * (1.0 / l)
    o_ref[...] = _merge_heads(o_t, q_ref.shape[0]).astype(BF16)


def _na_ctx_attention(q, k, vt, *, n_lat, layer):
    b, n, width = q.shape
    n_ctx = n - n_lat
    assert vt.shape[3] == n_ctx
    t = n_lat // n_ctx
    blk = pl.BlockSpec((None, n_ctx, LANES), lambda bi, h: (bi, t, h))
    return pl.pallas_call(
        _na_ctx_kernel,
        grid=(b, width // LANES),
        in_specs=[blk, blk, pl.BlockSpec((None, None, LANES, n_ctx), lambda bi, h: (bi, t, h, 0))],
        out_specs=pl.BlockSpec((None, n_ctx, LANES), lambda bi, h: (bi, 0, h)),
        out_shape=jax.ShapeDtypeStruct((b, n_ctx, width), BF16),
        compiler_params=_params("parallel", "parallel"),
        name="ctx_attn_" + str(layer),
    )(q, k, vt)


def _na_window_start(blk, rows, lib):
    lo = lib.minimum(lib.maximum(NA_BLOCK_ROWS * blk - NA_KH // 2, 0), rows - NA_KH)
    return lib.minimum(lo, rows - NA_WIN_ROWS)


def _na_kernel(q_ref, k_ref, vt_ref, bias_ref, o_ref, sl_ref, sc_ref, *, n_lat, n_ctx, rows):
    slab = vt_ref.shape[-1]
    nq = NA_BLOCK_ROWS * GRID_W
    n_blocks = rows // NA_BLOCK_ROWS
    nk = NA_WIN_ROWS * GRID_W
    kc = k_ref[n_lat:n_lat + n_ctx, :]
    n_here = q_ref.shape[0] // nq

    def scores(i):
        blk = pl.program_id(2) * NA_STEP_BLOCKS + i
        kind = jnp.where(blk == 0, 1, jnp.where(blk == n_blocks - 1, 2, 0))
        ws = _na_window_start(blk, rows, jnp)
        q2 = _split_heads(q_ref[i * nq:(i + 1) * nq, :])
        s_loc = _dot_nt(k_ref[pl.ds(pl.multiple_of(ws * GRID_W, slab), nk), :], q2) + bias_ref[kind]
        s_ctx = _dot_nt(kc, q2)
        sl_ref[i % 2, :, :s_loc.shape[1]] = s_loc
        sc_ref[i % 2, :, :s_ctx.shape[1]] = s_ctx
        return ws, jnp.maximum(jnp.max(s_loc, axis=0, keepdims=True), jnp.max(s_ctx, axis=0, keepdims=True))

    nxt = scores(0)
    for i in range(n_here):
        ws, m = nxt
        if i + 1 < n_here:
            nxt = scores(i + 1)
        parts = []
        for j in range(LANES // NA_HEAD_DIM):
            t = slice(j * nq, (j + 1) * nq)
            p_loc = jnp.exp2(sl_ref[i % 2, :, t] - m[:, t]).astype(BF16)
            p_ctx = jnp.exp2(sc_ref[i % 2, :, t] - m[:, t]).astype(BF16)
            o_t = _pv_t(vt_ref, ws * GRID_W // slab, p_loc, SUM_ROWS)
            o_t = o_t + _pv_t(vt_ref, n_lat // slab, p_ctx, SUM_ROWS)
            parts.append(o_t[j * NA_HEAD_DIM:(j + 1) * NA_HEAD_DIM, :] * (1.0 / o_t[LANES:LANES + 1, :]))
        o_ref[i * nq:(i + 1) * nq, :] = jnp.concatenate(parts, axis=0).T.astype(BF16)


def _na_bounded_kernel(q_ref, k_ref, vt_ref, bias_ref, bound_ref, o_ref, *, n_lat, n_ctx, rows):
    slab = vt_ref.shape[-1]
    nq = NA_BLOCK_ROWS * GRID_W
    n_blocks = rows // NA_BLOCK_ROWS
    nk = NA_WIN_ROWS * GRID_W
    kc = k_ref[n_lat:n_lat + n_ctx, :]
    bound = bound_ref[:, :1]
    for i in range(q_ref.shape[0] // nq):
        blk = pl.program_id(2) * NA_STEP_BLOCKS + i
        kind = jnp.where(blk == 0, 1, jnp.where(blk == n_blocks - 1, 2, 0))
        ws = _na_window_start(blk, rows, jnp)
        q2 = _split_heads(q_ref[i * nq:(i + 1) * nq, :])
        kw = k_ref[pl.ds(pl.multiple_of(ws * GRID_W, slab), nk), :]
        p_loc = jnp.exp2(_dot_nt(kw, q2) + (bias_ref[kind] - bound)).astype(BF16)
        p_ctx = jnp.exp2(_dot_nt(kc, q2) - bound).astype(BF16)
        o_t = _pv_t(vt_ref, ws * GRID_W // slab, p_loc, SUM_ROWS)
        o_t = o_t + _pv_t(vt_ref, n_lat // slab, p_ctx, SUM_ROWS)
        o_t = o_t[:LANES, :] * (1.0 / o_t[LANES:LANES + 1, :])
        o_ref[i * nq:(i + 1) * nq, :] = _merge_heads(o_t, nq).astype(BF16)


def _na_bias(rpb, rows):
    h, n_dr, n_dc = rpb.shape
    n_blocks = rows // NA_BLOCK_ROWS
    lead = GRID_W - NA_KW
    wide = 2 * GRID_W
    w = jnp.pad(rpb * LOG2E, ((0, 0), (0, 0), (lead, wide - lead - n_dc)), constant_values=MASK_VALUE)
    flat = jnp.broadcast_to(w[:, :, None, :], (h, n_dr, GRID_W, wide)).reshape(h, n_dr, GRID_W * wide)
    skew = flat[:, :, GRID_W - 1:GRID_W - 1 + GRID_W * (wide - 1)].reshape(h, n_dr, GRID_W, wide - 1)
    qc = jnp.arange(GRID_W)
    cs = jnp.clip(qc - NA_KW // 2, 0, GRID_W - NA_KW)
    col_ok = (qc[None, :] >= cs[:, None]) & (qc[None, :] < cs[:, None] + NA_KW)
    c_t = jnp.swapaxes(jnp.where(col_ok, skew[..., :GRID_W], MASK_VALUE), -1, -2)
    c_ext = jnp.concatenate([c_t, jnp.full((h, 1, GRID_W, GRID_W), MASK_VALUE, F32)], axis=1).astype(BF16)
    idx = np.full((3, NA_WIN_ROWS, NA_BLOCK_ROWS), n_dr, np.int32)
    for kind, blk in enumerate((1, 0, n_blocks - 1)):
        ws = _na_window_start(blk, rows, np)
        for j in range(NA_WIN_ROWS):
            for i in range(NA_BLOCK_ROWS):
                r = NA_BLOCK_ROWS * blk + i
                rs = min(max(r - NA_KH // 2, 0), rows - NA_KH)
                if rs <= ws + j < rs + NA_KH:
                    idx[kind, j, i] = ws + j - r + NA_KH - 1
    sub = LANES // NA_HEAD_DIM
    blocks = c_ext[:, idx].reshape(h // sub, sub, 3, NA_WIN_ROWS, NA_BLOCK_ROWS, GRID_W, GRID_W)
    bias = blocks.transpose(0, 2, 3, 5, 1, 4, 6)
    return bias.reshape(h // sub, 3, NA_WIN_ROWS * GRID_W, sub * NA_BLOCK_ROWS * GRID_W)


def _na_attention(q, k, vt, bias, bound, *, n_lat, layer):
    b, n, width = q.shape
    n_ctx = n - n_lat
    rows = n_lat // GRID_W
    n_slab, slab = vt.shape[1], vt.shape[3]
    tq = NA_STEP_BLOCKS * NA_BLOCK_ROWS * GRID_W
    in_specs = [
        pl.BlockSpec((None, tq, LANES), lambda hp, bi, rb: (bi, rb, hp)),
        pl.BlockSpec((None, n, LANES), lambda hp, bi, rb: (bi, 0, hp)),
        pl.BlockSpec((None, n_slab, LANES, slab), lambda hp, bi, rb: (bi, 0, hp, 0)),
        pl.BlockSpec((None,) + bias.shape[1:], lambda hp, bi, rb: (hp, 0, 0, 0)),
    ]
    common = dict(
        grid=(width // LANES, b, n_lat // tq),
        out_specs=pl.BlockSpec((None, tq, LANES), lambda hp, bi, rb: (bi, rb, hp)),
        out_shape=jax.ShapeDtypeStruct((b, n_lat, width), BF16),
        compiler_params=_params("parallel", "parallel", "arbitrary"),
    )

    def general():
        return pl.pallas_call(
            functools.partial(_na_kernel, n_lat=n_lat, n_ctx=n_ctx, rows=rows),
            in_specs=in_specs,
            scratch_shapes=[pltpu.VMEM((2, bias.shape[2], bias.shape[3] + SKEW_LANES), F32),
                            pltpu.VMEM((2, n_ctx, bias.shape[3] + SKEW_LANES), F32)],
            name="na_attn_" + str(layer), **common,
        )(q, k, vt, bias)

    def bounded():
        return pl.pallas_call(
            functools.partial(_na_bounded_kernel, n_lat=n_lat, n_ctx=n_ctx, rows=rows),
            in_specs=in_specs + [pl.BlockSpec((1, LANES), lambda hp, bi, rb: (0, 0))],
            name="na_attn_bounded_" + str(layer), **common,
        )(q, k, vt, bias, jnp.full((1, LANES), bound, F32))

    return lax.cond(bound <= SAFE_SCORE_BOUND, bounded, general)


def _out_proj_final_kernel(o_ref, g_ref, w_ref, x_ref, mod_ref, fg_ref, y_ref):
    a = (o_ref[...].astype(F32) * g_ref[...].astype(F32)).astype(BF16)
    xn = x_ref[...] + mod_ref[0, 2:3, :] * _dot(a, w_ref[...])
    y_ref[...] = xn * _rms(xn) * fg_ref[...]


def _out_proj_final(o, gs, w_out, x_lat, mod3, layer, n_lat, final_g):
    b, _, d = x_lat.shape
    tm = min(FINAL_TILE, n_lat)
    row = lambda bi, t: (bi, t, 0)
    w = o.shape[2]
    return pl.pallas_call(
        _out_proj_final_kernel,
        grid=(b, n_lat // tm),
        in_specs=[pl.BlockSpec((None, tm, w), row), pl.BlockSpec((None, tm, w), row),
                  pl.BlockSpec(w_out.shape, lambda bi, t: (0, 0)), pl.BlockSpec((None, tm, d), row),
                  pl.BlockSpec((1, 3, d), lambda bi, t: (layer * MOD_ROWS + bi, 0, 0)),
                  pl.BlockSpec((1, d), lambda bi, t: (0, 0))],
        out_specs=pl.BlockSpec((None, tm, d), row),
        out_shape=jax.ShapeDtypeStruct((b, n_lat, d), F32),
        compiler_params=_params("parallel", "parallel"),
        name="out_proj_final",
    )(o, gs, w_out, x_lat, mod3, final_g.reshape(1, d))


def _rope_tables(n_lat, n_ctx, rot_dim):
    n_freq = rot_dim // 4
    inv = ROPE_THETA ** (-jnp.arange(n_freq, dtype=F32) / n_freq)
    t = jnp.arange(n_lat)
    ang_r = (t // GRID_W).astype(F32)[:, None] * inv
    ang_c = (t % GRID_W).astype(F32)[:, None] * inv
    ang = jnp.concatenate([ang_r, ang_r, ang_c, ang_c], axis=-1)
    cos, sin = jnp.cos(ang), jnp.sin(ang)
    first = (jnp.arange(rot_dim) % (2 * n_freq)) < n_freq
    sin_up = jnp.where(first, -sin, 0.0)
    sin_dn = jnp.where(first, 0.0, sin)
    pad = LANES - rot_dim

    def finish(tab, fill):
        tab = jnp.pad(tab, ((0, 0), (0, pad)), constant_values=fill)
        return jnp.pad(tab, ((0, n_ctx), (0, 0)), constant_values=fill)

    return finish(cos, 1.0), finish(sin_up, 0.0), finish(sin_dn, 0.0)


def _mla_weights(w_in, w_uq, w_ukv):
    o_kv = MLA_Q_LORA
    o_kr = o_kv + MLA_KV_LORA
    o_g = o_kr + MLA_ROPE
    k_r = jnp.pad(w_in[:, o_kr:o_g], ((0, 0), (0, LANES - MLA_ROPE)))
    w_perm = jnp.concatenate([w_in[:, :o_kr], w_in[:, o_g:], k_r], axis=1).astype(BF16)
    uq = w_uq.reshape(MLA_Q_LORA, MLA_HEADS, MLA_NOPE + MLA_ROPE)
    uq = jnp.pad(uq, ((0, 0), (0, 0), (0, MLA_QK_PAD - MLA_NOPE - MLA_ROPE)))
    uq = uq.reshape(MLA_Q_LORA, MLA_HEADS * MLA_QK_PAD).astype(BF16)
    ukv = w_ukv.reshape(MLA_KV_LORA, MLA_HEADS, MLA_NOPE + MLA_V).astype(BF16)
    uk = ukv[:, :, :MLA_NOPE].reshape(MLA_KV_LORA, MLA_HEADS * MLA_NOPE)
    uv = ukv[:, :, MLA_NOPE:].reshape(MLA_KV_LORA, MLA_HEADS * MLA_V)
    return w_perm, uq, uk, uv


def kernel(x, c, ctx, c_ctx, mod_w, mod_b, norm_g, final_g, ga_w_in, ga_q_g, ga_k_g, ga_w_out, na_w_in, na_rpb, na_w_out, mla_w_in, mla_q_g, mla_kv_g, mla_w_uq, mla_w_ukv, mla_w_out):
    b, n_lat, d = x.shape
    n_ctx = ctx.shape[1]
    depth = mod_w.shape[0]
    assert n_lat % ROW_TILE == 0 and n_ctx == ROW_TILE and n_lat % n_ctx == 0
    assert b < MOD_ROWS and n_lat % (NA_STEP_BLOCKS * NA_BLOCK_ROWS * GRID_W) == 0
    assert n_lat // GRID_W >= NA_WIN_ROWS + NA_BLOCK_ROWS

    stream = (x, ctx, 0)
    cc = jnp.concatenate([c, c_ctx[None, :], jnp.zeros((MOD_ROWS - b - 1, d), F32)], axis=0)
    mod3 = _modulation(cc, mod_w, mod_b).reshape(depth * MOD_ROWS, 3, d)

    tab_a = _rope_tables(n_lat, n_ctx, GQA_HEAD_DIM)
    tab_m = _rope_tables(n_lat, n_ctx, MLA_ROPE)

    prev = None
    for i in range(depth):
        kind, j = i % N_MIXERS, i // N_MIXERS
        need_ctx = i < depth - 1

        def project(kern, consts, tables, widths, with_norms=False):
            outs = _project(kern, stream, mod3, i, norm_g[i], consts, tables, widths, n_lat, with_norms, prev)
            if prev is None:
                return stream, outs
            return (outs[0], outs[0], n_lat // ROW_TILE), outs[1:]

        if kind == 0:
            n_q = d // GQA_HEAD_DIM
            n_kv = n_q // GQA_GROUP
            kern = functools.partial(_gqa_proj_kernel, n_q=n_q, n_kv=n_kv, scale=GQA_HEAD_DIM ** -0.5 * LOG2E)
            consts = [ga_w_in[j].astype(BF16), ga_q_g[j].reshape(1, -1), ga_k_g[j].reshape(1, -1)]
            kvw = n_kv * GQA_HEAD_DIM
            stream, (q, k, vt, gs) = project(kern, consts, tab_a, [d, kvw, kvw, d])
            bound = (BOUND_SLACK * GQA_HEAD_DIM ** 0.5 * LOG2E
                     * jnp.max(jnp.abs(ga_q_g[j])) * jnp.max(jnp.abs(ga_k_g[j])))
            o = _flash_attention(q, k, vt, group=GQA_GROUP, dk=GQA_HEAD_DIM, n_lat=n_lat, layer=i, bound=bound)
            if need_ctx:
                oc = _ctx_attention(q, k, vt, group=GQA_GROUP, dk=GQA_HEAD_DIM, n_lat=n_lat, layer=i)
            w_out = ga_w_out[j]
        elif kind == 1:
            kern = functools.partial(_na_proj_kernel, width=d, scale=NA_HEAD_DIM ** -0.5 * LOG2E)
            stream, (q, k, vt, gs, norms) = project(kern, [na_w_in[j].astype(BF16)], (), [d, d, d, d], True)
            bound = (BOUND_SLACK * jnp.sqrt(jnp.max(norms[:, :, 0, :]) * jnp.max(norms[:, :, MOD_ROWS, :]))
                     + LOG2E * jnp.max(jnp.abs(na_rpb[j])))
            o = _na_attention(q, k, vt, _na_bias(na_rpb[j], n_lat // GRID_W), bound, n_lat=n_lat, layer=i)
            if need_ctx:
                oc = _na_ctx_attention(q, k, vt, n_lat=n_lat, layer=i)
            w_out = na_w_out[j]
        else:
            kern = functools.partial(_mla_proj_kernel, width=d, scale=(MLA_NOPE + MLA_ROPE) ** -0.5 * LOG2E)
            w_perm, uq, uk, uv = _mla_weights(mla_w_in[j], mla_w_uq[j], mla_w_ukv[j])
            consts = [w_perm, mla_q_g[j].reshape(1, -1), mla_kv_g[j].reshape(1, -1), uq, uk, uv]
            widths = [MLA_HEADS * MLA_QK_PAD, MLA_HEADS * MLA_QK_PAD, MLA_HEADS * MLA_V, d]
            stream, (q, k, vt, gs, norms) = project(kern, consts, tab_m, widths, True)
            bound = BOUND_SLACK * jnp.sqrt(jnp.max(norms[:, :, 0, 0]) * jnp.max(norms[:, :, MOD_ROWS, 0]))
            o = _flash_attention(q, k, vt, group=1, dk=MLA_QK_PAD, n_lat=n_lat, layer=i, bound=bound)
            if need_ctx:
                oc = _ctx_attention(q, k, vt, group=1, dk=MLA_QK_PAD, n_lat=n_lat, layer=i)
            w_out = mla_w_out[j]
        if need_ctx:
            prev = (o, oc, gs, w_out.astype(BF16), i)
    return _out_proj_final(o, gs, w_out.astype(BF16), stream[0], mod3, depth - 1, n_lat, final_g)
```
